```python
import math
import jax
import jax.numpy as jnp
from jax import lax
import numpy as np

D_MODEL = 2048
BATCH = 4
SEQ = 2048
DEPTH = 2

GRID_W = 64
CTX_LEN = 256
D_FF = 4 * D_MODEL
HEAD_DIM = 64
ROPE_THETA = 10000.0
NORM_EPS = 1e-6
N_MOD = 6

RWKV_HEADS = 16
RWKV_N = 64
RWKV_W = RWKV_HEADS * RWKV_N
DECAY_RANK = 64
ICLR_RANK = 64
GATE_RANK = 64
RWKV_GN_EPS = 64e-5
RWKV_COLS = 3 * RWKV_W + DECAY_RANK + ICLR_RANK + GATE_RANK

SWA_Q_HEADS = 16
SWA_KV_HEADS = 4
SWA_GROUP = SWA_Q_HEADS // SWA_KV_HEADS
SWA_WINDOW = 128
SWA_BLOCK = 128
SWA_COLS = (SWA_Q_HEADS + 2 * SWA_KV_HEADS) * HEAD_DIM

AB_IN = RWKV_COLS + SWA_COLS
AB_OUT = RWKV_W + SWA_Q_HEADS * HEAD_DIM

DIFF_HEADS = 8
DIFF_QK = HEAD_DIM
DIFF_V = 2 * DIFF_QK
DIFF_QBLOCK = 128
DIFF_COLS = 2 * DIFF_HEADS * 2 * DIFF_QK + DIFF_HEADS * DIFF_V

HGRN_HEADS = 8
HGRN_DK = 128
HGRN_DV = 128
HGRN_CHUNK = 64
HGRN_KW = HGRN_HEADS * HGRN_DK
HGRN_VW = HGRN_HEADS * HGRN_DV
HGRN_COLS = 3 * HGRN_KW + 2 * HGRN_VW

CD_IN = DIFF_COLS + HGRN_COLS
CD_OUT = DIFF_HEADS * DIFF_V + HGRN_VW

kernel_name = "hybrid_rwkv7_swa_diffattn_hgrn2_dit"


def rms_norm(x, eps=NORM_EPS):
    xf = x.astype(jnp.float32)
    return (xf * lax.rsqrt(jnp.mean(xf * xf, axis=-1, keepdims=True) + eps)).astype(x.dtype)


def modulate(h, shift, scale):
    return rms_norm(h) * (1 + scale) + shift


def sq_relu_mlp(h, w1, w2):
    return jnp.square(jax.nn.relu(h @ w1)) @ w2


def axial_rope_tables(rows, cols, dim, dtype):
    quarter = dim // 4
    inv = ROPE_THETA ** (-jnp.arange(quarter, dtype=jnp.float32) / quarter)
    ang = jnp.concatenate([rows[:, None] * inv, cols[:, None] * inv], axis=-1)
    return jnp.cos(ang).astype(dtype), jnp.sin(ang).astype(dtype)


def apply_rope(x, cos, sin):
    half = x.shape[-1] // 2
    x1, x2 = x[..., :half], x[..., half:]
    return jnp.concatenate([x1 * cos - x2 * sin, x1 * sin + x2 * cos], axis=-1)


def centred_shift(u):
    zero = jnp.zeros_like(u[:, :1])
    prev = jnp.concatenate([zero, u[:, :-1]], axis=1)
    nxt = jnp.concatenate([u[:, 1:], zero], axis=1)
    return 0.5 * (prev + nxt)


def stack_dirs(t_fwd, t_bwd):
    return jnp.stack([t_fwd, jnp.flip(t_bwd, axis=1)])


def sink_softmax(logits, sink):
    m = jnp.maximum(jnp.max(logits, axis=-1, keepdims=True), sink)
    p = jnp.exp(logits - m)
    return p / (jnp.sum(p, axis=-1, keepdims=True) + jnp.exp(sink - m))


def rwkv_prepare(u, mu, w0, w_up, a0, a_up, g_up, k_k, k_a):
    u = u + mu * (centred_shift(u) - u)
    idx = [RWKV_W, 2 * RWKV_W, 3 * RWKV_W, 3 * RWKV_W + DECAY_RANK, 3 * RWKV_W + DECAY_RANK + ICLR_RANK]
    r, k, v, wd, ad, gd = jnp.split(u, idx, axis=-1)
    w_log = -jax.nn.softplus(-(w0[:, None, None, :] + jnp.einsum('btr,xrc->xbtc', jnp.tanh(wd), w_up))) - 0.5
    decay = jnp.exp(-jnp.exp(w_log))
    a = jax.nn.sigmoid(a0 + ad @ a_up)
    g = jax.nn.sigmoid(gd) @ g_up
    heads = lambda t: t.reshape(*t.shape[:-1], RWKV_HEADS, RWKV_N)
    kk = heads(k * k_k).astype(jnp.float32)
    kk = (kk / jnp.maximum(jnp.sqrt(jnp.sum(kk * kk, axis=-1, keepdims=True)), 1e-12)).astype(k.dtype)
    k = k * (1 + (a - 1) * k_a)
    return heads(r), heads(decay), heads(k), heads(v), heads(a), kk, g


def rwkv_scan(r, w, k, v, a, b, s0, emit):
    xs = tuple(jnp.moveaxis(t, 2, 0) for t in (r, w, k, v, a, b))

    def step(s, inp):
        rt, wt, kt, vt, at, bt = inp
        sa = jnp.einsum('...vk,...k->...v', s, at)
        s = s * wt[..., None, :] + sa[..., :, None] * bt[..., None, :] + vt[..., :, None] * kt[..., None, :]
        y = jnp.einsum('...vk,...k->...v', s, rt) if emit else None
        return s, y

    s, y = lax.scan(step, s0, xs)
    return s, (jnp.moveaxis(y, 0, 2) if emit else None)


def head_group_norm(y, w, b):
    yf = y.astype(jnp.float32)
    mu = jnp.mean(yf, axis=-1, keepdims=True)
    var = jnp.mean(jnp.square(yf - mu), axis=-1, keepdims=True)
    yn = ((yf - mu) * lax.rsqrt(var + RWKV_GN_EPS)).astype(y.dtype)
    return yn.reshape(*y.shape[:-2], -1) * w + b


def rwkv_branch(u, s0, emit, mu, w0, w_up, a0, a_up, g_up, k_k, k_a, r_k, ln_w, ln_b):
    r, decay, k, v, a, kk, g = rwkv_prepare(u, mu, w0, w_up, a0, a_up, g_up, k_k, k_a)
    s_fin, y2 = rwkv_scan(stack_dirs(r, r), stack_dirs(decay[0], decay[1]), stack_dirs(k, k),
                          stack_dirs(v, v), stack_dirs(-kk, -kk), stack_dirs(kk * a, kk * a), s0, emit)
    if not emit:
        return None, s_fin
    y = y2[0] + jnp.flip(y2[1], axis=1)
    bonus = (jnp.sum(r * k * r_k, axis=-1, keepdims=True) * v).reshape(*y.shape[:-2], RWKV_W)
    return (head_group_norm(y, ln_w, ln_b) + bonus) * g, s_fin


def swa_latent(q, k, v, kc, vc, sink):
    B, T = q.shape[:2]
    nb = T // SWA_BLOCK
    nw = 3 * SWA_BLOCK
    qb = q.reshape(B, nb, SWA_BLOCK, SWA_KV_HEADS, SWA_GROUP, HEAD_DIM)
    pad = ((0, 0), (SWA_BLOCK, SWA_BLOCK), (0, 0), (0, 0))

    def band(t):
        tb = jnp.pad(t, pad).reshape(B, nb + 2, SWA_BLOCK, SWA_KV_HEADS, HEAD_DIM)
        return jnp.concatenate([tb[:, :-2], tb[:, 1:-1], tb[:, 2:]], axis=2)

    kw, vw = band(k), band(v)
    s_win = jnp.einsum('bnqhgd,bnkhd->bhgnqk', qb, kw)
    s_ctx = jnp.einsum('bnqhgd,bchd->bhgnqc', qb, kc)
    logits = jnp.concatenate([s_win, s_ctx], axis=-1).astype(jnp.float32) * HEAD_DIM ** -0.5
    qpos = jnp.arange(nb)[:, None, None] * SWA_BLOCK + jnp.arange(SWA_BLOCK)[None, :, None]
    kpos = jnp.arange(nb)[:, None, None] * SWA_BLOCK - SWA_BLOCK + jnp.arange(nw)[None, None, :]
    valid_win = (jnp.abs(kpos - qpos) <= SWA_WINDOW) & (kpos >= 0) & (kpos < T)
    valid = jnp.concatenate([valid_win, jnp.ones((nb, SWA_BLOCK, kc.shape[1]), bool)], axis=-1)
    logits = jnp.where(valid, logits, -jnp.inf)
    sink_b = sink.astype(jnp.float32).reshape(1, SWA_KV_HEADS, SWA_GROUP, 1, 1, 1)
    p = sink_softmax(logits, sink_b).astype(v.dtype)
    o = (jnp.einsum('bhgnqk,bnkhd->bnqhgd', p[..., :nw], vw)
         + jnp.einsum('bhgnqc,bchd->bnqhgd', p[..., nw:], vc))
    return o.reshape(B, T, SWA_Q_HEADS * HEAD_DIM)


def swa_context(qc, kc, vc, sink):
    B, C = qc.shape[:2]
    q = qc.reshape(B, C, SWA_KV_HEADS, SWA_GROUP, HEAD_DIM)
    logits = jnp.einsum('bqhgd,bkhd->bhgqk', q, kc).astype(jnp.float32) * HEAD_DIM ** -0.5
    p = sink_softmax(logits, sink.astype(jnp.float32).reshape(1, SWA_KV_HEADS, SWA_GROUP, 1, 1)).astype(vc.dtype)
    return jnp.einsum('bhgqk,bkhd->bqhgd', p, vc).reshape(B, C, SWA_Q_HEADS * HEAD_DIM)


def ab_mixer(hx, hc, cos, sin, need_ctx, w_in, w_out, mu, w0, w_up, a0, a_up, g_up,
             k_k, k_a, r_k, ln_w, ln_b, sink):
    B = hx.shape[0]
    ux, uc = hx @ w_in, hc @ w_in
    rw = (mu, w0, w_up, a0, a_up, g_up, k_k, k_a, r_k, ln_w, ln_b)
    s0 = jnp.zeros((2, B, RWKV_HEADS, RWKV_N, RWKV_N), hx.dtype)
    yc_a, s_ctx = rwkv_branch(uc[..., :RWKV_COLS], s0, need_ctx, *rw)
    yx_a, _ = rwkv_branch(ux[..., :RWKV_COLS], s_ctx, True, *rw)

    def qkv(u):
        n = u.shape[1]
        q, k, v = jnp.split(u[..., RWKV_COLS:], [SWA_Q_HEADS * HEAD_DIM, (SWA_Q_HEADS + SWA_KV_HEADS) * HEAD_DIM], axis=-1)
        return (q.reshape(B, n, SWA_Q_HEADS, HEAD_DIM), k.reshape(B, n, SWA_KV_HEADS, HEAD_DIM),
                v.reshape(B, n, SWA_KV_HEADS, HEAD_DIM))

    qx, kx, vx = qkv(ux)
    qc, kc, vc = qkv(uc)
    cs, sn = cos[:, None, :], sin[:, None, :]
    yx_b = swa_latent(apply_rope(qx, cs, sn), apply_rope(kx, cs, sn), vx, kc, vc, sink)
    yx = jnp.concatenate([yx_a, yx_b], axis=-1) @ w_out
    if not need_ctx:
        return yx, None
    yc = jnp.concatenate([yc_a, swa_context(qc, kc, vc, sink)], axis=-1) @ w_out
    return yx, yc


def diff_lambda(lam, lam_init):
    lf = lam.astype(jnp.float32)
    return jnp.exp(jnp.sum(lf[0] * lf[1])) - jnp.exp(jnp.sum(lf[2] * lf[3])) + lam_init


def diff_attend(q, k_all, v_all, lmb):
    s = jnp.einsum('bhiqd,bhikd->bhiqk', q, k_all).astype(jnp.float32) * DIFF_QK ** -0.5
    p = jax.nn.softmax(s, axis=-1)
    a = p[:, :, 0] - lmb * p[:, :, 1]
    return jnp.einsum('bhqk,bhkv->bhqv', a.astype(v_all.dtype), v_all)


def diff_post(o, subln, lam_init):
    o = rms_norm(o, 1e-5) * subln * (1 - lam_init)
    B, H, T, dv = o.shape
    return jnp.transpose(o, (0, 2, 1, 3)).reshape(B, T, H * dv)


def hgrn_prepare(u, lb):
    B, T, _ = u.shape
    q, zf, zb, i, g = jnp.split(u, [HGRN_KW, 2 * HGRN_KW, 3 * HGRN_KW, 3 * HGRN_KW + HGRN_VW], axis=-1)
    z = jnp.stack([zf, zb])
    lbb = lb[:, None, None, :]
    logf = jnp.logaddexp(jnp.log(lbb), jnp.log1p(-lbb) + jax.nn.log_sigmoid(z))
    k = (1 - lbb) * jax.nn.sigmoid(-z)

    def heads(t, d):
        return jnp.transpose(t.reshape(2, B, T, HGRN_HEADS, d), (0, 1, 3, 2, 4))

    return (heads(stack_dirs(q, q), HGRN_DK), heads(stack_dirs(k[0], k[1]), HGRN_DK),
            heads(stack_dirs(i, i), HGRN_DV), heads(stack_dirs(logf[0], logf[1]), HGRN_DK), g)


def hgrn_chunk_scan(q, k, v, logf, s0, emit):
    T = q.shape[-2]
    nc = T // HGRN_CHUNK
    chunks = lambda t: jnp.moveaxis(t.reshape(*t.shape[:-2], nc, HGRN_CHUNK, t.shape[-1]), -3, 0)
    incl = jnp.tril(jnp.ones((HGRN_CHUNK, HGRN_CHUNK), bool))[:, :, None]

    def step(s, inp):
        qc, kc, vc, gc = inp
        b = jnp.cumsum(gc, axis=-2)
        b_last = b[..., -1:, :]
        s_new = (jnp.exp(b_last[..., 0, :])[..., None] * s
                 + jnp.einsum('...sk,...sv->...kv', kc * jnp.exp(b_last - b), vc))
        if not emit:
            return s_new, None
        o = jnp.einsum('...tk,...kv->...tv', qc * jnp.exp(b), s)
        dec = jnp.exp(jnp.where(incl, b[..., :, None, :] - b[..., None, :, :], -jnp.inf))
        att = jnp.einsum('...tk,...sk,...tsk->...ts', qc, kc, dec)
        return s_new, o + jnp.einsum('...ts,...sv->...tv', att, vc)

    s, o = lax.scan(step, s0, tuple(chunks(t) for t in (q, k, v, logf)))
    if not emit:
        return s, None
    o = jnp.moveaxis(o, 0, -3)
    return s, o.reshape(*o.shape[:-3], T, o.shape[-1])


def hgrn_branch(u, s0, emit, lb, gnorm):
    q, k, v, logf, g = hgrn_prepare(u, lb)
    s_fin, o = hgrn_chunk_scan(q, k, v, logf, s0, emit)
    if not emit:
        return None, s_fin
    o = rms_norm(o[0] + jnp.flip(o[1], axis=-2)) * gnorm
    B, H, T, dv = o.shape
    o = jnp.transpose(o, (0, 2, 1, 3)).reshape(B, T, H * dv)
    return o * jax.nn.silu(g), s_fin


def cd_mixer(hx, hc, cos, sin, need_ctx, lam_init, lb, w_in, w_out, lam, subln, gnorm):
    B, T, _ = hx.shape
    ux, uc = hx @ w_in, hc @ w_in
    qk_w = DIFF_HEADS * 2 * DIFF_QK

    def qkv(u):
        n = u.shape[1]
        q, k, v = jnp.split(u[..., :DIFF_COLS], [qk_w, 2 * qk_w], axis=-1)
        return (q.reshape(B, n, DIFF_HEADS, 2, DIFF_QK), k.reshape(B, n, DIFF_HEADS, 2, DIFF_QK),
                v.reshape(B, n, DIFF_HEADS, DIFF_V))

    to_h = lambda t: jnp.moveaxis(t, 1, -2)
    qx, kx, vx = qkv(ux)
    qc, kc, vc = qkv(uc)
    cs, sn = cos[:, None, None, :], sin[:, None, None, :]
    qx, kx = apply_rope(qx, cs, sn), apply_rope(kx, cs, sn)
    lmb = diff_lambda(lam, lam_init)
    k_all = jnp.concatenate([to_h(kx), to_h(kc)], axis=-2)
    v_all = jnp.concatenate([to_h(vx), to_h(vc)], axis=-2)
    nb = T // DIFF_QBLOCK
    qb = jnp.moveaxis(to_h(qx).reshape(B, DIFF_HEADS, 2, nb, DIFF_QBLOCK, DIFF_QK), 3, 0)
    ox = lax.map(lambda qi: diff_attend(qi, k_all, v_all, lmb), qb)
    ox = jnp.moveaxis(ox, 0, 2).reshape(B, DIFF_HEADS, T, DIFF_V)
    yx_c = diff_post(ox, subln, lam_init)

    s0 = jnp.zeros((2, B, HGRN_HEADS, HGRN_DK, HGRN_DV), hx.dtype)
    yc_d, s_ctx = hgrn_branch(uc[..., DIFF_COLS:], s0, need_ctx, lb, gnorm)
    yx_d, _ = hgrn_branch(ux[..., DIFF_COLS:], s_ctx, True, lb, gnorm)
    yx = jnp.concatenate([yx_c, yx_d], axis=-1) @ w_out
    if not need_ctx:
        return yx, None
    oc = diff_attend(to_h(qc), to_h(kc), to_h(vc), lmb)
    yc = jnp.concatenate([diff_post(oc, subln, lam_init), yc_d], axis=-1) @ w_out
    return yx, yc


def setup_inputs(seed: int = 0) -> dict:
    key = jax.random.key(seed)
    keys = jax.random.split(key, 32)
    n_ab = (DEPTH + 1) // 2
    n_cd = DEPTH // 2
    f32 = jnp.float32
    nrm = lambda i, shape, s: s * jax.random.normal(keys[i], shape, f32)
    uni = lambda i, shape, lo, hi: jax.random.uniform(keys[i], shape, f32, lo, hi)
    return {
        "x": nrm(0, (BATCH, SEQ, D_MODEL), 1.0),
        "c": nrm(1, (BATCH, D_MODEL), 1.0),
        "ctx": nrm(2, (BATCH, CTX_LEN, D_MODEL), 1.0),
        "c_ctx": nrm(3, (D_MODEL,), 1.0),
        "ada_w": nrm(4, (DEPTH, D_MODEL, N_MOD * D_MODEL), 0.3 * D_MODEL ** -0.5),
        "ada_b": nrm(5, (DEPTH, N_MOD * D_MODEL), 0.02),
        "mlp_w1": nrm(6, (DEPTH, D_MODEL, D_FF), D_MODEL ** -0.5),
        "mlp_w2": nrm(7, (DEPTH, D_FF, D_MODEL), D_FF ** -0.5),
        "final_norm": 1.0 + nrm(8, (D_MODEL,), 0.05),
        "ab_w_in": nrm(9, (n_ab, D_MODEL, AB_IN), D_MODEL ** -0.5),
        "ab_w_out": nrm(10, (n_ab, AB_OUT, D_MODEL), AB_OUT ** -0.5),
        "ab_mu": uni(11, (n_ab, RWKV_COLS), 0.0, 1.0),
        "ab_w0": uni(12, (n_ab, 2, RWKV_W), -6.0, 1.0),
        "ab_w_up": nrm(13, (n_ab, 2, DECAY_RANK, RWKV_W), 0.1),
        "ab_a0": nrm(14, (n_ab, RWKV_W), 0.5),
        "ab_a_up": nrm(15, (n_ab, ICLR_RANK, RWKV_W), 0.05),
        "ab_g_up": nrm(16, (n_ab, GATE_RANK, RWKV_W), GATE_RANK ** -0.5),
        "ab_k_k": 0.85 + nrm(17, (n_ab, RWKV_W), 0.05),
        "ab_k_a": 1.0 + nrm(18, (n_ab, RWKV_W), 0.05),
        "ab_r_k": nrm(19, (n_ab, RWKV_HEADS, RWKV_N), 0.1),
        "ab_ln_w": 1.0 + nrm(20, (n_ab, RWKV_W), 0.05),
        "ab_ln_b": nrm(21, (n_ab, RWKV_W), 0.02),
        "ab_sink": nrm(22, (n_ab, SWA_Q_HEADS), 0.5),
        "cd_w_in": nrm(23, (n_cd, D_MODEL, CD_IN), D_MODEL ** -0.5),
        "cd_w_out": nrm(24, (n_cd, CD_OUT, D_MODEL), CD_OUT ** -0.5),
        "cd_lam": nrm(25, (n_cd, 4, DIFF_QK), 0.1),
        "cd_subln": 1.0 + nrm(26, (n_cd, DIFF_V), 0.05),
        "cd_gnorm": 1.0 + nrm(27, (n_cd, HGRN_DV), 0.05),
        "hgrn_lb_logits": nrm(28, (DEPTH, 2, HGRN_KW), 0.5),
    }


def reference(x, c, ctx, c_ctx, ada_w, ada_b, mlp_w1, mlp_w2, final_norm,
              ab_w_in, ab_w_out, ab_mu, ab_w0, ab_w_up, ab_a0, ab_a_up, ab_g_up,
              ab_k_k, ab_k_a, ab_r_k, ab_ln_w, ab_ln_b, ab_sink,
              cd_w_in, cd_w_out, cd_lam, cd_subln, cd_gnorm, hgrn_lb_logits):
    T = x.shape[1]
    ROWS = T // GRID_W
    rows = jnp.repeat(jnp.arange(ROWS), GRID_W).astype(jnp.float32)
    cols = (jnp.arange(ROWS * GRID_W) % GRID_W).astype(jnp.float32)
    cos, sin = axial_rope_tables(rows, cols, HEAD_DIM, x.dtype)

    lb_table = jnp.cumsum(jax.nn.softmax(hgrn_lb_logits.astype(jnp.float32), axis=0), axis=0)
    lb_table = (lb_table - lb_table[0]).astype(x.dtype)

    sc, sc_ctx = jax.nn.silu(c), jax.nn.silu(c_ctx)
    for l in range(DEPTH):
        last = l == DEPTH - 1
        mx = jnp.split((sc @ ada_w[l] + ada_b[l])[:, None, :], N_MOD, axis=-1)
        mc = jnp.split(sc_ctx @ ada_w[l] + ada_b[l], N_MOD, axis=-1)
        hx = modulate(x, mx[0], mx[1])
        hc = modulate(ctx, mc[0], mc[1])
        j = l // 2
        if l % 2 == 0:
            yx, yc = ab_mixer(hx, hc, cos, sin, not last, ab_w_in[j], ab_w_out[j], ab_mu[j], ab_w0[j],
                              ab_w_up[j], ab_a0[j], ab_a_up[j], ab_g_up[j], ab_k_k[j], ab_k_a[j],
                              ab_r_k[j], ab_ln_w[j], ab_ln_b[j], ab_sink[j])
        else:
            lam_init = 0.8 - 0.6 * math.exp(-0.3 * l)
            yx, yc = cd_mixer(hx, hc, cos, sin, not last, lam_init, lb_table[l], cd_w_in[j], cd_w_out[j],
                              cd_lam[j], cd_subln[j], cd_gnorm[j])
        x = x + mx[2] * yx
        x = x + mx[5] * sq_relu_mlp(modulate(x, mx[3], mx[4]), mlp_w1[l], mlp_w2[l])
        if not last:
            ctx = ctx + mc[2] * yc
            ctx = ctx + mc[5] * sq_relu_mlp(modulate(ctx, mc[3], mc[4]), mlp_w1[l], mlp_w2[l])
    return rms_norm(x) * final_norm
```

```python
import functools
import math

import jax
import jax.numpy as jnp
import numpy as np
from jax import lax
from jax.experimental import pallas as pl
from jax.experimental.pallas import tpu as pltpu

F32 = jnp.float32
BF16 = jnp.bfloat16

D = 2048
T = 2048
C = 256
GRID_W = 64
D_FF = 4 * D
HD = 64
ROPE_THETA = 10000.0
NORM_EPS = 1e-6
N_MOD = 6
RW = 1024
RWKV_GN_EPS = 64e-5
LORA = 64
SWA_WINDOW = 128
QB = 128
CHUNK = 64
SUB = 16
HG_CLAMP = 80.0
V7X_VMEM_LIMIT = 56 * 1024 * 1024

AB_Q, AB_KS, AB_VS, AB_LORA, AB_N = 3072, 4096, 4352, 4608, 4864


def _dot(a, b):
    return jnp.dot(a, b, preferred_element_type=F32)


def _dot_nt(a, b):
    return lax.dot_general(a, b, (((1,), (1,)), ((), ())), preferred_element_type=F32)


def _dot_tn(a, b):
    return lax.dot_general(a, b, (((0,), (0,)), ((), ())), preferred_element_type=F32)


def _split2(x):
    hi = x.astype(BF16)
    lo = (x - hi.astype(F32)).astype(BF16)
    return hi, lo


def _split3(x):
    hi = x.astype(BF16)
    r1 = x - hi.astype(F32)
    mid = r1.astype(BF16)
    lo = (r1 - mid.astype(F32)).astype(BF16)
    return hi, mid, lo


def _dot_exact_rhs(x, g):
    hi, lo = _split2(x)
    return _dot(hi, g) + _dot(lo, g)


def _tri_dot(tri, x):
    hi, mid, lo = _split3(x)
    return _dot(tri, hi) + _dot(tri, mid) + _dot(tri, lo)


def _seg_sum(x, g_down, g_up):
    return _dot_exact_rhs(_dot_exact_rhs(x, g_down), g_up)


def _rope(x, cosl, sinl):
    w = x.shape[-1]
    lane = lax.broadcasted_iota(jnp.int32, x.shape, x.ndim - 1)
    first = (lane & 63) < 32
    swapped = jnp.where(first, pltpu.roll(x, w - 32, axis=1), pltpu.roll(x, 32, axis=1))
    return x * cosl + swapped * sinl


def _sigmoid(x):
    return 1.0 / (1.0 + jnp.exp(-x))


def _softplus(x):
    return jnp.maximum(x, 0.0) + jnp.log(1.0 + jnp.exp(-jnp.abs(x)))


def _params(sem, vmem=None):
    return pltpu.CompilerParams(dimension_semantics=sem, vmem_limit_bytes=vmem)


def _ada_kernel(s_ref, w_ref, b_ref, o_ref):
    s = s_ref[...]
    s = s * _sigmoid(s)
    o_ref[...] = _dot(s.astype(BF16), w_ref[...].astype(BF16)) + b_ref[...]


def _ada(s_in, ada_w, ada_b):
    depth = ada_w.shape[0]
    n = ada_w.shape[2]
    tn = 1536
    out = pl.pallas_call(
        _ada_kernel,
        grid=(depth, n // tn),
        in_specs=[pl.BlockSpec((8, D), lambda l, j: (0, 0)),
                  pl.BlockSpec((None, D, tn), lambda l, j: (l, 0, j)),
                  pl.BlockSpec((None, 1, tn), lambda l, j: (l, 0, j))],
        out_specs=pl.BlockSpec((None, 8, tn), lambda l, j: (l, 0, j)),
        out_shape=jax.ShapeDtypeStruct((depth, 8, n), F32),
        compiler_params=_params(("arbitrary", "arbitrary"), V7X_VMEM_LIMIT),
        name="ada_mod",
    )(s_in, ada_w, ada_b.reshape(depth, 1, n))
    return out.reshape(depth, 8, 1, n)


def _mod_spec(layer, k, tm, nb):
    return pl.BlockSpec((None, None, 1, D),
                        lambda i, *_: (layer, jnp.minimum((i * tm) // T, nb), 0, k))


def _nmm_kernel(x_ref, sh_ref, sc_ref, w_ref, o_ref, lhs_ref):
    @pl.when(pl.program_id(1) == 0)
    def _():
        x = x_ref[...]
        xn = x * lax.rsqrt(jnp.mean(x * x, axis=-1, keepdims=True) + NORM_EPS)
        lhs_ref[...] = (xn * (1.0 + sc_ref[...]) + sh_ref[...]).astype(BF16)

    o_ref[...] = _dot(lhs_ref[...], w_ref[...]).astype(o_ref.dtype)


def _norm_mod_matmul(xc, mod, layer, w, nb, tn):
    rows = xc.shape[0]
    n = w.shape[1]
    tm = 512
    return pl.pallas_call(
        _nmm_kernel,
        grid=(rows // tm, n // tn),
        in_specs=[pl.BlockSpec((tm, D), lambda i, j: (i, 0)),
                  _mod_spec(layer, 0, tm, nb), _mod_spec(layer, 1, tm, nb),
                  pl.BlockSpec((D, tn), lambda i, j: (0, j))],
        out_specs=pl.BlockSpec((tm, tn), lambda i, j: (i, j)),
        out_shape=jax.ShapeDtypeStruct((rows, n), F32),
        scratch_shapes=[pltpu.VMEM((tm, D), BF16)],
        compiler_params=_params(("parallel", "arbitrary"), V7X_VMEM_LIMIT),
        name=f"in_proj_{layer}",
    )(xc, mod, mod, w)


def _mlp_kernel(x_ref, sh_ref, sc_ref, gt_ref, w1_ref, w2_ref, fn_ref, o_ref, lhs_ref, acc_ref, *, final):
    f = pl.program_id(1)

    @pl.when(f == 0)
    def _():
        x = x_ref[...]
        xn = x * lax.rsqrt(jnp.mean(x * x, axis=-1, keepdims=True) + NORM_EPS)
        lhs_ref[...] = (xn * (1.0 + sc_ref[...]) + sh_ref[...]).astype(BF16)
        acc_ref[...] = jnp.zeros_like(acc_ref)

    h = jnp.maximum(_dot(lhs_ref[...], w1_ref[...]), 0.0)
    acc_ref[...] += _dot((h * h).astype(BF16), w2_ref[...])

    @pl.when(f == pl.num_programs(1) - 1)
    def _():
        y = x_ref[...] + gt_ref[...] * acc_ref[...]
        if final:
            y = y * lax.rsqrt(jnp.mean(y * y, axis=-1, keepdims=True) + NORM_EPS) * fn_ref[...]
        o_ref[...] = y


def _mlp(xc, rows, mod, layer, w1, w2, final_norm, nb, final):
    tm, tf = 512, 1024
    return pl.pallas_call(
        functools.partial(_mlp_kernel, final=final),
        grid=(rows // tm, D_FF // tf),
        in_specs=[pl.BlockSpec((tm, D), lambda i, f: (i, 0)),
                  _mod_spec(layer, 3, tm, nb), _mod_spec(layer, 4, tm, nb), _mod_spec(layer, 5, tm, nb),
                  pl.BlockSpec((D, tf), lambda i, f: (0, f)),
                  pl.BlockSpec((tf, D), lambda i, f: (f, 0)),
                  pl.BlockSpec((1, D), lambda i, f: (0, 0))],
        out_specs=pl.BlockSpec((tm, D), lambda i, f: (i, 0)),
        out_shape=jax.ShapeDtypeStruct((rows, D), F32),
        scratch_shapes=[pltpu.VMEM((tm, D), BF16), pltpu.VMEM((tm, D), F32)],
        compiler_params=_params(("parallel", "arbitrary"), V7X_VMEM_LIMIT),
        name=f"mlp_{layer}",
    )(xc, mod, mod, mod, w1, w2, final_norm)


def _oproj_kernel(ya_ref, yb_ref, x_ref, gt_ref, wa_ref, wb_ref, o_ref):
    y = _dot(ya_ref[...].astype(BF16), wa_ref[...]) + _dot(yb_ref[...].astype(BF16), wb_ref[...])
    o_ref[...] = x_ref[...] + gt_ref[...] * y


def _out_proj(ya, yb, xc, rows, mod, layer, wa, wb, nb):
    tm = 512
    half = wa.shape[0]
    return pl.pallas_call(
        _oproj_kernel,
        grid=(rows // tm,),
        in_specs=[pl.BlockSpec((tm, half), lambda i: (i, 0)),
                  pl.BlockSpec((tm, half), lambda i: (i, 0)),
                  pl.BlockSpec((tm, D), lambda i: (i, 0)),
                  _mod_spec(layer, 2, tm, nb),
                  pl.BlockSpec((half, D), lambda i: (0, 0)),
                  pl.BlockSpec((half, D), lambda i: (0, 0))],
        out_specs=pl.BlockSpec((tm, D), lambda i: (i, 0)),
        out_shape=jax.ShapeDtypeStruct((rows, D), F32),
        compiler_params=_params(("parallel",), V7X_VMEM_LIMIT),
        name=f"out_proj_{layer}",
    )(ya, yb, xc, mod, wa, wb)


def _shifted(u, prev_row, next_row):
    tm = u.shape[0]
    row = lax.broadcasted_iota(jnp.int32, u.shape, 0)
    up = jnp.where(row == 0, prev_row, pltpu.roll(u, 1, axis=0))
    un = jnp.where(row == tm - 1, next_row, pltpu.roll(u, tm - 1, axis=0))
    return 0.5 * (up + un)


def _rwkv_prep_kernel(u_ref, ul_ref, up_ref, un_ref, ulp_ref, uln_ref, mu_ref, mul_ref, w0_ref, wup_ref,
                      a0_ref, aup_ref, gup_ref, kk_ref, ka_ref, rk_ref, gd_ref, gu_ref,
                      r_out, k_out, v_out, a_out, b_out, lw_out, g_out, bonus_out):
    u = u_ref[...]
    u = u + mu_ref[...] * (_shifted(u, up_ref[...], un_ref[...]) - u)
    ul = ul_ref[...]
    ul = ul + mul_ref[...] * (_shifted(ul, ulp_ref[...], uln_ref[...]) - ul)
    r, k, v = u[:, 0:RW], u[:, RW:2 * RW], u[:, 2 * RW:3 * RW]

    th = jnp.tanh(ul).astype(BF16)
    for d in range(2):
        w_log = -_softplus(-(w0_ref[d:d + 1, :] + _dot(th, wup_ref[d]))) - 0.5
        lw_out[d] = -jnp.exp(w_log)
    a = _sigmoid(a0_ref[...] + _dot(ul.astype(BF16), aup_ref[...]))
    g_out[...] = _dot(_sigmoid(ul).astype(BF16), gup_ref[...])

    kk = k * kk_ref[...]
    nrm = jnp.sqrt(_seg_sum(kk * kk, gd_ref[...], gu_ref[...]))
    kk = kk / jnp.maximum(nrm, 1e-12)
    k = k * (1.0 + (a - 1.0) * ka_ref[...])
    r_out[...] = r
    k_out[...] = k
    v_out[...] = v
    a_out[...] = -kk
    b_out[...] = kk * a
    bonus_out[...] = _seg_sum(r * k * rk_ref[...], gd_ref[...], gu_ref[...]) * v


def _seq_halo(u, tm, nb):
    rows = u.shape[0]
    nblk = rows // tm
    starts = np.arange(nblk) * tm
    seq_len = np.where(starts < nb * T, T, C)
    seq_off = np.where(starts < nb * T, starts % T, (starts - nb * T) % C)
    has_prev = seq_off > 0
    has_next = seq_off + tm < seq_len
    prev_idx = np.where(has_prev, starts - 1, 0)
    next_idx = np.where(has_next, starts + tm, 0)
    up = jnp.where(has_prev[:, None], u[prev_idx], 0.0)
    un = jnp.where(has_next[:, None], u[next_idx], 0.0)
    return up[:, None, :], un[:, None, :]


def _rwkv_prepare(u, nb, mu, w0, w_up, a0, a_up, g_up, k_k, k_a, r_k):
    rows = u.shape[0]
    tm = 256
    nblk = rows // tm
    up, un = _seq_halo(u, tm, nb)
    pad = jnp.zeros((LORA,), F32)
    mu_rkv = mu[:3 * RW].reshape(1, 3 * RW)
    mu_l = jnp.concatenate([mu[3 * RW:], pad]).reshape(1, 4 * LORA)

    def lora_w(w, slot):
        z = jnp.zeros((4 * LORA, RW), F32)
        return z.at[slot * LORA:(slot + 1) * LORA].set(w).astype(BF16)

    wup = jnp.stack([lora_w(w_up[0], 0), lora_w(w_up[1], 0)])
    aup = lora_w(a_up, 1)
    gup = lora_w(g_up, 2)
    head = np.arange(RW) // HD
    g_down = jnp.asarray(head[:, None] == np.arange(128)[None, :], BF16)
    g_upm = jnp.asarray(np.arange(128)[:, None] == head[None, :], BF16)

    row = lambda w: pl.BlockSpec((tm, w), lambda i: (i, 0))
    vec = lambda w: pl.BlockSpec((1, w), lambda i: (0, 0))
    full = lambda *s: pl.BlockSpec(s, lambda i: (0,) * len(s))
    out_sd = jax.ShapeDtypeStruct((rows, RW), F32)
    outs = pl.pallas_call(
        _rwkv_prep_kernel,
        grid=(nblk,),
        in_specs=[pl.BlockSpec((tm, 3 * RW), lambda i: (i, 0)),
                  pl.BlockSpec((tm, 4 * LORA), lambda i: (i, AB_LORA // (4 * LORA))),
                  pl.BlockSpec((None, 1, 3 * RW), lambda i: (i, 0, 0)),
                  pl.BlockSpec((None, 1, 3 * RW), lambda i: (i, 0, 0)),
                  pl.BlockSpec((None, 1, 4 * LORA), lambda i: (i, 0, AB_LORA // (4 * LORA))),
                  pl.BlockSpec((None, 1, 4 * LORA), lambda i: (i, 0, AB_LORA // (4 * LORA))),
                  vec(3 * RW), vec(4 * LORA), full(2, RW), full(2, 4 * LORA, RW),
                  vec(RW), full(4 * LORA, RW), full(4 * LORA, RW), vec(RW), vec(RW), vec(RW),
                  full(RW, 128), full(128, RW)],
        out_specs=[row(RW), row(RW), row(RW), row(RW), row(RW),
                   pl.BlockSpec((2, tm, RW), lambda i: (0, i, 0)), row(RW), row(RW)],
        out_shape=[out_sd, out_sd, out_sd, out_sd, out_sd,
                   jax.ShapeDtypeStruct((2, rows, RW), F32), out_sd, out_sd],
        compiler_params=_params(("parallel",), V7X_VMEM_LIMIT),
        name="rwkv_prepare",
    )(u, u, up, un, up, un, mu_rkv, mu_l, w0, wup, a0.reshape(1, RW), aup, gup,
      k_k.reshape(1, RW), k_a.reshape(1, RW), r_k.reshape(1, RW), g_down, g_upm)
    return outs


def _chunk_block(d, b, s, nb):
    nctx, nlat = C // CHUNK, T // CHUNK
    pos_c = jnp.where(d == 0, s, nctx - 1 - s)
    pos_l = jnp.where(d == 0, s - nctx, nlat - 1 - (s - nctx))
    return jnp.where(s < nctx, nb * nlat + b * nctx + pos_c, b * nlat + pos_l)


def _order_masks(d, n):
    row = lax.broadcasted_iota(jnp.int32, (n, n), 0)
    col = lax.broadcasted_iota(jnp.int32, (n, n), 1)
    flip = d * (n - 1)
    rr = jnp.abs(row - flip)
    cc = jnp.abs(col - flip)
    return cc <= rr, cc < rr


def _rwkv_scan_kernel(r_ref, k_ref, v_ref, a_ref, b_ref, lw_ref, y_ref, s_ref):
    d = pl.program_id(0)

    @pl.when(pl.program_id(2) == 0)
    def _():
        s_ref[...] = jnp.zeros_like(s_ref)

    n = CHUNK
    incl, strict = _order_masks(d, n)
    tri = jnp.where(incl, 1.0, 0.0).astype(BF16)
    eye = jnp.where(lax.broadcasted_iota(jnp.int32, (n, n), 0) == lax.broadcasted_iota(jnp.int32, (n, n), 1),
                    1.0, 0.0)
    fwd = (1 - d).astype(F32)
    lane = lax.broadcasted_iota(jnp.int32, (n, 128), 1)
    head_masks = (lane < HD, lane >= HD)
    same_head = ((lax.broadcasted_iota(jnp.int32, (128, 128), 0) < HD)
                 == (lax.broadcasted_iota(jnp.int32, (128, 128), 1) < HD))

    for j in range(RW // 128):
        sl = slice(j * 128, (j + 1) * 128)
        r, k, v, a, b, lw = r_ref[:, sl], k_ref[:, sl], v_ref[:, sl], a_ref[:, sl], b_ref[:, sl], lw_ref[:, sl]
        c = _tri_dot(tri, lw)
        c_last = fwd * c[n - 1:n] + (1.0 - fwd) * c[0:1]
        p_inv = jnp.exp(-c)
        p_end = jnp.exp(c_last - c)
        at = a * jnp.exp(c - lw)
        rt = r * jnp.exp(c)
        bt = (b * p_inv).astype(BF16)
        kt = (k * p_inv).astype(BF16)
        zero = jnp.zeros_like(at)
        lhs = jnp.concatenate([jnp.where(head_masks[0], at, zero), jnp.where(head_masks[1], at, zero),
                               jnp.where(head_masks[0], rt, zero), jnp.where(head_masks[1], rt, zero)],
                              axis=0).astype(BF16)
        gb = _dot_nt(lhs, bt)
        gk = _dot_nt(lhs, kt)
        w_acc = jnp.zeros((n, 128), F32)
        uv = jnp.zeros((n, 128), F32)
        yv = jnp.zeros((n, 128), F32)
        a_rb = []
        for h in range(2):
            mh = head_masks[h]
            a_ab = jnp.where(strict, gb[h * n:(h + 1) * n], 0.0)
            a_ak = jnp.where(strict, gk[h * n:(h + 1) * n], 0.0)
            a_rb.append(jnp.where(incl, gb[(2 + h) * n:(3 + h) * n], 0.0).astype(BF16))
            a_rk = jnp.where(incl, gk[(2 + h) * n:(3 + h) * n], 0.0)
            pw = a_ab.astype(BF16)
            tm = eye + a_ab
            for _ in range(5):
                p2 = _dot(pw, pw)
                pw = p2.astype(BF16)
                tm = tm + _dot(tm.astype(BF16), pw)
            vh = jnp.where(mh, v, zero).astype(BF16)
            av = _dot(jnp.concatenate([a_ak, a_rk], axis=0).astype(BF16), vh)
            tx = _dot(tm.astype(BF16),
                      jnp.concatenate([jnp.where(mh, at, zero), av[0:n]], axis=1).astype(BF16))
            w_acc = w_acc + tx[:, 0:128]
            uv = uv + tx[:, 128:256]
            yv = yv + av[n:2 * n]
        st = s_ref[j]
        uy0 = _dot_nt(jnp.concatenate([w_acc, rt], axis=0).astype(BF16), st.astype(BF16))
        u = uy0[0:n] + uv
        y = (uy0[n:2 * n] + yv
             + _dot(a_rb[0], jnp.where(head_masks[0], u, zero).astype(BF16))
             + _dot(a_rb[1], jnp.where(head_masks[1], u, zero).astype(BF16)))
        y_ref[:, sl] = y
        upd = _dot_tn(jnp.concatenate([u, v], axis=0).astype(BF16),
                      jnp.concatenate([b * p_end, k * p_end], axis=0).astype(BF16))
        s_ref[j] = st * jnp.exp(c_last) + jnp.where(same_head, upd, 0.0)


def _rwkv_scan(r, k, v, a, b, lw, nb):
    rows = r.shape[0]
    steps = (T + C) // CHUNK
    blk = lambda: pl.BlockSpec((CHUNK, RW), lambda d, bb, s: (_chunk_block(d, bb, s, nb), 0))
    return pl.pallas_call(
        _rwkv_scan_kernel,
        grid=(2, nb, steps),
        in_specs=[blk(), blk(), blk(), blk(), blk(),
                  pl.BlockSpec((None, CHUNK, RW), lambda d, bb, s: (d, _chunk_block(d, bb, s, nb), 0))],
        out_specs=pl.BlockSpec((None, CHUNK, RW), lambda d, bb, s: (d, _chunk_block(d, bb, s, nb), 0)),
        out_shape=jax.ShapeDtypeStruct((2, rows, RW), F32),
        scratch_shapes=[pltpu.VMEM((RW // 128, 128, 128), F32)],
        compiler_params=_params(("arbitrary", "arbitrary", "arbitrary"), V7X_VMEM_LIMIT),
        name="rwkv_scan",
    )(r, k, v, a, b, lw)


def _rwkv_post_kernel(y_ref, bonus_ref, g_ref, lnw_ref, lnb_ref, gd_ref, gu_ref, o_ref):
    y = y_ref[0] + y_ref[1]
    mu = _seg_sum(y, gd_ref[...], gu_ref[...]) * (1.0 / HD)
    yc = y - mu
    var = _seg_sum(yc * yc, gd_ref[...], gu_ref[...]) * (1.0 / HD)
    yn = yc * lax.rsqrt(var + RWKV_GN_EPS)
    o_ref[...] = (yn * lnw_ref[...] + lnb_ref[...] + bonus_ref[...]) * g_ref[...]


def _rwkv_post(y2, bonus, g, ln_w, ln_b):
    rows = bonus.shape[0]
    tm = 512
    head = np.arange(RW) // HD
    g_down = jnp.asarray(head[:, None] == np.arange(128)[None, :], BF16)
    g_upm = jnp.asarray(np.arange(128)[:, None] == head[None, :], BF16)
    row = pl.BlockSpec((tm, RW), lambda i: (i, 0))
    vec = pl.BlockSpec((1, RW), lambda i: (0, 0))
    return pl.pallas_call(
        _rwkv_post_kernel,
        grid=(rows // tm,),
        in_specs=[pl.BlockSpec((2, tm, RW), lambda i: (0, i, 0)), row, row, vec, vec,
                  pl.BlockSpec((RW, 128), lambda i: (0, 0)), pl.BlockSpec((128, RW), lambda i: (0, 0))],
        out_specs=row,
        out_shape=jax.ShapeDtypeStruct((rows, RW), F32),
        compiler_params=_params(("parallel",), V7X_VMEM_LIMIT),
        name="rwkv_post",
    )(y2, bonus, g, ln_w.reshape(1, RW), ln_b.reshape(1, RW), g_down, g_upm)


def _kv_block(i, nb):
    per = T // 256
    return jnp.where(i < nb * per, i // per, i - nb * per), jnp.where(i < nb * per, i % per, per)


def _swa_prep_kernel(k_ref, v_ref, cos_ref, sin_ref, ko_ref, vo_ref):
    cosl, sinl = cos_ref[...], sin_ref[...]
    lane = lax.broadcasted_iota(jnp.int32, cosl.shape, 1)
    low = lane < HD
    for src, dst, rope in ((k_ref, ko_ref, True), (v_ref, vo_ref, False)):
        for t in range(2):
            x = src[:, t * 128:(t + 1) * 128]
            if rope:
                x = _rope(x, cosl, sinl)
            lo = jnp.where(low, x, 0.0)
            hi = jnp.where(low, 0.0, x)
            parts = (lo, pltpu.roll(lo, HD, axis=1), pltpu.roll(hi, HD, axis=1), hi)
            for q, part in enumerate(parts):
                base = (4 * t + q) * 128
                dst[:, base:base + 128] = part.astype(BF16)


def _swa_prep(u, cosl, sinl, nb):
    rows = u.shape[0]
    tm = 256
    kv_out = pl.BlockSpec((None, tm, 1024), lambda i: (*_kv_block(i, nb), 0))
    sd = jax.ShapeDtypeStruct((nb, T + C, 1024), BF16)
    return pl.pallas_call(
        _swa_prep_kernel,
        grid=(rows // tm,),
        in_specs=[pl.BlockSpec((tm, 256), lambda i: (i, AB_KS // 256)),
                  pl.BlockSpec((tm, 256), lambda i: (i, AB_VS // 256)),
                  pl.BlockSpec((tm, 128), lambda i: (i, 0)),
                  pl.BlockSpec((tm, 128), lambda i: (i, 0))],
        out_specs=[kv_out, kv_out],
        out_shape=[sd, sd],
        compiler_params=_params(("parallel",), V7X_VMEM_LIMIT),
        name="swa_prep",
    )(u, u, cosl, sinl)


def _diff_prep_kernel(k_ref, v_ref, cos_ref, sin_ref, k0_ref, k1_ref, vo_ref):
    cosl, sinl = cos_ref[...], sin_ref[...]
    lane = lax.broadcasted_iota(jnp.int32, cosl.shape, 1)
    low = lane < HD
    for t in range(8):
        sl = slice(t * 128, (t + 1) * 128)
        x = _rope(k_ref[:, sl], cosl, sinl)
        k0_ref[:, sl] = jnp.where(low, x, 0.0).astype(BF16)
        k1_ref[:, sl] = jnp.where(low, 0.0, x).astype(BF16)
    vo_ref[...] = v_ref[...].astype(BF16)


def _diff_prep(u, cosl, sinl, nb):
    rows = u.shape[0]
    tm = 256
    kv_out = pl.BlockSpec((None, tm, 1024), lambda i: (*_kv_block(i, nb), 0))
    sd = jax.ShapeDtypeStruct((nb, T + C, 1024), BF16)
    return pl.pallas_call(
        _diff_prep_kernel,
        grid=(rows // tm,),
        in_specs=[pl.BlockSpec((tm, 1024), lambda i: (i, 1)),
                  pl.BlockSpec((tm, 1024), lambda i: (i, 2)),
                  pl.BlockSpec((tm, 128), lambda i: (i, 0)),
                  pl.BlockSpec((tm, 128), lambda i: (i, 0))],
        out_specs=[kv_out, kv_out, kv_out],
        out_shape=[sd, sd, sd],
        compiler_params=_params(("parallel",), V7X_VMEM_LIMIT),
        name="diff_prep",
    )(u, u, cosl, sinl)


def _swa_kernel(q_ref, cos_ref, sin_ref, k_ref, v_ref, sink_ref, o_ref, *, ctx_mode):
    n = pl.program_id(1)
    cosl, sinl = cos_ref[...], sin_ref[...]
    if not ctx_mode:
        start = pl.multiple_of(jnp.clip((n - 1) * QB, 0, T - 3 * QB), QB)
        kpos = start + lax.broadcasted_iota(jnp.int32, (QB, 3 * QB), 1)
        qpos = n * QB + lax.broadcasted_iota(jnp.int32, (QB, 3 * QB), 0)
        valid = jnp.abs(kpos - qpos) <= SWA_WINDOW
    for j in range(8):
        q = _rope(q_ref[:, j * 128:(j + 1) * 128], cosl, sinl) * (HD ** -0.5)
        qb = q.astype(BF16)
        acc = jnp.zeros((QB, 128), F32)
        for half in range(2):
            tile = ((j // 2) * 2 + half) * 128
            sink = sink_ref[2 * j + half:2 * j + half + 1, 0:1]
            s_c = _dot_nt(qb, k_ref[T:T + C, tile:tile + 128])
            m = jnp.maximum(jnp.max(s_c, axis=-1, keepdims=True), sink)
            if not ctx_mode:
                s_w = jnp.where(valid, _dot_nt(qb, k_ref[pl.ds(start, 3 * QB), tile:tile + 128]), -jnp.inf)
                m = jnp.maximum(m, jnp.max(s_w, axis=-1, keepdims=True))
            p_c = jnp.exp(s_c - m)
            den = jnp.sum(p_c, axis=-1, keepdims=True) + jnp.exp(sink - m)
            pv = _dot(p_c.astype(BF16), v_ref[T:T + C, tile:tile + 128])
            if not ctx_mode:
                p_w = jnp.exp(s_w - m)
                den = den + jnp.sum(p_w, axis=-1, keepdims=True)
                pv = pv + _dot(p_w.astype(BF16), v_ref[pl.ds(start, 3 * QB), tile:tile + 128])
            acc = acc + pv / den
        o_ref[:, j * 128:(j + 1) * 128] = acc


def _swa(u, cosl, sinl, kpad, vpad, sink, nb, ctx_mode):
    nq = (C if ctx_mode else T) // QB
    base = nb * (T // QB) if ctx_mode else 0
    rows = nb * nq * QB
    rowblk = lambda b, n: base + b * nq + n
    kv = pl.BlockSpec((None, T + C, 1024), lambda b, n: (b, 0, 0))
    return pl.pallas_call(
        functools.partial(_swa_kernel, ctx_mode=ctx_mode),
        grid=(nb, nq),
        in_specs=[pl.BlockSpec((QB, 1024), lambda b, n: (rowblk(b, n), AB_Q // 1024)),
                  pl.BlockSpec((QB, 128), lambda b, n: (rowblk(b, n), 0)),
                  pl.BlockSpec((QB, 128), lambda b, n: (rowblk(b, n), 0)),
                  kv, kv, pl.BlockSpec((16, 128), lambda b, n: (0, 0))],
        out_specs=pl.BlockSpec((QB, 1024), lambda b, n: (b * nq + n, 0)),
        out_shape=jax.ShapeDtypeStruct((rows, 1024), F32),
        compiler_params=_params(("parallel", "arbitrary"), V7X_VMEM_LIMIT),
        name="swa_ctx" if ctx_mode else "swa_latent",
    )(u, cosl, sinl, kpad, vpad, sink)


def _diff_kernel(q_ref, cos_ref, sin_ref, k0_ref, k1_ref, v_ref, lam_ref, sub_ref, o_ref):
    cosl, sinl = cos_ref[...], sin_ref[...]
    lam = lam_ref[...]
    for h in range(8):
        sl = slice(h * 128, (h + 1) * 128)
        qb = (_rope(q_ref[:, sl], cosl, sinl) * (HD ** -0.5)).astype(BF16)
        probs = []
        for kr in (k0_ref, k1_ref):
            s = _dot_nt(qb, kr[:, sl])
            e = jnp.exp(s - jnp.max(s, axis=-1, keepdims=True))
            probs.append(e / jnp.sum(e, axis=-1, keepdims=True))
        att = probs[0] - lam[:, 0:1] * probs[1]
        o = _dot(att.astype(BF16), v_ref[:, sl])
        o = o * lax.rsqrt(jnp.mean(o * o, axis=-1, keepdims=True) + 1e-5)
        o_ref[:, sl] = o * sub_ref[...]


def _diff_attn(u, cosl, sinl, k0, k1, vb, lam, sub, nb):
    nq = T // QB
    kv = pl.BlockSpec((None, T + C, 1024), lambda b, n: (b, 0, 0))
    vec = pl.BlockSpec((1, 128), lambda b, n: (0, 0))
    return pl.pallas_call(
        _diff_kernel,
        grid=(nb, nq),
        in_specs=[pl.BlockSpec((QB, 1024), lambda b, n: (b * nq + n, 0)),
                  pl.BlockSpec((QB, 128), lambda b, n: (b * nq + n, 0)),
                  pl.BlockSpec((QB, 128), lambda b, n: (b * nq + n, 0)),
                  kv, kv, kv, vec, vec],
        out_specs=pl.BlockSpec((QB, 1024), lambda b, n: (b * nq + n, 0)),
        out_shape=jax.ShapeDtypeStruct((nb * T, 1024), F32),
        compiler_params=_params(("parallel", "arbitrary"), V7X_VMEM_LIMIT),
        name="diff_attn",
    )(u, cosl, sinl, k0, k1, vb, lam, sub)


def _hgrn_scan_kernel(q_ref, z_ref, i_ref, lb_ref, o_ref, s_ref):
    d = pl.program_id(0)

    @pl.when(pl.program_id(2) == 0)
    def _():
        s_ref[...] = jnp.zeros_like(s_ref)

    n = CHUNK
    incl, _ = _order_masks(d, n)
    tri = jnp.where(incl, 1.0, 0.0).astype(BF16)
    fwd = (1 - d).astype(F32)
    for h in range(8):
        sl = slice(h * 128, (h + 1) * 128)
        q, z, v = q_ref[:, sl], z_ref[:, sl], i_ref[:, sl]
        lb = lb_ref[:, sl]
        logf = jnp.log(lb + (1.0 - lb) * _sigmoid(z))
        kk = (1.0 - lb) * _sigmoid(-z)
        bcum = _tri_dot(tri, logf)
        b_last = fwd * bcum[n - 1:n] + (1.0 - fwd) * bcum[0:1]
        b_excl = bcum - logf
        st = s_ref[h]
        o = _dot_nt((q * jnp.exp(bcum)).astype(BF16), st.astype(BF16))
        rows = []
        for sb in range(n // SUB):
            lo, hi = sb * SUB, (sb + 1) * SUB
            beta = fwd * b_excl[lo:lo + 1] + (1.0 - fwd) * b_excl[hi - 1:hi]
            qh = (q[lo:hi] * jnp.exp(bcum[lo:hi] - beta)).astype(BF16)
            kh = (kk * jnp.exp(jnp.minimum(beta - bcum, HG_CLAMP))).astype(BF16)
            rows.append(_dot_nt(qh, kh))
        att = jnp.where(incl, jnp.concatenate(rows, axis=0), 0.0)
        o_ref[:, sl] = o + _dot(att.astype(BF16), v.astype(BF16))
        upd = _dot_tn(v.astype(BF16), (kk * jnp.exp(b_last - bcum)).astype(BF16))
        s_ref[h] = st * jnp.exp(b_last) + upd


def _hgrn_scan(u, lb, nb):
    rows = u.shape[0]
    steps = (T + C) // CHUNK
    blk = lambda col: pl.BlockSpec((CHUNK, 1024), lambda d, bb, s: (_chunk_block(d, bb, s, nb), col))
    return pl.pallas_call(
        _hgrn_scan_kernel,
        grid=(2, nb, steps),
        in_specs=[blk(3),
                  pl.BlockSpec((CHUNK, 1024), lambda d, bb, s: (_chunk_block(d, bb, s, nb), 4 + d)),
                  blk(6),
                  pl.BlockSpec((None, 1, 1024), lambda d, bb, s: (d, 0, 0))],
        out_specs=pl.BlockSpec((None, CHUNK, 1024), lambda d, bb, s: (d, _chunk_block(d, bb, s, nb), 0)),
        out_shape=jax.ShapeDtypeStruct((2, rows, 1024), F32),
        scratch_shapes=[pltpu.VMEM((8, 128, 128), F32)],
        compiler_params=_params(("arbitrary", "arbitrary", "arbitrary"), V7X_VMEM_LIMIT),
        name="hgrn_scan",
    )(u, u, u, lb)


def _hgrn_post_kernel(o_ref, g_ref, gn_ref, y_ref):
    g = g_ref[...]
    for h in range(8):
        sl = slice(h * 128, (h + 1) * 128)
        o = o_ref[0, :, sl] + o_ref[1, :, sl]
        o = o * lax.rsqrt(jnp.mean(o * o, axis=-1, keepdims=True) + NORM_EPS) * gn_ref[...]
        gh = g[:, sl]
        y_ref[:, sl] = o * (gh * _sigmoid(gh))


def _hgrn_post(o2, u, gnorm, rows):
    tm = 512
    return pl.pallas_call(
        _hgrn_post_kernel,
        grid=(rows // tm,),
        in_specs=[pl.BlockSpec((2, tm, 1024), lambda i: (0, i, 0)),
                  pl.BlockSpec((tm, 1024), lambda i: (i, 7)),
                  pl.BlockSpec((1, 128), lambda i: (0, 0))],
        out_specs=pl.BlockSpec((tm, 1024), lambda i: (i, 0)),
        out_shape=jax.ShapeDtypeStruct((rows, 1024), F32),
        compiler_params=_params(("parallel",), V7X_VMEM_LIMIT),
        name="hgrn_post",
    )(o2, u, gnorm.reshape(1, 128))


def _rope_tables(nb):
    t = np.arange(T)
    quarter = HD // 4
    inv = ROPE_THETA ** (-jnp.arange(quarter, dtype=F32) / quarter)
    rows = jnp.asarray(t // GRID_W, F32)
    cols = jnp.asarray(t % GRID_W, F32)
    ang = jnp.concatenate([rows[:, None] * inv, cols[:, None] * inv], axis=-1)
    cos, sin = jnp.cos(ang), jnp.sin(ang)
    cosl = jnp.tile(jnp.concatenate([cos, cos], axis=-1), (nb, 2))
    sinl = jnp.tile(jnp.concatenate([-sin, sin], axis=-1), (nb, 2))
    cosl = jnp.concatenate([cosl, jnp.ones((nb * C, 128), F32)], axis=0)
    sinl = jnp.concatenate([sinl, jnp.zeros((nb * C, 128), F32)], axis=0)
    return cosl, sinl


def kernel(x, c, ctx, c_ctx, ada_w, ada_b, mlp_w1, mlp_w2, final_norm, ab_w_in, ab_w_out, ab_mu, ab_w0,
           ab_w_up, ab_a0, ab_a_up, ab_g_up, ab_k_k, ab_k_a, ab_r_k, ab_ln_w, ab_ln_b, ab_sink,
           cd_w_in, cd_w_out, cd_lam, cd_subln, cd_gnorm, hgrn_lb_logits):
    nb = x.shape[0]
    assert x.shape == (nb, T, D) and ctx.shape == (nb, C, D) and nb < 8
    assert ada_w.shape[0] == 2, "one AB layer followed by one CD layer"
    rx = nb * T
    xc = jnp.concatenate([x.reshape(rx, D), ctx.reshape(nb * C, D)], axis=0)
    cosl, sinl = _rope_tables(nb)

    s_in = jnp.zeros((8, D), F32).at[:nb].set(c).at[nb].set(c_ctx)
    mod = _ada(s_in, ada_w, ada_b)

    w = ab_w_in[0]
    w_in0 = jnp.concatenate([w[:, :3 * RW], w[:, 3 * RW + 3 * LORA:], w[:, 3 * RW:3 * RW + 3 * LORA],
                             jnp.zeros((D, LORA), F32)], axis=1).astype(BF16)
    u = _norm_mod_matmul(xc, mod, 0, w_in0, nb, AB_N // 2)
    r, k, v, a, b, lw, g, bonus = _rwkv_prepare(u, nb, ab_mu[0], ab_w0[0], ab_w_up[0], ab_a0[0], ab_a_up[0],
                                                ab_g_up[0], ab_k_k[0], ab_k_a[0], ab_r_k[0].reshape(RW))
    y2 = _rwkv_scan(r, k, v, a, b, lw, nb)
    ya = _rwkv_post(y2, bonus, g, ab_ln_w[0], ab_ln_b[0])
    kpad, vpad = _swa_prep(u, cosl, sinl, nb)
    sink = jnp.broadcast_to(ab_sink[0][:, None], (16, 128))
    yb = jnp.concatenate([_swa(u, cosl, sinl, kpad, vpad, sink, nb, False),
                          _swa(u, cosl, sinl, kpad, vpad, sink, nb, True)], axis=0)
    w_out0 = ab_w_out[0].astype(BF16)
    xc = _out_proj(ya, yb, xc, xc.shape[0], mod, 0, w_out0[:RW], w_out0[RW:], nb)
    fn = final_norm.reshape(1, D)
    xc = _mlp(xc, xc.shape[0], mod, 0, mlp_w1[0].astype(BF16), mlp_w2[0].astype(BF16), fn, nb, False)

    lam_init = 0.8 - 0.6 * math.exp(-0.3 * 1)
    lb_table = jnp.cumsum(jax.nn.softmax(hgrn_lb_logits.astype(F32), axis=0), axis=0)
    lb = (lb_table - lb_table[0])[1].reshape(2, 1, 1024)
    lf = cd_lam[0].astype(F32)
    lmb = jnp.exp(jnp.sum(lf[0] * lf[1])) - jnp.exp(jnp.sum(lf[2] * lf[3])) + lam_init
    u = _norm_mod_matmul(xc, mod, 1, cd_w_in[0].astype(BF16), nb, 2048)
    k0, k1, vb = _diff_prep(u, cosl, sinl, nb)
    yc = _diff_attn(u, cosl, sinl, k0, k1, vb, jnp.full((1, 128), lmb, F32),
                    (cd_subln[0] * (1.0 - lam_init)).reshape(1, 128), nb)
    o2 = _hgrn_scan(u, lb, nb)
    yd = _hgrn_post(o2, u, cd_gnorm[0], rx)
    w_out1 = cd_w_out[0].astype(BF16)
    xl = _out_proj(yc, yd, xc, rx, mod, 1, w_out1[:1024], w_out1[1024:], nb)
    out = _mlp(xl, rx, mod, 1, mlp_w1[1].astype(BF16), mlp_w2[1].astype(BF16), fn, nb, True)
    return out.reshape(nb, T, D)
```

```python
import functools
import math

import jax
import jax.numpy as jnp
import numpy as np
from jax import lax
from jax.experimental import pallas as pl
from jax.experimental.pallas import tpu as pltpu

F32 = jnp.float32
BF16 = jnp.bfloat16

D = 2048
T = 2048
C = 256
GRID_W = 64
D_FF = 4 * D
HD = 64
ROPE_THETA = 10000.0
NORM_EPS = 1e-6
N_MOD = 6
RW = 1024
RWKV_GN_EPS = 64e-5
LORA = 64
SWA_WINDOW = 128
QB = 128
CHUNK = 64
SUB = 16
HG_DK = 128
HG_CLAMP = 80.0
GROUP = 256
V7X_VMEM_LIMIT = 56 * 1024 * 1024

AB_Q, AB_KS, AB_VS, AB_LORA, AB_N = 3072, 4096, 4352, 4608, 4864


def _dot(a, b):
    return jnp.dot(a, b, preferred_element_type=F32)


def _dot_nt(a, b):
    return lax.dot_general(a, b, (((1,), (1,)), ((), ())), preferred_element_type=F32)


def _dot_tn(a, b):
    return lax.dot_general(a, b, (((0,), (0,)), ((), ())), preferred_element_type=F32)


def _split2(x):
    hi = x.astype(BF16)
    lo = (x - hi.astype(F32)).astype(BF16)
    return hi, lo


def _split3(x):
    hi = x.astype(BF16)
    r1 = x - hi.astype(F32)
    mid = r1.astype(BF16)
    lo = (r1 - mid.astype(F32)).astype(BF16)
    return hi, mid, lo


def _dot_exact_rhs(x, g):
    hi, lo = _split2(x)
    return _dot(hi, g) + _dot(lo, g)


def _tri_dot(tri, x):
    hi, mid, lo = _split3(x)
    return _dot(tri, hi) + _dot(tri, mid) + _dot(tri, lo)


def _seg_sum(x, g_down, g_up):
    return _dot_exact_rhs(_dot_exact_rhs(x, g_down), g_up)


def _rope(x, cosl, sinl):
    w = x.shape[-1]
    lane = lax.broadcasted_iota(jnp.int32, x.shape, x.ndim - 1)
    first = (lane & 63) < 32
    swapped = jnp.where(first, pltpu.roll(x, w - 32, axis=1), pltpu.roll(x, 32, axis=1))
    return x * cosl + swapped * sinl


def _sigmoid(x):
    return 1.0 / (1.0 + jnp.exp(-x))


def _softplus(x):
    return jnp.maximum(x, 0.0) + jnp.log(1.0 + jnp.exp(-jnp.abs(x)))


def _params(sem, vmem=None):
    return pltpu.CompilerParams(dimension_semantics=sem, vmem_limit_bytes=vmem)


def _ada_kernel(s_ref, w_ref, b_ref, o_ref):
    s = s_ref[...]
    s = s * _sigmoid(s)
    o_ref[...] = _dot(s.astype(BF16), w_ref[...].astype(BF16)) + b_ref[...]


def _ada(s_in, ada_w, ada_b):
    depth = ada_w.shape[0]
    n = ada_w.shape[2]
    tn = 1536
    out = pl.pallas_call(
        _ada_kernel,
        grid=(depth, n // tn),
        in_specs=[pl.BlockSpec((8, D), lambda l, j: (0, 0)),
                  pl.BlockSpec((None, D, tn), lambda l, j: (l, 0, j)),
                  pl.BlockSpec((None, 1, tn), lambda l, j: (l, 0, j))],
        out_specs=pl.BlockSpec((None, 8, tn), lambda l, j: (l, 0, j)),
        out_shape=jax.ShapeDtypeStruct((depth, 8, n), F32),
        compiler_params=_params(("arbitrary", "arbitrary"), V7X_VMEM_LIMIT),
        name="ada_mod",
    )(s_in, ada_w, ada_b.reshape(depth, 1, n))
    return out.reshape(depth, 8, 1, n)


def _mod_spec(layer, k, tm, nb):
    return pl.BlockSpec((None, None, 1, D),
                        lambda i, *_: (layer, jnp.minimum((i * tm) // T, nb), 0, k))


def _nmm_kernel(x_ref, sh_ref, sc_ref, w_ref, o_ref, lhs_ref):
    @pl.when(pl.program_id(1) == 0)
    def _():
        x = x_ref[...]
        xn = x * lax.rsqrt(jnp.mean(x * x, axis=-1, keepdims=True) + NORM_EPS)
        lhs_ref[...] = (xn * (1.0 + sc_ref[...]) + sh_ref[...]).astype(BF16)

    o_ref[...] = _dot(lhs_ref[...], w_ref[...]).astype(o_ref.dtype)


def _norm_mod_matmul(xc, mod, layer, w, nb, tn):
    rows = xc.shape[0]
    n = w.shape[1]
    tm = 512
    return pl.pallas_call(
        _nmm_kernel,
        grid=(rows // tm, n // tn),
        in_specs=[pl.BlockSpec((tm, D), lambda i, j: (i, 0)),
                  _mod_spec(layer, 0, tm, nb), _mod_spec(layer, 1, tm, nb),
                  pl.BlockSpec((D, tn), lambda i, j: (0, j))],
        out_specs=pl.BlockSpec((tm, tn), lambda i, j: (i, j)),
        out_shape=jax.ShapeDtypeStruct((rows, n), F32),
        scratch_shapes=[pltpu.VMEM((tm, D), BF16)],
        compiler_params=_params(("parallel", "arbitrary"), V7X_VMEM_LIMIT),
        name=f"in_proj_{layer}",
    )(xc, mod, mod, w)


def _mlp_kernel(x_ref, sh_ref, sc_ref, gt_ref, w1_ref, w2_ref, fn_ref, o_ref, lhs_ref, acc_ref, *, final):
    f = pl.program_id(1)

    @pl.when(f == 0)
    def _():
        x = x_ref[...]
        xn = x * lax.rsqrt(jnp.mean(x * x, axis=-1, keepdims=True) + NORM_EPS)
        lhs_ref[...] = (xn * (1.0 + sc_ref[...]) + sh_ref[...]).astype(BF16)
        acc_ref[...] = jnp.zeros_like(acc_ref)

    h = jnp.maximum(_dot(lhs_ref[...], w1_ref[...]), 0.0)
    acc_ref[...] += _dot((h * h).astype(BF16), w2_ref[...])

    @pl.when(f == pl.num_programs(1) - 1)
    def _():
        y = x_ref[...] + gt_ref[...] * acc_ref[...]
        if final:
            y = y * lax.rsqrt(jnp.mean(y * y, axis=-1, keepdims=True) + NORM_EPS) * fn_ref[...]
        o_ref[...] = y


def _mlp(xc, rows, mod, layer, w1, w2, final_norm, nb, final):
    tm, tf = 512, 1024
    return pl.pallas_call(
        functools.partial(_mlp_kernel, final=final),
        grid=(rows // tm, D_FF // tf),
        in_specs=[pl.BlockSpec((tm, D), lambda i, f: (i, 0)),
                  _mod_spec(layer, 3, tm, nb), _mod_spec(layer, 4, tm, nb), _mod_spec(layer, 5, tm, nb),
                  pl.BlockSpec((D, tf), lambda i, f: (0, f)),
                  pl.BlockSpec((tf, D), lambda i, f: (f, 0)),
                  pl.BlockSpec((1, D), lambda i, f: (0, 0))],
        out_specs=pl.BlockSpec((tm, D), lambda i, f: (i, 0)),
        out_shape=jax.ShapeDtypeStruct((rows, D), F32),
        scratch_shapes=[pltpu.VMEM((tm, D), BF16), pltpu.VMEM((tm, D), F32)],
        compiler_params=_params(("parallel", "arbitrary"), V7X_VMEM_LIMIT),
        name=f"mlp_{layer}",
    )(xc, mod, mod, mod, w1, w2, final_norm)


def _oproj_kernel(ya_ref, yb_ref, x_ref, gt_ref, wa_ref, wb_ref, o_ref):
    y = _dot(ya_ref[...].astype(BF16), wa_ref[...]) + _dot(yb_ref[...].astype(BF16), wb_ref[...])
    o_ref[...] = x_ref[...] + gt_ref[...] * y


def _out_proj(ya, yb, xc, rows, mod, layer, wa, wb, nb):
    tm = 512
    half = wa.shape[0]
    return pl.pallas_call(
        _oproj_kernel,
        grid=(rows // tm,),
        in_specs=[pl.BlockSpec((tm, half), lambda i: (i, 0)),
                  pl.BlockSpec((tm, half), lambda i: (i, 0)),
                  pl.BlockSpec((tm, D), lambda i: (i, 0)),
                  _mod_spec(layer, 2, tm, nb),
                  pl.BlockSpec((half, D), lambda i: (0, 0)),
                  pl.BlockSpec((half, D), lambda i: (0, 0))],
        out_specs=pl.BlockSpec((tm, D), lambda i: (i, 0)),
        out_shape=jax.ShapeDtypeStruct((rows, D), F32),
        compiler_params=_params(("parallel",), V7X_VMEM_LIMIT),
        name=f"out_proj_{layer}",
    )(ya, yb, xc, mod, wa, wb)


def _shifted(u, prev_row, next_row):
    tm = u.shape[0]
    row = lax.broadcasted_iota(jnp.int32, u.shape, 0)
    up = jnp.where(row == 0, prev_row, pltpu.roll(u, 1, axis=0))
    un = jnp.where(row == tm - 1, next_row, pltpu.roll(u, tm - 1, axis=0))
    return 0.5 * (up + un)


def _rwkv_prep_kernel(u_ref, ul_ref, up_ref, un_ref, ulp_ref, uln_ref, mu_ref, mul_ref, w0_ref, wup_ref,
                      a0_ref, aup_ref, gup_ref, kk_ref, ka_ref, rk_ref, gd_ref, gu_ref,
                      r_out, k_out, v_out, a_out, b_out, lw_out, g_out, bonus_out):
    u = u_ref[...]
    u = u + mu_ref[...] * (_shifted(u, up_ref[...], un_ref[...]) - u)
    ul = ul_ref[...]
    ul = ul + mul_ref[...] * (_shifted(ul, ulp_ref[...], uln_ref[...]) - ul)
    r, k, v = u[:, 0:RW], u[:, RW:2 * RW], u[:, 2 * RW:3 * RW]

    th = jnp.tanh(ul).astype(BF16)
    for d in range(2):
        w_log = -_softplus(-(w0_ref[d:d + 1, :] + _dot(th, wup_ref[d]))) - 0.5
        lw_out[d] = -jnp.exp(w_log)
    a = _sigmoid(a0_ref[...] + _dot(ul.astype(BF16), aup_ref[...]))
    g_out[...] = _dot(_sigmoid(ul).astype(BF16), gup_ref[...])

    kk = k * kk_ref[...]
    nrm = jnp.sqrt(_seg_sum(kk * kk, gd_ref[...], gu_ref[...]))
    kk = kk / jnp.maximum(nrm, 1e-12)
    k = k * (1.0 + (a - 1.0) * ka_ref[...])
    r_out[...] = r
    k_out[...] = k
    v_out[...] = v
    a_out[...] = -kk
    b_out[...] = kk * a
    bonus_out[...] = _seg_sum(r * k * rk_ref[...], gd_ref[...], gu_ref[...]) * v


def _seq_halo(u, tm, nb):
    rows = u.shape[0]
    nblk = rows // tm
    starts = np.arange(nblk) * tm
    seq_len = np.where(starts < nb * T, T, C)
    seq_off = np.where(starts < nb * T, starts % T, (starts - nb * T) % C)
    has_prev = seq_off > 0
    has_next = seq_off + tm < seq_len
    prev_idx = np.where(has_prev, starts - 1, 0)
    next_idx = np.where(has_next, starts + tm, 0)
    up = jnp.where(has_prev[:, None], u[prev_idx], 0.0)
    un = jnp.where(has_next[:, None], u[next_idx], 0.0)
    return up[:, None, :], un[:, None, :]


def _rwkv_prepare(u, nb, mu, w0, w_up, a0, a_up, g_up, k_k, k_a, r_k):
    rows = u.shape[0]
    tm = 256
    nblk = rows // tm
    up, un = _seq_halo(u, tm, nb)
    pad = jnp.zeros((LORA,), F32)
    mu_rkv = mu[:3 * RW].reshape(1, 3 * RW)
    mu_l = jnp.concatenate([mu[3 * RW:], pad]).reshape(1, 4 * LORA)

    def lora_w(w, slot):
        z = jnp.zeros((4 * LORA, RW), F32)
        return z.at[slot * LORA:(slot + 1) * LORA].set(w).astype(BF16)

    wup = jnp.stack([lora_w(w_up[0], 0), lora_w(w_up[1], 0)])
    aup = lora_w(a_up, 1)
    gup = lora_w(g_up, 2)
    head = np.arange(RW) // HD
    g_down = jnp.asarray(head[:, None] == np.arange(128)[None, :], BF16)
    g_upm = jnp.asarray(np.arange(128)[:, None] == head[None, :], BF16)

    row = lambda w: pl.BlockSpec((tm, w), lambda i: (i, 0))
    vec = lambda w: pl.BlockSpec((1, w), lambda i: (0, 0))
    full = lambda *s: pl.BlockSpec(s, lambda i: (0,) * len(s))
    out_sd = jax.ShapeDtypeStruct((rows, RW), F32)
    outs = pl.pallas_call(
        _rwkv_prep_kernel,
        grid=(nblk,),
        in_specs=[pl.BlockSpec((tm, 3 * RW), lambda i: (i, 0)),
                  pl.BlockSpec((tm, 4 * LORA), lambda i: (i, AB_LORA // (4 * LORA))),
                  pl.BlockSpec((None, 1, 3 * RW), lambda i: (i, 0, 0)),
                  pl.BlockSpec((None, 1, 3 * RW), lambda i: (i, 0, 0)),
                  pl.BlockSpec((None, 1, 4 * LORA), lambda i: (i, 0, AB_LORA // (4 * LORA))),
                  pl.BlockSpec((None, 1, 4 * LORA), lambda i: (i, 0, AB_LORA // (4 * LORA))),
                  vec(3 * RW), vec(4 * LORA), full(2, RW), full(2, 4 * LORA, RW),
                  vec(RW), full(4 * LORA, RW), full(4 * LORA, RW), vec(RW), vec(RW), vec(RW),
                  full(RW, 128), full(128, RW)],
        out_specs=[row(RW), row(RW), row(RW), row(RW), row(RW),
                   pl.BlockSpec((2, tm, RW), lambda i: (0, i, 0)), row(RW), row(RW)],
        out_shape=[out_sd, out_sd, out_sd, out_sd, out_sd,
                   jax.ShapeDtypeStruct((2, rows, RW), F32), out_sd, out_sd],
        compiler_params=_params(("parallel",), V7X_VMEM_LIMIT),
        name="rwkv_prepare",
    )(u, u, up, un, up, un, mu_rkv, mu_l, w0, wup, a0.reshape(1, RW), aup, gup,
      k_k.reshape(1, RW), k_a.reshape(1, RW), r_k.reshape(1, RW), g_down, g_upm)
    return outs


def _chunk_block(d, b, s, nb):
    nctx, nlat = C // CHUNK, T // CHUNK
    pos_c = jnp.where(d == 0, s, nctx - 1 - s)
    pos_l = jnp.where(d == 0, s - nctx, nlat - 1 - (s - nctx))
    return jnp.where(s < nctx, nb * nlat + b * nctx + pos_c, b * nlat + pos_l)


def _lane_stack(x, hw):
    head = lax.broadcasted_iota(jnp.int32, x.shape, 1) // hw
    return jnp.concatenate([jnp.where(head == h, x, 0.0) for h in range(GROUP // hw)], axis=0)


def _fold_rows(x, n):
    out = x[0:n]
    for h in range(1, x.shape[0] // n):
        out = out + x[h * n:(h + 1) * n]
    return out


def _block_masks(size, blk, d):
    row = lax.broadcasted_iota(jnp.int32, (size, size), 0)
    col = lax.broadcasted_iota(jnp.int32, (size, size), 1)
    rr, cc = row % blk, col % blk
    same = (row // blk) == (col // blk)
    if d == 0:
        return cc <= rr, cc < rr, same
    return cc >= rr, cc > rr, same


def _rwkv_scan_kernel(*refs):
    in_refs = (refs[0:6], refs[6:12])
    y_refs = refs[12:14]
    s_ref = refs[14]

    @pl.when(pl.program_id(1) == 0)
    def _():
        s_ref[...] = jnp.zeros_like(s_ref)

    n = CHUNK
    ngrp = RW // GROUP
    eye = jnp.where(lax.broadcasted_iota(jnp.int32, (GROUP, GROUP), 0)
                    == lax.broadcasted_iota(jnp.int32, (GROUP, GROUP), 1), 1.0, 0.0)
    masks = [_block_masks(GROUP, n, d) for d in range(2)]
    pre = []
    for d in range(2):
        r, k, v, a, b, lw = [ref[...] for ref in in_refs[d]]
        tri = jnp.where(_block_masks(n, n, d)[0], 1.0, 0.0).astype(BF16)
        c = _tri_dot(tri, lw)
        c_last = c[n - 1:n] if d == 0 else c[0:1]
        p_inv = jnp.exp(-c)
        p_end = jnp.exp(c_last - c)
        pre.append(dict(at=a * jnp.exp(c - lw), rt=r * jnp.exp(c), bt=b * p_inv, kt=k * p_inv,
                        bp=b * p_end, kp=k * p_end, v=v, dec=jnp.exp(c_last)))
    chains = [(d, g) for d in range(2) for g in range(ngrp)]
    sl = lambda g: slice(g * GROUP, (g + 1) * GROUP)
    part = lambda d, g, name: pre[d][name][:, sl(g)]

    a_st = [_lane_stack(part(d, g, "at"), HD).astype(BF16) for d, g in chains]
    v_st = [_lane_stack(part(d, g, "v"), HD).astype(BF16) for d, g in chains]
    gram = []
    for i, (d, g) in enumerate(chains):
        lhs = jnp.concatenate([a_st[i], _lane_stack(part(d, g, "rt"), HD).astype(BF16)], axis=0)
        rhs = jnp.concatenate([_lane_stack(part(d, g, "bt"), HD), _lane_stack(part(d, g, "kt"), HD)],
                              axis=0).astype(BF16)
        gram.append(_dot_nt(lhs, rhs))
    a_ab, a_ak, a_r = [], [], []
    for i, (d, g) in enumerate(chains):
        incl, strict, _ = masks[d]
        a_ab.append(jnp.where(strict, gram[i][0:GROUP, 0:GROUP], 0.0))
        a_ak.append(jnp.where(strict, gram[i][0:GROUP, GROUP:], 0.0).astype(BF16))
        a_r.append(jnp.where(jnp.concatenate([incl, incl], axis=1), gram[i][GROUP:, :], 0.0).astype(BF16))
    av = [_dot(a_ak[i], v_st[i]) for i in range(len(chains))]
    pw = [m.astype(BF16) for m in a_ab]
    tm = [eye + m for m in a_ab]
    for _ in range(5):
        pw = [_dot(p, p).astype(BF16) for p in pw]
        tm = [t + _dot(t.astype(BF16), p) for t, p in zip(tm, pw)]
    tx = [_dot(tm[i].astype(BF16), jnp.concatenate([a_st[i], av[i].astype(BF16)], axis=1))
          for i in range(len(chains))]
    st = [s_ref[d, g] for d, g in chains]
    uy0 = [_dot_nt(jnp.concatenate([_fold_rows(tx[i][:, 0:GROUP], n), part(d, g, "rt")], axis=0).astype(BF16),
                   st[i].astype(BF16)) for i, (d, g) in enumerate(chains)]
    u = [uy0[i][0:n] + _fold_rows(tx[i][:, GROUP:], n) for i in range(len(chains))]
    for i, (d, g) in enumerate(chains):
        uv = jnp.concatenate([_lane_stack(u[i], HD).astype(BF16), v_st[i]], axis=0)
        y_refs[d][:, sl(g)] = uy0[i][n:2 * n] + _fold_rows(_dot(a_r[i], uv), n)
    for i, (d, g) in enumerate(chains):
        upd = _dot_tn(jnp.concatenate([u[i], part(d, g, "v")], axis=0).astype(BF16),
                      jnp.concatenate([part(d, g, "bp"), part(d, g, "kp")], axis=0).astype(BF16))
        s_ref[d, g] = st[i] * part(d, g, "dec") + jnp.where(masks[d][2], upd, 0.0)


def _rwkv_scan(r, k, v, a, b, lw, nb):
    rows = r.shape[0]
    steps = (T + C) // CHUNK
    in_specs, args = [], []
    for d in range(2):
        blk = lambda bb, s, d=d: (_chunk_block(d, bb, s, nb), 0)
        in_specs += [pl.BlockSpec((CHUNK, RW), blk)] * 5
        in_specs.append(pl.BlockSpec((None, CHUNK, RW), lambda bb, s, d=d: (d, _chunk_block(d, bb, s, nb), 0)))
        args += [r, k, v, a, b, lw]
    out_sd = jax.ShapeDtypeStruct((rows, RW), F32)
    return pl.pallas_call(
        _rwkv_scan_kernel,
        grid=(nb, steps),
        in_specs=in_specs,
        out_specs=[pl.BlockSpec((CHUNK, RW), lambda bb, s, d=d: (_chunk_block(d, bb, s, nb), 0)) for d in range(2)],
        out_shape=[out_sd, out_sd],
        scratch_shapes=[pltpu.VMEM((2, RW // GROUP, GROUP, GROUP), F32)],
        compiler_params=_params(("parallel", "arbitrary"), V7X_VMEM_LIMIT),
        name="rwkv_scan",
    )(*args)


def _rwkv_post_kernel(y0_ref, y1_ref, bonus_ref, g_ref, lnw_ref, lnb_ref, gd_ref, gu_ref, o_ref):
    y = y0_ref[...] + y1_ref[...]
    mu = _seg_sum(y, gd_ref[...], gu_ref[...]) * (1.0 / HD)
    yc = y - mu
    var = _seg_sum(yc * yc, gd_ref[...], gu_ref[...]) * (1.0 / HD)
    yn = yc * lax.rsqrt(var + RWKV_GN_EPS)
    o_ref[...] = (yn * lnw_ref[...] + lnb_ref[...] + bonus_ref[...]) * g_ref[...]


def _rwkv_post(y0, y1, bonus, g, ln_w, ln_b):
    rows = bonus.shape[0]
    tm = 512
    head = np.arange(RW) // HD
    g_down = jnp.asarray(head[:, None] == np.arange(128)[None, :], BF16)
    g_upm = jnp.asarray(np.arange(128)[:, None] == head[None, :], BF16)
    row = pl.BlockSpec((tm, RW), lambda i: (i, 0))
    vec = pl.BlockSpec((1, RW), lambda i: (0, 0))
    return pl.pallas_call(
        _rwkv_post_kernel,
        grid=(rows // tm,),
        in_specs=[row, row, row, row, vec, vec,
                  pl.BlockSpec((RW, 128), lambda i: (0, 0)), pl.BlockSpec((128, RW), lambda i: (0, 0))],
        out_specs=row,
        out_shape=jax.ShapeDtypeStruct((rows, RW), F32),
        compiler_params=_params(("parallel",), V7X_VMEM_LIMIT),
        name="rwkv_post",
    )(y0, y1, bonus, g, ln_w.reshape(1, RW), ln_b.reshape(1, RW), g_down, g_upm)


def _kv_block(i, nb):
    per = T // 256
    return jnp.where(i < nb * per, i // per, i - nb * per), jnp.where(i < nb * per, i % per, per)


def _swa_prep_kernel(k_ref, v_ref, cos_ref, sin_ref, ko_ref, vo_ref):
    cosl, sinl = cos_ref[...], sin_ref[...]
    lane = lax.broadcasted_iota(jnp.int32, cosl.shape, 1)
    low = lane < HD
    for src, dst, rope in ((k_ref, ko_ref, True), (v_ref, vo_ref, False)):
        for t in range(2):
            x = src[:, t * 128:(t + 1) * 128]
            if rope:
                x = _rope(x, cosl, sinl)
            lo = jnp.where(low, x, 0.0)
            hi = jnp.where(low, 0.0, x)
            parts = (lo, pltpu.roll(lo, HD, axis=1), pltpu.roll(hi, HD, axis=1), hi)
            for q, part in enumerate(parts):
                base = (4 * t + q) * 128
                dst[:, base:base + 128] = part.astype(BF16)


def _swa_prep(u, cosl, sinl, nb):
    rows = u.shape[0]
    tm = 256
    kv_out = pl.BlockSpec((None, tm, 1024), lambda i: (*_kv_block(i, nb), 0))
    sd = jax.ShapeDtypeStruct((nb, T + C, 1024), BF16)
    return pl.pallas_call(
        _swa_prep_kernel,
        grid=(rows // tm,),
        in_specs=[pl.BlockSpec((tm, 256), lambda i: (i, AB_KS // 256)),
                  pl.BlockSpec((tm, 256), lambda i: (i, AB_VS // 256)),
                  pl.BlockSpec((tm, 128), lambda i: (i, 0)),
                  pl.BlockSpec((tm, 128), lambda i: (i, 0))],
        out_specs=[kv_out, kv_out],
        out_shape=[sd, sd],
        compiler_params=_params(("parallel",), V7X_VMEM_LIMIT),
        name="swa_prep",
    )(u, u, cosl, sinl)


def _diff_prep_kernel(k_ref, v_ref, cos_ref, sin_ref, k0_ref, k1_ref, vo_ref):
    cosl, sinl = cos_ref[...], sin_ref[...]
    lane = lax.broadcasted_iota(jnp.int32, cosl.shape, 1)
    low = lane < HD
    for t in range(8):
        sl = slice(t * 128, (t + 1) * 128)
        x = _rope(k_ref[:, sl], cosl, sinl)
        k0_ref[:, sl] = jnp.where(low, x, 0.0).astype(BF16)
        k1_ref[:, sl] = jnp.where(low, 0.0, x).astype(BF16)
    vo_ref[...] = v_ref[...].astype(BF16)


def _diff_prep(u, cosl, sinl, nb):
    rows = u.shape[0]
    tm = 256
    kv_out = pl.BlockSpec((None, tm, 1024), lambda i: (*_kv_block(i, nb), 0))
    sd = jax.ShapeDtypeStruct((nb, T + C, 1024), BF16)
    return pl.pallas_call(
        _diff_prep_kernel,
        grid=(rows // tm,),
        in_specs=[pl.BlockSpec((tm, 1024), lambda i: (i, 1)),
                  pl.BlockSpec((tm, 1024), lambda i: (i, 2)),
                  pl.BlockSpec((tm, 128), lambda i: (i, 0)),
                  pl.BlockSpec((tm, 128), lambda i: (i, 0))],
        out_specs=[kv_out, kv_out, kv_out],
        out_shape=[sd, sd, sd],
        compiler_params=_params(("parallel",), V7X_VMEM_LIMIT),
        name="diff_prep",
    )(u, u, cosl, sinl)


def _swa_kernel(q_ref, cos_ref, sin_ref, k_ref, v_ref, sink_ref, o_ref, *, ctx_mode):
    n = pl.program_id(1)
    cosl, sinl = cos_ref[...], sin_ref[...]
    if not ctx_mode:
        start = pl.multiple_of(jnp.clip((n - 1) * QB, 0, T - 3 * QB), QB)
        kpos = start + lax.broadcasted_iota(jnp.int32, (QB, 3 * QB), 1)
        qpos = n * QB + lax.broadcasted_iota(jnp.int32, (QB, 3 * QB), 0)
        valid = jnp.abs(kpos - qpos) <= SWA_WINDOW
    for j in range(8):
        q = _rope(q_ref[:, j * 128:(j + 1) * 128], cosl, sinl) * (HD ** -0.5)
        qb = q.astype(BF16)
        acc = jnp.zeros((QB, 128), F32)
        for half in range(2):
            tile = ((j // 2) * 2 + half) * 128
            sink = sink_ref[2 * j + half:2 * j + half + 1, 0:1]
            s_c = _dot_nt(qb, k_ref[T:T + C, tile:tile + 128])
            m = jnp.maximum(jnp.max(s_c, axis=-1, keepdims=True), sink)
            if not ctx_mode:
                s_w = jnp.where(valid, _dot_nt(qb, k_ref[pl.ds(start, 3 * QB), tile:tile + 128]), -jnp.inf)
                m = jnp.maximum(m, jnp.max(s_w, axis=-1, keepdims=True))
            p_c = jnp.exp(s_c - m)
            den = jnp.sum(p_c, axis=-1, keepdims=True) + jnp.exp(sink - m)
            pv = _dot(p_c.astype(BF16), v_ref[T:T + C, tile:tile + 128])
            if not ctx_mode:
                p_w = jnp.exp(s_w - m)
                den = den + jnp.sum(p_w, axis=-1, keepdims=True)
                pv = pv + _dot(p_w.astype(BF16), v_ref[pl.ds(start, 3 * QB), tile:tile + 128])
            acc = acc + pv / den
        o_ref[:, j * 128:(j + 1) * 128] = acc


def _swa(u, cosl, sinl, kpad, vpad, sink, nb, ctx_mode):
    nq = (C if ctx_mode else T) // QB
    base = nb * (T // QB) if ctx_mode else 0
    rows = nb * nq * QB
    rowblk = lambda b, n: base + b * nq + n
    kv = pl.BlockSpec((None, T + C, 1024), lambda b, n: (b, 0, 0))
    return pl.pallas_call(
        functools.partial(_swa_kernel, ctx_mode=ctx_mode),
        grid=(nb, nq),
        in_specs=[pl.BlockSpec((QB, 1024), lambda b, n: (rowblk(b, n), AB_Q // 1024)),
                  pl.BlockSpec((QB, 128), lambda b, n: (rowblk(b, n), 0)),
                  pl.BlockSpec((QB, 128), lambda b, n: (rowblk(b, n), 0)),
                  kv, kv, pl.BlockSpec((16, 128), lambda b, n: (0, 0))],
        out_specs=pl.BlockSpec((QB, 1024), lambda b, n: (b * nq + n, 0)),
        out_shape=jax.ShapeDtypeStruct((rows, 1024), F32),
        compiler_params=_params(("parallel", "arbitrary"), V7X_VMEM_LIMIT),
        name="swa_ctx" if ctx_mode else "swa_latent",
    )(u, cosl, sinl, kpad, vpad, sink)


def _diff_kernel(q_ref, cos_ref, sin_ref, k0_ref, k1_ref, v_ref, lam_ref, sub_ref, o_ref):
    cosl, sinl = cos_ref[...], sin_ref[...]
    lam = lam_ref[...]
    for h in range(8):
        sl = slice(h * 128, (h + 1) * 128)
        qb = (_rope(q_ref[:, sl], cosl, sinl) * (HD ** -0.5)).astype(BF16)
        probs = []
        for kr in (k0_ref, k1_ref):
            s = _dot_nt(qb, kr[:, sl])
            e = jnp.exp(s - jnp.max(s, axis=-1, keepdims=True))
            probs.append(e / jnp.sum(e, axis=-1, keepdims=True))
        att = probs[0] - lam[:, 0:1] * probs[1]
        o = _dot(att.astype(BF16), v_ref[:, sl])
        o = o * lax.rsqrt(jnp.mean(o * o, axis=-1, keepdims=True) + 1e-5)
        o_ref[:, sl] = o * sub_ref[...]


def _diff_attn(u, cosl, sinl, k0, k1, vb, lam, sub, nb):
    nq = T // QB
    kv = pl.BlockSpec((None, T + C, 1024), lambda b, n: (b, 0, 0))
    vec = pl.BlockSpec((1, 128), lambda b, n: (0, 0))
    return pl.pallas_call(
        _diff_kernel,
        grid=(nb, nq),
        in_specs=[pl.BlockSpec((QB, 1024), lambda b, n: (b * nq + n, 0)),
                  pl.BlockSpec((QB, 128), lambda b, n: (b * nq + n, 0)),
                  pl.BlockSpec((QB, 128), lambda b, n: (b * nq + n, 0)),
                  kv, kv, kv, vec, vec],
        out_specs=pl.BlockSpec((QB, 1024), lambda b, n: (b * nq + n, 0)),
        out_shape=jax.ShapeDtypeStruct((nb * T, 1024), F32),
        compiler_params=_params(("parallel", "arbitrary"), V7X_VMEM_LIMIT),
        name="diff_attn",
    )(u, cosl, sinl, k0, k1, vb, lam, sub)


def _hgrn_exact_att(q, kk, bcum, tmp_ref):
    n = CHUNK
    tmp_ref[0] = bcum
    tmp_ref[1] = kk
    coli = lax.broadcasted_iota(jnp.int32, (n, 2 * n), 1)

    def body(s, acc):
        bs = tmp_ref[0, pl.ds(s, 1), :]
        ks = tmp_ref[1, pl.ds(s, 1), :]
        w = q * jnp.exp(jnp.minimum(bcum - bs, 0.0)) * ks
        c0 = jnp.sum(w[:, 0:HG_DK], axis=-1, keepdims=True)
        c1 = jnp.sum(w[:, HG_DK:], axis=-1, keepdims=True)
        return acc + jnp.concatenate([jnp.where(coli == s, c0, 0.0), jnp.where(coli == s + n, c1, 0.0)], axis=0)

    return lax.fori_loop(0, n, body, jnp.zeros((2 * n, 2 * n), F32))


def _hgrn_scan_kernel(q0_ref, z0_ref, i0_ref, lb0_ref, q1_ref, z1_ref, i1_ref, lb1_ref,
                      o0_ref, o1_ref, s_ref, g_ref, tmp_ref):
    in_refs = ((q0_ref, z0_ref, i0_ref, lb0_ref), (q1_ref, z1_ref, i1_ref, lb1_ref))
    o_refs = (o0_ref, o1_ref)

    @pl.when(pl.program_id(1) == 0)
    def _():
        s_ref[...] = jnp.zeros_like(s_ref)

    n = CHUNK
    ngrp = 1024 // GROUP
    nsub = n // SUB
    pre = []
    for d in range(2):
        q, z, v, lb = [ref[...] for ref in in_refs[d]]
        logf = jnp.log(lb + (1.0 - lb) * _sigmoid(z))
        kk = (1.0 - lb) * _sigmoid(-z)
        tri = jnp.where(_block_masks(n, n, d)[0], 1.0, 0.0).astype(BF16)
        bcum = _tri_dot(tri, logf)
        b_last = bcum[n - 1:n] if d == 0 else bcum[0:1]
        b_excl = bcum - logf
        qh, kh = [], []
        for sb in range(nsub):
            lo, hi = sb * SUB, (sb + 1) * SUB
            beta = b_excl[lo:lo + 1] if d == 0 else b_excl[hi - 1:hi]
            qh.append(q[lo:hi] * jnp.exp(bcum[lo:hi] - beta))
            kh.append(kk * jnp.exp(beta - bcum))
        pre.append(dict(q=q, v=v, kk=kk, bcum=bcum, qh=qh, kh=kh, qe=q * jnp.exp(bcum),
                        ke=kk * jnp.exp(b_last - bcum), dec=jnp.exp(b_last), min_logf=jnp.min(logf)))
    chains = [(d, g) for d in range(2) for g in range(ngrp)]
    sl = lambda g: slice(g * GROUP, (g + 1) * GROUP)

    for i, (d, g) in enumerate(chains):
        rows = [_dot_nt(_lane_stack(pre[d]["qh"][sb][:, sl(g)], HG_DK).astype(BF16),
                        _lane_stack(pre[d]["kh"][sb][:, sl(g)], HG_DK).astype(BF16)) for sb in range(nsub)]
        g_ref[i] = jnp.concatenate([rows[sb][h * SUB:(h + 1) * SUB] for h in range(2) for sb in range(nsub)], axis=0)

    @pl.when(jnp.minimum(pre[0]["min_logf"], pre[1]["min_logf"]) < -(HG_CLAMP / SUB))
    def _():
        for i, (d, g) in enumerate(chains):
            g_ref[i] = _hgrn_exact_att(pre[d]["q"][:, sl(g)], pre[d]["kk"][:, sl(g)], pre[d]["bcum"][:, sl(g)],
                                       tmp_ref)

    st = [s_ref[d, g] for d, g in chains]
    inter = [_dot_nt(pre[d]["qe"][:, sl(g)].astype(BF16), st[i].astype(BF16)) for i, (d, g) in enumerate(chains)]
    for i, (d, g) in enumerate(chains):
        incl, _, same = _block_masks(2 * n, n, d)
        att = jnp.where(incl & same, g_ref[i], 0.0).astype(BF16)
        v_st = _lane_stack(pre[d]["v"][:, sl(g)], HG_DK).astype(BF16)
        o_refs[d][:, sl(g)] = inter[i] + _fold_rows(_dot(att, v_st), n)
    for i, (d, g) in enumerate(chains):
        upd = _dot_tn(pre[d]["v"][:, sl(g)].astype(BF16), pre[d]["ke"][:, sl(g)].astype(BF16))
        s_ref[d, g] = st[i] * pre[d]["dec"][:, sl(g)] + jnp.where(_block_masks(GROUP, HG_DK, d)[2], upd, 0.0)


def _hgrn_scan(u, lb, nb):
    rows = u.shape[0]
    steps = (T + C) // CHUNK
    in_specs, args = [], []
    for d in range(2):
        col = lambda j, d=d: pl.BlockSpec((CHUNK, 1024), lambda bb, s: (_chunk_block(d, bb, s, nb), j))
        in_specs += [col(3), col(4 + d), col(6), pl.BlockSpec((None, 1, 1024), lambda bb, s, d=d: (d, 0, 0))]
        args += [u, u, u, lb]
    out_sd = jax.ShapeDtypeStruct((rows, 1024), F32)
    ngrp = 1024 // GROUP
    return pl.pallas_call(
        _hgrn_scan_kernel,
        grid=(nb, steps),
        in_specs=in_specs,
        out_specs=[pl.BlockSpec((CHUNK, 1024), lambda bb, s, d=d: (_chunk_block(d, bb, s, nb), 0)) for d in range(2)],
        out_shape=[out_sd, out_sd],
        scratch_shapes=[pltpu.VMEM((2, ngrp, GROUP, GROUP), F32),
                        pltpu.VMEM((2 * ngrp, 2 * CHUNK, 2 * CHUNK), F32),
                        pltpu.VMEM((2, CHUNK, GROUP), F32)],
        compiler_params=_params(("parallel", "arbitrary"), V7X_VMEM_LIMIT),
        name="hgrn_scan",
    )(*args)


def _hgrn_post_kernel(o0_ref, o1_ref, g_ref, gn_ref, y_ref):
    g = g_ref[...]
    for h in range(8):
        sl = slice(h * 128, (h + 1) * 128)
        o = o0_ref[:, sl] + o1_ref[:, sl]
        o = o * lax.rsqrt(jnp.mean(o * o, axis=-1, keepdims=True) + NORM_EPS) * gn_ref[...]
        gh = g[:, sl]
        y_ref[:, sl] = o * (gh * _sigmoid(gh))


def _hgrn_post(o0, o1, u, gnorm, rows):
    tm = 512
    return pl.pallas_call(
        _hgrn_post_kernel,
        grid=(rows // tm,),
        in_specs=[pl.BlockSpec((tm, 1024), lambda i: (i, 0)),
                  pl.BlockSpec((tm, 1024), lambda i: (i, 0)),
                  pl.BlockSpec((tm, 1024), lambda i: (i, 7)),
                  pl.BlockSpec((1, 128), lambda i: (0, 0))],
        out_specs=pl.BlockSpec((tm, 1024), lambda i: (i, 0)),
        out_shape=jax.ShapeDtypeStruct((rows, 1024), F32),
        compiler_params=_params(("parallel",), V7X_VMEM_LIMIT),
        name="hgrn_post",
    )(o0, o1, u, gnorm.reshape(1, 128))


def _rope_tables(nb):
    t = np.arange(T)
    quarter = HD // 4
    inv = ROPE_THETA ** (-jnp.arange(quarter, dtype=F32) / quarter)
    rows = jnp.asarray(t // GRID_W, F32)
    cols = jnp.asarray(t % GRID_W, F32)
    ang = jnp.concatenate([rows[:, None] * inv, cols[:, None] * inv], axis=-1)
    cos, sin = jnp.cos(ang), jnp.sin(ang)
    cosl = jnp.tile(jnp.concatenate([cos, cos], axis=-1), (nb, 2))
    sinl = jnp.tile(jnp.concatenate([-sin, sin], axis=-1), (nb, 2))
    cosl = jnp.concatenate([cosl, jnp.ones((nb * C, 128), F32)], axis=0)
    sinl = jnp.concatenate([sinl, jnp.zeros((nb * C, 128), F32)], axis=0)
    return cosl, sinl


def kernel(x, c, ctx, c_ctx, ada_w, ada_b, mlp_w1, mlp_w2, final_norm, ab_w_in, ab_w_out, ab_mu, ab_w0,
           ab_w_up, ab_a0, ab_a_up, ab_g_up, ab_k_k, ab_k_a, ab_r_k, ab_ln_w, ab_ln_b, ab_sink,
           cd_w_in, cd_w_out, cd_lam, cd_subln, cd_gnorm, hgrn_lb_logits):
    nb = x.shape[0]
    assert x.shape == (nb, T, D) and ctx.shape == (nb, C, D) and nb < 8
    assert ada_w.shape[0] == 2, "one AB layer followed by one CD layer"
    rx = nb * T
    xc = jnp.concatenate([x.reshape(rx, D), ctx.reshape(nb * C, D)], axis=0)
    cosl, sinl = _rope_tables(nb)

    s_in = jnp.zeros((8, D), F32).at[:nb].set(c).at[nb].set(c_ctx)
    mod = _ada(s_in, ada_w, ada_b)

    w = ab_w_in[0]
    w_in0 = jnp.concatenate([w[:, :3 * RW], w[:, 3 * RW + 3 * LORA:], w[:, 3 * RW:3 * RW + 3 * LORA],
                             jnp.zeros((D, LORA), F32)], axis=1).astype(BF16)
    u = _norm_mod_matmul(xc, mod, 0, w_in0, nb, AB_N // 2)
    r, k, v, a, b, lw, g, bonus = _rwkv_prepare(u, nb, ab_mu[0], ab_w0[0], ab_w_up[0], ab_a0[0], ab_a_up[0],
                                                ab_g_up[0], ab_k_k[0], ab_k_a[0], ab_r_k[0].reshape(RW))
    y0, y1 = _rwkv_scan(r, k, v, a, b, lw, nb)
    ya = _rwkv_post(y0, y1, bonus, g, ab_ln_w[0], ab_ln_b[0])
    kpad, vpad = _swa_prep(u, cosl, sinl, nb)
    sink = jnp.broadcast_to(ab_sink[0][:, None], (16, 128))
    yb = jnp.concatenate([_swa(u, cosl, sinl, kpad, vpad, sink, nb, False),
                          _swa(u, cosl, sinl, kpad, vpad, sink, nb, True)], axis=0)
    w_out0 = ab_w_out[0].astype(BF16)
    xc = _out_proj(ya, yb, xc, xc.shape[0], mod, 0, w_out0[:RW], w_out0[RW:], nb)
    fn = final_norm.reshape(1, D)
    xc = _mlp(xc, xc.shape[0], mod, 0, mlp_w1[0].astype(BF16), mlp_w2[0].astype(BF16), fn, nb, False)

    lam_init = 0.8 - 0.6 * math.exp(-0.3 * 1)
    lb_table = jnp.cumsum(jax.nn.softmax(hgrn_lb_logits.astype(F32), axis=0), axis=0)
    lb = (lb_table - lb_table[0])[1].reshape(2, 1, 1024)
    lf = cd_lam[0].astype(F32)
    lmb = jnp.exp(jnp.sum(lf[0] * lf[1])) - jnp.exp(jnp.sum(lf[2] * lf[3])) + lam_init
    u = _norm_mod_matmul(xc, mod, 1, cd_w_in[0].astype(BF16), nb, 2048)
    k0, k1, vb = _diff_prep(u, cosl, sinl, nb)
    yc = _diff_attn(u, cosl, sinl, k0, k1, vb, jnp.full((1, 128), lmb, F32),
                    (cd_subln[0] * (1.0 - lam_init)).reshape(1, 128), nb)
    o0, o1 = _hgrn_scan(u, lb, nb)
    yd = _hgrn_post(o0, o1, u, cd_gnorm[0], rx)
    w_out1 = cd_w_out[0].astype(BF16)
    xl = _out_proj(yc, yd, xc, rx, mod, 1, w_out1[:1024], w_out1[1024:], nb)
    out = _mlp(xl, rx, mod, 1, mlp_w1[1].astype(BF16), mlp_w2[1].astype(BF16), fn, nb, True)
    return out.reshape(nb, T, D)
```

```python
import functools
import math

import jax
import jax.numpy as jnp
import numpy as np
from jax import lax
from jax.experimental import pallas as pl
from jax.experimental.pallas import tpu as pltpu

F32 = jnp.float32
BF16 = jnp.bfloat16

D = 2048
T = 2048
C = 256
GRID_W = 64
D_FF = 4 * D
HD = 64
ROPE_THETA = 10000.0
LOG2E = 1.4426950408889634
NORM_EPS = 1e-6
N_MOD = 6
RW = 1024
RWKV_GN_EPS = 64e-5
LORA = 64
SWA_WINDOW = 128
QB = 128
CHUNK = 64
SUB = 16
HG_DK = 128
HG_CLAMP = 80.0
GROUP = 256
V7X_VMEM_LIMIT = 56 * 1024 * 1024

AB_Q, AB_KS, AB_VS, AB_LORA, AB_N = 3072, 4096, 4352, 4608, 4864


def _dot(a, b):
    return jnp.dot(a, b, preferred_element_type=F32)


def _dot_nt(a, b):
    return lax.dot_general(a, b, (((1,), (1,)), ((), ())), preferred_element_type=F32)


def _dot_tn(a, b):
    return lax.dot_general(a, b, (((0,), (0,)), ((), ())), preferred_element_type=F32)


def _split2(x):
    hi = x.astype(BF16)
    lo = (x - hi.astype(F32)).astype(BF16)
    return hi, lo


def _split3(x):
    hi = x.astype(BF16)
    r1 = x - hi.astype(F32)
    mid = r1.astype(BF16)
    lo = (r1 - mid.astype(F32)).astype(BF16)
    return hi, mid, lo


def _dot_exact_rhs(x, g):
    hi, lo = _split2(x)
    return _dot(hi, g) + _dot(lo, g)


def _tri_dot(tri, x):
    hi, mid, lo = _split3(x)
    return _dot(tri, hi) + _dot(tri, mid) + _dot(tri, lo)


def _seg_sum(x, g_down, g_up):
    return _dot_exact_rhs(_dot_exact_rhs(x, g_down), g_up)


def _rope(x, cosl, sinl):
    w = x.shape[-1]
    lane = lax.broadcasted_iota(jnp.int32, x.shape, x.ndim - 1)
    first = (lane & 63) < 32
    swapped = jnp.where(first, pltpu.roll(x, w - 32, axis=1), pltpu.roll(x, 32, axis=1))
    return x * cosl + swapped * sinl


def _sigmoid(x):
    return 1.0 / (1.0 + jnp.exp(-x))


def _softplus(x):
    return jnp.maximum(x, 0.0) + jnp.log(1.0 + jnp.exp(-jnp.abs(x)))


def _params(sem, vmem=None):
    return pltpu.CompilerParams(dimension_semantics=sem, vmem_limit_bytes=vmem)


def _ada_kernel(s_ref, w_ref, b_ref, o_ref):
    s = s_ref[...]
    s = s * _sigmoid(s)
    o_ref[...] = _dot(s.astype(BF16), w_ref[...].astype(BF16)) + b_ref[...]


def _ada(s_in, ada_w, ada_b):
    depth = ada_w.shape[0]
    n = ada_w.shape[2]
    tn = 1536
    out = pl.pallas_call(
        _ada_kernel,
        grid=(depth, n // tn),
        in_specs=[pl.BlockSpec((8, D), lambda l, j: (0, 0)),
                  pl.BlockSpec((None, D, tn), lambda l, j: (l, 0, j)),
                  pl.BlockSpec((None, 1, tn), lambda l, j: (l, 0, j))],
        out_specs=pl.BlockSpec((None, 8, tn), lambda l, j: (l, 0, j)),
        out_shape=jax.ShapeDtypeStruct((depth, 8, n), F32),
        compiler_params=_params(("arbitrary", "arbitrary"), V7X_VMEM_LIMIT),
        name="ada_mod",
    )(s_in, ada_w, ada_b.reshape(depth, 1, n))
    return out.reshape(depth, 8, 1, n)


def _mod_spec(layer, k, tm, nb):
    return pl.BlockSpec((None, None, 1, D),
                        lambda i, *_: (layer, jnp.minimum((i * tm) // T, nb), 0, k))


def _nmm_kernel(x_ref, sh_ref, sc_ref, w_ref, o_ref, lhs_ref):
    @pl.when(pl.program_id(1) == 0)
    def _():
        x = x_ref[...]
        xn = x * lax.rsqrt(jnp.mean(x * x, axis=-1, keepdims=True) + NORM_EPS)
        lhs_ref[...] = (xn * (1.0 + sc_ref[...]) + sh_ref[...]).astype(BF16)

    o_ref[...] = _dot(lhs_ref[...], w_ref[...]).astype(o_ref.dtype)


def _norm_mod_matmul(xc, mod, layer, w, nb, tn):
    rows = xc.shape[0]
    n = w.shape[1]
    tm = 512
    return pl.pallas_call(
        _nmm_kernel,
        grid=(rows // tm, n // tn),
        in_specs=[pl.BlockSpec((tm, D), lambda i, j: (i, 0)),
                  _mod_spec(layer, 0, tm, nb), _mod_spec(layer, 1, tm, nb),
                  pl.BlockSpec((D, tn), lambda i, j: (0, j))],
        out_specs=pl.BlockSpec((tm, tn), lambda i, j: (i, j)),
        out_shape=jax.ShapeDtypeStruct((rows, n), F32),
        scratch_shapes=[pltpu.VMEM((tm, D), BF16)],
        compiler_params=_params(("parallel", "arbitrary"), V7X_VMEM_LIMIT),
        name=f"in_proj_{layer}",
    )(xc, mod, mod, w)


def _mlp_kernel(x_ref, sh_ref, sc_ref, gt_ref, w1_ref, w2_ref, fn_ref, o_ref, lhs_ref, acc_ref, *, final):
    f = pl.program_id(1)

    @pl.when(f == 0)
    def _():
        x = x_ref[...]
        xn = x * lax.rsqrt(jnp.mean(x * x, axis=-1, keepdims=True) + NORM_EPS)
        lhs_ref[...] = (xn * (1.0 + sc_ref[...]) + sh_ref[...]).astype(BF16)
        acc_ref[...] = jnp.zeros_like(acc_ref)

    h = jnp.maximum(_dot(lhs_ref[...], w1_ref[...]), 0.0)
    acc_ref[...] += _dot((h * h).astype(BF16), w2_ref[...])

    @pl.when(f == pl.num_programs(1) - 1)
    def _():
        y = x_ref[...] + gt_ref[...] * acc_ref[...]
        if final:
            y = y * lax.rsqrt(jnp.mean(y * y, axis=-1, keepdims=True) + NORM_EPS) * fn_ref[...]
        o_ref[...] = y


def _mlp(xc, rows, mod, layer, w1, w2, final_norm, nb, final):
    tm, tf = 512, 1024
    return pl.pallas_call(
        functools.partial(_mlp_kernel, final=final),
        grid=(rows // tm, D_FF // tf),
        in_specs=[pl.BlockSpec((tm, D), lambda i, f: (i, 0)),
                  _mod_spec(layer, 3, tm, nb), _mod_spec(layer, 4, tm, nb), _mod_spec(layer, 5, tm, nb),
                  pl.BlockSpec((D, tf), lambda i, f: (0, f)),
                  pl.BlockSpec((tf, D), lambda i, f: (f, 0)),
                  pl.BlockSpec((1, D), lambda i, f: (0, 0))],
        out_specs=pl.BlockSpec((tm, D), lambda i, f: (i, 0)),
        out_shape=jax.ShapeDtypeStruct((rows, D), F32),
        scratch_shapes=[pltpu.VMEM((tm, D), BF16), pltpu.VMEM((tm, D), F32)],
        compiler_params=_params(("parallel", "arbitrary"), V7X_VMEM_LIMIT),
        name=f"mlp_{layer}",
    )(xc, mod, mod, mod, w1, w2, final_norm)


def _oproj_kernel(ya_ref, yb_ref, x_ref, gt_ref, wa_ref, wb_ref, o_ref):
    y = _dot(ya_ref[...].astype(BF16), wa_ref[...]) + _dot(yb_ref[...].astype(BF16), wb_ref[...])
    o_ref[...] = x_ref[...] + gt_ref[...] * y


def _out_proj(ya, yb, xc, rows, mod, layer, wa, wb, nb):
    tm = 512
    half = wa.shape[0]
    return pl.pallas_call(
        _oproj_kernel,
        grid=(rows // tm,),
        in_specs=[pl.BlockSpec((tm, half), lambda i: (i, 0)),
                  pl.BlockSpec((tm, half), lambda i: (i, 0)),
                  pl.BlockSpec((tm, D), lambda i: (i, 0)),
                  _mod_spec(layer, 2, tm, nb),
                  pl.BlockSpec((half, D), lambda i: (0, 0)),
                  pl.BlockSpec((half, D), lambda i: (0, 0))],
        out_specs=pl.BlockSpec((tm, D), lambda i: (i, 0)),
        out_shape=jax.ShapeDtypeStruct((rows, D), F32),
        compiler_params=_params(("parallel",), V7X_VMEM_LIMIT),
        name=f"out_proj_{layer}",
    )(ya, yb, xc, mod, wa, wb)


def _shifted(u, prev_row, next_row):
    tm = u.shape[0]
    row = lax.broadcasted_iota(jnp.int32, u.shape, 0)
    up = jnp.where(row == 0, prev_row, pltpu.roll(u, 1, axis=0))
    un = jnp.where(row == tm - 1, next_row, pltpu.roll(u, tm - 1, axis=0))
    return 0.5 * (up + un)


def _rwkv_prep_kernel(u_ref, ul_ref, up_ref, un_ref, ulp_ref, uln_ref, mu_ref, mul_ref, w0_ref, wup_ref,
                      a0_ref, aup_ref, gup_ref, kk_ref, ka_ref, rk_ref, gd_ref, gu_ref,
                      r_out, k_out, v_out, a_out, b_out, lw_out, g_out, bonus_out):
    u = u_ref[...]
    u = u + mu_ref[...] * (_shifted(u, up_ref[...], un_ref[...]) - u)
    ul = ul_ref[...]
    ul = ul + mul_ref[...] * (_shifted(ul, ulp_ref[...], uln_ref[...]) - ul)
    r, k, v = u[:, 0:RW], u[:, RW:2 * RW], u[:, 2 * RW:3 * RW]

    th = jnp.tanh(ul).astype(BF16)
    for d in range(2):
        w_log = -_softplus(-(w0_ref[d:d + 1, :] + _dot(th, wup_ref[d]))) - 0.5
        lw_out[d] = -jnp.exp(w_log)
    a = _sigmoid(a0_ref[...] + _dot(ul.astype(BF16), aup_ref[...]))
    g_out[...] = _dot(_sigmoid(ul).astype(BF16), gup_ref[...])

    kk = k * kk_ref[...]
    nrm = jnp.sqrt(_seg_sum(kk * kk, gd_ref[...], gu_ref[...]))
    kk = kk / jnp.maximum(nrm, 1e-12)
    k = k * (1.0 + (a - 1.0) * ka_ref[...])
    r_out[...] = r
    k_out[...] = k
    v_out[...] = v
    a_out[...] = -kk
    b_out[...] = kk * a
    bonus_out[...] = _seg_sum(r * k * rk_ref[...], gd_ref[...], gu_ref[...]) * v


def _seq_halo(u, tm, nb):
    rows = u.shape[0]
    nblk = rows // tm
    starts = np.arange(nblk) * tm
    seq_len = np.where(starts < nb * T, T, C)
    seq_off = np.where(starts < nb * T, starts % T, (starts - nb * T) % C)
    has_prev = seq_off > 0
    has_next = seq_off + tm < seq_len
    prev_idx = np.where(has_prev, starts - 1, 0)
    next_idx = np.where(has_next, starts + tm, 0)
    up = jnp.where(has_prev[:, None], u[prev_idx], 0.0)
    un = jnp.where(has_next[:, None], u[next_idx], 0.0)
    return up[:, None, :], un[:, None, :]


def _rwkv_prepare(u, nb, mu, w0, w_up, a0, a_up, g_up, k_k, k_a, r_k):
    rows = u.shape[0]
    tm = 256
    nblk = rows // tm
    up, un = _seq_halo(u, tm, nb)
    pad = jnp.zeros((LORA,), F32)
    mu_rkv = mu[:3 * RW].reshape(1, 3 * RW)
    mu_l = jnp.concatenate([mu[3 * RW:], pad]).reshape(1, 4 * LORA)

    def lora_w(w, slot):
        z = jnp.zeros((4 * LORA, RW), F32)
        return z.at[slot * LORA:(slot + 1) * LORA].set(w).astype(BF16)

    wup = jnp.stack([lora_w(w_up[0], 0), lora_w(w_up[1], 0)])
    aup = lora_w(a_up, 1)
    gup = lora_w(g_up, 2)
    head = np.arange(RW) // HD
    g_down = jnp.asarray(head[:, None] == np.arange(128)[None, :], BF16)
    g_upm = jnp.asarray(np.arange(128)[:, None] == head[None, :], BF16)

    row = lambda w: pl.BlockSpec((tm, w), lambda i: (i, 0))
    vec = lambda w: pl.BlockSpec((1, w), lambda i: (0, 0))
    full = lambda *s: pl.BlockSpec(s, lambda i: (0,) * len(s))
    out_sd = jax.ShapeDtypeStruct((rows, RW), F32)
    outs = pl.pallas_call(
        _rwkv_prep_kernel,
        grid=(nblk,),
        in_specs=[pl.BlockSpec((tm, 3 * RW), lambda i: (i, 0)),
                  pl.BlockSpec((tm, 4 * LORA), lambda i: (i, AB_LORA // (4 * LORA))),
                  pl.BlockSpec((None, 1, 3 * RW), lambda i: (i, 0, 0)),
                  pl.BlockSpec((None, 1, 3 * RW), lambda i: (i, 0, 0)),
                  pl.BlockSpec((None, 1, 4 * LORA), lambda i: (i, 0, AB_LORA // (4 * LORA))),
                  pl.BlockSpec((None, 1, 4 * LORA), lambda i: (i, 0, AB_LORA // (4 * LORA))),
                  vec(3 * RW), vec(4 * LORA), full(2, RW), full(2, 4 * LORA, RW),
                  vec(RW), full(4 * LORA, RW), full(4 * LORA, RW), vec(RW), vec(RW), vec(RW),
                  full(RW, 128), full(128, RW)],
        out_specs=[row(RW), row(RW), row(RW), row(RW), row(RW),
                   pl.BlockSpec((2, tm, RW), lambda i: (0, i, 0)), row(RW), row(RW)],
        out_shape=[out_sd, out_sd, out_sd, out_sd, out_sd,
                   jax.ShapeDtypeStruct((2, rows, RW), F32), out_sd, out_sd],
        compiler_params=_params(("parallel",), V7X_VMEM_LIMIT),
        name="rwkv_prepare",
    )(u, u, up, un, up, un, mu_rkv, mu_l, w0, wup, a0.reshape(1, RW), aup, gup,
      k_k.reshape(1, RW), k_a.reshape(1, RW), r_k.reshape(1, RW), g_down, g_upm)
    return outs


def _chunk_block(d, b, s, nb):
    nctx, nlat = C // CHUNK, T // CHUNK
    pos_c = jnp.where(d == 0, s, nctx - 1 - s)
    pos_l = jnp.where(d == 0, s - nctx, nlat - 1 - (s - nctx))
    return jnp.where(s < nctx, nb * nlat + b * nctx + pos_c, b * nlat + pos_l)


def _lane_stack(x, hw):
    head = lax.broadcasted_iota(jnp.int32, x.shape, 1) // hw
    return jnp.concatenate([jnp.where(head == h, x, 0.0) for h in range(GROUP // hw)], axis=0)


def _fold_rows(x, n):
    out = x[0:n]
    for h in range(1, x.shape[0] // n):
        out = out + x[h * n:(h + 1) * n]
    return out


def _block_masks(size, blk, d):
    row = lax.broadcasted_iota(jnp.int32, (size, size), 0)
    col = lax.broadcasted_iota(jnp.int32, (size, size), 1)
    rr, cc = row % blk, col % blk
    same = (row // blk) == (col // blk)
    if d == 0:
        return cc <= rr, cc < rr, same
    return cc >= rr, cc > rr, same


def _rwkv_scan_kernel(*refs):
    in_refs = (refs[0:6], refs[6:12])
    y_refs = refs[12:14]
    s_ref = refs[14]

    @pl.when(pl.program_id(1) == 0)
    def _():
        s_ref[...] = jnp.zeros_like(s_ref)

    n = CHUNK
    ngrp = RW // GROUP
    nch = 2 * ngrp
    rrow = lax.broadcasted_iota(jnp.int32, (n, GROUP), 0)
    rcol = lax.broadcasted_iota(jnp.int32, (n, GROUP), 1) % n
    eye_row = jnp.where(rrow == rcol, 1.0, 0.0)
    incl_row = (rcol <= rrow, rcol >= rrow)
    strict_row = (rcol < rrow, rcol > rrow)
    same = _block_masks(GROUP, n, 0)[2]

    def block_diag(x_row):
        xb = x_row.astype(BF16)
        return jnp.where(same, jnp.concatenate([xb] * (GROUP // n), axis=0), jnp.zeros((), BF16))

    def stack(x):
        return _lane_stack(x.astype(BF16), HD)

    pre = []
    for d in range(2):
        r, k, v, a, b, lw = [ref[...] for ref in in_refs[d]]
        tri = jnp.where(_block_masks(n, n, d)[0], 1.0, 0.0).astype(BF16)
        c = _tri_dot(tri, lw)
        c_last = c[n - 1:n] if d == 0 else c[0:1]
        p_inv = jnp.exp(-c)
        p_end = jnp.exp(c_last - c)
        pre.append(dict(at=a * jnp.exp(c - lw), rt=r * jnp.exp(c), bt=b * p_inv, kt=k * p_inv,
                        bp=b * p_end, kp=k * p_end, v=v, dec=jnp.exp(c_last)))
    chains = [(d, g) for d in range(2) for g in range(ngrp)]
    sl = lambda g: slice(g * GROUP, (g + 1) * GROUP)
    part = lambda d, g, name: pre[d][name][:, sl(g)]

    a_st = [stack(part(d, g, "at")) for d, g in chains]
    v_st = [stack(part(d, g, "v")) for d, g in chains]
    gram = [_dot_nt(jnp.concatenate([part(d, g, "at"), part(d, g, "rt")], axis=0).astype(BF16),
                    jnp.concatenate([stack(part(d, g, "bt")), stack(part(d, g, "kt"))], axis=0))
            for d, g in chains]
    a_ab = [jnp.where(strict_row[d], gram[i][0:n, 0:GROUP], 0.0) for i, (d, g) in enumerate(chains)]
    a_ak = [jnp.where(strict_row[d], gram[i][0:n, GROUP:], 0.0).astype(BF16) for i, (d, g) in enumerate(chains)]
    a_r = [jnp.where(jnp.concatenate([incl_row[d]] * 2, axis=1), gram[i][n:2 * n, :], 0.0).astype(BF16)
           for i, (d, g) in enumerate(chains)]
    av = [_dot(a_ak[i], v_st[i]) for i in range(nch)]
    tm = [eye_row + m for m in a_ab]
    pw = [_dot(m.astype(BF16), block_diag(m)) for m in a_ab]
    for _ in range(4):
        both = [_dot(jnp.concatenate([t, p], axis=0).astype(BF16), block_diag(p)) for t, p in zip(tm, pw)]
        tm = [t + x[0:n] for t, x in zip(tm, both)]
        pw = [x[n:2 * n] for x in both]
    tm = [t + _dot(t.astype(BF16), block_diag(p)) for t, p in zip(tm, pw)]
    tx = [_dot(tm[i].astype(BF16), jnp.concatenate([a_st[i], stack(av[i])], axis=1))
          for i in range(nch)]
    st = [s_ref[d, g] for d, g in chains]
    uy0 = [_dot_nt(jnp.concatenate([tx[i][:, 0:GROUP], part(d, g, "rt")], axis=0).astype(BF16),
                   st[i].astype(BF16)) for i, (d, g) in enumerate(chains)]
    u = [uy0[i][0:n] + tx[i][:, GROUP:] for i in range(nch)]
    for i, (d, g) in enumerate(chains):
        y_refs[d][:, sl(g)] = uy0[i][n:2 * n] + _dot(a_r[i], jnp.concatenate([stack(u[i]), v_st[i]], axis=0))
    for i, (d, g) in enumerate(chains):
        upd = _dot_tn(jnp.concatenate([u[i], part(d, g, "v")], axis=0).astype(BF16),
                      jnp.concatenate([part(d, g, "bp"), part(d, g, "kp")], axis=0).astype(BF16))
        s_ref[d, g] = st[i] * part(d, g, "dec") + jnp.where(same, upd, 0.0)


def _rwkv_scan(r, k, v, a, b, lw, nb):
    rows = r.shape[0]
    steps = (T + C) // CHUNK
    in_specs, args = [], []
    for d in range(2):
        blk = lambda bb, s, d=d: (_chunk_block(d, bb, s, nb), 0)
        in_specs += [pl.BlockSpec((CHUNK, RW), blk)] * 5
        in_specs.append(pl.BlockSpec((None, CHUNK, RW), lambda bb, s, d=d: (d, _chunk_block(d, bb, s, nb), 0)))
        args += [r, k, v, a, b, lw]
    out_sd = jax.ShapeDtypeStruct((rows, RW), F32)
    return pl.pallas_call(
        _rwkv_scan_kernel,
        grid=(nb, steps),
        in_specs=in_specs,
        out_specs=[pl.BlockSpec((CHUNK, RW), lambda bb, s, d=d: (_chunk_block(d, bb, s, nb), 0)) for d in range(2)],
        out_shape=[out_sd, out_sd],
        scratch_shapes=[pltpu.VMEM((2, RW // GROUP, GROUP, GROUP), F32)],
        compiler_params=_params(("parallel", "arbitrary"), V7X_VMEM_LIMIT),
        name="rwkv_scan",
    )(*args)


def _rwkv_post_kernel(y0_ref, y1_ref, bonus_ref, g_ref, lnw_ref, lnb_ref, gd_ref, gu_ref, o_ref):
    y = y0_ref[...] + y1_ref[...]
    mu = _seg_sum(y, gd_ref[...], gu_ref[...]) * (1.0 / HD)
    yc = y - mu
    var = _seg_sum(yc * yc, gd_ref[...], gu_ref[...]) * (1.0 / HD)
    yn = yc * lax.rsqrt(var + RWKV_GN_EPS)
    o_ref[...] = (yn * lnw_ref[...] + lnb_ref[...] + bonus_ref[...]) * g_ref[...]


def _rwkv_post(y0, y1, bonus, g, ln_w, ln_b):
    rows = bonus.shape[0]
    tm = 512
    head = np.arange(RW) // HD
    g_down = jnp.asarray(head[:, None] == np.arange(128)[None, :], BF16)
    g_upm = jnp.asarray(np.arange(128)[:, None] == head[None, :], BF16)
    row = pl.BlockSpec((tm, RW), lambda i: (i, 0))
    vec = pl.BlockSpec((1, RW), lambda i: (0, 0))
    return pl.pallas_call(
        _rwkv_post_kernel,
        grid=(rows // tm,),
        in_specs=[row, row, row, row, vec, vec,
                  pl.BlockSpec((RW, 128), lambda i: (0, 0)), pl.BlockSpec((128, RW), lambda i: (0, 0))],
        out_specs=row,
        out_shape=jax.ShapeDtypeStruct((rows, RW), F32),
        compiler_params=_params(("parallel",), V7X_VMEM_LIMIT),
        name="rwkv_post",
    )(y0, y1, bonus, g, ln_w.reshape(1, RW), ln_b.reshape(1, RW), g_down, g_upm)


def _kv_block(i, nb):
    per = T // 256
    return jnp.where(i < nb * per, i // per, i - nb * per), jnp.where(i < nb * per, i % per, per)


def _swa_prep_kernel(k_ref, v_ref, cos_ref, sin_ref, ko_ref, vo_ref):
    cosl, sinl = cos_ref[...], sin_ref[...]
    lane = lax.broadcasted_iota(jnp.int32, cosl.shape, 1)
    low = lane < HD
    for src, dst, rope in ((k_ref, ko_ref, True), (v_ref, vo_ref, False)):
        for t in range(2):
            x = src[:, t * 128:(t + 1) * 128]
            if rope:
                x = _rope(x, cosl, sinl)
            swapped = pltpu.roll(x, HD, axis=1)
            dst[:, (2 * t) * 128:(2 * t + 1) * 128] = jnp.where(low, x, swapped).astype(BF16)
            dst[:, (2 * t + 1) * 128:(2 * t + 2) * 128] = jnp.where(low, swapped, x).astype(BF16)


def _swa_prep(u, cosl, sinl, nb):
    rows = u.shape[0]
    tm = 256
    kv_out = pl.BlockSpec((None, tm, 512), lambda i: (*_kv_block(i, nb), 0))
    sd = jax.ShapeDtypeStruct((nb, T + C, 512), BF16)
    return pl.pallas_call(
        _swa_prep_kernel,
        grid=(rows // tm,),
        in_specs=[pl.BlockSpec((tm, 256), lambda i: (i, AB_KS // 256)),
                  pl.BlockSpec((tm, 256), lambda i: (i, AB_VS // 256)),
                  pl.BlockSpec((tm, 128), lambda i: (i, 0)),
                  pl.BlockSpec((tm, 128), lambda i: (i, 0))],
        out_specs=[kv_out, kv_out],
        out_shape=[sd, sd],
        compiler_params=_params(("parallel",), V7X_VMEM_LIMIT),
        name="swa_prep",
    )(u, u, cosl, sinl)


def _diff_prep_kernel(k_ref, v_ref, cos_ref, sin_ref, ko_ref, vo_ref):
    cosl, sinl = cos_ref[...], sin_ref[...]
    for t in range(8):
        sl = slice(t * 128, (t + 1) * 128)
        ko_ref[:, sl] = _rope(k_ref[:, sl], cosl, sinl).astype(BF16)
    vo_ref[...] = v_ref[...].astype(BF16)


def _diff_prep(u, cosl, sinl, nb):
    rows = u.shape[0]
    tm = 256
    kv_out = pl.BlockSpec((None, tm, 1024), lambda i: (*_kv_block(i, nb), 0))
    sd = jax.ShapeDtypeStruct((nb, T + C, 1024), BF16)
    return pl.pallas_call(
        _diff_prep_kernel,
        grid=(rows // tm,),
        in_specs=[pl.BlockSpec((tm, 1024), lambda i: (i, 1)),
                  pl.BlockSpec((tm, 1024), lambda i: (i, 2)),
                  pl.BlockSpec((tm, 128), lambda i: (i, 0)),
                  pl.BlockSpec((tm, 128), lambda i: (i, 0))],
        out_specs=[kv_out, kv_out],
        out_shape=[sd, sd],
        compiler_params=_params(("parallel",), V7X_VMEM_LIMIT),
        name="diff_prep",
    )(u, u, cosl, sinl)


def _swa_kernel(q_ref, cos_ref, sin_ref, k_ref, v_ref, sink_ref, o_ref, *, ctx_mode):
    n = pl.program_id(1)
    cosl, sinl = cos_ref[...], sin_ref[...]
    if not ctx_mode:
        start = pl.multiple_of(jnp.clip((n - 1) * QB, 0, T - 3 * QB), QB)
        kpos = start + lax.broadcasted_iota(jnp.int32, (2 * QB, 3 * QB), 1)
        qpos = n * QB + lax.broadcasted_iota(jnp.int32, (2 * QB, 3 * QB), 0) % QB
        valid = jnp.abs(kpos - qpos) <= SWA_WINDOW
    low = lax.broadcasted_iota(jnp.int32, (QB, 128), 1) < HD
    top = lax.broadcasted_iota(jnp.int32, (2 * QB, 1), 0) < QB
    tile = lambda j: slice((j // 2) * 128, (j // 2 + 1) * 128)
    qs, sink = [], []
    for j in range(8):
        q = _rope(q_ref[:, j * 128:(j + 1) * 128], cosl, sinl) * (HD ** -0.5 * LOG2E)
        qs.append(jnp.concatenate([jnp.where(low, q, 0.0), jnp.where(low, 0.0, q)], axis=0).astype(BF16))
        sink.append(jnp.where(top, sink_ref[2 * j:2 * j + 1, 0:1], sink_ref[2 * j + 1:2 * j + 2, 0:1]) * LOG2E)
    s_c = [_dot_nt(qs[j], k_ref[T:T + C, tile(j)]) for j in range(8)]
    m = [jnp.maximum(jnp.max(s, axis=-1, keepdims=True), sk) for s, sk in zip(s_c, sink)]
    if not ctx_mode:
        s_w = [jnp.where(valid, _dot_nt(qs[j], k_ref[pl.ds(start, 3 * QB), tile(j)]), -jnp.inf) for j in range(8)]
        m = [jnp.maximum(mm, jnp.max(s, axis=-1, keepdims=True)) for mm, s in zip(m, s_w)]
    p_c = [jnp.exp2(s - mm) for s, mm in zip(s_c, m)]
    den = [jnp.sum(p, axis=-1, keepdims=True) + jnp.exp2(sk - mm) for p, sk, mm in zip(p_c, sink, m)]
    pv = [_dot(p.astype(BF16), v_ref[T:T + C, tile(j)]) for j, p in enumerate(p_c)]
    if not ctx_mode:
        p_w = [jnp.exp2(s - mm) for s, mm in zip(s_w, m)]
        den = [dd + jnp.sum(p, axis=-1, keepdims=True) for dd, p in zip(den, p_w)]
        pv = [x + _dot(p.astype(BF16), v_ref[pl.ds(start, 3 * QB), tile(j)]) for j, (x, p) in enumerate(zip(pv, p_w))]
    for j in range(8):
        o = pv[j] / den[j]
        o_ref[:, j * 128:(j + 1) * 128] = jnp.where(low, o[0:QB], o[QB:2 * QB])


def _swa(u, cosl, sinl, kpad, vpad, sink, nb, ctx_mode):
    nq = (C if ctx_mode else T) // QB
    base = nb * (T // QB) if ctx_mode else 0
    rows = nb * nq * QB
    rowblk = lambda b, n: base + b * nq + n
    kv = pl.BlockSpec((None, T + C, 512), lambda b, n: (b, 0, 0))
    return pl.pallas_call(
        functools.partial(_swa_kernel, ctx_mode=ctx_mode),
        grid=(nb, nq),
        in_specs=[pl.BlockSpec((QB, 1024), lambda b, n: (rowblk(b, n), AB_Q // 1024)),
                  pl.BlockSpec((QB, 128), lambda b, n: (rowblk(b, n), 0)),
                  pl.BlockSpec((QB, 128), lambda b, n: (rowblk(b, n), 0)),
                  kv, kv, pl.BlockSpec((16, 128), lambda b, n: (0, 0))],
        out_specs=pl.BlockSpec((QB, 1024), lambda b, n: (b * nq + n, 0)),
        out_shape=jax.ShapeDtypeStruct((rows, 1024), F32),
        compiler_params=_params(("parallel", "arbitrary"), V7X_VMEM_LIMIT),
        name="swa_ctx" if ctx_mode else "swa_latent",
    )(u, cosl, sinl, kpad, vpad, sink)


def _diff_kernel(q_ref, cos_ref, sin_ref, k_ref, v_ref, lam_ref, sub_ref, o_ref):
    cosl, sinl = cos_ref[...], sin_ref[...]
    lam = lam_ref[...]
    low = lax.broadcasted_iota(jnp.int32, (QB, 128), 1) < HD
    for h in range(8):
        sl = slice(h * 128, (h + 1) * 128)
        q = _rope(q_ref[:, sl], cosl, sinl) * (HD ** -0.5 * LOG2E)
        qs = jnp.concatenate([jnp.where(low, q, 0.0), jnp.where(low, 0.0, q)], axis=0).astype(BF16)
        s = _dot_nt(qs, k_ref[:, sl])
        e = jnp.exp2(s - jnp.max(s, axis=-1, keepdims=True))
        inv = 1.0 / jnp.sum(e, axis=-1, keepdims=True)
        o2 = _dot(e.astype(BF16), v_ref[:, sl])
        o = o2[0:QB] * inv[0:QB] - o2[QB:2 * QB] * (lam[:, 0:1] * inv[QB:2 * QB])
        o = o * lax.rsqrt(jnp.mean(o * o, axis=-1, keepdims=True) + 1e-5)
        o_ref[:, sl] = o * sub_ref[...]


def _diff_attn(u, cosl, sinl, kb, vb, lam, sub, nb):
    nq = T // QB
    kv = pl.BlockSpec((None, T + C, 1024), lambda b, n: (b, 0, 0))
    vec = pl.BlockSpec((1, 128), lambda b, n: (0, 0))
    return pl.pallas_call(
        _diff_kernel,
        grid=(nb, nq),
        in_specs=[pl.BlockSpec((QB, 1024), lambda b, n: (b * nq + n, 0)),
                  pl.BlockSpec((QB, 128), lambda b, n: (b * nq + n, 0)),
                  pl.BlockSpec((QB, 128), lambda b, n: (b * nq + n, 0)),
                  kv, kv, vec, vec],
        out_specs=pl.BlockSpec((QB, 1024), lambda b, n: (b * nq + n, 0)),
        out_shape=jax.ShapeDtypeStruct((nb * T, 1024), F32),
        compiler_params=_params(("parallel", "arbitrary"), V7X_VMEM_LIMIT),
        name="diff_attn",
    )(u, cosl, sinl, kb, vb, lam, sub)


def _hgrn_exact_att(q, kk, bcum, tmp_ref):
    n = CHUNK
    tmp_ref[0] = bcum
    tmp_ref[1] = kk
    coli = lax.broadcasted_iota(jnp.int32, (n, 2 * n), 1)

    def body(s, acc):
        bs = tmp_ref[0, pl.ds(s, 1), :]
        ks = tmp_ref[1, pl.ds(s, 1), :]
        w = q * jnp.exp(jnp.minimum(bcum - bs, 0.0)) * ks
        c0 = jnp.sum(w[:, 0:HG_DK], axis=-1, keepdims=True)
        c1 = jnp.sum(w[:, HG_DK:], axis=-1, keepdims=True)
        return acc + jnp.concatenate([jnp.where(coli == s, c0, 0.0), jnp.where(coli == s + n, c1, 0.0)], axis=0)

    return lax.fori_loop(0, n, body, jnp.zeros((2 * n, 2 * n), F32))


def _hgrn_scan_kernel(q0_ref, z0_ref, i0_ref, lb0_ref, q1_ref, z1_ref, i1_ref, lb1_ref,
                      o0_ref, o1_ref, s_ref, g_ref, tmp_ref):
    in_refs = ((q0_ref, z0_ref, i0_ref, lb0_ref), (q1_ref, z1_ref, i1_ref, lb1_ref))
    o_refs = (o0_ref, o1_ref)

    @pl.when(pl.program_id(1) == 0)
    def _():
        s_ref[...] = jnp.zeros_like(s_ref)

    n = CHUNK
    ngrp = 1024 // GROUP
    nsub = n // SUB
    pre = []
    for d in range(2):
        q, z, v, lb = [ref[...] for ref in in_refs[d]]
        logf = jnp.log(lb + (1.0 - lb) * _sigmoid(z))
        kk = (1.0 - lb) * _sigmoid(-z)
        tri = jnp.where(_block_masks(n, n, d)[0], 1.0, 0.0).astype(BF16)
        bcum = _tri_dot(tri, logf)
        b_last = bcum[n - 1:n] if d == 0 else bcum[0:1]
        b_excl = bcum - logf
        qh, kh = [], []
        for sb in range(nsub):
            lo, hi = sb * SUB, (sb + 1) * SUB
            beta = b_excl[lo:lo + 1] if d == 0 else b_excl[hi - 1:hi]
            qh.append(q[lo:hi] * jnp.exp(bcum[lo:hi] - beta))
            kh.append(kk * jnp.exp(beta - bcum))
        pre.append(dict(q=q, v=v, kk=kk, bcum=bcum, qh=qh, kh=kh, qe=q * jnp.exp(bcum),
                        ke=kk * jnp.exp(b_last - bcum), dec=jnp.exp(b_last), min_logf=jnp.min(logf)))
    chains = [(d, g) for d in range(2) for g in range(ngrp)]
    sl = lambda g: slice(g * GROUP, (g + 1) * GROUP)

    for i, (d, g) in enumerate(chains):
        rows = [_dot_nt(_lane_stack(pre[d]["qh"][sb][:, sl(g)], HG_DK).astype(BF16),
                        _lane_stack(pre[d]["kh"][sb][:, sl(g)], HG_DK).astype(BF16)) for sb in range(nsub)]
        g_ref[i] = jnp.concatenate([rows[sb][h * SUB:(h + 1) * SUB] for h in range(2) for sb in range(nsub)], axis=0)

    @pl.when(jnp.minimum(pre[0]["min_logf"], pre[1]["min_logf"]) < -(HG_CLAMP / SUB))
    def _():
        for i, (d, g) in enumerate(chains):
            g_ref[i] = _hgrn_exact_att(pre[d]["q"][:, sl(g)], pre[d]["kk"][:, sl(g)], pre[d]["bcum"][:, sl(g)],
                                       tmp_ref)

    st = [s_ref[d, g] for d, g in chains]
    inter = [_dot_nt(pre[d]["qe"][:, sl(g)].astype(BF16), st[i].astype(BF16)) for i, (d, g) in enumerate(chains)]
    for i, (d, g) in enumerate(chains):
        incl, _, same = _block_masks(2 * n, n, d)
        att = jnp.where(incl & same, g_ref[i], 0.0).astype(BF16)
        v_st = _lane_stack(pre[d]["v"][:, sl(g)], HG_DK).astype(BF16)
        o_refs[d][:, sl(g)] = inter[i] + _fold_rows(_dot(att, v_st), n)
    for i, (d, g) in enumerate(chains):
        upd = _dot_tn(pre[d]["v"][:, sl(g)].astype(BF16), pre[d]["ke"][:, sl(g)].astype(BF16))
        s_ref[d, g] = st[i] * pre[d]["dec"][:, sl(g)] + jnp.where(_block_masks(GROUP, HG_DK, d)[2], upd, 0.0)


def _hgrn_scan(u, lb, nb):
    rows = u.shape[0]
    steps = (T + C) // CHUNK
    in_specs, args = [], []
    for d in range(2):
        col = lambda j, d=d: pl.BlockSpec((CHUNK, 1024), lambda bb, s: (_chunk_block(d, bb, s, nb), j))
        in_specs += [col(3), col(4 + d), col(6), pl.BlockSpec((None, 1, 1024), lambda bb, s, d=d: (d, 0, 0))]
        args += [u, u, u, lb]
    out_sd = jax.ShapeDtypeStruct((rows, 1024), F32)
    ngrp = 1024 // GROUP
    return pl.pallas_call(
        _hgrn_scan_kernel,
        grid=(nb, steps),
        in_specs=in_specs,
        out_specs=[pl.BlockSpec((CHUNK, 1024), lambda bb, s, d=d: (_chunk_block(d, bb, s, nb), 0)) for d in range(2)],
        out_shape=[out_sd, out_sd],
        scratch_shapes=[pltpu.VMEM((2, ngrp, GROUP, GROUP), F32),
                        pltpu.VMEM((2 * ngrp, 2 * CHUNK, 2 * CHUNK), F32),
                        pltpu.VMEM((2, CHUNK, GROUP), F32)],
        compiler_params=_params(("parallel", "arbitrary"), V7X_VMEM_LIMIT),
        name="hgrn_scan",
    )(*args)


def _hgrn_post_kernel(o0_ref, o1_ref, g_ref, gn_ref, y_ref):
    g = g_ref[...]
    for h in range(8):
        sl = slice(h * 128, (h + 1) * 128)
        o = o0_ref[:, sl] + o1_ref[:, sl]
        o = o * lax.rsqrt(jnp.mean(o * o, axis=-1, keepdims=True) + NORM_EPS) * gn_ref[...]
        gh = g[:, sl]
        y_ref[:, sl] = o * (gh * _sigmoid(gh))


def _hgrn_post(o0, o1, u, gnorm, rows):
    tm = 512
    return pl.pallas_call(
        _hgrn_post_kernel,
        grid=(rows // tm,),
        in_specs=[pl.BlockSpec((tm, 1024), lambda i: (i, 0)),
                  pl.BlockSpec((tm, 1024), lambda i: (i, 0)),
                  pl.BlockSpec((tm, 1024), lambda i: (i, 7)),
                  pl.BlockSpec((1, 128), lambda i: (0, 0))],
        out_specs=pl.BlockSpec((tm, 1024), lambda i: (i, 0)),
        out_shape=jax.ShapeDtypeStruct((rows, 1024), F32),
        compiler_params=_params(("parallel",), V7X_VMEM_LIMIT),
        name="hgrn_post",
    )(o0, o1, u, gnorm.reshape(1, 128))


def _rope_tables(nb):
    t = np.arange(T)
    quarter = HD // 4
    inv = ROPE_THETA ** (-jnp.arange(quarter, dtype=F32) / quarter)
    rows = jnp.asarray(t // GRID_W, F32)
    cols = jnp.asarray(t % GRID_W, F32)
    ang = jnp.concatenate([rows[:, None] * inv, cols[:, None] * inv], axis=-1)
    cos, sin = jnp.cos(ang), jnp.sin(ang)
    cosl = jnp.tile(jnp.concatenate([cos, cos], axis=-1), (nb, 2))
    sinl = jnp.tile(jnp.concatenate([-sin, sin], axis=-1), (nb, 2))
    cosl = jnp.concatenate([cosl, jnp.ones((nb * C, 128), F32)], axis=0)
    sinl = jnp.concatenate([sinl, jnp.zeros((nb * C, 128), F32)], axis=0)
    return cosl, sinl


def kernel(x, c, ctx, c_ctx, ada_w, ada_b, mlp_w1, mlp_w2, final_norm, ab_w_in, ab_w_out, ab_mu, ab_w0,
           ab_w_up, ab_a0, ab_a_up, ab_g_up, ab_k_k, ab_k_a, ab_r_k, ab_ln_w, ab_ln_b, ab_sink,
           cd_w_in, cd_w_out, cd_lam, cd_subln, cd_gnorm, hgrn_lb_logits):
    nb = x.shape[0]
    assert x.shape == (nb, T, D) and ctx.shape == (nb, C, D) and nb < 8
    assert ada_w.shape[0] == 2, "one AB layer followed by one CD layer"
    rx = nb * T
    xc = jnp.concatenate([x.reshape(rx, D), ctx.reshape(nb * C, D)], axis=0)
    cosl, sinl = _rope_tables(nb)

    s_in = jnp.zeros((8, D), F32).at[:nb].set(c).at[nb].set(c_ctx)
    mod = _ada(s_in, ada_w, ada_b)

    w = ab_w_in[0]
    w_in0 = jnp.concatenate([w[:, :3 * RW], w[:, 3 * RW + 3 * LORA:], w[:, 3 * RW:3 * RW + 3 * LORA],
                             jnp.zeros((D, LORA), F32)], axis=1).astype(BF16)
    u = _norm_mod_matmul(xc, mod, 0, w_in0, nb, AB_N // 2)
    r, k, v, a, b, lw, g, bonus = _rwkv_prepare(u, nb, ab_mu[0], ab_w0[0], ab_w_up[0], ab_a0[0], ab_a_up[0],
                                                ab_g_up[0], ab_k_k[0], ab_k_a[0], ab_r_k[0].reshape(RW))
    y0, y1 = _rwkv_scan(r, k, v, a, b, lw, nb)
    ya = _rwkv_post(y0, y1, bonus, g, ab_ln_w[0], ab_ln_b[0])
    kpad, vpad = _swa_prep(u, cosl, sinl, nb)
    sink = jnp.broadcast_to(ab_sink[0][:, None], (16, 128))
    yb = jnp.concatenate([_swa(u, cosl, sinl, kpad, vpad, sink, nb, False),
                          _swa(u, cosl, sinl, kpad, vpad, sink, nb, True)], axis=0)
    w_out0 = ab_w_out[0].astype(BF16)
    xc = _out_proj(ya, yb, xc, xc.shape[0], mod, 0, w_out0[:RW], w_out0[RW:], nb)
    fn = final_norm.reshape(1, D)
    xc = _mlp(xc, xc.shape[0], mod, 0, mlp_w1[0].astype(BF16), mlp_w2[0].astype(BF16), fn, nb, False)

    lam_init = 0.8 - 0.6 * math.exp(-0.3 * 1)
    lb_table = jnp.cumsum(jax.nn.softmax(hgrn_lb_logits.astype(F32), axis=0), axis=0)
    lb = (lb_table - lb_table[0])[1].reshape(2, 1, 1024)
    lf = cd_lam[0].astype(F32)
    lmb = jnp.exp(jnp.sum(lf[0] * lf[1])) - jnp.exp(jnp.sum(lf[2] * lf[3])) + lam_init
    u = _norm_mod_matmul(xc, mod, 1, cd_w_in[0].astype(BF16), nb, 2048)
    kb, vb = _diff_prep(u, cosl, sinl, nb)
    yc = _diff_attn(u, cosl, sinl, kb, vb, jnp.full((1, 128), lmb, F32),
                    (cd_subln[0] * (1.0 - lam_init)).reshape(1, 128), nb)
    o0, o1 = _hgrn_scan(u, lb, nb)
    yd = _hgrn_post(o0, o1, u, cd_gnorm[0], rx)
    w_out1 = cd_w_out[0].astype(BF16)
    xl = _out_proj(yc, yd, xc, rx, mod, 1, w_out1[:1024], w_out1[1024:], nb)
    out = _mlp(xl, rx, mod, 1, mlp_w1[1].astype(BF16), mlp_w2[1].astype(BF16), fn, nb, True)
    return out.reshape(nb, T, D)
```

```python
import functools
import math

import jax
import jax.numpy as jnp
import numpy as np
from jax import lax
from jax.experimental import pallas as pl
from jax.experimental.pallas import tpu as pltpu

F32 = jnp.float32
BF16 = jnp.bfloat16

D = 2048
T = 2048
C = 256
GRID_W = 64
D_FF = 4 * D
HD = 64
ROPE_THETA = 10000.0
LOG2E = 1.4426950408889634
NORM_EPS = 1e-6
N_MOD = 6
RW = 1024
RWKV_GN_EPS = 64e-5
LORA = 64
SWA_WINDOW = 128
QB = 128
DIFF_QB = 256
CHUNK = 64
SUB = 16
HG_DK = 128
HG_CLAMP = 80.0
GROUP = 256
V7X_VMEM_LIMIT = 56 * 1024 * 1024

AB_Q, AB_KS, AB_VS, AB_LORA, AB_N = 3072, 4096, 4352, 4608, 4864


def _dot(a, b):
    return jnp.dot(a, b, preferred_element_type=F32)


def _dot_nt(a, b):
    return lax.dot_general(a, b, (((1,), (1,)), ((), ())), preferred_element_type=F32)


def _dot_tn(a, b):
    return lax.dot_general(a, b, (((0,), (0,)), ((), ())), preferred_element_type=F32)


def _split2(x):
    hi = x.astype(BF16)
    lo = (x - hi.astype(F32)).astype(BF16)
    return hi, lo


def _split3(x):
    hi = x.astype(BF16)
    r1 = x - hi.astype(F32)
    mid = r1.astype(BF16)
    lo = (r1 - mid.astype(F32)).astype(BF16)
    return hi, mid, lo


def _dot_exact_rhs(x, g):
    hi, lo = _split2(x)
    return _dot(hi, g) + _dot(lo, g)


def _tri_dot(tri, x):
    hi, mid, lo = _split3(x)
    return _dot(tri, hi) + _dot(tri, mid) + _dot(tri, lo)


def _seg_sum(x, g_down, g_up):
    return _dot_exact_rhs(_dot_exact_rhs(x, g_down), g_up)


def _rope(x, cosl, sinl):
    w = x.shape[-1]
    lane = lax.broadcasted_iota(jnp.int32, x.shape, x.ndim - 1)
    first = (lane & 63) < 32
    swapped = jnp.where(first, pltpu.roll(x, w - 32, axis=1), pltpu.roll(x, 32, axis=1))
    return x * cosl + swapped * sinl


def _sigmoid(x):
    return 1.0 / (1.0 + jnp.exp(-x))


def _softplus(x):
    return jnp.maximum(x, 0.0) + jnp.log(1.0 + jnp.exp(-jnp.abs(x)))


def _params(sem, vmem=None):
    return pltpu.CompilerParams(dimension_semantics=sem, vmem_limit_bytes=vmem)


def _ada_kernel(s_ref, w_ref, b_ref, o_ref):
    s = s_ref[...]
    s = s * _sigmoid(s)
    o_ref[...] = _dot(s.astype(BF16), w_ref[...].astype(BF16)) + b_ref[...]


def _ada(s_in, ada_w, ada_b):
    depth = ada_w.shape[0]
    n = ada_w.shape[2]
    tn = 1536
    out = pl.pallas_call(
        _ada_kernel,
        grid=(depth, n // tn),
        in_specs=[pl.BlockSpec((8, D), lambda l, j: (0, 0)),
                  pl.BlockSpec((None, D, tn), lambda l, j: (l, 0, j)),
                  pl.BlockSpec((None, 1, tn), lambda l, j: (l, 0, j))],
        out_specs=pl.BlockSpec((None, 8, tn), lambda l, j: (l, 0, j)),
        out_shape=jax.ShapeDtypeStruct((depth, 8, n), F32),
        compiler_params=_params(("arbitrary", "arbitrary"), V7X_VMEM_LIMIT),
        name="ada_mod",
    )(s_in, ada_w, ada_b.reshape(depth, 1, n))
    return out.reshape(depth, 8, 1, n)


def _mod_spec(layer, k, tm, nb):
    return pl.BlockSpec((None, None, 1, D),
                        lambda i, *_: (layer, jnp.minimum((i * tm) // T, nb), 0, k))


def _nmm_kernel(x_ref, sh_ref, sc_ref, w_ref, o_ref, lhs_ref):
    @pl.when(pl.program_id(1) == 0)
    def _():
        x = x_ref[...]
        xn = x * lax.rsqrt(jnp.mean(x * x, axis=-1, keepdims=True) + NORM_EPS)
        lhs_ref[...] = (xn * (1.0 + sc_ref[...]) + sh_ref[...]).astype(BF16)

    o_ref[...] = _dot(lhs_ref[...], w_ref[...].astype(BF16)).astype(o_ref.dtype)


def _norm_mod_matmul(xc, mod, layer, w, nb, tm, tn):
    rows = xc.shape[0]
    n = w.shape[1]
    return pl.pallas_call(
        _nmm_kernel,
        grid=(rows // tm, n // tn),
        in_specs=[pl.BlockSpec((tm, D), lambda i, j: (i, 0), pipeline_mode=pl.Buffered(1)),
                  _mod_spec(layer, 0, tm, nb), _mod_spec(layer, 1, tm, nb),
                  pl.BlockSpec((D, tn), lambda i, j: (0, j))],
        out_specs=pl.BlockSpec((tm, tn), lambda i, j: (i, j)),
        out_shape=jax.ShapeDtypeStruct((rows, n), F32),
        scratch_shapes=[pltpu.VMEM((tm, D), BF16)],
        compiler_params=_params(("parallel", "arbitrary"), V7X_VMEM_LIMIT),
        name=f"in_proj_{layer}",
    )(xc, mod, mod, w)


def _mlp_kernel(x_ref, sh_ref, sc_ref, gt_ref, w1_ref, w2_ref, fn_ref, o_ref, lhs_ref, *, final):
    f = pl.program_id(1)

    @pl.when(f == 0)
    def _():
        x = x_ref[...]
        xn = x * lax.rsqrt(jnp.mean(x * x, axis=-1, keepdims=True) + NORM_EPS)
        lhs_ref[...] = (xn * (1.0 + sc_ref[...]) + sh_ref[...]).astype(BF16)
        o_ref[...] = jnp.zeros_like(o_ref)

    h = jnp.maximum(_dot(lhs_ref[...], w1_ref[...].astype(BF16)), 0.0)
    o_ref[...] += _dot((h * h).astype(BF16), w2_ref[...].astype(BF16))

    @pl.when(f == pl.num_programs(1) - 1)
    def _():
        y = x_ref[...] + gt_ref[...] * o_ref[...]
        if final:
            y = y * lax.rsqrt(jnp.mean(y * y, axis=-1, keepdims=True) + NORM_EPS) * fn_ref[...]
        o_ref[...] = y


def _mlp(xc, rows, mod, layer, w1, w2, final_norm, nb, final):
    tm, tf = 1024, 512
    return pl.pallas_call(
        functools.partial(_mlp_kernel, final=final),
        grid=(rows // tm, D_FF // tf),
        in_specs=[pl.BlockSpec((tm, D), lambda i, f: (i, 0), pipeline_mode=pl.Buffered(1)),
                  _mod_spec(layer, 3, tm, nb), _mod_spec(layer, 4, tm, nb), _mod_spec(layer, 5, tm, nb),
                  pl.BlockSpec((None, D, tf), lambda i, f: (layer, 0, f)),
                  pl.BlockSpec((None, tf, D), lambda i, f: (layer, f, 0)),
                  pl.BlockSpec((1, D), lambda i, f: (0, 0))],
        out_specs=pl.BlockSpec((tm, D), lambda i, f: (i, 0)),
        out_shape=jax.ShapeDtypeStruct((rows, D), F32),
        scratch_shapes=[pltpu.VMEM((tm, D), BF16)],
        compiler_params=_params(("parallel", "arbitrary"), V7X_VMEM_LIMIT),
        name=f"mlp_{layer}",
    )(xc, mod, mod, mod, w1, w2, final_norm)


def _oproj_kernel(ya_ref, yb_ref, x_ref, gt_ref, wa_ref, wb_ref, o_ref):
    y = _dot(ya_ref[...].astype(BF16), wa_ref[...]) + _dot(yb_ref[...].astype(BF16), wb_ref[...])
    o_ref[...] = x_ref[...] + gt_ref[...] * y


def _out_proj(ya, yb, xc, rows, mod, layer, wa, wb, nb):
    tm = 512
    half = wa.shape[0]
    return pl.pallas_call(
        _oproj_kernel,
        grid=(rows // tm,),
        in_specs=[pl.BlockSpec((tm, half), lambda i: (i, 0)),
                  pl.BlockSpec((tm, half), lambda i: (i, 0)),
                  pl.BlockSpec((tm, D), lambda i: (i, 0)),
                  _mod_spec(layer, 2, tm, nb),
                  pl.BlockSpec((half, D), lambda i: (0, 0)),
                  pl.BlockSpec((half, D), lambda i: (0, 0))],
        out_specs=pl.BlockSpec((tm, D), lambda i: (i, 0)),
        out_shape=jax.ShapeDtypeStruct((rows, D), F32),
        compiler_params=_params(("parallel",), V7X_VMEM_LIMIT),
        name=f"out_proj_{layer}",
    )(ya, yb, xc, mod, wa, wb)


def _shifted(u, prev_row, next_row):
    tm = u.shape[0]
    row = lax.broadcasted_iota(jnp.int32, u.shape, 0)
    up = jnp.where(row == 0, prev_row, pltpu.roll(u, 1, axis=0))
    un = jnp.where(row == tm - 1, next_row, pltpu.roll(u, tm - 1, axis=0))
    return 0.5 * (up + un)


def _rwkv_prep_kernel(u_ref, ul_ref, up_ref, un_ref, ulp_ref, uln_ref, mu_ref, mul_ref, w0_ref, wup_ref,
                      a0_ref, aup_ref, gup_ref, kk_ref, ka_ref, rk_ref, gd_ref, gu_ref,
                      r_out, k_out, v_out, a_out, b_out, lw_out, g_out, bonus_out):
    u = u_ref[...]
    u = u + mu_ref[...] * (_shifted(u, up_ref[...], un_ref[...]) - u)
    ul = ul_ref[...]
    ul = ul + mul_ref[...] * (_shifted(ul, ulp_ref[...], uln_ref[...]) - ul)
    r, k, v = u[:, 0:RW], u[:, RW:2 * RW], u[:, 2 * RW:3 * RW]

    th = jnp.tanh(ul).astype(BF16)
    for d in range(2):
        w_log = -_softplus(-(w0_ref[d:d + 1, :] + _dot(th, wup_ref[d]))) - 0.5
        lw_out[d] = -jnp.exp(w_log)
    a = _sigmoid(a0_ref[...] + _dot(ul.astype(BF16), aup_ref[...]))
    g_out[...] = _dot(_sigmoid(ul).astype(BF16), gup_ref[...])

    kk = k * kk_ref[...]
    nrm = jnp.sqrt(_seg_sum(kk * kk, gd_ref[...], gu_ref[...]))
    kk = kk / jnp.maximum(nrm, 1e-12)
    k = k * (1.0 + (a - 1.0) * ka_ref[...])
    r_out[...] = r
    k_out[...] = k
    v_out[...] = v
    a_out[...] = -kk
    b_out[...] = kk * a
    bonus_out[...] = _seg_sum(r * k * rk_ref[...], gd_ref[...], gu_ref[...]) * v


def _seq_halo(u, tm, nb):
    rows = u.shape[0]
    nblk = rows // tm
    starts = np.arange(nblk) * tm
    seq_len = np.where(starts < nb * T, T, C)
    seq_off = np.where(starts < nb * T, starts % T, (starts - nb * T) % C)
    has_prev = seq_off > 0
    has_next = seq_off + tm < seq_len
    prev_idx = np.where(has_prev, starts - 1, 0)
    next_idx = np.where(has_next, starts + tm, 0)
    up = jnp.where(has_prev[:, None], u[prev_idx], 0.0)
    un = jnp.where(has_next[:, None], u[next_idx], 0.0)
    return up[:, None, :], un[:, None, :]


def _rwkv_prepare(u, nb, mu, w0, w_up, a0, a_up, g_up, k_k, k_a, r_k):
    rows = u.shape[0]
    tm = 256
    nblk = rows // tm
    up, un = _seq_halo(u, tm, nb)
    pad = jnp.zeros((LORA,), F32)
    mu_rkv = mu[:3 * RW].reshape(1, 3 * RW)
    mu_l = jnp.concatenate([mu[3 * RW:], pad]).reshape(1, 4 * LORA)

    def lora_w(w, slot):
        z = jnp.zeros((4 * LORA, RW), F32)
        return z.at[slot * LORA:(slot + 1) * LORA].set(w).astype(BF16)

    wup = jnp.stack([lora_w(w_up[0], 0), lora_w(w_up[1], 0)])
    aup = lora_w(a_up, 1)
    gup = lora_w(g_up, 2)
    head = np.arange(RW) // HD
    g_down = jnp.asarray(head[:, None] == np.arange(128)[None, :], BF16)
    g_upm = jnp.asarray(np.arange(128)[:, None] == head[None, :], BF16)

    row = lambda w: pl.BlockSpec((tm, w), lambda i: (i, 0))
    vec = lambda w: pl.BlockSpec((1, w), lambda i: (0, 0))
    full = lambda *s: pl.BlockSpec(s, lambda i: (0,) * len(s))
    out_sd = jax.ShapeDtypeStruct((rows, RW), F32)
    outs = pl.pallas_call(
        _rwkv_prep_kernel,
        grid=(nblk,),
        in_specs=[pl.BlockSpec((tm, 3 * RW), lambda i: (i, 0)),
                  pl.BlockSpec((tm, 4 * LORA), lambda i: (i, AB_LORA // (4 * LORA))),
                  pl.BlockSpec((None, 1, 3 * RW), lambda i: (i, 0, 0)),
                  pl.BlockSpec((None, 1, 3 * RW), lambda i: (i, 0, 0)),
                  pl.BlockSpec((None, 1, 4 * LORA), lambda i: (i, 0, AB_LORA // (4 * LORA))),
                  pl.BlockSpec((None, 1, 4 * LORA), lambda i: (i, 0, AB_LORA // (4 * LORA))),
                  vec(3 * RW), vec(4 * LORA), full(2, RW), full(2, 4 * LORA, RW),
                  vec(RW), full(4 * LORA, RW), full(4 * LORA, RW), vec(RW), vec(RW), vec(RW),
                  full(RW, 128), full(128, RW)],
        out_specs=[row(RW), row(RW), row(RW), row(RW), row(RW),
                   pl.BlockSpec((2, tm, RW), lambda i: (0, i, 0)), row(RW), row(RW)],
        out_shape=[out_sd, out_sd, out_sd, out_sd, out_sd,
                   jax.ShapeDtypeStruct((2, rows, RW), F32), out_sd, out_sd],
        compiler_params=_params(("parallel",), V7X_VMEM_LIMIT),
        name="rwkv_prepare",
    )(u, u, up, un, up, un, mu_rkv, mu_l, w0, wup, a0.reshape(1, RW), aup, gup,
      k_k.reshape(1, RW), k_a.reshape(1, RW), r_k.reshape(1, RW), g_down, g_upm)
    return outs


def _chunk_block(d, b, s, nb):
    nctx, nlat = C // CHUNK, T // CHUNK
    pos_c = jnp.where(d == 0, s, nctx - 1 - s)
    pos_l = jnp.where(d == 0, s - nctx, nlat - 1 - (s - nctx))
    return jnp.where(s < nctx, nb * nlat + b * nctx + pos_c, b * nlat + pos_l)


def _lane_stack(x, hw):
    head = lax.broadcasted_iota(jnp.int32, x.shape, 1) // hw
    return jnp.concatenate([jnp.where(head == h, x, 0.0) for h in range(GROUP // hw)], axis=0)


def _fold_rows(x, n):
    out = x[0:n]
    for h in range(1, x.shape[0] // n):
        out = out + x[h * n:(h + 1) * n]
    return out


def _block_masks(size, blk, d):
    row = lax.broadcasted_iota(jnp.int32, (size, size), 0)
    col = lax.broadcasted_iota(jnp.int32, (size, size), 1)
    rr, cc = row % blk, col % blk
    same = (row // blk) == (col // blk)
    if d == 0:
        return cc <= rr, cc < rr, same
    return cc >= rr, cc > rr, same


def _rwkv_scan_kernel(*refs):
    in_refs = (refs[0:6], refs[6:12])
    y_refs = refs[12:14]
    s_ref = refs[14]

    @pl.when(pl.program_id(1) == 0)
    def _():
        s_ref[...] = jnp.zeros_like(s_ref)

    n = CHUNK
    ngrp = RW // GROUP
    nch = 2 * ngrp
    rrow = lax.broadcasted_iota(jnp.int32, (n, GROUP), 0)
    rcol = lax.broadcasted_iota(jnp.int32, (n, GROUP), 1) % n
    eye_row = jnp.where(rrow == rcol, 1.0, 0.0)
    incl_row = (rcol <= rrow, rcol >= rrow)
    strict_row = (rcol < rrow, rcol > rrow)
    same = _block_masks(GROUP, n, 0)[2]

    def block_diag(x_row):
        xb = x_row.astype(BF16)
        return jnp.where(same, jnp.concatenate([xb] * (GROUP // n), axis=0), jnp.zeros((), BF16))

    def stack(x):
        return _lane_stack(x.astype(BF16), HD)

    pre = []
    for d in range(2):
        r, k, v, a, b, lw = [ref[...] for ref in in_refs[d]]
        tri = jnp.where(_block_masks(n, n, d)[0], 1.0, 0.0).astype(BF16)
        c = _tri_dot(tri, lw)
        c_last = c[n - 1:n] if d == 0 else c[0:1]
        p_inv = jnp.exp(-c)
        p_end = jnp.exp(c_last - c)
        pre.append(dict(at=a * jnp.exp(c - lw), rt=r * jnp.exp(c), bt=b * p_inv, kt=k * p_inv,
                        bp=b * p_end, kp=k * p_end, v=v, dec=jnp.exp(c_last)))
    chains = [(d, g) for d in range(2) for g in range(ngrp)]
    sl = lambda g: slice(g * GROUP, (g + 1) * GROUP)
    part = lambda d, g, name: pre[d][name][:, sl(g)]

    a_st = [stack(part(d, g, "at")) for d, g in chains]
    v_st = [stack(part(d, g, "v")) for d, g in chains]
    gram = [_dot_nt(jnp.concatenate([part(d, g, "at"), part(d, g, "rt")], axis=0).astype(BF16),
                    jnp.concatenate([stack(part(d, g, "bt")), stack(part(d, g, "kt"))], axis=0))
            for d, g in chains]
    a_ab = [jnp.where(strict_row[d], gram[i][0:n, 0:GROUP], 0.0) for i, (d, g) in enumerate(chains)]
    a_ak = [jnp.where(strict_row[d], gram[i][0:n, GROUP:], 0.0).astype(BF16) for i, (d, g) in enumerate(chains)]
    a_r = [jnp.where(jnp.concatenate([incl_row[d]] * 2, axis=1), gram[i][n:2 * n, :], 0.0).astype(BF16)
           for i, (d, g) in enumerate(chains)]
    av = [_dot(a_ak[i], v_st[i]) for i in range(nch)]
    tm = [eye_row + m for m in a_ab]
    pw = [_dot(m.astype(BF16), block_diag(m)) for m in a_ab]
    for _ in range(4):
        both = [_dot(jnp.concatenate([t, p], axis=0).astype(BF16), block_diag(p)) for t, p in zip(tm, pw)]
        tm = [t + x[0:n] for t, x in zip(tm, both)]
        pw = [x[n:2 * n] for x in both]
    tm = [t + _dot(t.astype(BF16), block_diag(p)) for t, p in zip(tm, pw)]
    tx = [_dot(tm[i].astype(BF16), jnp.concatenate([a_st[i], stack(av[i])], axis=1))
          for i in range(nch)]
    st = [s_ref[d, g] for d, g in chains]
    uy0 = [_dot_nt(jnp.concatenate([tx[i][:, 0:GROUP], part(d, g, "rt")], axis=0).astype(BF16),
                   st[i].astype(BF16)) for i, (d, g) in enumerate(chains)]
    u = [uy0[i][0:n] + tx[i][:, GROUP:] for i in range(nch)]
    for i, (d, g) in enumerate(chains):
        y_refs[d][:, sl(g)] = uy0[i][n:2 * n] + _dot(a_r[i], jnp.concatenate([stack(u[i]), v_st[i]], axis=0))
    for i, (d, g) in enumerate(chains):
        upd = _dot_tn(jnp.concatenate([u[i], part(d, g, "v")], axis=0).astype(BF16),
                      jnp.concatenate([part(d, g, "bp"), part(d, g, "kp")], axis=0).astype(BF16))
        s_ref[d, g] = st[i] * part(d, g, "dec") + jnp.where(same, upd, 0.0)


def _rwkv_scan(r, k, v, a, b, lw, nb):
    rows = r.shape[0]
    steps = (T + C) // CHUNK
    in_specs, args = [], []
    for d in range(2):
        blk = lambda bb, s, d=d: (_chunk_block(d, bb, s, nb), 0)
        in_specs += [pl.BlockSpec((CHUNK, RW), blk)] * 5
        in_specs.append(pl.BlockSpec((None, CHUNK, RW), lambda bb, s, d=d: (d, _chunk_block(d, bb, s, nb), 0)))
        args += [r, k, v, a, b, lw]
    out_sd = jax.ShapeDtypeStruct((rows, RW), F32)
    return pl.pallas_call(
        _rwkv_scan_kernel,
        grid=(nb, steps),
        in_specs=in_specs,
        out_specs=[pl.BlockSpec((CHUNK, RW), lambda bb, s, d=d: (_chunk_block(d, bb, s, nb), 0)) for d in range(2)],
        out_shape=[out_sd, out_sd],
        scratch_shapes=[pltpu.VMEM((2, RW // GROUP, GROUP, GROUP), F32)],
        compiler_params=_params(("parallel", "arbitrary"), V7X_VMEM_LIMIT),
        name="rwkv_scan",
    )(*args)


def _rwkv_post_kernel(y0_ref, y1_ref, bonus_ref, g_ref, lnw_ref, lnb_ref, gd_ref, gu_ref, o_ref):
    y = y0_ref[...] + y1_ref[...]
    mu = _seg_sum(y, gd_ref[...], gu_ref[...]) * (1.0 / HD)
    yc = y - mu
    var = _seg_sum(yc * yc, gd_ref[...], gu_ref[...]) * (1.0 / HD)
    yn = yc * lax.rsqrt(var + RWKV_GN_EPS)
    o_ref[...] = (yn * lnw_ref[...] + lnb_ref[...] + bonus_ref[...]) * g_ref[...]


def _rwkv_post(y0, y1, bonus, g, ln_w, ln_b):
    rows = bonus.shape[0]
    tm = 512
    head = np.arange(RW) // HD
    g_down = jnp.asarray(head[:, None] == np.arange(128)[None, :], BF16)
    g_upm = jnp.asarray(np.arange(128)[:, None] == head[None, :], BF16)
    row = pl.BlockSpec((tm, RW), lambda i: (i, 0))
    vec = pl.BlockSpec((1, RW), lambda i: (0, 0))
    return pl.pallas_call(
        _rwkv_post_kernel,
        grid=(rows // tm,),
        in_specs=[row, row, row, row, vec, vec,
                  pl.BlockSpec((RW, 128), lambda i: (0, 0)), pl.BlockSpec((128, RW), lambda i: (0, 0))],
        out_specs=row,
        out_shape=jax.ShapeDtypeStruct((rows, RW), F32),
        compiler_params=_params(("parallel",), V7X_VMEM_LIMIT),
        name="rwkv_post",
    )(y0, y1, bonus, g, ln_w.reshape(1, RW), ln_b.reshape(1, RW), g_down, g_upm)


def _kv_block(i, nb):
    per = T // 256
    return jnp.where(i < nb * per, i // per, i - nb * per), jnp.where(i < nb * per, i % per, per)


def _swa_prep_kernel(k_ref, v_ref, cos_ref, sin_ref, ko_ref, vo_ref):
    cosl, sinl = cos_ref[...], sin_ref[...]
    lane = lax.broadcasted_iota(jnp.int32, cosl.shape, 1)
    low = lane < HD
    for src, dst, rope in ((k_ref, ko_ref, True), (v_ref, vo_ref, False)):
        for t in range(2):
            x = src[:, t * 128:(t + 1) * 128]
            if rope:
                x = _rope(x, cosl, sinl)
            swapped = pltpu.roll(x, HD, axis=1)
            dst[:, (2 * t) * 128:(2 * t + 1) * 128] = jnp.where(low, x, swapped).astype(BF16)
            dst[:, (2 * t + 1) * 128:(2 * t + 2) * 128] = jnp.where(low, swapped, x).astype(BF16)


def _swa_prep(u, cosl, sinl, nb):
    rows = u.shape[0]
    tm = 256
    kv_out = pl.BlockSpec((None, tm, 512), lambda i: (*_kv_block(i, nb), 0))
    sd = jax.ShapeDtypeStruct((nb, T + C, 512), BF16)
    return pl.pallas_call(
        _swa_prep_kernel,
        grid=(rows // tm,),
        in_specs=[pl.BlockSpec((tm, 256), lambda i: (i, AB_KS // 256)),
                  pl.BlockSpec((tm, 256), lambda i: (i, AB_VS // 256)),
                  pl.BlockSpec((tm, 128), lambda i: (i, 0)),
                  pl.BlockSpec((tm, 128), lambda i: (i, 0))],
        out_specs=[kv_out, kv_out],
        out_shape=[sd, sd],
        compiler_params=_params(("parallel",), V7X_VMEM_LIMIT),
        name="swa_prep",
    )(u, u, cosl, sinl)


def _diff_prep_kernel(k_ref, v_ref, cos_ref, sin_ref, ko_ref, vo_ref):
    cosl, sinl = cos_ref[...], sin_ref[...]
    for t in range(8):
        sl = slice(t * 128, (t + 1) * 128)
        ko_ref[:, sl] = _rope(k_ref[:, sl], cosl, sinl).astype(BF16)
    vo_ref[...] = v_ref[...].astype(BF16)


def _diff_prep(u, cosl, sinl, nb):
    rows = u.shape[0]
    tm = 256
    kv_out = pl.BlockSpec((None, tm, 1024), lambda i: (*_kv_block(i, nb), 0))
    sd = jax.ShapeDtypeStruct((nb, T + C, 1024), BF16)
    return pl.pallas_call(
        _diff_prep_kernel,
        grid=(rows // tm,),
        in_specs=[pl.BlockSpec((tm, 1024), lambda i: (i, 1)),
                  pl.BlockSpec((tm, 1024), lambda i: (i, 2)),
                  pl.BlockSpec((tm, 128), lambda i: (i, 0)),
                  pl.BlockSpec((tm, 128), lambda i: (i, 0))],
        out_specs=[kv_out, kv_out],
        out_shape=[sd, sd],
        compiler_params=_params(("parallel",), V7X_VMEM_LIMIT),
        name="diff_prep",
    )(u, u, cosl, sinl)


def _swa_kernel(q_ref, cos_ref, sin_ref, k_ref, v_ref, sink_ref, o_ref, *, ctx_mode):
    n = pl.program_id(1)
    cosl, sinl = cos_ref[...], sin_ref[...]
    if not ctx_mode:
        start = pl.multiple_of(jnp.clip((n - 1) * QB, 0, T - 3 * QB), QB)
        kpos = start + lax.broadcasted_iota(jnp.int32, (2 * QB, 3 * QB), 1)
        qpos = n * QB + lax.broadcasted_iota(jnp.int32, (2 * QB, 3 * QB), 0) % QB
        valid = jnp.abs(kpos - qpos) <= SWA_WINDOW
    low = lax.broadcasted_iota(jnp.int32, (QB, 128), 1) < HD
    top = lax.broadcasted_iota(jnp.int32, (2 * QB, 1), 0) < QB
    tile = lambda j: slice((j // 2) * 128, (j // 2 + 1) * 128)
    qs, sink = [], []
    for j in range(8):
        q = _rope(q_ref[:, j * 128:(j + 1) * 128], cosl, sinl) * (HD ** -0.5 * LOG2E)
        qs.append(jnp.concatenate([jnp.where(low, q, 0.0), jnp.where(low, 0.0, q)], axis=0).astype(BF16))
        sink.append(jnp.where(top, sink_ref[2 * j:2 * j + 1, 0:1], sink_ref[2 * j + 1:2 * j + 2, 0:1]) * LOG2E)
    s_c = [_dot_nt(qs[j], k_ref[T:T + C, tile(j)]) for j in range(8)]
    m = [jnp.maximum(jnp.max(s, axis=-1, keepdims=True), sk) for s, sk in zip(s_c, sink)]
    if not ctx_mode:
        s_w = [jnp.where(valid, _dot_nt(qs[j], k_ref[pl.ds(start, 3 * QB), tile(j)]), -jnp.inf) for j in range(8)]
        m = [jnp.maximum(mm, jnp.max(s, axis=-1, keepdims=True)) for mm, s in zip(m, s_w)]
    p_c = [jnp.exp2(s - mm) for s, mm in zip(s_c, m)]
    den = [jnp.sum(p, axis=-1, keepdims=True) + jnp.exp2(sk - mm) for p, sk, mm in zip(p_c, sink, m)]
    pv = [_dot(p.astype(BF16), v_ref[T:T + C, tile(j)]) for j, p in enumerate(p_c)]
    if not ctx_mode:
        p_w = [jnp.exp2(s - mm) for s, mm in zip(s_w, m)]
        den = [dd + jnp.sum(p, axis=-1, keepdims=True) for dd, p in zip(den, p_w)]
        pv = [x + _dot(p.astype(BF16), v_ref[pl.ds(start, 3 * QB), tile(j)]) for j, (x, p) in enumerate(zip(pv, p_w))]
    for j in range(8):
        o = pv[j] / den[j]
        o_ref[:, j * 128:(j + 1) * 128] = jnp.where(low, o[0:QB], o[QB:2 * QB])


def _swa(u, cosl, sinl, kpad, vpad, sink, nb, ctx_mode):
    nq = (C if ctx_mode else T) // QB
    base = nb * (T // QB) if ctx_mode else 0
    rows = nb * nq * QB
    rowblk = lambda b, n: base + b * nq + n
    kv = pl.BlockSpec((None, T + C, 512), lambda b, n: (b, 0, 0))
    return pl.pallas_call(
        functools.partial(_swa_kernel, ctx_mode=ctx_mode),
        grid=(nb, nq),
        in_specs=[pl.BlockSpec((QB, 1024), lambda b, n: (rowblk(b, n), AB_Q // 1024)),
                  pl.BlockSpec((QB, 128), lambda b, n: (rowblk(b, n), 0)),
                  pl.BlockSpec((QB, 128), lambda b, n: (rowblk(b, n), 0)),
                  kv, kv, pl.BlockSpec((16, 128), lambda b, n: (0, 0))],
        out_specs=pl.BlockSpec((QB, 1024), lambda b, n: (b * nq + n, 0)),
        out_shape=jax.ShapeDtypeStruct((rows, 1024), F32),
        compiler_params=_params(("parallel", "arbitrary"), V7X_VMEM_LIMIT),
        name="swa_ctx" if ctx_mode else "swa_latent",
    )(u, cosl, sinl, kpad, vpad, sink)


def _diff_kernel(q_ref, cos_ref, sin_ref, k_ref, v_ref, lam_ref, sub_ref, o_ref):
    cosl, sinl = cos_ref[...], sin_ref[...]
    lam = lam_ref[...]
    nq = q_ref.shape[0]
    low = lax.broadcasted_iota(jnp.int32, (nq, 128), 1) < HD
    for h in range(8):
        sl = slice(h * 128, (h + 1) * 128)
        q = _rope(q_ref[:, sl], cosl, sinl) * (HD ** -0.5 * LOG2E)
        qs = jnp.concatenate([jnp.where(low, q, 0.0), jnp.where(low, 0.0, q)], axis=0).astype(BF16)
        s = _dot_nt(qs, k_ref[:, sl])
        e = jnp.exp2(s - jnp.max(s, axis=-1, keepdims=True))
        inv = 1.0 / jnp.sum(e, axis=-1, keepdims=True)
        o2 = _dot(e.astype(BF16), v_ref[:, sl])
        o = o2[0:nq] * inv[0:nq] - o2[nq:2 * nq] * (lam[:, 0:1] * inv[nq:2 * nq])
        o = o * lax.rsqrt(jnp.mean(o * o, axis=-1, keepdims=True) + 1e-5)
        o_ref[:, sl] = o * sub_ref[...]


def _diff_attn(u, cosl, sinl, kb, vb, lam, sub, nb):
    QB = DIFF_QB
    nq = T // QB
    kv = pl.BlockSpec((None, T + C, 1024), lambda b, n: (b, 0, 0))
    vec = pl.BlockSpec((1, 128), lambda b, n: (0, 0))
    return pl.pallas_call(
        _diff_kernel,
        grid=(nb, nq),
        in_specs=[pl.BlockSpec((QB, 1024), lambda b, n: (b * nq + n, 0)),
                  pl.BlockSpec((QB, 128), lambda b, n: (b * nq + n, 0)),
                  pl.BlockSpec((QB, 128), lambda b, n: (b * nq + n, 0)),
                  kv, kv, vec, vec],
        out_specs=pl.BlockSpec((QB, 1024), lambda b, n: (b * nq + n, 0)),
        out_shape=jax.ShapeDtypeStruct((nb * T, 1024), F32),
        compiler_params=_params(("parallel", "arbitrary"), V7X_VMEM_LIMIT),
        name="diff_attn",
    )(u, cosl, sinl, kb, vb, lam, sub)


def _hgrn_exact_att(q, kk, bcum, tmp_ref):
    n = CHUNK
    tmp_ref[0] = bcum
    tmp_ref[1] = kk
    coli = lax.broadcasted_iota(jnp.int32, (n, 2 * n), 1)

    def body(s, acc):
        bs = tmp_ref[0, pl.ds(s, 1), :]
        ks = tmp_ref[1, pl.ds(s, 1), :]
        w = q * jnp.exp(jnp.minimum(bcum - bs, 0.0)) * ks
        c0 = jnp.sum(w[:, 0:HG_DK], axis=-1, keepdims=True)
        c1 = jnp.sum(w[:, HG_DK:], axis=-1, keepdims=True)
        return acc + jnp.concatenate([jnp.where(coli == s, c0, 0.0), jnp.where(coli == s + n, c1, 0.0)], axis=0)

    return lax.fori_loop(0, n, body, jnp.zeros((2 * n, 2 * n), F32))


def _hgrn_scan_kernel(q0_ref, z0_ref, i0_ref, lb0_ref, q1_ref, z1_ref, i1_ref, lb1_ref,
                      o0_ref, o1_ref, s_ref, g_ref, tmp_ref):
    in_refs = ((q0_ref, z0_ref, i0_ref, lb0_ref), (q1_ref, z1_ref, i1_ref, lb1_ref))
    o_refs = (o0_ref, o1_ref)

    @pl.when(pl.program_id(1) == 0)
    def _():
        s_ref[...] = jnp.zeros_like(s_ref)

    n = CHUNK
    ngrp = 1024 // GROUP
    nsub = n // SUB
    pre = []
    for d in range(2):
        q, z, v, lb = [ref[...] for ref in in_refs[d]]
        logf = jnp.log(lb + (1.0 - lb) * _sigmoid(z))
        kk = (1.0 - lb) * _sigmoid(-z)
        tri = jnp.where(_block_masks(n, n, d)[0], 1.0, 0.0).astype(BF16)
        bcum = _tri_dot(tri, logf)
        b_last = bcum[n - 1:n] if d == 0 else bcum[0:1]
        b_excl = bcum - logf
        qh, kh = [], []
        for sb in range(nsub):
            lo, hi = sb * SUB, (sb + 1) * SUB
            beta = b_excl[lo:lo + 1] if d == 0 else b_excl[hi - 1:hi]
            qh.append(q[lo:hi] * jnp.exp(bcum[lo:hi] - beta))
            kh.append(kk * jnp.exp(beta - bcum))
        pre.append(dict(q=q, v=v, kk=kk, bcum=bcum, qh=qh, kh=kh, qe=q * jnp.exp(bcum),
                        ke=kk * jnp.exp(b_last - bcum), dec=jnp.exp(b_last), min_logf=jnp.min(logf)))
    chains = [(d, g) for d in range(2) for g in range(ngrp)]
    sl = lambda g: slice(g * GROUP, (g + 1) * GROUP)

    for i, (d, g) in enumerate(chains):
        rows = [_dot_nt(_lane_stack(pre[d]["qh"][sb][:, sl(g)], HG_DK).astype(BF16),
                        _lane_stack(pre[d]["kh"][sb][:, sl(g)], HG_DK).astype(BF16)) for sb in range(nsub)]
        g_ref[i] = jnp.concatenate([rows[sb][h * SUB:(h + 1) * SUB] for h in range(2) for sb in range(nsub)], axis=0)

    @pl.when(jnp.minimum(pre[0]["min_logf"], pre[1]["min_logf"]) < -(HG_CLAMP / SUB))
    def _():
        for i, (d, g) in enumerate(chains):
            g_ref[i] = _hgrn_exact_att(pre[d]["q"][:, sl(g)], pre[d]["kk"][:, sl(g)], pre[d]["bcum"][:, sl(g)],
                                       tmp_ref)

    st = [s_ref[d, g] for d, g in chains]
    inter = [_dot_nt(pre[d]["qe"][:, sl(g)].astype(BF16), st[i].astype(BF16)) for i, (d, g) in enumerate(chains)]
    for i, (d, g) in enumerate(chains):
        incl, _, same = _block_masks(2 * n, n, d)
        att = jnp.where(incl & same, g_ref[i], 0.0).astype(BF16)
        v_st = _lane_stack(pre[d]["v"][:, sl(g)], HG_DK).astype(BF16)
        o_refs[d][:, sl(g)] = inter[i] + _fold_rows(_dot(att, v_st), n)
    for i, (d, g) in enumerate(chains):
        upd = _dot_tn(pre[d]["v"][:, sl(g)].astype(BF16), pre[d]["ke"][:, sl(g)].astype(BF16))
        s_ref[d, g] = st[i] * pre[d]["dec"][:, sl(g)] + jnp.where(_block_masks(GROUP, HG_DK, d)[2], upd, 0.0)


def _hgrn_scan(u, lb, nb):
    rows = u.shape[0]
    steps = (T + C) // CHUNK
    in_specs, args = [], []
    for d in range(2):
        col = lambda j, d=d: pl.BlockSpec((CHUNK, 1024), lambda bb, s: (_chunk_block(d, bb, s, nb), j))
        in_specs += [col(3), col(4 + d), col(6), pl.BlockSpec((None, 1, 1024), lambda bb, s, d=d: (d, 0, 0))]
        args += [u, u, u, lb]
    out_sd = jax.ShapeDtypeStruct((rows, 1024), F32)
    ngrp = 1024 // GROUP
    return pl.pallas_call(
        _hgrn_scan_kernel,
        grid=(nb, steps),
        in_specs=in_specs,
        out_specs=[pl.BlockSpec((CHUNK, 1024), lambda bb, s, d=d: (_chunk_block(d, bb, s, nb), 0)) for d in range(2)],
        out_shape=[out_sd, out_sd],
        scratch_shapes=[pltpu.VMEM((2, ngrp, GROUP, GROUP), F32),
                        pltpu.VMEM((2 * ngrp, 2 * CHUNK, 2 * CHUNK), F32),
                        pltpu.VMEM((2, CHUNK, GROUP), F32)],
        compiler_params=_params(("parallel", "arbitrary"), V7X_VMEM_LIMIT),
        name="hgrn_scan",
    )(*args)


def _hgrn_post_kernel(o0_ref, o1_ref, g_ref, gn_ref, y_ref):
    g = g_ref[...]
    for h in range(8):
        sl = slice(h * 128, (h + 1) * 128)
        o = o0_ref[:, sl] + o1_ref[:, sl]
        o = o * lax.rsqrt(jnp.mean(o * o, axis=-1, keepdims=True) + NORM_EPS) * gn_ref[...]
        gh = g[:, sl]
        y_ref[:, sl] = o * (gh * _sigmoid(gh))


def _hgrn_post(o0, o1, u, gnorm, rows):
    tm = 512
    return pl.pallas_call(
        _hgrn_post_kernel,
        grid=(rows // tm,),
        in_specs=[pl.BlockSpec((tm, 1024), lambda i: (i, 0)),
                  pl.BlockSpec((tm, 1024), lambda i: (i, 0)),
                  pl.BlockSpec((tm, 1024), lambda i: (i, 7)),
                  pl.BlockSpec((1, 128), lambda i: (0, 0))],
        out_specs=pl.BlockSpec((tm, 1024), lambda i: (i, 0)),
        out_shape=jax.ShapeDtypeStruct((rows, 1024), F32),
        compiler_params=_params(("parallel",), V7X_VMEM_LIMIT),
        name="hgrn_post",
    )(o0, o1, u, gnorm.reshape(1, 128))


def _rope_tables(nb):
    t = np.arange(T)
    quarter = HD // 4
    inv = ROPE_THETA ** (-jnp.arange(quarter, dtype=F32) / quarter)
    rows = jnp.asarray(t // GRID_W, F32)
    cols = jnp.asarray(t % GRID_W, F32)
    ang = jnp.concatenate([rows[:, None] * inv, cols[:, None] * inv], axis=-1)
    cos, sin = jnp.cos(ang), jnp.sin(ang)
    cosl = jnp.tile(jnp.concatenate([cos, cos], axis=-1), (nb, 2))
    sinl = jnp.tile(jnp.concatenate([-sin, sin], axis=-1), (nb, 2))
    cosl = jnp.concatenate([cosl, jnp.ones((nb * C, 128), F32)], axis=0)
    sinl = jnp.concatenate([sinl, jnp.zeros((nb * C, 128), F32)], axis=0)
    return cosl, sinl


def kernel(x, c, ctx, c_ctx, ada_w, ada_b, mlp_w1, mlp_w2, final_norm, ab_w_in, ab_w_out, ab_mu, ab_w0,
           ab_w_up, ab_a0, ab_a_up, ab_g_up, ab_k_k, ab_k_a, ab_r_k, ab_ln_w, ab_ln_b, ab_sink,
           cd_w_in, cd_w_out, cd_lam, cd_subln, cd_gnorm, hgrn_lb_logits):
    nb = x.shape[0]
    assert x.shape == (nb, T, D) and ctx.shape == (nb, C, D) and nb < 8
    assert ada_w.shape[0] == 2, "one AB layer followed by one CD layer"
    rx = nb * T
    xc = jnp.concatenate([x.reshape(rx, D), ctx.reshape(nb * C, D)], axis=0)
    cosl, sinl = _rope_tables(nb)

    s_in = jnp.zeros((8, D), F32).at[:nb].set(c).at[nb].set(c_ctx)
    mod = _ada(s_in, ada_w, ada_b)

    w = ab_w_in[0]
    w_in0 = jnp.concatenate([w[:, :3 * RW], w[:, 3 * RW + 3 * LORA:], w[:, 3 * RW:3 * RW + 3 * LORA],
                             jnp.zeros((D, LORA), F32)], axis=1).astype(BF16)
    u = _norm_mod_matmul(xc, mod, 0, w_in0, nb, 512, AB_N // 2)
    r, k, v, a, b, lw, g, bonus = _rwkv_prepare(u, nb, ab_mu[0], ab_w0[0], ab_w_up[0], ab_a0[0], ab_a_up[0],
                                                ab_g_up[0], ab_k_k[0], ab_k_a[0], ab_r_k[0].reshape(RW))
    y0, y1 = _rwkv_scan(r, k, v, a, b, lw, nb)
    ya = _rwkv_post(y0, y1, bonus, g, ab_ln_w[0], ab_ln_b[0])
    kpad, vpad = _swa_prep(u, cosl, sinl, nb)
    sink = jnp.broadcast_to(ab_sink[0][:, None], (16, 128))
    yb = jnp.concatenate([_swa(u, cosl, sinl, kpad, vpad, sink, nb, False),
                          _swa(u, cosl, sinl, kpad, vpad, sink, nb, True)], axis=0)
    w_out0 = ab_w_out[0].astype(BF16)
    xc = _out_proj(ya, yb, xc, xc.shape[0], mod, 0, w_out0[:RW], w_out0[RW:], nb)
    fn = final_norm.reshape(1, D)
    xc = _mlp(xc, xc.shape[0], mod, 0, mlp_w1, mlp_w2, fn, nb, False)

    lam_init = 0.8 - 0.6 * math.exp(-0.3 * 1)
    lb_table = jnp.cumsum(jax.nn.softmax(hgrn_lb_logits.astype(F32), axis=0), axis=0)
    lb = (lb_table - lb_table[0])[1].reshape(2, 1, 1024)
    lf = cd_lam[0].astype(F32)
    lmb = jnp.exp(jnp.sum(lf[0] * lf[1])) - jnp.exp(jnp.sum(lf[2] * lf[3])) + lam_init
    u = _norm_mod_matmul(xc, mod, 1, cd_w_in[0], nb, 1024, 1024)
    kb, vb = _diff_prep(u, cosl, sinl, nb)
    yc = _diff_attn(u, cosl, sinl, kb, vb, jnp.full((1, 128), lmb, F32),
                    (cd_subln[0] * (1.0 - lam_init)).reshape(1, 128), nb)
    o0, o1 = _hgrn_scan(u, lb, nb)
    yd = _hgrn_post(o0, o1, u, cd_gnorm[0], rx)
    w_out1 = cd_w_out[0].astype(BF16)
    xl = _out_proj(yc, yd, xc, rx, mod, 1, w_out1[:1024], w_out1[1024:], nb)
    out = _mlp(xl, rx, mod, 1, mlp_w1, mlp_w2, fn, nb, True)
    return out.reshape(nb, T, D)
```

```python
import functools
import math

import jax
import jax.numpy as jnp
import numpy as np
from jax import lax
from jax.experimental import pallas as pl
from jax.experimental.pallas import tpu as pltpu

F32 = jnp.float32
BF16 = jnp.bfloat16

D = 2048
T = 2048
C = 256
GRID_W = 64
D_FF = 4 * D
HD = 64
ROPE_THETA = 10000.0
LOG2E = 1.4426950408889634
NORM_EPS = 1e-6
N_MOD = 6
RW = 1024
RWKV_GN_EPS = 64e-5
LORA = 64
SWA_WINDOW = 128
QB = 128
DIFF_QB = 256
OPROJ_TM = 512
CHUNK = 64
SUB = 16
HG_DK = 128
HG_CLAMP = 80.0
GROUP = 256
V7X_VMEM_BYTES = 64 * 1024 * 1024
V7X_VMEM_LIMIT = V7X_VMEM_BYTES - 8 * 1024 * 1024

AB_Q, AB_KS, AB_VS, AB_LORA, AB_N = 3072, 4096, 4352, 4608, 4864


def _dot(a, b):
    return jnp.dot(a, b, preferred_element_type=F32)


def _dot_nt(a, b):
    return lax.dot_general(a, b, (((1,), (1,)), ((), ())), preferred_element_type=F32)


def _dot_tn(a, b):
    return lax.dot_general(a, b, (((0,), (0,)), ((), ())), preferred_element_type=F32)


def _split2(x):
    hi = x.astype(BF16)
    lo = (x - hi.astype(F32)).astype(BF16)
    return hi, lo


def _split3(x):
    hi = x.astype(BF16)
    r1 = x - hi.astype(F32)
    mid = r1.astype(BF16)
    lo = (r1 - mid.astype(F32)).astype(BF16)
    return hi, mid, lo


def _dot_exact_rhs(x, g):
    hi, lo = _split2(x)
    return _dot(hi, g) + _dot(lo, g)


def _tri_dot(tri, x):
    hi, mid, lo = _split3(x)
    return _dot(tri, hi) + _dot(tri, mid) + _dot(tri, lo)


def _seg_sum(x, g_down, g_up):
    return _dot_exact_rhs(_dot_exact_rhs(x, g_down), g_up)


def _rope(x, cosl, sinl):
    w = x.shape[-1]
    lane = lax.broadcasted_iota(jnp.int32, x.shape, x.ndim - 1)
    first = (lane & 63) < 32
    swapped = jnp.where(first, pltpu.roll(x, w - 32, axis=1), pltpu.roll(x, 32, axis=1))
    return x * cosl + swapped * sinl


def _sigmoid(x):
    return 1.0 / (1.0 + jnp.exp(-x))


def _softplus(x):
    return jnp.maximum(x, 0.0) + jnp.log(1.0 + jnp.exp(-jnp.abs(x)))


def _params(sem, vmem=None):
    return pltpu.CompilerParams(dimension_semantics=sem, vmem_limit_bytes=vmem)


def _ada_kernel(s_ref, w_ref, b_ref, o_ref):
    s = s_ref[...]
    s = s * _sigmoid(s)
    o_ref[...] = _dot(s.astype(BF16), w_ref[...].astype(BF16)) + b_ref[...]


def _ada(s_in, ada_w, ada_b):
    depth = ada_w.shape[0]
    n = ada_w.shape[2]
    tn = 1536
    out = pl.pallas_call(
        _ada_kernel,
        grid=(depth, n // tn),
        in_specs=[pl.BlockSpec((8, D), lambda l, j: (0, 0)),
                  pl.BlockSpec((None, D, tn), lambda l, j: (l, 0, j)),
                  pl.BlockSpec((None, 1, tn), lambda l, j: (l, 0, j))],
        out_specs=pl.BlockSpec((None, 8, tn), lambda l, j: (l, 0, j)),
        out_shape=jax.ShapeDtypeStruct((depth, 8, n), F32),
        compiler_params=_params(("arbitrary", "arbitrary"), V7X_VMEM_LIMIT),
        name="ada_mod",
    )(s_in, ada_w, ada_b.reshape(depth, 1, n))
    return out.reshape(depth, 8, 1, n)


def _mod_spec(layer, k, tm, nb):
    return pl.BlockSpec((None, None, 1, D),
                        lambda i, *_: (layer, jnp.minimum((i * tm) // T, nb), 0, k))


def _split_rows(lat, ctx, tm, width, col=0):
    if ctx is None:
        return (lat,), [pl.BlockSpec((tm, width), lambda i, *_: (i, col))], lambda i, ref: ref[...]
    n_lat = lat.shape[0] // tm
    specs = [pl.BlockSpec((tm, width), lambda i, *_: (jnp.minimum(i, n_lat - 1), col)),
             pl.BlockSpec((tm, width), lambda i, *_: (jnp.maximum(i - n_lat, 0), col),
                          pipeline_mode=pl.Buffered(1))]
    return (lat, ctx), specs, lambda i, lat_ref, ctx_ref: jnp.where(i < n_lat, lat_ref[...], ctx_ref[...])


def _nmm_kernel(*refs, n_x, pick):
    sh_ref, sc_ref, w_ref, o_ref, lhs_ref = refs[n_x:]

    @pl.when(pl.program_id(1) == 0)
    def _():
        x = pick(pl.program_id(0), *refs[:n_x])
        xn = x * lax.rsqrt(jnp.mean(x * x, axis=-1, keepdims=True) + NORM_EPS)
        lhs_ref[...] = (xn * (1.0 + sc_ref[...]) + sh_ref[...]).astype(BF16)

    o_ref[...] = _dot(lhs_ref[...], w_ref[...].astype(BF16)).astype(o_ref.dtype)


def _norm_mod_matmul(x_lat, x_ctx, mod, layer, w, nb, tm, tn):
    rows = x_lat.shape[0] + (0 if x_ctx is None else x_ctx.shape[0])
    n = w.shape[1]
    x_args, x_specs, pick = _split_rows(x_lat, x_ctx, tm, D)
    if x_ctx is None and tm > 512:
        x_specs = [pl.BlockSpec((tm, D), lambda i, j: (i, 0), pipeline_mode=pl.Buffered(1))]
    return pl.pallas_call(
        functools.partial(_nmm_kernel, n_x=len(x_args), pick=pick),
        grid=(rows // tm, n // tn),
        in_specs=x_specs + [_mod_spec(layer, 0, tm, nb), _mod_spec(layer, 1, tm, nb),
                            pl.BlockSpec((D, tn), lambda i, j: (0, j))],
        out_specs=pl.BlockSpec((tm, tn), lambda i, j: (i, j)),
        out_shape=jax.ShapeDtypeStruct((rows, n), F32),
        scratch_shapes=[pltpu.VMEM((tm, D), BF16)],
        compiler_params=_params(("parallel", "arbitrary"), V7X_VMEM_LIMIT),
        name=f"in_proj_{layer}",
    )(*x_args, mod, mod, w)


def _mlp_kernel(x_ref, sh_ref, sc_ref, gt_ref, w1_ref, w2_ref, fn_ref, o_ref, lhs_ref, *, final):
    f = pl.program_id(1)

    @pl.when(f == 0)
    def _():
        x = x_ref[...]
        xn = x * lax.rsqrt(jnp.mean(x * x, axis=-1, keepdims=True) + NORM_EPS)
        lhs_ref[...] = (xn * (1.0 + sc_ref[...]) + sh_ref[...]).astype(BF16)
        o_ref[...] = jnp.zeros_like(o_ref)

    h = jnp.maximum(_dot(lhs_ref[...], w1_ref[...].astype(BF16)), 0.0)
    o_ref[...] += _dot((h * h).astype(BF16), w2_ref[...].astype(BF16))

    @pl.when(f == pl.num_programs(1) - 1)
    def _():
        y = x_ref[...] + gt_ref[...] * o_ref[...]
        if final:
            y = y * lax.rsqrt(jnp.mean(y * y, axis=-1, keepdims=True) + NORM_EPS) * fn_ref[...]
        o_ref[...] = y


def _mlp(xc, rows, mod, layer, w1, w2, final_norm, nb, final):
    tm, tf = 1024, 512
    return pl.pallas_call(
        functools.partial(_mlp_kernel, final=final),
        grid=(rows // tm, D_FF // tf),
        in_specs=[pl.BlockSpec((tm, D), lambda i, f: (i, 0), pipeline_mode=pl.Buffered(1)),
                  _mod_spec(layer, 3, tm, nb), _mod_spec(layer, 4, tm, nb), _mod_spec(layer, 5, tm, nb),
                  pl.BlockSpec((None, D, tf), lambda i, f: (layer, 0, f)),
                  pl.BlockSpec((None, tf, D), lambda i, f: (layer, f, 0)),
                  pl.BlockSpec((1, D), lambda i, f: (0, 0))],
        out_specs=pl.BlockSpec((tm, D), lambda i, f: (i, 0)),
        out_shape=jax.ShapeDtypeStruct((rows, D), F32),
        scratch_shapes=[pltpu.VMEM((tm, D), BF16)],
        compiler_params=_params(("parallel", "arbitrary"), V7X_VMEM_LIMIT),
        name=f"mlp_{layer}",
    )(xc, mod, mod, mod, w1, w2, final_norm)


def _oproj_kernel(*refs, n_mix, mix, n_x, pick):
    x_refs = refs[n_mix:n_mix + n_x]
    gt_ref, wa_ref, wb_ref, o_ref = refs[n_mix + n_x:]
    ya, yb = mix(*refs[:n_mix])
    y = _dot(ya.astype(BF16), wa_ref[...]) + _dot(yb.astype(BF16), wb_ref[...])
    o_ref[...] = pick(pl.program_id(0), *x_refs) + gt_ref[...] * y


def _out_proj(mix, mix_args, mix_specs, x_lat, x_ctx, rows, mod, layer, wa, wb, nb):
    tm = OPROJ_TM
    half = wa.shape[0]
    x_args, x_specs, pick = _split_rows(x_lat, x_ctx, tm, D)
    return pl.pallas_call(
        functools.partial(_oproj_kernel, n_mix=len(mix_args), mix=mix, n_x=len(x_args), pick=pick),
        grid=(rows // tm,),
        in_specs=list(mix_specs) + x_specs + [
            _mod_spec(layer, 2, tm, nb),
            pl.BlockSpec((half, D), lambda i: (0, 0), pipeline_mode=pl.Buffered(1)),
            pl.BlockSpec((half, D), lambda i: (0, 0), pipeline_mode=pl.Buffered(1))],
        out_specs=pl.BlockSpec((tm, D), lambda i: (i, 0)),
        out_shape=jax.ShapeDtypeStruct((rows, D), F32),
        compiler_params=_params(("parallel",), V7X_VMEM_LIMIT),
        name=f"out_proj_{layer}",
    )(*mix_args, *x_args, mod, wa, wb)


def _shifted(u, prev_row, next_row):
    tm = u.shape[0]
    row = lax.broadcasted_iota(jnp.int32, u.shape, 0)
    up = jnp.where(row == 0, prev_row, pltpu.roll(u, 1, axis=0))
    un = jnp.where(row == tm - 1, next_row, pltpu.roll(u, tm - 1, axis=0))
    return 0.5 * (up + un)


def _rwkv_prep_kernel(u_ref, ul_ref, up_ref, un_ref, ulp_ref, uln_ref, mu_ref, mul_ref, w0_ref, wup_ref,
                      a0_ref, aup_ref, gup_ref, kk_ref, ka_ref, rk_ref, gd_ref, gu_ref,
                      r_out, k_out, v_out, a_out, b_out, lw_out, g_out, bonus_out):
    u = u_ref[...]
    u = u + mu_ref[...] * (_shifted(u, up_ref[...], un_ref[...]) - u)
    ul = ul_ref[...]
    ul = ul + mul_ref[...] * (_shifted(ul, ulp_ref[...], uln_ref[...]) - ul)
    r, k, v = u[:, 0:RW], u[:, RW:2 * RW], u[:, 2 * RW:3 * RW]

    th = jnp.tanh(ul).astype(BF16)
    for d in range(2):
        w_log = -_softplus(-(w0_ref[d:d + 1, :] + _dot(th, wup_ref[d]))) - 0.5
        lw_out[d] = -jnp.exp(w_log)
    a = _sigmoid(a0_ref[...] + _dot(ul.astype(BF16), aup_ref[...]))
    g_out[...] = _dot(_sigmoid(ul).astype(BF16), gup_ref[...])

    kk = k * kk_ref[...]
    nrm = jnp.sqrt(_seg_sum(kk * kk, gd_ref[...], gu_ref[...]))
    kk = kk / jnp.maximum(nrm, 1e-12)
    k = k * (1.0 + (a - 1.0) * ka_ref[...])
    r_out[...] = r
    k_out[...] = k
    v_out[...] = v
    a_out[...] = -kk
    b_out[...] = kk * a
    bonus_out[...] = _seg_sum(r * k * rk_ref[...], gd_ref[...], gu_ref[...]) * v


def _seq_halo(u, tm, nb):
    rows = u.shape[0]
    nblk = rows // tm
    starts = np.arange(nblk) * tm
    seq_len = np.where(starts < nb * T, T, C)
    seq_off = np.where(starts < nb * T, starts % T, (starts - nb * T) % C)
    has_prev = seq_off > 0
    has_next = seq_off + tm < seq_len
    prev_idx = np.where(has_prev, starts - 1, 0)
    next_idx = np.where(has_next, starts + tm, 0)
    up = jnp.where(has_prev[:, None], u[prev_idx], 0.0)
    un = jnp.where(has_next[:, None], u[next_idx], 0.0)
    return up[:, None, :], un[:, None, :]


def _rwkv_prepare(u, nb, mu, w0, w_up, a0, a_up, g_up, k_k, k_a, r_k):
    rows = u.shape[0]
    tm = 256
    nblk = rows // tm
    up, un = _seq_halo(u, tm, nb)
    pad = jnp.zeros((LORA,), F32)
    mu_rkv = mu[:3 * RW].reshape(1, 3 * RW)
    mu_l = jnp.concatenate([mu[3 * RW:], pad]).reshape(1, 4 * LORA)

    def lora_w(w, slot):
        z = jnp.zeros((4 * LORA, RW), F32)
        return z.at[slot * LORA:(slot + 1) * LORA].set(w).astype(BF16)

    wup = jnp.stack([lora_w(w_up[0], 0), lora_w(w_up[1], 0)])
    aup = lora_w(a_up, 1)
    gup = lora_w(g_up, 2)
    head = np.arange(RW) // HD
    g_down = jnp.asarray(head[:, None] == np.arange(128)[None, :], BF16)
    g_upm = jnp.asarray(np.arange(128)[:, None] == head[None, :], BF16)

    row = lambda w: pl.BlockSpec((tm, w), lambda i: (i, 0))
    vec = lambda w: pl.BlockSpec((1, w), lambda i: (0, 0))
    full = lambda *s: pl.BlockSpec(s, lambda i: (0,) * len(s))
    out_sd = jax.ShapeDtypeStruct((rows, RW), F32)
    outs = pl.pallas_call(
        _rwkv_prep_kernel,
        grid=(nblk,),
        in_specs=[pl.BlockSpec((tm, 3 * RW), lambda i: (i, 0)),
                  pl.BlockSpec((tm, 4 * LORA), lambda i: (i, AB_LORA // (4 * LORA))),
                  pl.BlockSpec((None, 1, 3 * RW), lambda i: (i, 0, 0)),
                  pl.BlockSpec((None, 1, 3 * RW), lambda i: (i, 0, 0)),
                  pl.BlockSpec((None, 1, 4 * LORA), lambda i: (i, 0, AB_LORA // (4 * LORA))),
                  pl.BlockSpec((None, 1, 4 * LORA), lambda i: (i, 0, AB_LORA // (4 * LORA))),
                  vec(3 * RW), vec(4 * LORA), full(2, RW), full(2, 4 * LORA, RW),
                  vec(RW), full(4 * LORA, RW), full(4 * LORA, RW), vec(RW), vec(RW), vec(RW),
                  full(RW, 128), full(128, RW)],
        out_specs=[row(RW), row(RW), row(RW), row(RW), row(RW),
                   pl.BlockSpec((2, tm, RW), lambda i: (0, i, 0)), row(RW), row(RW)],
        out_shape=[out_sd, out_sd, out_sd, out_sd, out_sd,
                   jax.ShapeDtypeStruct((2, rows, RW), F32), out_sd, out_sd],
        compiler_params=_params(("parallel",), V7X_VMEM_LIMIT),
        name="rwkv_prepare",
    )(u, u, up, un, up, un, mu_rkv, mu_l, w0, wup, a0.reshape(1, RW), aup, gup,
      k_k.reshape(1, RW), k_a.reshape(1, RW), r_k.reshape(1, RW), g_down, g_upm)
    return outs


def _chunk_block(d, b, s, nb):
    nctx, nlat = C // CHUNK, T // CHUNK
    pos_c = jnp.where(d == 0, s, nctx - 1 - s)
    pos_l = jnp.where(d == 0, s - nctx, nlat - 1 - (s - nctx))
    return jnp.where(s < nctx, nb * nlat + b * nctx + pos_c, b * nlat + pos_l)


def _lane_stack(x, hw):
    head = lax.broadcasted_iota(jnp.int32, x.shape, 1) // hw
    return jnp.concatenate([jnp.where(head == h, x, 0.0) for h in range(GROUP // hw)], axis=0)


def _fold_rows(x, n):
    out = x[0:n]
    for h in range(1, x.shape[0] // n):
        out = out + x[h * n:(h + 1) * n]
    return out


def _block_masks(size, blk, d):
    row = lax.broadcasted_iota(jnp.int32, (size, size), 0)
    col = lax.broadcasted_iota(jnp.int32, (size, size), 1)
    rr, cc = row % blk, col % blk
    same = (row // blk) == (col // blk)
    if d == 0:
        return cc <= rr, cc < rr, same
    return cc >= rr, cc > rr, same


def _rwkv_scan_kernel(*refs):
    in_refs = (refs[0:6], refs[6:12])
    y_refs = refs[12:14]
    s_ref = refs[14]

    @pl.when(pl.program_id(1) == 0)
    def _():
        s_ref[...] = jnp.zeros_like(s_ref)

    n = CHUNK
    ngrp = RW // GROUP
    nch = 2 * ngrp
    rrow = lax.broadcasted_iota(jnp.int32, (n, GROUP), 0)
    rcol = lax.broadcasted_iota(jnp.int32, (n, GROUP), 1) % n
    eye_row = jnp.where(rrow == rcol, 1.0, 0.0)
    incl_row = (rcol <= rrow, rcol >= rrow)
    strict_row = (rcol < rrow, rcol > rrow)
    same = _block_masks(GROUP, n, 0)[2]

    def block_diag(x_row):
        xb = x_row.astype(BF16)
        return jnp.where(same, jnp.concatenate([xb] * (GROUP // n), axis=0), jnp.zeros((), BF16))

    def stack(x):
        return _lane_stack(x.astype(BF16), HD)

    pre = []
    for d in range(2):
        r, k, v, a, b, lw = [ref[...] for ref in in_refs[d]]
        tri = jnp.where(_block_masks(n, n, d)[0], 1.0, 0.0).astype(BF16)
        c = _tri_dot(tri, lw)
        c_last = c[n - 1:n] if d == 0 else c[0:1]
        p_inv = jnp.exp(-c)
        p_end = jnp.exp(c_last - c)
        pre.append(dict(at=a * jnp.exp(c - lw), rt=r * jnp.exp(c), bt=b * p_inv, kt=k * p_inv,
                        bp=b * p_end, kp=k * p_end, v=v, dec=jnp.exp(c_last)))
    chains = [(d, g) for d in range(2) for g in range(ngrp)]
    sl = lambda g: slice(g * GROUP, (g + 1) * GROUP)
    part = lambda d, g, name: pre[d][name][:, sl(g)]

    a_st = [stack(part(d, g, "at")) for d, g in chains]
    v_st = [stack(part(d, g, "v")) for d, g in chains]
    gram = [_dot_nt(jnp.concatenate([part(d, g, "at"), part(d, g, "rt")], axis=0).astype(BF16),
                    jnp.concatenate([stack(part(d, g, "bt")), stack(part(d, g, "kt"))], axis=0))
            for d, g in chains]
    a_ab = [jnp.where(strict_row[d], gram[i][0:n, 0:GROUP], 0.0) for i, (d, g) in enumerate(chains)]
    a_ak = [jnp.where(strict_row[d], gram[i][0:n, GROUP:], 0.0).astype(BF16) for i, (d, g) in enumerate(chains)]
    a_r = [jnp.where(jnp.concatenate([incl_row[d]] * 2, axis=1), gram[i][n:2 * n, :], 0.0).astype(BF16)
           for i, (d, g) in enumerate(chains)]
    av = [_dot(a_ak[i], v_st[i]) for i in range(nch)]
    tm = [eye_row + m for m in a_ab]
    pw = [_dot(m.astype(BF16), block_diag(m)) for m in a_ab]
    for _ in range(4):
        both = [_dot(jnp.concatenate([t, p], axis=0).astype(BF16), block_diag(p)) for t, p in zip(tm, pw)]
        tm = [t + x[0:n] for t, x in zip(tm, both)]
        pw = [x[n:2 * n] for x in both]
    tm = [t + _dot(t.astype(BF16), block_diag(p)) for t, p in zip(tm, pw)]
    tx = [_dot(tm[i].astype(BF16), jnp.concatenate([a_st[i], stack(av[i])], axis=1))
          for i in range(nch)]
    st = [s_ref[d, g] for d, g in chains]
    uy0 = [_dot_nt(jnp.concatenate([tx[i][:, 0:GROUP], part(d, g, "rt")], axis=0).astype(BF16),
                   st[i].astype(BF16)) for i, (d, g) in enumerate(chains)]
    u = [uy0[i][0:n] + tx[i][:, GROUP:] for i in range(nch)]
    for i, (d, g) in enumerate(chains):
        y_refs[d][:, sl(g)] = uy0[i][n:2 * n] + _dot(a_r[i], jnp.concatenate([stack(u[i]), v_st[i]], axis=0))
    for i, (d, g) in enumerate(chains):
        upd = _dot_tn(jnp.concatenate([u[i], part(d, g, "v")], axis=0).astype(BF16),
                      jnp.concatenate([part(d, g, "bp"), part(d, g, "kp")], axis=0).astype(BF16))
        s_ref[d, g] = st[i] * part(d, g, "dec") + jnp.where(same, upd, 0.0)


def _rwkv_scan(r, k, v, a, b, lw, nb):
    rows = r.shape[0]
    steps = (T + C) // CHUNK
    in_specs, args = [], []
    for d in range(2):
        blk = lambda bb, s, d=d: (_chunk_block(d, bb, s, nb), 0)
        in_specs += [pl.BlockSpec((CHUNK, RW), blk)] * 5
        in_specs.append(pl.BlockSpec((None, CHUNK, RW), lambda bb, s, d=d: (d, _chunk_block(d, bb, s, nb), 0)))
        args += [r, k, v, a, b, lw]
    out_sd = jax.ShapeDtypeStruct((rows, RW), F32)
    return pl.pallas_call(
        _rwkv_scan_kernel,
        grid=(nb, steps),
        in_specs=in_specs,
        out_specs=[pl.BlockSpec((CHUNK, RW), lambda bb, s, d=d: (_chunk_block(d, bb, s, nb), 0)) for d in range(2)],
        out_shape=[out_sd, out_sd],
        scratch_shapes=[pltpu.VMEM((2, RW // GROUP, GROUP, GROUP), F32)],
        compiler_params=_params(("parallel", "arbitrary"), V7X_VMEM_LIMIT),
        name="rwkv_scan",
    )(*args)


def _ab_mix(y0_ref, y1_ref, bonus_ref, g_ref, lnw_ref, lnb_ref, gd_ref, gu_ref, *yb_refs, pick):
    y = y0_ref[...] + y1_ref[...]
    mu = _seg_sum(y, gd_ref[...], gu_ref[...]) * (1.0 / HD)
    yc = y - mu
    var = _seg_sum(yc * yc, gd_ref[...], gu_ref[...]) * (1.0 / HD)
    yn = yc * lax.rsqrt(var + RWKV_GN_EPS)
    return (yn * lnw_ref[...] + lnb_ref[...] + bonus_ref[...]) * g_ref[...], pick(pl.program_id(0), *yb_refs)


def _ab_mix_inputs(y0, y1, bonus, g, ln_w, ln_b, yb_lat, yb_ctx):
    tm = OPROJ_TM
    head = np.arange(RW) // HD
    g_down = jnp.asarray(head[:, None] == np.arange(128)[None, :], BF16)
    g_upm = jnp.asarray(np.arange(128)[:, None] == head[None, :], BF16)
    row = pl.BlockSpec((tm, RW), lambda i: (i, 0))
    vec = pl.BlockSpec((1, RW), lambda i: (0, 0))
    yb_args, yb_specs, pick = _split_rows(yb_lat, yb_ctx, tm, 1024)
    specs = [row, row, row, row, vec, vec,
             pl.BlockSpec((RW, 128), lambda i: (0, 0)), pl.BlockSpec((128, RW), lambda i: (0, 0))] + yb_specs
    args = (y0, y1, bonus, g, ln_w.reshape(1, RW), ln_b.reshape(1, RW), g_down, g_upm) + yb_args
    return functools.partial(_ab_mix, pick=pick), args, specs


def _kv_block(i, nb):
    per = T // 256
    return jnp.where(i < nb * per, i // per, i - nb * per), jnp.where(i < nb * per, i % per, per)


def _swa_prep_kernel(k_ref, v_ref, cos_ref, sin_ref, ko_ref, vo_ref):
    cosl, sinl = cos_ref[...], sin_ref[...]
    lane = lax.broadcasted_iota(jnp.int32, cosl.shape, 1)
    low = lane < HD
    for src, dst, rope in ((k_ref, ko_ref, True), (v_ref, vo_ref, False)):
        for t in range(2):
            x = src[:, t * 128:(t + 1) * 128]
            if rope:
                x = _rope(x, cosl, sinl)
            swapped = pltpu.roll(x, HD, axis=1)
            dst[:, (2 * t) * 128:(2 * t + 1) * 128] = jnp.where(low, x, swapped).astype(BF16)
            dst[:, (2 * t + 1) * 128:(2 * t + 2) * 128] = jnp.where(low, swapped, x).astype(BF16)


def _swa_prep(u, cosl, sinl, nb):
    rows = u.shape[0]
    tm = 256
    kv_out = pl.BlockSpec((None, tm, 512), lambda i: (*_kv_block(i, nb), 0))
    sd = jax.ShapeDtypeStruct((nb, T + C, 512), BF16)
    return pl.pallas_call(
        _swa_prep_kernel,
        grid=(rows // tm,),
        in_specs=[pl.BlockSpec((tm, 256), lambda i: (i, AB_KS // 256)),
                  pl.BlockSpec((tm, 256), lambda i: (i, AB_VS // 256)),
                  pl.BlockSpec((tm, 128), lambda i: (i, 0)),
                  pl.BlockSpec((tm, 128), lambda i: (i, 0))],
        out_specs=[kv_out, kv_out],
        out_shape=[sd, sd],
        compiler_params=_params(("parallel",), V7X_VMEM_LIMIT),
        name="swa_prep",
    )(u, u, cosl, sinl)


def _diff_prep_kernel(k_ref, v_ref, cos_ref, sin_ref, ko_ref, vo_ref):
    cosl, sinl = cos_ref[...], sin_ref[...]
    for t in range(8):
        sl = slice(t * 128, (t + 1) * 128)
        ko_ref[:, sl] = _rope(k_ref[:, sl], cosl, sinl).astype(BF16)
    vo_ref[...] = v_ref[...].astype(BF16)


def _diff_prep(u, cosl, sinl, nb):
    rows = u.shape[0]
    tm = 256
    kv_out = pl.BlockSpec((None, tm, 1024), lambda i: (*_kv_block(i, nb), 0))
    sd = jax.ShapeDtypeStruct((nb, T + C, 1024), BF16)
    return pl.pallas_call(
        _diff_prep_kernel,
        grid=(rows // tm,),
        in_specs=[pl.BlockSpec((tm, 1024), lambda i: (i, 1)),
                  pl.BlockSpec((tm, 1024), lambda i: (i, 2)),
                  pl.BlockSpec((tm, 128), lambda i: (i, 0)),
                  pl.BlockSpec((tm, 128), lambda i: (i, 0))],
        out_specs=[kv_out, kv_out],
        out_shape=[sd, sd],
        compiler_params=_params(("parallel",), V7X_VMEM_LIMIT),
        name="diff_prep",
    )(u, u, cosl, sinl)


def _swa_kernel(q_ref, cos_ref, sin_ref, k_ref, v_ref, sink_ref, o_ref, *, ctx_mode):
    n = pl.program_id(1)
    cosl, sinl = cos_ref[...], sin_ref[...]
    if not ctx_mode:
        start = pl.multiple_of(jnp.clip((n - 1) * QB, 0, T - 3 * QB), QB)
        kpos = start + lax.broadcasted_iota(jnp.int32, (2 * QB, 3 * QB), 1)
        qpos = n * QB + lax.broadcasted_iota(jnp.int32, (2 * QB, 3 * QB), 0) % QB
        valid = jnp.abs(kpos - qpos) <= SWA_WINDOW
    low = lax.broadcasted_iota(jnp.int32, (QB, 128), 1) < HD
    top = lax.broadcasted_iota(jnp.int32, (2 * QB, 1), 0) < QB
    tile = lambda j: slice((j // 2) * 128, (j // 2 + 1) * 128)
    qs, sink = [], []
    for j in range(8):
        q = _rope(q_ref[:, j * 128:(j + 1) * 128], cosl, sinl) * (HD ** -0.5 * LOG2E)
        qs.append(jnp.concatenate([jnp.where(low, q, 0.0), jnp.where(low, 0.0, q)], axis=0).astype(BF16))
        sink.append(jnp.where(top, sink_ref[2 * j:2 * j + 1, 0:1], sink_ref[2 * j + 1:2 * j + 2, 0:1]) * LOG2E)
    s_c = [_dot_nt(qs[j], k_ref[T:T + C, tile(j)]) for j in range(8)]
    m = [jnp.maximum(jnp.max(s, axis=-1, keepdims=True), sk) for s, sk in zip(s_c, sink)]
    if not ctx_mode:
        s_w = [jnp.where(valid, _dot_nt(qs[j], k_ref[pl.ds(start, 3 * QB), tile(j)]), -jnp.inf) for j in range(8)]
        m = [jnp.maximum(mm, jnp.max(s, axis=-1, keepdims=True)) for mm, s in zip(m, s_w)]
    p_c = [jnp.exp2(s - mm) for s, mm in zip(s_c, m)]
    den = [jnp.sum(p, axis=-1, keepdims=True) + jnp.exp2(sk - mm) for p, sk, mm in zip(p_c, sink, m)]
    pv = [_dot(p.astype(BF16), v_ref[T:T + C, tile(j)]) for j, p in enumerate(p_c)]
    if not ctx_mode:
        p_w = [jnp.exp2(s - mm) for s, mm in zip(s_w, m)]
        den = [dd + jnp.sum(p, axis=-1, keepdims=True) for dd, p in zip(den, p_w)]
        pv = [x + _dot(p.astype(BF16), v_ref[pl.ds(start, 3 * QB), tile(j)]) for j, (x, p) in enumerate(zip(pv, p_w))]
    for j in range(8):
        o = pv[j] / den[j]
        o_ref[:, j * 128:(j + 1) * 128] = jnp.where(low, o[0:QB], o[QB:2 * QB])


def _swa(u, cosl, sinl, kpad, vpad, sink, nb, ctx_mode):
    nq = (C if ctx_mode else T) // QB
    base = nb * (T // QB) if ctx_mode else 0
    rows = nb * nq * QB
    rowblk = lambda b, n: base + b * nq + n
    kv = pl.BlockSpec((None, T + C, 512), lambda b, n: (b, 0, 0))
    return pl.pallas_call(
        functools.partial(_swa_kernel, ctx_mode=ctx_mode),
        grid=(nb, nq),
        in_specs=[pl.BlockSpec((QB, 1024), lambda b, n: (rowblk(b, n), AB_Q // 1024)),
                  pl.BlockSpec((QB, 128), lambda b, n: (rowblk(b, n), 0)),
                  pl.BlockSpec((QB, 128), lambda b, n: (rowblk(b, n), 0)),
                  kv, kv, pl.BlockSpec((16, 128), lambda b, n: (0, 0))],
        out_specs=pl.BlockSpec((QB, 1024), lambda b, n: (b * nq + n, 0)),
        out_shape=jax.ShapeDtypeStruct((rows, 1024), F32),
        compiler_params=_params(("parallel", "arbitrary"), V7X_VMEM_LIMIT),
        name="swa_ctx" if ctx_mode else "swa_latent",
    )(u, cosl, sinl, kpad, vpad, sink)


def _diff_kernel(q_ref, cos_ref, sin_ref, k_ref, v_ref, lam_ref, sub_ref, o_ref):
    cosl, sinl = cos_ref[...], sin_ref[...]
    lam = lam_ref[...]
    nq = q_ref.shape[0]
    low = lax.broadcasted_iota(jnp.int32, (nq, 128), 1) < HD
    tile = lambda h: slice(h * 128, (h + 1) * 128)

    def logits(h):
        q = _rope(q_ref[:, tile(h)], cosl, sinl) * (HD ** -0.5 * LOG2E)
        qs = jnp.concatenate([jnp.where(low, q, 0.0), jnp.where(low, 0.0, q)], axis=0).astype(BF16)
        return _dot_nt(qs, k_ref[:, tile(h)])

    s_next = logits(0)
    for h in range(8):
        s = s_next
        if h + 1 < 8:
            s_next = logits(h + 1)
        e = jnp.exp2(s - jnp.max(s, axis=-1, keepdims=True))
        inv = 1.0 / jnp.sum(e, axis=-1, keepdims=True)
        o2 = _dot(e.astype(BF16), v_ref[:, tile(h)])
        o = o2[0:nq] * inv[0:nq] - o2[nq:2 * nq] * (lam[:, 0:1] * inv[nq:2 * nq])
        o = o * lax.rsqrt(jnp.mean(o * o, axis=-1, keepdims=True) + 1e-5)
        o_ref[:, tile(h)] = o * sub_ref[...]


def _diff_attn(u, cosl, sinl, kb, vb, lam, sub, nb):
    QB = DIFF_QB
    nq = T // QB
    kv = pl.BlockSpec((None, T + C, 1024), lambda b, n: (b, 0, 0))
    vec = pl.BlockSpec((1, 128), lambda b, n: (0, 0))
    return pl.pallas_call(
        _diff_kernel,
        grid=(nb, nq),
        in_specs=[pl.BlockSpec((QB, 1024), lambda b, n: (b * nq + n, 0)),
                  pl.BlockSpec((QB, 128), lambda b, n: (b * nq + n, 0)),
                  pl.BlockSpec((QB, 128), lambda b, n: (b * nq + n, 0)),
                  kv, kv, vec, vec],
        out_specs=pl.BlockSpec((QB, 1024), lambda b, n: (b * nq + n, 0)),
        out_shape=jax.ShapeDtypeStruct((nb * T, 1024), F32),
        compiler_params=_params(("parallel", "arbitrary"), V7X_VMEM_LIMIT),
        name="diff_attn",
    )(u, cosl, sinl, kb, vb, lam, sub)


def _hgrn_exact_att(q, kk, bcum, tmp_ref):
    n = CHUNK
    tmp_ref[0] = bcum
    tmp_ref[1] = kk
    coli = lax.broadcasted_iota(jnp.int32, (n, 2 * n), 1)

    def body(s, acc):
        bs = tmp_ref[0, pl.ds(s, 1), :]
        ks = tmp_ref[1, pl.ds(s, 1), :]
        w = q * jnp.exp(jnp.minimum(bcum - bs, 0.0)) * ks
        c0 = jnp.sum(w[:, 0:HG_DK], axis=-1, keepdims=True)
        c1 = jnp.sum(w[:, HG_DK:], axis=-1, keepdims=True)
        return acc + jnp.concatenate([jnp.where(coli == s, c0, 0.0), jnp.where(coli == s + n, c1, 0.0)], axis=0)

    return lax.fori_loop(0, n, body, jnp.zeros((2 * n, 2 * n), F32))


def _hgrn_scan_kernel(q0_ref, z0_ref, i0_ref, lb0_ref, q1_ref, z1_ref, i1_ref, lb1_ref,
                      o0_ref, o1_ref, s_ref, g_ref, tmp_ref):
    in_refs = ((q0_ref, z0_ref, i0_ref, lb0_ref), (q1_ref, z1_ref, i1_ref, lb1_ref))
    o_refs = (o0_ref, o1_ref)

    @pl.when(pl.program_id(1) == 0)
    def _():
        s_ref[...] = jnp.zeros_like(s_ref)

    n = CHUNK
    ngrp = 1024 // GROUP
    nsub = n // SUB
    pre = []
    for d in range(2):
        q, z, v, lb = [ref[...] for ref in in_refs[d]]
        logf = jnp.log(lb + (1.0 - lb) * _sigmoid(z))
        kk = (1.0 - lb) * _sigmoid(-z)
        tri = jnp.where(_block_masks(n, n, d)[0], 1.0, 0.0).astype(BF16)
        bcum = _tri_dot(tri, logf)
        b_last = bcum[n - 1:n] if d == 0 else bcum[0:1]
        b_excl = bcum - logf
        qh, kh = [], []
        for sb in range(nsub):
            lo, hi = sb * SUB, (sb + 1) * SUB
            beta = b_excl[lo:lo + 1] if d == 0 else b_excl[hi - 1:hi]
            qh.append(q[lo:hi] * jnp.exp(bcum[lo:hi] - beta))
            kh.append(kk * jnp.exp(beta - bcum))
        pre.append(dict(q=q, v=v, kk=kk, bcum=bcum, qh=qh, kh=kh, qe=q * jnp.exp(bcum),
                        ke=kk * jnp.exp(b_last - bcum), dec=jnp.exp(b_last), min_logf=jnp.min(logf)))
    chains = [(d, g) for d in range(2) for g in range(ngrp)]
    sl = lambda g: slice(g * GROUP, (g + 1) * GROUP)

    for i, (d, g) in enumerate(chains):
        rows = [_dot_nt(_lane_stack(pre[d]["qh"][sb][:, sl(g)], HG_DK).astype(BF16),
                        _lane_stack(pre[d]["kh"][sb][:, sl(g)], HG_DK).astype(BF16)) for sb in range(nsub)]
        g_ref[i] = jnp.concatenate([rows[sb][h * SUB:(h + 1) * SUB] for h in range(2) for sb in range(nsub)], axis=0)

    @pl.when(jnp.minimum(pre[0]["min_logf"], pre[1]["min_logf"]) < -(HG_CLAMP / SUB))
    def _():
        for i, (d, g) in enumerate(chains):
            g_ref[i] = _hgrn_exact_att(pre[d]["q"][:, sl(g)], pre[d]["kk"][:, sl(g)], pre[d]["bcum"][:, sl(g)],
                                       tmp_ref)

    st = [s_ref[d, g] for d, g in chains]
    inter = [_dot_nt(pre[d]["qe"][:, sl(g)].astype(BF16), st[i].astype(BF16)) for i, (d, g) in enumerate(chains)]
    for i, (d, g) in enumerate(chains):
        incl, _, same = _block_masks(2 * n, n, d)
        att = jnp.where(incl & same, g_ref[i], 0.0).astype(BF16)
        v_st = _lane_stack(pre[d]["v"][:, sl(g)], HG_DK).astype(BF16)
        o_refs[d][:, sl(g)] = inter[i] + _fold_rows(_dot(att, v_st), n)
    for i, (d, g) in enumerate(chains):
        upd = _dot_tn(pre[d]["v"][:, sl(g)].astype(BF16), pre[d]["ke"][:, sl(g)].astype(BF16))
        s_ref[d, g] = st[i] * pre[d]["dec"][:, sl(g)] + jnp.where(_block_masks(GROUP, HG_DK, d)[2], upd, 0.0)


def _hgrn_scan(u, lb, nb):
    rows = u.shape[0]
    steps = (T + C) // CHUNK
    in_specs, args = [], []
    for d in range(2):
        col = lambda j, d=d: pl.BlockSpec((CHUNK, 1024), lambda bb, s: (_chunk_block(d, bb, s, nb), j))
        in_specs += [col(3), col(4 + d), col(6), pl.BlockSpec((None, 1, 1024), lambda bb, s, d=d: (d, 0, 0))]
        args += [u, u, u, lb]
    out_sd = jax.ShapeDtypeStruct((rows, 1024), F32)
    ngrp = 1024 // GROUP
    return pl.pallas_call(
        _hgrn_scan_kernel,
        grid=(nb, steps),
        in_specs=in_specs,
        out_specs=[pl.BlockSpec((CHUNK, 1024), lambda bb, s, d=d: (_chunk_block(d, bb, s, nb), 0)) for d in range(2)],
        out_shape=[out_sd, out_sd],
        scratch_shapes=[pltpu.VMEM((2, ngrp, GROUP, GROUP), F32),
                        pltpu.VMEM((2 * ngrp, 2 * CHUNK, 2 * CHUNK), F32),
                        pltpu.VMEM((2, CHUNK, GROUP), F32)],
        compiler_params=_params(("parallel", "arbitrary"), V7X_VMEM_LIMIT),
        name="hgrn_scan",
    )(*args)


def _cd_mix(yc_ref, o0_ref, o1_ref, g_ref, gn_ref):
    g = g_ref[...]
    heads = []
    for h in range(8):
        sl = slice(h * 128, (h + 1) * 128)
        o = o0_ref[:, sl] + o1_ref[:, sl]
        o = o * lax.rsqrt(jnp.mean(o * o, axis=-1, keepdims=True) + NORM_EPS) * gn_ref[...]
        gh = g[:, sl]
        heads.append(o * (gh * _sigmoid(gh)))
    return yc_ref[...], jnp.concatenate(heads, axis=1)


def _cd_mix_inputs(yc, o0, o1, u, gnorm):
    tm = OPROJ_TM
    row = pl.BlockSpec((tm, 1024), lambda i: (i, 0))
    specs = [row, row, row, pl.BlockSpec((tm, 1024), lambda i: (i, 7)), pl.BlockSpec((1, 128), lambda i: (0, 0))]
    return (yc, o0, o1, u, gnorm.reshape(1, 128)), specs


def _rope_tables(nb):
    t = np.arange(T)
    quarter = HD // 4
    inv = ROPE_THETA ** (-jnp.arange(quarter, dtype=F32) / quarter)
    rows = jnp.asarray(t // GRID_W, F32)
    cols = jnp.asarray(t % GRID_W, F32)
    ang = jnp.concatenate([rows[:, None] * inv, cols[:, None] * inv], axis=-1)
    cos, sin = jnp.cos(ang), jnp.sin(ang)
    cosl = jnp.tile(jnp.concatenate([cos, cos], axis=-1), (nb, 2))
    sinl = jnp.tile(jnp.concatenate([-sin, sin], axis=-1), (nb, 2))
    cosl = jnp.concatenate([cosl, jnp.ones((nb * C, 128), F32)], axis=0)
    sinl = jnp.concatenate([sinl, jnp.zeros((nb * C, 128), F32)], axis=0)
    return cosl, sinl


def kernel(x, c, ctx, c_ctx, ada_w, ada_b, mlp_w1, mlp_w2, final_norm, ab_w_in, ab_w_out, ab_mu, ab_w0,
           ab_w_up, ab_a0, ab_a_up, ab_g_up, ab_k_k, ab_k_a, ab_r_k, ab_ln_w, ab_ln_b, ab_sink,
           cd_w_in, cd_w_out, cd_lam, cd_subln, cd_gnorm, hgrn_lb_logits):
    nb = x.shape[0]
    assert x.shape == (nb, T, D) and ctx.shape == (nb, C, D) and nb < 8
    assert ada_w.shape[0] == 2, "one AB layer followed by one CD layer"
    rx = nb * T
    rows = rx + nb * C
    x2, c2 = x.reshape(rx, D), ctx.reshape(nb * C, D)
    cosl, sinl = _rope_tables(nb)

    s_in = jnp.zeros((8, D), F32).at[:nb].set(c).at[nb].set(c_ctx)
    mod = _ada(s_in, ada_w, ada_b)

    w = ab_w_in[0]
    w_in0 = jnp.concatenate([w[:, :3 * RW], w[:, 3 * RW + 3 * LORA:], w[:, 3 * RW:3 * RW + 3 * LORA],
                             jnp.zeros((D, LORA), F32)], axis=1).astype(BF16)
    u = _norm_mod_matmul(x2, c2, mod, 0, w_in0, nb, 512, AB_N // 2)
    r, k, v, a, b, lw, g, bonus = _rwkv_prepare(u, nb, ab_mu[0], ab_w0[0], ab_w_up[0], ab_a0[0], ab_a_up[0],
                                                ab_g_up[0], ab_k_k[0], ab_k_a[0], ab_r_k[0].reshape(RW))
    y0, y1 = _rwkv_scan(r, k, v, a, b, lw, nb)
    kpad, vpad = _swa_prep(u, cosl, sinl, nb)
    sink = jnp.broadcast_to(ab_sink[0][:, None], (16, 128))
    yb_lat = _swa(u, cosl, sinl, kpad, vpad, sink, nb, False)
    yb_ctx = _swa(u, cosl, sinl, kpad, vpad, sink, nb, True)
    w_out0 = ab_w_out[0].astype(BF16)
    mix, mix_args, mix_specs = _ab_mix_inputs(y0, y1, bonus, g, ab_ln_w[0], ab_ln_b[0], yb_lat, yb_ctx)
    xc = _out_proj(mix, mix_args, mix_specs, x2, c2, rows, mod, 0, w_out0[:RW], w_out0[RW:], nb)
    fn = final_norm.reshape(1, D)
    xc = _mlp(xc, rows, mod, 0, mlp_w1, mlp_w2, fn, nb, False)

    lam_init = 0.8 - 0.6 * math.exp(-0.3 * 1)
    lb_table = jnp.cumsum(jax.nn.softmax(hgrn_lb_logits.astype(F32), axis=0), axis=0)
    lb = (lb_table - lb_table[0])[1].reshape(2, 1, 1024)
    lf = cd_lam[0].astype(F32)
    lmb = jnp.exp(jnp.sum(lf[0] * lf[1])) - jnp.exp(jnp.sum(lf[2] * lf[3])) + lam_init
    u = _norm_mod_matmul(xc, None, mod, 1, cd_w_in[0], nb, 1024, 1024)
    kb, vb = _diff_prep(u, cosl, sinl, nb)
    yc = _diff_attn(u, cosl, sinl, kb, vb, jnp.full((1, 128), lmb, F32),
                    (cd_subln[0] * (1.0 - lam_init)).reshape(1, 128), nb)
    o0, o1 = _hgrn_scan(u, lb, nb)
    w_out1 = cd_w_out[0].astype(BF16)
    mix_args, mix_specs = _cd_mix_inputs(yc, o0, o1, u, cd_gnorm[0])
    xl = _out_proj(_cd_mix, mix_args, mix_specs, xc, None, rx, mod, 1, w_out1[:1024], w_out1[1024:], nb)
    out = _mlp(xl, rx, mod, 1, mlp_w1, mlp_w2, fn, nb, True)
    return out.reshape(nb, T, D)
```

```python
import functools
import math

import jax
import jax.numpy as jnp
import numpy as np
from jax import lax
from jax.experimental import pallas as pl
from jax.experimental.pallas import tpu as pltpu

F32 = jnp.float32
BF16 = jnp.bfloat16

D = 2048
T = 2048
C = 256
GRID_W = 64
D_FF = 4 * D
HD = 64
ROPE_THETA = 10000.0
LOG2E = 1.4426950408889634
NORM_EPS = 1e-6
N_MOD = 6
RW = 1024
RWKV_GN_EPS = 64e-5
LORA = 64
SWA_WINDOW = 128
SWA_QB = 128
DIFF_QB = 256
OPROJ_TM = 512
CHUNK = 64
SCAN_CHUNKS = 4
SUB = 16
HG_DK = 128
HG_CLAMP = 80.0
GROUP = 256
V7X_VMEM_BYTES = 64 * 1024 * 1024
V7X_VMEM_LIMIT = V7X_VMEM_BYTES - 8 * 1024 * 1024

AB_Q, AB_KS, AB_VS, AB_LORA, AB_N = 3072, 4096, 4352, 4608, 4864


def _dot(a, b):
    return jnp.dot(a, b, preferred_element_type=F32)


def _dot_nt(a, b):
    return lax.dot_general(a, b, (((1,), (1,)), ((), ())), preferred_element_type=F32)


def _dot_tn(a, b):
    return lax.dot_general(a, b, (((0,), (0,)), ((), ())), preferred_element_type=F32)


def _split2(x):
    hi = x.astype(BF16)
    lo = (x - hi.astype(F32)).astype(BF16)
    return hi, lo


def _split3(x):
    hi = x.astype(BF16)
    r1 = x - hi.astype(F32)
    mid = r1.astype(BF16)
    lo = (r1 - mid.astype(F32)).astype(BF16)
    return hi, mid, lo


def _dot_exact_rhs(x, g):
    hi, lo = _split2(x)
    return _dot(hi, g) + _dot(lo, g)


def _tri_dot(tri, x):
    hi, mid, lo = _split3(x)
    return _dot(tri, hi) + _dot(tri, mid) + _dot(tri, lo)


def _seg_sum(x, g_down, g_up):
    return _dot_exact_rhs(_dot_exact_rhs(x, g_down), g_up)


def _rope(x, cosl, sinl):
    w = x.shape[-1]
    lane = lax.broadcasted_iota(jnp.int32, x.shape, x.ndim - 1)
    first = (lane & 63) < 32
    swapped = jnp.where(first, pltpu.roll(x, w - 32, axis=1), pltpu.roll(x, 32, axis=1))
    return x * cosl + swapped * sinl


def _sigmoid(x):
    return 1.0 / (1.0 + jnp.exp(-x))


def _softplus(x):
    return jnp.maximum(x, 0.0) + jnp.log(1.0 + jnp.exp(-jnp.abs(x)))


def _params(sem, vmem=None):
    return pltpu.CompilerParams(dimension_semantics=sem, vmem_limit_bytes=vmem)


def _ada_kernel(s_ref, w_ref, b_ref, o_ref):
    s = s_ref[...]
    s = s * _sigmoid(s)
    o_ref[...] = _dot(s.astype(BF16), w_ref[...].astype(BF16)) + b_ref[...]


def _ada(s_in, ada_w, ada_b):
    depth = ada_w.shape[0]
    n = ada_w.shape[2]
    tn = 1536
    out = pl.pallas_call(
        _ada_kernel,
        grid=(depth, n // tn),
        in_specs=[pl.BlockSpec((8, D), lambda l, j: (0, 0)),
                  pl.BlockSpec((None, D, tn), lambda l, j: (l, 0, j)),
                  pl.BlockSpec((None, 1, tn), lambda l, j: (l, 0, j))],
        out_specs=pl.BlockSpec((None, 8, tn), lambda l, j: (l, 0, j)),
        out_shape=jax.ShapeDtypeStruct((depth, 8, n), F32),
        compiler_params=_params(("arbitrary", "arbitrary"), V7X_VMEM_LIMIT),
        name="ada_mod",
    )(s_in, ada_w, ada_b.reshape(depth, 1, n))
    return out.reshape(depth, 8, 1, n)


def _mod_spec(layer, k, tm, nb, tile=lambda i: i):
    return pl.BlockSpec((None, None, 1, D),
                        lambda i, *_: (layer, jnp.minimum((tile(i) * tm) // T, nb), 0, k))


def _split_rows(lat, ctx, tm, width, col=0, tile=lambda i: i):
    if ctx is None:
        return (lat,), [pl.BlockSpec((tm, width), lambda i, *_: (tile(i), col))], lambda t, ref: ref[...]
    n_lat = lat.shape[0] // tm
    specs = [pl.BlockSpec((tm, width), lambda i, *_: (jnp.minimum(tile(i), n_lat - 1), col)),
             pl.BlockSpec((tm, width), lambda i, *_: (jnp.maximum(tile(i) - n_lat, 0), col),
                          pipeline_mode=pl.Buffered(1))]
    return (lat, ctx), specs, lambda t, lat_ref, ctx_ref: jnp.where(t < n_lat, lat_ref[...], ctx_ref[...])


def _nmm_kernel(*refs, n_x, pick):
    sh_ref, sc_ref, w_ref, o_ref, lhs_ref = refs[n_x:]

    @pl.when(pl.program_id(1) == 0)
    def _():
        x = pick(pl.program_id(0), *refs[:n_x])
        xn = x * lax.rsqrt(jnp.mean(x * x, axis=-1, keepdims=True) + NORM_EPS)
        lhs_ref[...] = (xn * (1.0 + sc_ref[...]) + sh_ref[...]).astype(BF16)

    o_ref[...] = _dot(lhs_ref[...], w_ref[...].astype(BF16)).astype(o_ref.dtype)


def _norm_mod_matmul(x_lat, x_ctx, mod, layer, w, nb, tm, tn):
    rows = x_lat.shape[0] + (0 if x_ctx is None else x_ctx.shape[0])
    n = w.shape[1]
    x_args, x_specs, pick = _split_rows(x_lat, x_ctx, tm, D)
    if x_ctx is None and tm > 512:
        x_specs = [pl.BlockSpec((tm, D), lambda i, j: (i, 0), pipeline_mode=pl.Buffered(1))]
    return pl.pallas_call(
        functools.partial(_nmm_kernel, n_x=len(x_args), pick=pick),
        grid=(rows // tm, n // tn),
        in_specs=x_specs + [_mod_spec(layer, 0, tm, nb), _mod_spec(layer, 1, tm, nb),
                            pl.BlockSpec((D, tn), lambda i, j: (0, j))],
        out_specs=pl.BlockSpec((tm, tn), lambda i, j: (i, j)),
        out_shape=jax.ShapeDtypeStruct((rows, n), F32),
        scratch_shapes=[pltpu.VMEM((tm, D), BF16)],
        compiler_params=_params(("parallel", "arbitrary"), V7X_VMEM_LIMIT),
        name=f"in_proj_{layer}",
    )(*x_args, mod, mod, w)


def _mlp_kernel(x_ref, sh_ref, sc_ref, gt_ref, w1_ref, w2_ref, fn_ref, o_ref, lhs_ref, *, final):
    f = pl.program_id(1)

    @pl.when(f == 0)
    def _():
        x = x_ref[...]
        xn = x * lax.rsqrt(jnp.mean(x * x, axis=-1, keepdims=True) + NORM_EPS)
        lhs_ref[...] = (xn * (1.0 + sc_ref[...]) + sh_ref[...]).astype(BF16)
        o_ref[...] = jnp.zeros_like(o_ref)

    h = jnp.maximum(_dot(lhs_ref[...], w1_ref[...].astype(BF16)), 0.0)
    o_ref[...] += _dot((h * h).astype(BF16), w2_ref[...].astype(BF16))

    @pl.when(f == pl.num_programs(1) - 1)
    def _():
        y = x_ref[...] + gt_ref[...] * o_ref[...]
        if final:
            y = y * lax.rsqrt(jnp.mean(y * y, axis=-1, keepdims=True) + NORM_EPS) * fn_ref[...]
        o_ref[...] = y


def _mlp(xc, rows, mod, layer, w1, w2, final_norm, nb, final):
    tm, tf = 1024, 512
    return pl.pallas_call(
        functools.partial(_mlp_kernel, final=final),
        grid=(rows // tm, D_FF // tf),
        in_specs=[pl.BlockSpec((tm, D), lambda i, f: (i, 0), pipeline_mode=pl.Buffered(1)),
                  _mod_spec(layer, 3, tm, nb), _mod_spec(layer, 4, tm, nb), _mod_spec(layer, 5, tm, nb),
                  pl.BlockSpec((None, D, tf), lambda i, f: (layer, 0, f)),
                  pl.BlockSpec((None, tf, D), lambda i, f: (layer, f, 0)),
                  pl.BlockSpec((1, D), lambda i, f: (0, 0))],
        out_specs=pl.BlockSpec((tm, D), lambda i, f: (i, 0)),
        out_shape=jax.ShapeDtypeStruct((rows, D), F32),
        scratch_shapes=[pltpu.VMEM((tm, D), BF16)],
        compiler_params=_params(("parallel", "arbitrary"), V7X_VMEM_LIMIT),
        name=f"mlp_{layer}",
    )(xc, mod, mod, mod, w1, w2, final_norm)


def _oproj_kernel(*refs, n_mix, mix, n_x, pick):
    x_refs = refs[n_mix:n_mix + n_x]
    gt_ref, wa_ref, wb_ref, o_ref = refs[n_mix + n_x:]
    i = pl.program_id(0)
    ya, yb = mix(*refs[:n_mix], tile=i)
    y = _dot(ya.astype(BF16), wa_ref[...]) + _dot(yb.astype(BF16), wb_ref[...])
    o_ref[...] = pick(i, *x_refs) + gt_ref[...] * y


def _out_proj(mix_inputs, x_lat, x_ctx, rows, mod, layer, wa, wb, nb):
    tm = OPROJ_TM
    half = wa.shape[0]
    mix, mix_args, mix_specs = mix_inputs(lambda i: i)
    x_args, x_specs, pick = _split_rows(x_lat, x_ctx, tm, D)
    return pl.pallas_call(
        functools.partial(_oproj_kernel, n_mix=len(mix_args), mix=mix, n_x=len(x_args), pick=pick),
        grid=(rows // tm,),
        in_specs=list(mix_specs) + x_specs + [
            _mod_spec(layer, 2, tm, nb),
            pl.BlockSpec((half, D), lambda i: (0, 0), pipeline_mode=pl.Buffered(1)),
            pl.BlockSpec((half, D), lambda i: (0, 0), pipeline_mode=pl.Buffered(1))],
        out_specs=pl.BlockSpec((tm, D), lambda i: (i, 0)),
        out_shape=jax.ShapeDtypeStruct((rows, D), F32),
        compiler_params=_params(("parallel",), V7X_VMEM_LIMIT),
        name=f"out_proj_{layer}",
    )(*mix_args, *x_args, mod, wa, wb)


def _shifted(u, prev_row, next_row):
    tm = u.shape[0]
    row = lax.broadcasted_iota(jnp.int32, u.shape, 0)
    up = jnp.where(row == 0, prev_row, pltpu.roll(u, 1, axis=0))
    un = jnp.where(row == tm - 1, next_row, pltpu.roll(u, tm - 1, axis=0))
    return 0.5 * (up + un)


def _rwkv_prep_kernel(u_ref, ul_ref, up_ref, un_ref, ulp_ref, uln_ref, mu_ref, mul_ref, w0_ref, wup_ref,
                      a0_ref, aup_ref, gup_ref, kk_ref, ka_ref, rk_ref, gd_ref, gu_ref,
                      r_out, k_out, v_out, a_out, b_out, lw_out, g_out, bonus_out):
    u = u_ref[...]
    u = u + mu_ref[...] * (_shifted(u, up_ref[...], un_ref[...]) - u)
    ul = ul_ref[...]
    ul = ul + mul_ref[...] * (_shifted(ul, ulp_ref[...], uln_ref[...]) - ul)
    r, k, v = u[:, 0:RW], u[:, RW:2 * RW], u[:, 2 * RW:3 * RW]

    th = jnp.tanh(ul).astype(BF16)
    for d in range(2):
        w_log = -_softplus(-(w0_ref[d:d + 1, :] + _dot(th, wup_ref[d]))) - 0.5
        lw_out[d] = -jnp.exp(w_log)
    a = _sigmoid(a0_ref[...] + _dot(ul.astype(BF16), aup_ref[...]))
    g_out[...] = _dot(_sigmoid(ul).astype(BF16), gup_ref[...])

    kk = k * kk_ref[...]
    nrm = jnp.sqrt(_seg_sum(kk * kk, gd_ref[...], gu_ref[...]))
    kk = kk / jnp.maximum(nrm, 1e-12)
    k = k * (1.0 + (a - 1.0) * ka_ref[...])
    r_out[...] = r
    k_out[...] = k
    v_out[...] = v
    a_out[...] = -kk
    b_out[...] = kk * a
    bonus_out[...] = _seg_sum(r * k * rk_ref[...], gd_ref[...], gu_ref[...]) * v


def _seq_halo(u, tm, nb):
    rows = u.shape[0]
    nblk = rows // tm
    starts = np.arange(nblk) * tm
    seq_len = np.where(starts < nb * T, T, C)
    seq_off = np.where(starts < nb * T, starts % T, (starts - nb * T) % C)
    has_prev = seq_off > 0
    has_next = seq_off + tm < seq_len
    prev_idx = np.where(has_prev, starts - 1, 0)
    next_idx = np.where(has_next, starts + tm, 0)
    up = jnp.where(has_prev[:, None], u[prev_idx], 0.0)
    un = jnp.where(has_next[:, None], u[next_idx], 0.0)
    return up[:, None, :], un[:, None, :]


def _rwkv_prepare(u, nb, mu, w0, w_up, a0, a_up, g_up, k_k, k_a, r_k):
    rows = u.shape[0]
    tm = 256
    nblk = rows // tm
    up, un = _seq_halo(u, tm, nb)
    pad = jnp.zeros((LORA,), F32)
    mu_rkv = mu[:3 * RW].reshape(1, 3 * RW)
    mu_l = jnp.concatenate([mu[3 * RW:], pad]).reshape(1, 4 * LORA)

    def lora_w(w, slot):
        z = jnp.zeros((4 * LORA, RW), F32)
        return z.at[slot * LORA:(slot + 1) * LORA].set(w).astype(BF16)

    wup = jnp.stack([lora_w(w_up[0], 0), lora_w(w_up[1], 0)])
    aup = lora_w(a_up, 1)
    gup = lora_w(g_up, 2)
    head = np.arange(RW) // HD
    g_down = jnp.asarray(head[:, None] == np.arange(128)[None, :], BF16)
    g_upm = jnp.asarray(np.arange(128)[:, None] == head[None, :], BF16)

    row = lambda w: pl.BlockSpec((tm, w), lambda i: (i, 0))
    vec = lambda w: pl.BlockSpec((1, w), lambda i: (0, 0))
    full = lambda *s: pl.BlockSpec(s, lambda i: (0,) * len(s))
    out_sd = jax.ShapeDtypeStruct((rows, RW), F32)
    outs = pl.pallas_call(
        _rwkv_prep_kernel,
        grid=(nblk,),
        in_specs=[pl.BlockSpec((tm, 3 * RW), lambda i: (i, 0)),
                  pl.BlockSpec((tm, 4 * LORA), lambda i: (i, AB_LORA // (4 * LORA))),
                  pl.BlockSpec((None, 1, 3 * RW), lambda i: (i, 0, 0)),
                  pl.BlockSpec((None, 1, 3 * RW), lambda i: (i, 0, 0)),
                  pl.BlockSpec((None, 1, 4 * LORA), lambda i: (i, 0, AB_LORA // (4 * LORA))),
                  pl.BlockSpec((None, 1, 4 * LORA), lambda i: (i, 0, AB_LORA // (4 * LORA))),
                  vec(3 * RW), vec(4 * LORA), full(2, RW), full(2, 4 * LORA, RW),
                  vec(RW), full(4 * LORA, RW), full(4 * LORA, RW), vec(RW), vec(RW), vec(RW),
                  full(RW, 128), full(128, RW)],
        out_specs=[row(RW), row(RW), row(RW), row(RW), row(RW),
                   pl.BlockSpec((2, tm, RW), lambda i: (0, i, 0)), row(RW), row(RW)],
        out_shape=[out_sd, out_sd, out_sd, out_sd, out_sd,
                   jax.ShapeDtypeStruct((2, rows, RW), F32), out_sd, out_sd],
        compiler_params=_params(("parallel",), V7X_VMEM_LIMIT),
        name="rwkv_prepare",
    )(u, u, up, un, up, un, mu_rkv, mu_l, w0, wup, a0.reshape(1, RW), aup, gup,
      k_k.reshape(1, RW), k_a.reshape(1, RW), r_k.reshape(1, RW), g_down, g_upm)
    return outs


def _chunk_block(d, b, s, nb, rows=CHUNK):
    nctx, nlat = C // rows, T // rows
    pos_c = jnp.where(d == 0, s, nctx - 1 - s)
    pos_l = jnp.where(d == 0, s - nctx, nlat - 1 - (s - nctx))
    return jnp.where(s < nctx, nb * nlat + b * nctx + pos_c, b * nlat + pos_l)


def _lane_stack(x, hw):
    head = lax.broadcasted_iota(jnp.int32, x.shape, 1) // hw
    return jnp.concatenate([jnp.where(head == h, x, 0.0) for h in range(GROUP // hw)], axis=0)


def _fold_rows(x, n):
    out = x[0:n]
    for h in range(1, x.shape[0] // n):
        out = out + x[h * n:(h + 1) * n]
    return out


def _block_masks(size, blk, d):
    row = lax.broadcasted_iota(jnp.int32, (size, size), 0)
    col = lax.broadcasted_iota(jnp.int32, (size, size), 1)
    rr, cc = row % blk, col % blk
    same = (row // blk) == (col // blk)
    if d == 0:
        return cc <= rr, cc < rr, same
    return cc >= rr, cc > rr, same


def _rwkv_scan_kernel(*refs):
    in_refs = (refs[0:6], refs[6:12])
    y_refs = refs[12:14]
    s_ref = refs[14]

    @pl.when(pl.program_id(1) == 0)
    def _():
        s_ref[...] = jnp.zeros_like(s_ref)

    n = CHUNK
    ngrp = RW // GROUP
    nch = 2 * ngrp
    rrow = lax.broadcasted_iota(jnp.int32, (n, GROUP), 0)
    rcol = lax.broadcasted_iota(jnp.int32, (n, GROUP), 1) % n
    eye_row = jnp.where(rrow == rcol, 1.0, 0.0)
    incl_row = (rcol <= rrow, rcol >= rrow)
    strict_row = (rcol < rrow, rcol > rrow)
    same = _block_masks(GROUP, n, 0)[2]

    def block_diag(x_row):
        xb = x_row.astype(BF16)
        return jnp.where(same, jnp.concatenate([xb] * (GROUP // n), axis=0), jnp.zeros((), BF16))

    def stack(x):
        return _lane_stack(x.astype(BF16), HD)

    pre = []
    for d in range(2):
        tri = jnp.where(_block_masks(n, n, d)[0], 1.0, 0.0).astype(BF16)
        order = range(SCAN_CHUNKS) if d == 0 else range(SCAN_CHUNKS - 1, -1, -1)
        pre_d = []
        for ci in order:
            rows = slice(ci * n, (ci + 1) * n)
            r, k, v, a, b, lw = [ref[rows, :] for ref in in_refs[d]]
            c = _tri_dot(tri, lw)
            c_last = c[n - 1:n] if d == 0 else c[0:1]
            p_inv = jnp.exp(-c)
            p_end = jnp.exp(c_last - c)
            pre_d.append(dict(rows=rows, at=a * jnp.exp(c - lw), rt=r * jnp.exp(c), bt=b * p_inv, kt=k * p_inv,
                              bp=b * p_end, kp=k * p_end, v=v, dec=jnp.exp(c_last)))
        pre.append(pre_d)
    chains = [(d, p, g) for p in range(SCAN_CHUNKS) for d in range(2) for g in range(ngrp)]
    nch = len(chains)
    sl = lambda g: slice(g * GROUP, (g + 1) * GROUP)
    part = lambda d, p, g, name: pre[d][p][name][:, sl(g)]

    a_st = [stack(part(d, p, g, "at")) for d, p, g in chains]
    v_st = [stack(part(d, p, g, "v")) for d, p, g in chains]
    gram = [_dot_nt(jnp.concatenate([part(d, p, g, "at"), part(d, p, g, "rt")], axis=0).astype(BF16),
                    jnp.concatenate([stack(part(d, p, g, "bt")), stack(part(d, p, g, "kt"))], axis=0))
            for d, p, g in chains]
    a_ab = [jnp.where(strict_row[d], gram[i][0:n, 0:GROUP], 0.0) for i, (d, p, g) in enumerate(chains)]
    a_ak = [jnp.where(strict_row[d], gram[i][0:n, GROUP:], 0.0).astype(BF16) for i, (d, p, g) in enumerate(chains)]
    a_r = [jnp.where(jnp.concatenate([incl_row[d]] * 2, axis=1), gram[i][n:2 * n, :], 0.0).astype(BF16)
           for i, (d, p, g) in enumerate(chains)]
    av = [_dot(a_ak[i], v_st[i]) for i in range(nch)]
    tm = [eye_row + m for m in a_ab]
    pw = [_dot(m.astype(BF16), block_diag(m)) for m in a_ab]
    for _ in range(4):
        both = [_dot(jnp.concatenate([t, p], axis=0).astype(BF16), block_diag(p)) for t, p in zip(tm, pw)]
        tm = [t + x[0:n] for t, x in zip(tm, both)]
        pw = [x[n:2 * n] for x in both]
    tm = [t + _dot(t.astype(BF16), block_diag(p)) for t, p in zip(tm, pw)]
    tx = [_dot(tm[i].astype(BF16), jnp.concatenate([a_st[i], stack(av[i])], axis=1))
          for i in range(nch)]

    state = {(d, g): s_ref[d, g] for d in range(2) for g in range(ngrp)}
    for p in range(SCAN_CHUNKS):
        ids = [i for i, ch in enumerate(chains) if ch[1] == p]
        uy0 = {i: _dot_nt(jnp.concatenate([tx[i][:, 0:GROUP], part(*chains[i], "rt")], axis=0).astype(BF16),
                          state[chains[i][0], chains[i][2]].astype(BF16)) for i in ids}
        u = {i: uy0[i][0:n] + tx[i][:, GROUP:] for i in ids}
        for i in ids:
            d, _, g = chains[i]
            y_refs[d][pre[d][p]["rows"], sl(g)] = (
                uy0[i][n:2 * n] + _dot(a_r[i], jnp.concatenate([stack(u[i]), v_st[i]], axis=0)))
        for i in ids:
            d, _, g = chains[i]
            upd = _dot_tn(jnp.concatenate([u[i], part(d, p, g, "v")], axis=0).astype(BF16),
                          jnp.concatenate([part(d, p, g, "bp"), part(d, p, g, "kp")], axis=0).astype(BF16))
            state[d, g] = state[d, g] * part(d, p, g, "dec") + jnp.where(same, upd, 0.0)
    for (d, g), s_new in state.items():
        s_ref[d, g] = s_new


def _rwkv_scan(r, k, v, a, b, lw, nb):
    rows = r.shape[0]
    blk_rows = SCAN_CHUNKS * CHUNK
    steps = (T + C) // blk_rows
    in_specs, args = [], []
    for d in range(2):
        blk = lambda bb, s, d=d: (_chunk_block(d, bb, s, nb, blk_rows), 0)
        in_specs += [pl.BlockSpec((blk_rows, RW), blk)] * 5
        in_specs.append(pl.BlockSpec((None, blk_rows, RW),
                                     lambda bb, s, d=d: (d, _chunk_block(d, bb, s, nb, blk_rows), 0)))
        args += [r, k, v, a, b, lw]
    out_sd = jax.ShapeDtypeStruct((rows, RW), F32)
    return pl.pallas_call(
        _rwkv_scan_kernel,
        grid=(nb, steps),
        in_specs=in_specs,
        out_specs=[pl.BlockSpec((blk_rows, RW), lambda bb, s, d=d: (_chunk_block(d, bb, s, nb, blk_rows), 0))
                   for d in range(2)],
        out_shape=[out_sd, out_sd],
        scratch_shapes=[pltpu.VMEM((2, RW // GROUP, GROUP, GROUP), F32)],
        compiler_params=_params(("parallel", "arbitrary"), V7X_VMEM_LIMIT),
        name="rwkv_scan",
    )(*args)


def _ab_mix(y0_ref, y1_ref, bonus_ref, g_ref, lnw_ref, lnb_ref, gd_ref, gu_ref, *yb_refs, pick, tile):
    y = y0_ref[...] + y1_ref[...]
    mu = _seg_sum(y, gd_ref[...], gu_ref[...]) * (1.0 / HD)
    yc = y - mu
    var = _seg_sum(yc * yc, gd_ref[...], gu_ref[...]) * (1.0 / HD)
    yn = yc * lax.rsqrt(var + RWKV_GN_EPS)
    return (yn * lnw_ref[...] + lnb_ref[...] + bonus_ref[...]) * g_ref[...], pick(tile, *yb_refs)


def _ab_mix_inputs(tile, y0, y1, bonus, g, ln_w, ln_b, yb_lat, yb_ctx):
    tm = OPROJ_TM
    head = np.arange(RW) // HD
    g_down = jnp.asarray(head[:, None] == np.arange(128)[None, :], BF16)
    g_upm = jnp.asarray(np.arange(128)[:, None] == head[None, :], BF16)
    row = pl.BlockSpec((tm, RW), lambda i: (tile(i), 0))
    vec = pl.BlockSpec((1, RW), lambda i: (0, 0))
    yb_args, yb_specs, pick = _split_rows(yb_lat, yb_ctx, tm, 1024, tile=tile)
    specs = [row, row, row, row, vec, vec,
             pl.BlockSpec((RW, 128), lambda i: (0, 0)), pl.BlockSpec((128, RW), lambda i: (0, 0))] + yb_specs
    args = (y0, y1, bonus, g, ln_w.reshape(1, RW), ln_b.reshape(1, RW), g_down, g_upm) + yb_args
    return functools.partial(_ab_mix, pick=pick), args, specs


def _kv_block(i, nb):
    per = T // 256
    return jnp.where(i < nb * per, i // per, i - nb * per), jnp.where(i < nb * per, i % per, per)


def _swa_prep_kernel(k_ref, v_ref, cos_ref, sin_ref, ko_ref, vo_ref):
    cosl, sinl = cos_ref[...], sin_ref[...]
    lane = lax.broadcasted_iota(jnp.int32, cosl.shape, 1)
    low = lane < HD
    for src, dst, rope in ((k_ref, ko_ref, True), (v_ref, vo_ref, False)):
        for t in range(2):
            x = src[:, t * 128:(t + 1) * 128]
            if rope:
                x = _rope(x, cosl, sinl)
            swapped = pltpu.roll(x, HD, axis=1)
            dst[:, (2 * t) * 128:(2 * t + 1) * 128] = jnp.where(low, x, swapped).astype(BF16)
            dst[:, (2 * t + 1) * 128:(2 * t + 2) * 128] = jnp.where(low, swapped, x).astype(BF16)


def _swa_prep(u, cosl, sinl, nb):
    rows = u.shape[0]
    tm = 256
    kv_out = pl.BlockSpec((None, tm, 512), lambda i: (*_kv_block(i, nb), 0))
    sd = jax.ShapeDtypeStruct((nb, T + C, 512), BF16)
    return pl.pallas_call(
        _swa_prep_kernel,
        grid=(rows // tm,),
        in_specs=[pl.BlockSpec((tm, 256), lambda i: (i, AB_KS // 256)),
                  pl.BlockSpec((tm, 256), lambda i: (i, AB_VS // 256)),
                  pl.BlockSpec((tm, 128), lambda i: (i, 0)),
                  pl.BlockSpec((tm, 128), lambda i: (i, 0))],
        out_specs=[kv_out, kv_out],
        out_shape=[sd, sd],
        compiler_params=_params(("parallel",), V7X_VMEM_LIMIT),
        name="swa_prep",
    )(u, u, cosl, sinl)


def _diff_prep_kernel(k_ref, v_ref, cos_ref, sin_ref, ko_ref, vo_ref):
    cosl, sinl = cos_ref[...], sin_ref[...]
    for t in range(8):
        sl = slice(t * 128, (t + 1) * 128)
        ko_ref[:, sl] = _rope(k_ref[:, sl], cosl, sinl).astype(BF16)
    vo_ref[...] = v_ref[...].astype(BF16)


def _diff_prep(u, cosl, sinl, nb):
    rows = u.shape[0]
    tm = 256
    kv_out = pl.BlockSpec((None, tm, 1024), lambda i: (*_kv_block(i, nb), 0))
    sd = jax.ShapeDtypeStruct((nb, T + C, 1024), BF16)
    return pl.pallas_call(
        _diff_prep_kernel,
        grid=(rows // tm,),
        in_specs=[pl.BlockSpec((tm, 1024), lambda i: (i, 1)),
                  pl.BlockSpec((tm, 1024), lambda i: (i, 2)),
                  pl.BlockSpec((tm, 128), lambda i: (i, 0)),
                  pl.BlockSpec((tm, 128), lambda i: (i, 0))],
        out_specs=[kv_out, kv_out],
        out_shape=[sd, sd],
        compiler_params=_params(("parallel",), V7X_VMEM_LIMIT),
        name="diff_prep",
    )(u, u, cosl, sinl)


def _swa_kernel(q_ref, cos_ref, sin_ref, k_ref, v_ref, sink_ref, o_ref, *, ctx_mode):
    n = pl.program_id(1)
    cosl, sinl = cos_ref[...], sin_ref[...]
    QB = q_ref.shape[0]
    span = QB + 2 * SWA_WINDOW
    if not ctx_mode:
        start = pl.multiple_of(jnp.clip(n * QB - SWA_WINDOW, 0, T - span), SWA_WINDOW)
        kpos = start + lax.broadcasted_iota(jnp.int32, (2 * QB, span), 1)
        qpos = n * QB + lax.broadcasted_iota(jnp.int32, (2 * QB, span), 0) % QB
        valid = jnp.abs(kpos - qpos) <= SWA_WINDOW
    low = lax.broadcasted_iota(jnp.int32, (QB, 128), 1) < HD
    top = lax.broadcasted_iota(jnp.int32, (2 * QB, 1), 0) < QB
    tile = lambda j: slice((j // 2) * 128, (j // 2 + 1) * 128)
    qs, sink = [], []
    for j in range(8):
        q = _rope(q_ref[:, j * 128:(j + 1) * 128], cosl, sinl) * (HD ** -0.5 * LOG2E)
        qs.append(jnp.concatenate([jnp.where(low, q, 0.0), jnp.where(low, 0.0, q)], axis=0).astype(BF16))
        sink.append(jnp.where(top, sink_ref[2 * j:2 * j + 1, 0:1], sink_ref[2 * j + 1:2 * j + 2, 0:1]) * LOG2E)
    s_c = [_dot_nt(qs[j], k_ref[T:T + C, tile(j)]) for j in range(8)]
    m = [jnp.maximum(jnp.max(s, axis=-1, keepdims=True), sk) for s, sk in zip(s_c, sink)]
    if not ctx_mode:
        s_w = [jnp.where(valid, _dot_nt(qs[j], k_ref[pl.ds(start, span), tile(j)]), -jnp.inf) for j in range(8)]
        m = [jnp.maximum(mm, jnp.max(s, axis=-1, keepdims=True)) for mm, s in zip(m, s_w)]
    p_c = [jnp.exp2(s - mm) for s, mm in zip(s_c, m)]
    den = [jnp.sum(p, axis=-1, keepdims=True) + jnp.exp2(sk - mm) for p, sk, mm in zip(p_c, sink, m)]
    pv = [_dot(p.astype(BF16), v_ref[T:T + C, tile(j)]) for j, p in enumerate(p_c)]
    if not ctx_mode:
        p_w = [jnp.exp2(s - mm) for s, mm in zip(s_w, m)]
        den = [dd + jnp.sum(p, axis=-1, keepdims=True) for dd, p in zip(den, p_w)]
        pv = [x + _dot(p.astype(BF16), v_ref[pl.ds(start, span), tile(j)]) for j, (x, p) in enumerate(zip(pv, p_w))]
    for j in range(8):
        o = pv[j] / den[j]
        o_ref[:, j * 128:(j + 1) * 128] = jnp.where(low, o[0:QB], o[QB:2 * QB])


def _swa(u, cosl, sinl, kpad, vpad, sink, nb, ctx_mode):
    QB = SWA_QB
    nq = (C if ctx_mode else T) // QB
    base = nb * (T // QB) if ctx_mode else 0
    rows = nb * nq * QB
    rowblk = lambda b, n: base + b * nq + n
    kv = pl.BlockSpec((None, T + C, 512), lambda b, n: (b, 0, 0))
    return pl.pallas_call(
        functools.partial(_swa_kernel, ctx_mode=ctx_mode),
        grid=(nb, nq),
        in_specs=[pl.BlockSpec((QB, 1024), lambda b, n: (rowblk(b, n), AB_Q // 1024)),
                  pl.BlockSpec((QB, 128), lambda b, n: (rowblk(b, n), 0)),
                  pl.BlockSpec((QB, 128), lambda b, n: (rowblk(b, n), 0)),
                  kv, kv, pl.BlockSpec((16, 128), lambda b, n: (0, 0))],
        out_specs=pl.BlockSpec((QB, 1024), lambda b, n: (b * nq + n, 0)),
        out_shape=jax.ShapeDtypeStruct((rows, 1024), F32),
        compiler_params=_params(("parallel", "arbitrary"), V7X_VMEM_LIMIT),
        name="swa_ctx" if ctx_mode else "swa_latent",
    )(u, cosl, sinl, kpad, vpad, sink)


def _diff_kernel(q_ref, cos_ref, sin_ref, k_ref, v_ref, lam_ref, sub_ref, o_ref):
    cosl, sinl = cos_ref[...], sin_ref[...]
    lam = lam_ref[...]
    nq = q_ref.shape[0]
    low = lax.broadcasted_iota(jnp.int32, (nq, 128), 1) < HD
    tile = lambda h: slice(h * 128, (h + 1) * 128)

    def logits(h):
        q = _rope(q_ref[:, tile(h)], cosl, sinl) * (HD ** -0.5 * LOG2E)
        qs = jnp.concatenate([jnp.where(low, q, 0.0), jnp.where(low, 0.0, q)], axis=0).astype(BF16)
        return _dot_nt(qs, k_ref[:, tile(h)])

    s_next = logits(0)
    for h in range(8):
        s = s_next
        if h + 1 < 8:
            s_next = logits(h + 1)
        e = jnp.exp2(s - jnp.max(s, axis=-1, keepdims=True))
        inv = 1.0 / jnp.sum(e, axis=-1, keepdims=True)
        o2 = _dot(e.astype(BF16), v_ref[:, tile(h)])
        o = o2[0:nq] * inv[0:nq] - o2[nq:2 * nq] * (lam[:, 0:1] * inv[nq:2 * nq])
        o = o * lax.rsqrt(jnp.mean(o * o, axis=-1, keepdims=True) + 1e-5)
        o_ref[:, tile(h)] = o * sub_ref[...]


def _diff_attn(u, cosl, sinl, kb, vb, lam, sub, nb):
    QB = DIFF_QB
    nq = T // QB
    kv = pl.BlockSpec((None, T + C, 1024), lambda b, n: (b, 0, 0))
    vec = pl.BlockSpec((1, 128), lambda b, n: (0, 0))
    return pl.pallas_call(
        _diff_kernel,
        grid=(nb, nq),
        in_specs=[pl.BlockSpec((QB, 1024), lambda b, n: (b * nq + n, 0)),
                  pl.BlockSpec((QB, 128), lambda b, n: (b * nq + n, 0)),
                  pl.BlockSpec((QB, 128), lambda b, n: (b * nq + n, 0)),
                  kv, kv, vec, vec],
        out_specs=pl.BlockSpec((QB, 1024), lambda b, n: (b * nq + n, 0)),
        out_shape=jax.ShapeDtypeStruct((nb * T, 1024), F32),
        compiler_params=_params(("parallel", "arbitrary"), V7X_VMEM_LIMIT),
        name="diff_attn",
    )(u, cosl, sinl, kb, vb, lam, sub)


def _hgrn_exact_att(q, kk, bcum, tmp_ref):
    n = CHUNK
    tmp_ref[0] = bcum
    tmp_ref[1] = kk
    coli = lax.broadcasted_iota(jnp.int32, (n, 2 * n), 1)

    def body(s, acc):
        bs = tmp_ref[0, pl.ds(s, 1), :]
        ks = tmp_ref[1, pl.ds(s, 1), :]
        w = q * jnp.exp(jnp.minimum(bcum - bs, 0.0)) * ks
        c0 = jnp.sum(w[:, 0:HG_DK], axis=-1, keepdims=True)
        c1 = jnp.sum(w[:, HG_DK:], axis=-1, keepdims=True)
        return acc + jnp.concatenate([jnp.where(coli == s, c0, 0.0), jnp.where(coli == s + n, c1, 0.0)], axis=0)

    return lax.fori_loop(0, n, body, jnp.zeros((2 * n, 2 * n), F32))


def _hgrn_scan_kernel(q0_ref, z0_ref, i0_ref, lb0_ref, q1_ref, z1_ref, i1_ref, lb1_ref,
                      o0_ref, o1_ref, s_ref, g_ref, tmp_ref):
    in_refs = ((q0_ref, z0_ref, i0_ref, lb0_ref), (q1_ref, z1_ref, i1_ref, lb1_ref))
    o_refs = (o0_ref, o1_ref)

    @pl.when(pl.program_id(1) == 0)
    def _():
        s_ref[...] = jnp.zeros_like(s_ref)

    n = CHUNK
    ngrp = 1024 // GROUP
    nsub = n // SUB
    pre = []
    for d in range(2):
        tri = jnp.where(_block_masks(n, n, d)[0], 1.0, 0.0).astype(BF16)
        lb = in_refs[d][3][...]
        order = range(SCAN_CHUNKS) if d == 0 else range(SCAN_CHUNKS - 1, -1, -1)
        pre_d = []
        for ci in order:
            rows = slice(ci * n, (ci + 1) * n)
            q, z, v = [ref[rows, :] for ref in in_refs[d][0:3]]
            logf = jnp.log(lb + (1.0 - lb) * _sigmoid(z))
            kk = (1.0 - lb) * _sigmoid(-z)
            bcum = _tri_dot(tri, logf)
            b_last = bcum[n - 1:n] if d == 0 else bcum[0:1]
            b_excl = bcum - logf
            qh, kh = [], []
            for sb in range(nsub):
                lo, hi = sb * SUB, (sb + 1) * SUB
                beta = b_excl[lo:lo + 1] if d == 0 else b_excl[hi - 1:hi]
                qh.append(q[lo:hi] * jnp.exp(bcum[lo:hi] - beta))
                kh.append(kk * jnp.exp(beta - bcum))
            pre_d.append(dict(rows=rows, q=q, v=v, kk=kk, bcum=bcum, qh=qh, kh=kh, qe=q * jnp.exp(bcum),
                              ke=kk * jnp.exp(b_last - bcum), dec=jnp.exp(b_last), min_logf=jnp.min(logf)))
        pre.append(pre_d)
    chains = [(d, p, g) for p in range(SCAN_CHUNKS) for d in range(2) for g in range(ngrp)]
    sl = lambda g: slice(g * GROUP, (g + 1) * GROUP)

    for i, (d, p, g) in enumerate(chains):
        c = pre[d][p]
        rows = [_dot_nt(_lane_stack(c["qh"][sb][:, sl(g)].astype(BF16), HG_DK),
                        _lane_stack(c["kh"][sb][:, sl(g)].astype(BF16), HG_DK)) for sb in range(nsub)]
        g_ref[i] = jnp.concatenate([rows[sb][h * SUB:(h + 1) * SUB] for h in range(2) for sb in range(nsub)], axis=0)

    min_logf = functools.reduce(jnp.minimum, [c["min_logf"] for pre_d in pre for c in pre_d])

    @pl.when(min_logf < -(HG_CLAMP / SUB))
    def _():
        for i, (d, p, g) in enumerate(chains):
            c = pre[d][p]
            g_ref[i] = _hgrn_exact_att(c["q"][:, sl(g)], c["kk"][:, sl(g)], c["bcum"][:, sl(g)], tmp_ref)

    state = {(d, g): s_ref[d, g] for d in range(2) for g in range(ngrp)}
    head_same = _block_masks(GROUP, HG_DK, 0)[2]
    for p in range(SCAN_CHUNKS):
        ids = [i for i, ch in enumerate(chains) if ch[1] == p]
        inter = {i: _dot_nt(pre[chains[i][0]][p]["qe"][:, sl(chains[i][2])].astype(BF16),
                            state[chains[i][0], chains[i][2]].astype(BF16)) for i in ids}
        for i in ids:
            d, _, g = chains[i]
            incl, _, same = _block_masks(2 * n, n, d)
            att = jnp.where(incl & same, g_ref[i], 0.0).astype(BF16)
            v_st = _lane_stack(pre[d][p]["v"][:, sl(g)].astype(BF16), HG_DK)
            o_refs[d][pre[d][p]["rows"], sl(g)] = inter[i] + _fold_rows(_dot(att, v_st), n)
        for i in ids:
            d, _, g = chains[i]
            upd = _dot_tn(pre[d][p]["v"][:, sl(g)].astype(BF16), pre[d][p]["ke"][:, sl(g)].astype(BF16))
            state[d, g] = state[d, g] * pre[d][p]["dec"][:, sl(g)] + jnp.where(head_same, upd, 0.0)
    for (d, g), s_new in state.items():
        s_ref[d, g] = s_new


def _hgrn_scan(u, lb, nb):
    rows = u.shape[0]
    blk_rows = SCAN_CHUNKS * CHUNK
    steps = (T + C) // blk_rows
    in_specs, args = [], []
    for d in range(2):
        col = lambda j, d=d: pl.BlockSpec((blk_rows, 1024),
                                          lambda bb, s: (_chunk_block(d, bb, s, nb, blk_rows), j))
        in_specs += [col(3), col(4 + d), col(6), pl.BlockSpec((None, 1, 1024), lambda bb, s, d=d: (d, 0, 0))]
        args += [u, u, u, lb]
    out_sd = jax.ShapeDtypeStruct((rows, 1024), F32)
    ngrp = 1024 // GROUP
    return pl.pallas_call(
        _hgrn_scan_kernel,
        grid=(nb, steps),
        in_specs=in_specs,
        out_specs=[pl.BlockSpec((blk_rows, 1024), lambda bb, s, d=d: (_chunk_block(d, bb, s, nb, blk_rows), 0))
                   for d in range(2)],
        out_shape=[out_sd, out_sd],
        scratch_shapes=[pltpu.VMEM((2, ngrp, GROUP, GROUP), F32),
                        pltpu.VMEM((2 * ngrp * SCAN_CHUNKS, 2 * CHUNK, 2 * CHUNK), F32),
                        pltpu.VMEM((2, CHUNK, GROUP), F32)],
        compiler_params=_params(("parallel", "arbitrary"), V7X_VMEM_LIMIT),
        name="hgrn_scan",
    )(*args)


def _cd_mix(yc_ref, o0_ref, o1_ref, g_ref, gn_ref, *, tile):
    g = g_ref[...]
    heads = []
    for h in range(8):
        sl = slice(h * 128, (h + 1) * 128)
        o = o0_ref[:, sl] + o1_ref[:, sl]
        o = o * lax.rsqrt(jnp.mean(o * o, axis=-1, keepdims=True) + NORM_EPS) * gn_ref[...]
        gh = g[:, sl]
        heads.append(o * (gh * _sigmoid(gh)))
    return yc_ref[...], jnp.concatenate(heads, axis=1)


def _cd_mix_inputs(tile, yc, o0, o1, u, gnorm):
    tm = OPROJ_TM
    row = pl.BlockSpec((tm, 1024), lambda i: (tile(i), 0))
    specs = [row, row, row, pl.BlockSpec((tm, 1024), lambda i: (tile(i), 7)),
             pl.BlockSpec((1, 128), lambda i: (0, 0))]
    return _cd_mix, (yc, o0, o1, u, gnorm.reshape(1, 128)), specs


def _rope_tables(nb):
    t = np.arange(T)
    quarter = HD // 4
    inv = ROPE_THETA ** (-jnp.arange(quarter, dtype=F32) / quarter)
    rows = jnp.asarray(t // GRID_W, F32)
    cols = jnp.asarray(t % GRID_W, F32)
    ang = jnp.concatenate([rows[:, None] * inv, cols[:, None] * inv], axis=-1)
    cos, sin = jnp.cos(ang), jnp.sin(ang)
    cosl = jnp.tile(jnp.concatenate([cos, cos], axis=-1), (nb, 2))
    sinl = jnp.tile(jnp.concatenate([-sin, sin], axis=-1), (nb, 2))
    cosl = jnp.concatenate([cosl, jnp.ones((nb * C, 128), F32)], axis=0)
    sinl = jnp.concatenate([sinl, jnp.zeros((nb * C, 128), F32)], axis=0)
    return cosl, sinl


def kernel(x, c, ctx, c_ctx, ada_w, ada_b, mlp_w1, mlp_w2, final_norm, ab_w_in, ab_w_out, ab_mu, ab_w0,
           ab_w_up, ab_a0, ab_a_up, ab_g_up, ab_k_k, ab_k_a, ab_r_k, ab_ln_w, ab_ln_b, ab_sink,
           cd_w_in, cd_w_out, cd_lam, cd_subln, cd_gnorm, hgrn_lb_logits):
    nb = x.shape[0]
    assert x.shape == (nb, T, D) and ctx.shape == (nb, C, D) and nb < 8
    assert ada_w.shape[0] == 2, "one AB layer followed by one CD layer"
    rx = nb * T
    rows = rx + nb * C
    x2, c2 = x.reshape(rx, D), ctx.reshape(nb * C, D)
    cosl, sinl = _rope_tables(nb)

    s_in = jnp.zeros((8, D), F32).at[:nb].set(c).at[nb].set(c_ctx)
    mod = _ada(s_in, ada_w, ada_b)

    w = ab_w_in[0]
    w_in0 = jnp.concatenate([w[:, :3 * RW], w[:, 3 * RW + 3 * LORA:], w[:, 3 * RW:3 * RW + 3 * LORA],
                             jnp.zeros((D, LORA), F32)], axis=1).astype(BF16)
    u = _norm_mod_matmul(x2, c2, mod, 0, w_in0, nb, 512, AB_N // 2)
    r, k, v, a, b, lw, g, bonus = _rwkv_prepare(u, nb, ab_mu[0], ab_w0[0], ab_w_up[0], ab_a0[0], ab_a_up[0],
                                                ab_g_up[0], ab_k_k[0], ab_k_a[0], ab_r_k[0].reshape(RW))
    y0, y1 = _rwkv_scan(r, k, v, a, b, lw, nb)
    kpad, vpad = _swa_prep(u, cosl, sinl, nb)
    sink = jnp.broadcast_to(ab_sink[0][:, None], (16, 128))
    yb_lat = _swa(u, cosl, sinl, kpad, vpad, sink, nb, False)
    yb_ctx = _swa(u, cosl, sinl, kpad, vpad, sink, nb, True)
    w_out0 = ab_w_out[0].astype(BF16)
    mix_inputs = functools.partial(_ab_mix_inputs, y0=y0, y1=y1, bonus=bonus, g=g, ln_w=ab_ln_w[0],
                                   ln_b=ab_ln_b[0], yb_lat=yb_lat, yb_ctx=yb_ctx)
    xc = _out_proj(mix_inputs, x2, c2, rows, mod, 0, w_out0[:RW], w_out0[RW:], nb)
    fn = final_norm.reshape(1, D)
    xc = _mlp(xc, rows, mod, 0, mlp_w1, mlp_w2, fn, nb, False)

    lam_init = 0.8 - 0.6 * math.exp(-0.3 * 1)
    lb_table = jnp.cumsum(jax.nn.softmax(hgrn_lb_logits.astype(F32), axis=0), axis=0)
    lb = (lb_table - lb_table[0])[1].reshape(2, 1, 1024)
    lf = cd_lam[0].astype(F32)
    lmb = jnp.exp(jnp.sum(lf[0] * lf[1])) - jnp.exp(jnp.sum(lf[2] * lf[3])) + lam_init
    u = _norm_mod_matmul(xc, None, mod, 1, cd_w_in[0], nb, 1024, 1024)
    kb, vb = _diff_prep(u, cosl, sinl, nb)
    yc = _diff_attn(u, cosl, sinl, kb, vb, jnp.full((1, 128), lmb, F32),
                    (cd_subln[0] * (1.0 - lam_init)).reshape(1, 128), nb)
    o0, o1 = _hgrn_scan(u, lb, nb)
    w_out1 = cd_w_out[0].astype(BF16)
    mix_inputs = functools.partial(_cd_mix_inputs, yc=yc, o0=o0, o1=o1, u=u, gnorm=cd_gnorm[0])
    xl = _out_proj(mix_inputs, xc, None, rx, mod, 1, w_out1[:1024], w_out1[1024:], nb)
    out = _mlp(xl, rx, mod, 1, mlp_w1, mlp_w2, fn, nb, True)
    return out.reshape(nb, T, D)
```

```python
import functools
import math

import jax
import jax.numpy as jnp
import numpy as np
from jax import lax
from jax.experimental import pallas as pl
from jax.experimental.pallas import tpu as pltpu

F32 = jnp.float32
BF16 = jnp.bfloat16

D = 2048
T = 2048
C = 256
GRID_W = 64
D_FF = 4 * D
HD = 64
ROPE_THETA = 10000.0
LOG2E = 1.4426950408889634
NORM_EPS = 1e-6
N_MOD = 6
RW = 1024
RWKV_GN_EPS = 64e-5
LORA = 64
SWA_WINDOW = 128
SWA_QB = 128
DIFF_QB = 256
OPROJ_TM = 512
CHUNK = 64
SCAN_CHUNKS = 4
SUB = 16
HG_DK = 128
HG_CLAMP = 80.0
GROUP = 256
V7X_VMEM_BYTES = 64 * 1024 * 1024
V7X_VMEM_LIMIT = V7X_VMEM_BYTES - 8 * 1024 * 1024

AB_Q, AB_KS, AB_VS, AB_LORA, AB_N = 3072, 4096, 4352, 4608, 4864


def _dot(a, b):
    return jnp.dot(a, b, preferred_element_type=F32)


def _dot_nt(a, b):
    return lax.dot_general(a, b, (((1,), (1,)), ((), ())), preferred_element_type=F32)


def _dot_tn(a, b):
    return lax.dot_general(a, b, (((0,), (0,)), ((), ())), preferred_element_type=F32)


def _split2(x):
    hi = x.astype(BF16)
    lo = (x - hi.astype(F32)).astype(BF16)
    return hi, lo


def _split3(x):
    hi = x.astype(BF16)
    r1 = x - hi.astype(F32)
    mid = r1.astype(BF16)
    lo = (r1 - mid.astype(F32)).astype(BF16)
    return hi, mid, lo


def _dot_exact_rhs(x, g):
    hi, lo = _split2(x)
    return _dot(hi, g) + _dot(lo, g)


def _tri_dot(tri, x):
    hi, mid, lo = _split3(x)
    return _dot(tri, hi) + _dot(tri, mid) + _dot(tri, lo)


def _seg_sum(x, g_down, g_up):
    return _dot_exact_rhs(_dot_exact_rhs(x, g_down), g_up)


def _rope(x, cosl, sinl):
    w = x.shape[-1]
    lane = lax.broadcasted_iota(jnp.int32, x.shape, x.ndim - 1)
    first = (lane & 63) < 32
    swapped = jnp.where(first, pltpu.roll(x, w - 32, axis=1), pltpu.roll(x, 32, axis=1))
    return x * cosl + swapped * sinl


def _sigmoid(x):
    return 1.0 / (1.0 + jnp.exp(-x))


def _softplus(x):
    return jnp.maximum(x, 0.0) + jnp.log(1.0 + jnp.exp(-jnp.abs(x)))


def _params(sem, vmem=None):
    return pltpu.CompilerParams(dimension_semantics=sem, vmem_limit_bytes=vmem)


def _ada_kernel(s_ref, w_ref, b_ref, o_ref):
    s = s_ref[...]
    s = s * _sigmoid(s)
    o_ref[...] = _dot(s.astype(BF16), w_ref[...].astype(BF16)) + b_ref[...]


def _ada(s_in, ada_w, ada_b):
    depth = ada_w.shape[0]
    n = ada_w.shape[2]
    tn = 1536
    out = pl.pallas_call(
        _ada_kernel,
        grid=(depth, n // tn),
        in_specs=[pl.BlockSpec((8, D), lambda l, j: (0, 0)),
                  pl.BlockSpec((None, D, tn), lambda l, j: (l, 0, j)),
                  pl.BlockSpec((None, 1, tn), lambda l, j: (l, 0, j))],
        out_specs=pl.BlockSpec((None, 8, tn), lambda l, j: (l, 0, j)),
        out_shape=jax.ShapeDtypeStruct((depth, 8, n), F32),
        compiler_params=_params(("arbitrary", "arbitrary"), V7X_VMEM_LIMIT),
        name="ada_mod",
    )(s_in, ada_w, ada_b.reshape(depth, 1, n))
    return out.reshape(depth, 8, 1, n)


def _mod_spec(layer, k, tm, nb, tile=lambda i: i):
    return pl.BlockSpec((None, None, 1, D),
                        lambda i, *_: (layer, jnp.minimum((tile(i) * tm) // T, nb), 0, k))


def _split_rows(lat, ctx, tm, width, col=0, tile=lambda i: i):
    if ctx is None:
        return (lat,), [pl.BlockSpec((tm, width), lambda i, *_: (tile(i), col))], lambda t, ref: ref[...]
    n_lat = lat.shape[0] // tm
    specs = [pl.BlockSpec((tm, width), lambda i, *_: (jnp.minimum(tile(i), n_lat - 1), col)),
             pl.BlockSpec((tm, width), lambda i, *_: (jnp.maximum(tile(i) - n_lat, 0), col),
                          pipeline_mode=pl.Buffered(1))]
    return (lat, ctx), specs, lambda t, lat_ref, ctx_ref: jnp.where(t < n_lat, lat_ref[...], ctx_ref[...])


def _nmm_kernel(*refs, n_x, pick):
    sh_ref, sc_ref, w_ref, o_ref, lhs_ref = refs[n_x:]

    @pl.when(pl.program_id(1) == 0)
    def _():
        x = pick(pl.program_id(0), *refs[:n_x])
        xn = x * lax.rsqrt(jnp.mean(x * x, axis=-1, keepdims=True) + NORM_EPS)
        lhs_ref[...] = (xn * (1.0 + sc_ref[...]) + sh_ref[...]).astype(BF16)

    o_ref[...] = _dot(lhs_ref[...], w_ref[...].astype(BF16)).astype(o_ref.dtype)


def _norm_mod_matmul(x_lat, x_ctx, mod, layer, w, nb, tm, tn):
    rows = x_lat.shape[0] + (0 if x_ctx is None else x_ctx.shape[0])
    n = w.shape[1]
    x_args, x_specs, pick = _split_rows(x_lat, x_ctx, tm, D)
    if x_ctx is None and tm > 512:
        x_specs = [pl.BlockSpec((tm, D), lambda i, j: (i, 0), pipeline_mode=pl.Buffered(1))]
    return pl.pallas_call(
        functools.partial(_nmm_kernel, n_x=len(x_args), pick=pick),
        grid=(rows // tm, n // tn),
        in_specs=x_specs + [_mod_spec(layer, 0, tm, nb), _mod_spec(layer, 1, tm, nb),
                            pl.BlockSpec((D, tn), lambda i, j: (0, j))],
        out_specs=pl.BlockSpec((tm, tn), lambda i, j: (i, j)),
        out_shape=jax.ShapeDtypeStruct((rows, n), F32),
        scratch_shapes=[pltpu.VMEM((tm, D), BF16)],
        compiler_params=_params(("parallel", "arbitrary"), V7X_VMEM_LIMIT),
        name=f"in_proj_{layer}",
    )(*x_args, mod, mod, w)


def _mlp_kernel(x_ref, sh_ref, sc_ref, gt_ref, w1_ref, w2_ref, fn_ref, o_ref, lhs_ref, *, final):
    f = pl.program_id(1)

    @pl.when(f == 0)
    def _():
        x = x_ref[...]
        xn = x * lax.rsqrt(jnp.mean(x * x, axis=-1, keepdims=True) + NORM_EPS)
        lhs_ref[...] = (xn * (1.0 + sc_ref[...]) + sh_ref[...]).astype(BF16)
        o_ref[...] = jnp.zeros_like(o_ref)

    h = jnp.maximum(_dot(lhs_ref[...], w1_ref[...].astype(BF16)), 0.0)
    o_ref[...] += _dot((h * h).astype(BF16), w2_ref[...].astype(BF16))

    @pl.when(f == pl.num_programs(1) - 1)
    def _():
        y = x_ref[...] + gt_ref[...] * o_ref[...]
        if final:
            y = y * lax.rsqrt(jnp.mean(y * y, axis=-1, keepdims=True) + NORM_EPS) * fn_ref[...]
        o_ref[...] = y


def _mlp(xc, rows, mod, layer, w1, w2, final_norm, nb, final):
    tm, tf = 1024, 512
    return pl.pallas_call(
        functools.partial(_mlp_kernel, final=final),
        grid=(rows // tm, D_FF // tf),
        in_specs=[pl.BlockSpec((tm, D), lambda i, f: (i, 0), pipeline_mode=pl.Buffered(1)),
                  _mod_spec(layer, 3, tm, nb), _mod_spec(layer, 4, tm, nb), _mod_spec(layer, 5, tm, nb),
                  pl.BlockSpec((None, D, tf), lambda i, f: (layer, 0, f)),
                  pl.BlockSpec((None, tf, D), lambda i, f: (layer, f, 0)),
                  pl.BlockSpec((1, D), lambda i, f: (0, 0))],
        out_specs=pl.BlockSpec((tm, D), lambda i, f: (i, 0)),
        out_shape=jax.ShapeDtypeStruct((rows, D), F32),
        scratch_shapes=[pltpu.VMEM((tm, D), BF16)],
        compiler_params=_params(("parallel", "arbitrary"), V7X_VMEM_LIMIT),
        name=f"mlp_{layer}",
    )(xc, mod, mod, mod, w1, w2, final_norm)


def _oproj_kernel(*refs, n_mix, mix, n_x, pick):
    x_refs = refs[n_mix:n_mix + n_x]
    gt_ref, wa_ref, wb_ref, o_ref = refs[n_mix + n_x:]
    i = pl.program_id(0)
    ya, yb = mix(*refs[:n_mix], tile=i)
    y = _dot(ya.astype(BF16), wa_ref[...]) + _dot(yb.astype(BF16), wb_ref[...])
    o_ref[...] = pick(i, *x_refs) + gt_ref[...] * y


def _out_proj(mix_inputs, x_lat, x_ctx, rows, mod, layer, wa, wb, nb):
    tm = OPROJ_TM
    half = wa.shape[0]
    mix, mix_args, mix_specs = mix_inputs(lambda i: i)
    x_args, x_specs, pick = _split_rows(x_lat, x_ctx, tm, D)
    return pl.pallas_call(
        functools.partial(_oproj_kernel, n_mix=len(mix_args), mix=mix, n_x=len(x_args), pick=pick),
        grid=(rows // tm,),
        in_specs=list(mix_specs) + x_specs + [
            _mod_spec(layer, 2, tm, nb),
            pl.BlockSpec((half, D), lambda i: (0, 0), pipeline_mode=pl.Buffered(1)),
            pl.BlockSpec((half, D), lambda i: (0, 0), pipeline_mode=pl.Buffered(1))],
        out_specs=pl.BlockSpec((tm, D), lambda i: (i, 0)),
        out_shape=jax.ShapeDtypeStruct((rows, D), F32),
        compiler_params=_params(("parallel",), V7X_VMEM_LIMIT),
        name=f"out_proj_{layer}",
    )(*mix_args, *x_args, mod, wa, wb)


def _shifted(u, prev_row, next_row):
    tm = u.shape[0]
    row = lax.broadcasted_iota(jnp.int32, u.shape, 0)
    up = jnp.where(row == 0, prev_row, pltpu.roll(u, 1, axis=0))
    un = jnp.where(row == tm - 1, next_row, pltpu.roll(u, tm - 1, axis=0))
    return 0.5 * (up + un)


def _rwkv_prep_kernel(u_ref, ul_ref, up_ref, un_ref, ulp_ref, uln_ref, mu_ref, mul_ref, w0_ref, wup_ref,
                      a0_ref, aup_ref, gup_ref, kk_ref, ka_ref, rk_ref, gd_ref, gu_ref,
                      r_out, k_out, v_out, a_out, b_out, lw_out, g_out, bonus_out):
    u = u_ref[...]
    u = u + mu_ref[...] * (_shifted(u, up_ref[...], un_ref[...]) - u)
    ul = ul_ref[...]
    ul = ul + mul_ref[...] * (_shifted(ul, ulp_ref[...], uln_ref[...]) - ul)
    r, k, v = u[:, 0:RW], u[:, RW:2 * RW], u[:, 2 * RW:3 * RW]

    th = jnp.tanh(ul).astype(BF16)
    for d in range(2):
        w_log = -_softplus(-(w0_ref[d:d + 1, :] + _dot(th, wup_ref[d]))) - 0.5
        lw_out[d] = -jnp.exp(w_log)
    a = _sigmoid(a0_ref[...] + _dot(ul.astype(BF16), aup_ref[...]))
    g_out[...] = _dot(_sigmoid(ul).astype(BF16), gup_ref[...])

    kk = k * kk_ref[...]
    nrm = jnp.sqrt(_seg_sum(kk * kk, gd_ref[...], gu_ref[...]))
    kk = kk / jnp.maximum(nrm, 1e-12)
    k = k * (1.0 + (a - 1.0) * ka_ref[...])
    r_out[...] = r
    k_out[...] = k
    v_out[...] = v
    a_out[...] = -kk
    b_out[...] = kk * a
    bonus_out[...] = _seg_sum(r * k * rk_ref[...], gd_ref[...], gu_ref[...]) * v


def _seq_halo(u, tm, nb):
    rows = u.shape[0]
    nblk = rows // tm
    starts = np.arange(nblk) * tm
    seq_len = np.where(starts < nb * T, T, C)
    seq_off = np.where(starts < nb * T, starts % T, (starts - nb * T) % C)
    has_prev = seq_off > 0
    has_next = seq_off + tm < seq_len
    prev_idx = np.where(has_prev, starts - 1, 0)
    next_idx = np.where(has_next, starts + tm, 0)
    up = jnp.where(has_prev[:, None], u[prev_idx], 0.0)
    un = jnp.where(has_next[:, None], u[next_idx], 0.0)
    return up[:, None, :], un[:, None, :]


def _rwkv_prepare(u, nb, mu, w0, w_up, a0, a_up, g_up, k_k, k_a, r_k):
    rows = u.shape[0]
    tm = 256
    nblk = rows // tm
    up, un = _seq_halo(u, tm, nb)
    pad = jnp.zeros((LORA,), F32)
    mu_rkv = mu[:3 * RW].reshape(1, 3 * RW)
    mu_l = jnp.concatenate([mu[3 * RW:], pad]).reshape(1, 4 * LORA)

    def lora_w(w, slot):
        z = jnp.zeros((4 * LORA, RW), F32)
        return z.at[slot * LORA:(slot + 1) * LORA].set(w).astype(BF16)

    wup = jnp.stack([lora_w(w_up[0], 0), lora_w(w_up[1], 0)])
    aup = lora_w(a_up, 1)
    gup = lora_w(g_up, 2)
    head = np.arange(RW) // HD
    g_down = jnp.asarray(head[:, None] == np.arange(128)[None, :], BF16)
    g_upm = jnp.asarray(np.arange(128)[:, None] == head[None, :], BF16)

    row = lambda w: pl.BlockSpec((tm, w), lambda i: (i, 0))
    vec = lambda w: pl.BlockSpec((1, w), lambda i: (0, 0))
    full = lambda *s: pl.BlockSpec(s, lambda i: (0,) * len(s))
    out_sd = jax.ShapeDtypeStruct((rows, RW), F32)
    outs = pl.pallas_call(
        _rwkv_prep_kernel,
        grid=(nblk,),
        in_specs=[pl.BlockSpec((tm, 3 * RW), lambda i: (i, 0)),
                  pl.BlockSpec((tm, 4 * LORA), lambda i: (i, AB_LORA // (4 * LORA))),
                  pl.BlockSpec((None, 1, 3 * RW), lambda i: (i, 0, 0)),
                  pl.BlockSpec((None, 1, 3 * RW), lambda i: (i, 0, 0)),
                  pl.BlockSpec((None, 1, 4 * LORA), lambda i: (i, 0, AB_LORA // (4 * LORA))),
                  pl.BlockSpec((None, 1, 4 * LORA), lambda i: (i, 0, AB_LORA // (4 * LORA))),
                  vec(3 * RW), vec(4 * LORA), full(2, RW), full(2, 4 * LORA, RW),
                  vec(RW), full(4 * LORA, RW), full(4 * LORA, RW), vec(RW), vec(RW), vec(RW),
                  full(RW, 128), full(128, RW)],
        out_specs=[row(RW), row(RW), row(RW), row(RW), row(RW),
                   pl.BlockSpec((2, tm, RW), lambda i: (0, i, 0)), row(RW), row(RW)],
        out_shape=[out_sd, out_sd, out_sd, out_sd, out_sd,
                   jax.ShapeDtypeStruct((2, rows, RW), F32), out_sd, out_sd],
        compiler_params=_params(("parallel",), V7X_VMEM_LIMIT),
        name="rwkv_prepare",
    )(u, u, up, un, up, un, mu_rkv, mu_l, w0, wup, a0.reshape(1, RW), aup, gup,
      k_k.reshape(1, RW), k_a.reshape(1, RW), r_k.reshape(1, RW), g_down, g_upm)
    return outs


def _chunk_block(d, b, s, nb, rows=CHUNK):
    nctx, nlat = C // rows, T // rows
    pos_c = jnp.where(d == 0, s, nctx - 1 - s)
    pos_l = jnp.where(d == 0, s - nctx, nlat - 1 - (s - nctx))
    return jnp.where(s < nctx, nb * nlat + b * nctx + pos_c, b * nlat + pos_l)


def _lane_stack(x, hw):
    head = lax.broadcasted_iota(jnp.int32, x.shape, 1) // hw
    return jnp.concatenate([jnp.where(head == h, x, 0.0) for h in range(GROUP // hw)], axis=0)


def _fold_rows(x, n):
    out = x[0:n]
    for h in range(1, x.shape[0] // n):
        out = out + x[h * n:(h + 1) * n]
    return out


def _block_masks(size, blk, d):
    row = lax.broadcasted_iota(jnp.int32, (size, size), 0)
    col = lax.broadcasted_iota(jnp.int32, (size, size), 1)
    rr, cc = row % blk, col % blk
    same = (row // blk) == (col // blk)
    if d == 0:
        return cc <= rr, cc < rr, same
    return cc >= rr, cc > rr, same


def _rwkv_scan_kernel(*refs):
    in_refs = (refs[0:6], refs[6:12])
    y_refs = refs[12:14]
    s_ref = refs[14]

    @pl.when(pl.program_id(1) == 0)
    def _():
        s_ref[...] = jnp.zeros_like(s_ref)

    n = CHUNK
    ngrp = RW // GROUP
    nch = 2 * ngrp
    rrow = lax.broadcasted_iota(jnp.int32, (n, GROUP), 0)
    rcol = lax.broadcasted_iota(jnp.int32, (n, GROUP), 1) % n
    eye_row = jnp.where(rrow == rcol, 1.0, 0.0)
    incl_row = (rcol <= rrow, rcol >= rrow)
    strict_row = (rcol < rrow, rcol > rrow)
    same = _block_masks(GROUP, n, 0)[2]

    def block_diag(x_row):
        xb = x_row.astype(BF16)
        return jnp.where(same, jnp.concatenate([xb] * (GROUP // n), axis=0), jnp.zeros((), BF16))

    def stack(x):
        return _lane_stack(x.astype(BF16), HD)

    pre = []
    for d in range(2):
        tri = jnp.where(_block_masks(n, n, d)[0], 1.0, 0.0).astype(BF16)
        order = range(SCAN_CHUNKS) if d == 0 else range(SCAN_CHUNKS - 1, -1, -1)
        pre_d = []
        for ci in order:
            rows = slice(ci * n, (ci + 1) * n)
            r, k, v, a, b, lw = [ref[rows, :] for ref in in_refs[d]]
            c = _tri_dot(tri, lw)
            c_last = c[n - 1:n] if d == 0 else c[0:1]
            p_inv = jnp.exp(-c)
            p_end = jnp.exp(c_last - c)
            pre_d.append(dict(rows=rows, at=a * jnp.exp(c - lw), rt=r * jnp.exp(c), bt=b * p_inv, kt=k * p_inv,
                              bp=b * p_end, kp=k * p_end, v=v, dec=jnp.exp(c_last)))
        pre.append(pre_d)
    chains = [(d, p, g) for p in range(SCAN_CHUNKS) for d in range(2) for g in range(ngrp)]
    nch = len(chains)
    sl = lambda g: slice(g * GROUP, (g + 1) * GROUP)
    part = lambda d, p, g, name: pre[d][p][name][:, sl(g)]

    a_st = [stack(part(d, p, g, "at")) for d, p, g in chains]
    v_st = [stack(part(d, p, g, "v")) for d, p, g in chains]
    gram = [_dot_nt(jnp.concatenate([part(d, p, g, "at"), part(d, p, g, "rt")], axis=0).astype(BF16),
                    jnp.concatenate([stack(part(d, p, g, "bt")), stack(part(d, p, g, "kt"))], axis=0))
            for d, p, g in chains]
    a_ab = [jnp.where(strict_row[d], gram[i][0:n, 0:GROUP], 0.0) for i, (d, p, g) in enumerate(chains)]
    a_ak = [jnp.where(strict_row[d], gram[i][0:n, GROUP:], 0.0).astype(BF16) for i, (d, p, g) in enumerate(chains)]
    a_r = [jnp.where(jnp.concatenate([incl_row[d]] * 2, axis=1), gram[i][n:2 * n, :], 0.0).astype(BF16)
           for i, (d, p, g) in enumerate(chains)]
    av = [_dot(a_ak[i], v_st[i]) for i in range(nch)]
    tm = [eye_row + m for m in a_ab]
    pw = [_dot(m.astype(BF16), block_diag(m)) for m in a_ab]
    for _ in range(4):
        both = [_dot(jnp.concatenate([t, p], axis=0).astype(BF16), block_diag(p)) for t, p in zip(tm, pw)]
        tm = [t + x[0:n] for t, x in zip(tm, both)]
        pw = [x[n:2 * n] for x in both]
    tm = [t + _dot(t.astype(BF16), block_diag(p)) for t, p in zip(tm, pw)]
    tx = [_dot(tm[i].astype(BF16), jnp.concatenate([a_st[i], stack(av[i])], axis=1))
          for i in range(nch)]

    state = {(d, g): s_ref[d, g] for d in range(2) for g in range(ngrp)}
    for p in range(SCAN_CHUNKS):
        ids = [i for i, ch in enumerate(chains) if ch[1] == p]
        uy0 = {i: _dot_nt(jnp.concatenate([tx[i][:, 0:GROUP], part(*chains[i], "rt")], axis=0).astype(BF16),
                          state[chains[i][0], chains[i][2]].astype(BF16)) for i in ids}
        u = {i: uy0[i][0:n] + tx[i][:, GROUP:] for i in ids}
        for i in ids:
            d, _, g = chains[i]
            y_refs[d][pre[d][p]["rows"], sl(g)] = (
                uy0[i][n:2 * n] + _dot(a_r[i], jnp.concatenate([stack(u[i]), v_st[i]], axis=0)))
        for i in ids:
            d, _, g = chains[i]
            upd = _dot_tn(jnp.concatenate([u[i], part(d, p, g, "v")], axis=0).astype(BF16),
                          jnp.concatenate([part(d, p, g, "bp"), part(d, p, g, "kp")], axis=0).astype(BF16))
            state[d, g] = state[d, g] * part(d, p, g, "dec") + jnp.where(same, upd, 0.0)
    for (d, g), s_new in state.items():
        s_ref[d, g] = s_new


def _rwkv_scan(r, k, v, a, b, lw, nb):
    rows = r.shape[0]
    blk_rows = SCAN_CHUNKS * CHUNK
    steps = (T + C) // blk_rows
    in_specs, args = [], []
    for d in range(2):
        blk = lambda bb, s, d=d: (_chunk_block(d, bb, s, nb, blk_rows), 0)
        in_specs += [pl.BlockSpec((blk_rows, RW), blk)] * 5
        in_specs.append(pl.BlockSpec((None, blk_rows, RW),
                                     lambda bb, s, d=d: (d, _chunk_block(d, bb, s, nb, blk_rows), 0)))
        args += [r, k, v, a, b, lw]
    out_sd = jax.ShapeDtypeStruct((rows, RW), F32)
    return pl.pallas_call(
        _rwkv_scan_kernel,
        grid=(nb, steps),
        in_specs=in_specs,
        out_specs=[pl.BlockSpec((blk_rows, RW), lambda bb, s, d=d: (_chunk_block(d, bb, s, nb, blk_rows), 0))
                   for d in range(2)],
        out_shape=[out_sd, out_sd],
        scratch_shapes=[pltpu.VMEM((2, RW // GROUP, GROUP, GROUP), F32)],
        compiler_params=_params(("parallel", "arbitrary"), V7X_VMEM_LIMIT),
        name="rwkv_scan",
    )(*args)


def _ab_mix(y0_ref, y1_ref, bonus_ref, g_ref, lnw_ref, lnb_ref, gd_ref, gu_ref, *yb_refs, pick, tile):
    y = y0_ref[...] + y1_ref[...]
    mu = _seg_sum(y, gd_ref[...], gu_ref[...]) * (1.0 / HD)
    yc = y - mu
    var = _seg_sum(yc * yc, gd_ref[...], gu_ref[...]) * (1.0 / HD)
    yn = yc * lax.rsqrt(var + RWKV_GN_EPS)
    return (yn * lnw_ref[...] + lnb_ref[...] + bonus_ref[...]) * g_ref[...], pick(tile, *yb_refs)


def _ab_mix_inputs(tile, y0, y1, bonus, g, ln_w, ln_b, yb_lat, yb_ctx):
    tm = OPROJ_TM
    head = np.arange(RW) // HD
    g_down = jnp.asarray(head[:, None] == np.arange(128)[None, :], BF16)
    g_upm = jnp.asarray(np.arange(128)[:, None] == head[None, :], BF16)
    row = pl.BlockSpec((tm, RW), lambda i: (tile(i), 0))
    vec = pl.BlockSpec((1, RW), lambda i: (0, 0))
    yb_args, yb_specs, pick = _split_rows(yb_lat, yb_ctx, tm, 1024, tile=tile)
    specs = [row, row, row, row, vec, vec,
             pl.BlockSpec((RW, 128), lambda i: (0, 0)), pl.BlockSpec((128, RW), lambda i: (0, 0))] + yb_specs
    args = (y0, y1, bonus, g, ln_w.reshape(1, RW), ln_b.reshape(1, RW), g_down, g_upm) + yb_args
    return functools.partial(_ab_mix, pick=pick), args, specs


def _kv_block(i, nb):
    per = T // 256
    return jnp.where(i < nb * per, i // per, i - nb * per), jnp.where(i < nb * per, i % per, per)


def _swa_prep_kernel(k_ref, v_ref, cos_ref, sin_ref, ko_ref, vo_ref):
    cosl, sinl = cos_ref[...], sin_ref[...]
    lane = lax.broadcasted_iota(jnp.int32, cosl.shape, 1)
    low = lane < HD
    for src, dst, rope in ((k_ref, ko_ref, True), (v_ref, vo_ref, False)):
        for t in range(2):
            x = src[:, t * 128:(t + 1) * 128]
            if rope:
                x = _rope(x, cosl, sinl)
            swapped = pltpu.roll(x, HD, axis=1)
            dst[:, (2 * t) * 128:(2 * t + 1) * 128] = jnp.where(low, x, swapped).astype(BF16)
            dst[:, (2 * t + 1) * 128:(2 * t + 2) * 128] = jnp.where(low, swapped, x).astype(BF16)


def _swa_prep(u, cosl, sinl, nb):
    rows = u.shape[0]
    tm = 256
    kv_out = pl.BlockSpec((None, tm, 512), lambda i: (*_kv_block(i, nb), 0))
    sd = jax.ShapeDtypeStruct((nb, T + C, 512), BF16)
    return pl.pallas_call(
        _swa_prep_kernel,
        grid=(rows // tm,),
        in_specs=[pl.BlockSpec((tm, 256), lambda i: (i, AB_KS // 256)),
                  pl.BlockSpec((tm, 256), lambda i: (i, AB_VS // 256)),
                  pl.BlockSpec((tm, 128), lambda i: (i, 0)),
                  pl.BlockSpec((tm, 128), lambda i: (i, 0))],
        out_specs=[kv_out, kv_out],
        out_shape=[sd, sd],
        compiler_params=_params(("parallel",), V7X_VMEM_LIMIT),
        name="swa_prep",
    )(u, u, cosl, sinl)


def _diff_prep_kernel(k_ref, v_ref, cos_ref, sin_ref, ko_ref, vo_ref):
    cosl, sinl = cos_ref[...], sin_ref[...]
    for t in range(8):
        sl = slice(t * 128, (t + 1) * 128)
        ko_ref[:, sl] = _rope(k_ref[:, sl], cosl, sinl).astype(BF16)
    vo_ref[...] = v_ref[...].T.astype(BF16)


def _diff_prep(u, cosl, sinl, nb):
    rows = u.shape[0]
    tm = 256
    k_out = pl.BlockSpec((None, tm, 1024), lambda i: (*_kv_block(i, nb), 0))
    vt_out = pl.BlockSpec((None, 1024, tm), lambda i: (_kv_block(i, nb)[0], 0, _kv_block(i, nb)[1]))
    return pl.pallas_call(
        _diff_prep_kernel,
        grid=(rows // tm,),
        in_specs=[pl.BlockSpec((tm, 1024), lambda i: (i, 1)),
                  pl.BlockSpec((tm, 1024), lambda i: (i, 2)),
                  pl.BlockSpec((tm, 128), lambda i: (i, 0)),
                  pl.BlockSpec((tm, 128), lambda i: (i, 0))],
        out_specs=[k_out, vt_out],
        out_shape=[jax.ShapeDtypeStruct((nb, T + C, 1024), BF16), jax.ShapeDtypeStruct((nb, 1024, T + C), BF16)],
        compiler_params=_params(("parallel",), V7X_VMEM_LIMIT),
        name="diff_prep",
    )(u, u, cosl, sinl)


def _swa_kernel(q_ref, cos_ref, sin_ref, k_ref, v_ref, sink_ref, o_ref, *, ctx_mode):
    n = pl.program_id(1)
    cosl, sinl = cos_ref[...], sin_ref[...]
    QB = q_ref.shape[0]
    span = QB + 2 * SWA_WINDOW
    if not ctx_mode:
        start = pl.multiple_of(jnp.clip(n * QB - SWA_WINDOW, 0, T - span), SWA_WINDOW)
        kpos = start + lax.broadcasted_iota(jnp.int32, (2 * QB, span), 1)
        qpos = n * QB + lax.broadcasted_iota(jnp.int32, (2 * QB, span), 0) % QB
        valid = jnp.abs(kpos - qpos) <= SWA_WINDOW
    low = lax.broadcasted_iota(jnp.int32, (QB, 128), 1) < HD
    top = lax.broadcasted_iota(jnp.int32, (2 * QB, 1), 0) < QB
    tile = lambda j: slice((j // 2) * 128, (j // 2 + 1) * 128)
    qs, sink = [], []
    for j in range(8):
        q = _rope(q_ref[:, j * 128:(j + 1) * 128], cosl, sinl) * (HD ** -0.5 * LOG2E)
        qs.append(jnp.concatenate([jnp.where(low, q, 0.0), jnp.where(low, 0.0, q)], axis=0).astype(BF16))
        sink.append(jnp.where(top, sink_ref[2 * j:2 * j + 1, 0:1], sink_ref[2 * j + 1:2 * j + 2, 0:1]) * LOG2E)
    s_c = [_dot_nt(qs[j], k_ref[T:T + C, tile(j)]) for j in range(8)]
    m = [jnp.maximum(jnp.max(s, axis=-1, keepdims=True), sk) for s, sk in zip(s_c, sink)]
    if not ctx_mode:
        s_w = [jnp.where(valid, _dot_nt(qs[j], k_ref[pl.ds(start, span), tile(j)]), -jnp.inf) for j in range(8)]
        m = [jnp.maximum(mm, jnp.max(s, axis=-1, keepdims=True)) for mm, s in zip(m, s_w)]
    p_c = [jnp.exp2(s - mm) for s, mm in zip(s_c, m)]
    den = [jnp.sum(p, axis=-1, keepdims=True) + jnp.exp2(sk - mm) for p, sk, mm in zip(p_c, sink, m)]
    pv = [_dot(p.astype(BF16), v_ref[T:T + C, tile(j)]) for j, p in enumerate(p_c)]
    if not ctx_mode:
        p_w = [jnp.exp2(s - mm) for s, mm in zip(s_w, m)]
        den = [dd + jnp.sum(p, axis=-1, keepdims=True) for dd, p in zip(den, p_w)]
        pv = [x + _dot(p.astype(BF16), v_ref[pl.ds(start, span), tile(j)]) for j, (x, p) in enumerate(zip(pv, p_w))]
    for j in range(8):
        o = pv[j] / den[j]
        o_ref[:, j * 128:(j + 1) * 128] = jnp.where(low, o[0:QB], o[QB:2 * QB])


def _swa(u, cosl, sinl, kpad, vpad, sink, nb, ctx_mode):
    QB = SWA_QB
    nq = (C if ctx_mode else T) // QB
    base = nb * (T // QB) if ctx_mode else 0
    rows = nb * nq * QB
    rowblk = lambda b, n: base + b * nq + n
    kv = pl.BlockSpec((None, T + C, 512), lambda b, n: (b, 0, 0))
    return pl.pallas_call(
        functools.partial(_swa_kernel, ctx_mode=ctx_mode),
        grid=(nb, nq),
        in_specs=[pl.BlockSpec((QB, 1024), lambda b, n: (rowblk(b, n), AB_Q // 1024)),
                  pl.BlockSpec((QB, 128), lambda b, n: (rowblk(b, n), 0)),
                  pl.BlockSpec((QB, 128), lambda b, n: (rowblk(b, n), 0)),
                  kv, kv, pl.BlockSpec((16, 128), lambda b, n: (0, 0))],
        out_specs=pl.BlockSpec((QB, 1024), lambda b, n: (b * nq + n, 0)),
        out_shape=jax.ShapeDtypeStruct((rows, 1024), F32),
        compiler_params=_params(("parallel", "arbitrary"), V7X_VMEM_LIMIT),
        name="swa_ctx" if ctx_mode else "swa_latent",
    )(u, cosl, sinl, kpad, vpad, sink)


def _diff_kernel(q_ref, cos_ref, sin_ref, k_ref, vt_ref, lam_ref, sub_ref, o_ref):
    cosl, sinl = cos_ref[...], sin_ref[...]
    lam = lam_ref[...]
    nq = q_ref.shape[0]
    low = lax.broadcasted_iota(jnp.int32, (nq, 128), 1) < HD
    tile = lambda h: slice(h * 128, (h + 1) * 128)

    def logits(h):
        q = _rope(q_ref[:, tile(h)], cosl, sinl) * (HD ** -0.5 * LOG2E)
        qs = jnp.concatenate([jnp.where(low, q, 0.0), jnp.where(low, 0.0, q)], axis=0).astype(BF16)
        return _dot_nt(k_ref[:, tile(h)], qs)

    s_next = logits(0)
    for h in range(8):
        s = s_next
        if h + 1 < 8:
            s_next = logits(h + 1)
        e = jnp.exp2(s - jnp.max(s, axis=0, keepdims=True))
        inv = 1.0 / jnp.sum(e, axis=0, keepdims=True)
        o2 = _dot(vt_ref[tile(h), :], e.astype(BF16))
        o = o2[:, 0:nq] * inv[:, 0:nq] - o2[:, nq:2 * nq] * (lam[:, 0:1] * inv[:, nq:2 * nq])
        o = o * lax.rsqrt(jnp.mean(o * o, axis=0, keepdims=True) + 1e-5)
        o_ref[:, tile(h)] = o.T * sub_ref[...]


def _diff_attn(u, cosl, sinl, kb, vt, lam, sub, nb):
    QB = DIFF_QB
    nq = T // QB
    kv = pl.BlockSpec((None, T + C, 1024), lambda b, n: (b, 0, 0))
    vec = pl.BlockSpec((1, 128), lambda b, n: (0, 0))
    return pl.pallas_call(
        _diff_kernel,
        grid=(nb, nq),
        in_specs=[pl.BlockSpec((QB, 1024), lambda b, n: (b * nq + n, 0)),
                  pl.BlockSpec((QB, 128), lambda b, n: (b * nq + n, 0)),
                  pl.BlockSpec((QB, 128), lambda b, n: (b * nq + n, 0)),
                  kv, pl.BlockSpec((None, 1024, T + C), lambda b, n: (b, 0, 0)), vec, vec],
        out_specs=pl.BlockSpec((QB, 1024), lambda b, n: (b * nq + n, 0)),
        out_shape=jax.ShapeDtypeStruct((nb * T, 1024), F32),
        compiler_params=_params(("parallel", "arbitrary"), V7X_VMEM_LIMIT),
        name="diff_attn",
    )(u, cosl, sinl, kb, vt, lam, sub)


def _hgrn_exact_att(q, kk, bcum, tmp_ref):
    n = CHUNK
    tmp_ref[0] = bcum
    tmp_ref[1] = kk
    coli = lax.broadcasted_iota(jnp.int32, (n, 2 * n), 1)

    def body(s, acc):
        bs = tmp_ref[0, pl.ds(s, 1), :]
        ks = tmp_ref[1, pl.ds(s, 1), :]
        w = q * jnp.exp(jnp.minimum(bcum - bs, 0.0)) * ks
        c0 = jnp.sum(w[:, 0:HG_DK], axis=-1, keepdims=True)
        c1 = jnp.sum(w[:, HG_DK:], axis=-1, keepdims=True)
        return acc + jnp.concatenate([jnp.where(coli == s, c0, 0.0), jnp.where(coli == s + n, c1, 0.0)], axis=0)

    return lax.fori_loop(0, n, body, jnp.zeros((2 * n, 2 * n), F32))


def _hgrn_scan_kernel(q0_ref, z0_ref, i0_ref, lb0_ref, q1_ref, z1_ref, i1_ref, lb1_ref,
                      o0_ref, o1_ref, s_ref, g_ref, tmp_ref):
    in_refs = ((q0_ref, z0_ref, i0_ref, lb0_ref), (q1_ref, z1_ref, i1_ref, lb1_ref))
    o_refs = (o0_ref, o1_ref)

    @pl.when(pl.program_id(1) == 0)
    def _():
        s_ref[...] = jnp.zeros_like(s_ref)

    n = CHUNK
    ngrp = 1024 // GROUP
    nsub = n // SUB
    pre = []
    for d in range(2):
        tri = jnp.where(_block_masks(n, n, d)[0], 1.0, 0.0).astype(BF16)
        lb = in_refs[d][3][...]
        order = range(SCAN_CHUNKS) if d == 0 else range(SCAN_CHUNKS - 1, -1, -1)
        pre_d = []
        for ci in order:
            rows = slice(ci * n, (ci + 1) * n)
            q, z, v = [ref[rows, :] for ref in in_refs[d][0:3]]
            logf = jnp.log(lb + (1.0 - lb) * _sigmoid(z))
            kk = (1.0 - lb) * _sigmoid(-z)
            bcum = _tri_dot(tri, logf)
            b_last = bcum[n - 1:n] if d == 0 else bcum[0:1]
            b_excl = bcum - logf
            qh, kh = [], []
            for sb in range(nsub):
                lo, hi = sb * SUB, (sb + 1) * SUB
                beta = b_excl[lo:lo + 1] if d == 0 else b_excl[hi - 1:hi]
                qh.append(q[lo:hi] * jnp.exp(bcum[lo:hi] - beta))
                kh.append(kk * jnp.exp(beta - bcum))
            pre_d.append(dict(rows=rows, q=q, v=v, kk=kk, bcum=bcum, qh=qh, kh=kh, qe=q * jnp.exp(bcum),
                              ke=kk * jnp.exp(b_last - bcum), dec=jnp.exp(b_last), min_logf=jnp.min(logf)))
        pre.append(pre_d)
    chains = [(d, p, g) for p in range(SCAN_CHUNKS) for d in range(2) for g in range(ngrp)]
    sl = lambda g: slice(g * GROUP, (g + 1) * GROUP)

    for i, (d, p, g) in enumerate(chains):
        c = pre[d][p]
        rows = [_dot_nt(_lane_stack(c["qh"][sb][:, sl(g)].astype(BF16), HG_DK),
                        _lane_stack(c["kh"][sb][:, sl(g)].astype(BF16), HG_DK)) for sb in range(nsub)]
        g_ref[i] = jnp.concatenate([rows[sb][h * SUB:(h + 1) * SUB] for h in range(2) for sb in range(nsub)], axis=0)

    min_logf = functools.reduce(jnp.minimum, [c["min_logf"] for pre_d in pre for c in pre_d])

    @pl.when(min_logf < -(HG_CLAMP / SUB))
    def _():
        for i, (d, p, g) in enumerate(chains):
            c = pre[d][p]
            g_ref[i] = _hgrn_exact_att(c["q"][:, sl(g)], c["kk"][:, sl(g)], c["bcum"][:, sl(g)], tmp_ref)

    state = {(d, g): s_ref[d, g] for d in range(2) for g in range(ngrp)}
    head_same = _block_masks(GROUP, HG_DK, 0)[2]
    for p in range(SCAN_CHUNKS):
        ids = [i for i, ch in enumerate(chains) if ch[1] == p]
        inter = {i: _dot_nt(pre[chains[i][0]][p]["qe"][:, sl(chains[i][2])].astype(BF16),
                            state[chains[i][0], chains[i][2]].astype(BF16)) for i in ids}
        for i in ids:
            d, _, g = chains[i]
            incl, _, same = _block_masks(2 * n, n, d)
            att = jnp.where(incl & same, g_ref[i], 0.0).astype(BF16)
            v_st = _lane_stack(pre[d][p]["v"][:, sl(g)].astype(BF16), HG_DK)
            o_refs[d][pre[d][p]["rows"], sl(g)] = inter[i] + _fold_rows(_dot(att, v_st), n)
        for i in ids:
            d, _, g = chains[i]
            upd = _dot_tn(pre[d][p]["v"][:, sl(g)].astype(BF16), pre[d][p]["ke"][:, sl(g)].astype(BF16))
            state[d, g] = state[d, g] * pre[d][p]["dec"][:, sl(g)] + jnp.where(head_same, upd, 0.0)
    for (d, g), s_new in state.items():
        s_ref[d, g] = s_new


def _hgrn_scan(u, lb, nb):
    rows = u.shape[0]
    blk_rows = SCAN_CHUNKS * CHUNK
    steps = (T + C) // blk_rows
    in_specs, args = [], []
    for d in range(2):
        col = lambda j, d=d: pl.BlockSpec((blk_rows, 1024),
                                          lambda bb, s: (_chunk_block(d, bb, s, nb, blk_rows), j))
        in_specs += [col(3), col(4 + d), col(6), pl.BlockSpec((None, 1, 1024), lambda bb, s, d=d: (d, 0, 0))]
        args += [u, u, u, lb]
    out_sd = jax.ShapeDtypeStruct((rows, 1024), F32)
    ngrp = 1024 // GROUP
    return pl.pallas_call(
        _hgrn_scan_kernel,
        grid=(nb, steps),
        in_specs=in_specs,
        out_specs=[pl.BlockSpec((blk_rows, 1024), lambda bb, s, d=d: (_chunk_block(d, bb, s, nb, blk_rows), 0))
                   for d in range(2)],
        out_shape=[out_sd, out_sd],
        scratch_shapes=[pltpu.VMEM((2, ngrp, GROUP, GROUP), F32),
                        pltpu.VMEM((2 * ngrp * SCAN_CHUNKS, 2 * CHUNK, 2 * CHUNK), F32),
                        pltpu.VMEM((2, CHUNK, GROUP), F32)],
        compiler_params=_params(("parallel", "arbitrary"), V7X_VMEM_LIMIT),
        name="hgrn_scan",
    )(*args)


def _cd_mix(yc_ref, o0_ref, o1_ref, g_ref, gn_ref, *, tile):
    g = g_ref[...]
    heads = []
    for h in range(8):
        sl = slice(h * 128, (h + 1) * 128)
        o = o0_ref[:, sl] + o1_ref[:, sl]
        o = o * lax.rsqrt(jnp.mean(o * o, axis=-1, keepdims=True) + NORM_EPS) * gn_ref[...]
        gh = g[:, sl]
        heads.append(o * (gh * _sigmoid(gh)))
    return yc_ref[...], jnp.concatenate(heads, axis=1)


def _cd_mix_inputs(tile, yc, o0, o1, u, gnorm):
    tm = OPROJ_TM
    row = pl.BlockSpec((tm, 1024), lambda i: (tile(i), 0))
    specs = [row, row, row, pl.BlockSpec((tm, 1024), lambda i: (tile(i), 7)),
             pl.BlockSpec((1, 128), lambda i: (0, 0))]
    return _cd_mix, (yc, o0, o1, u, gnorm.reshape(1, 128)), specs


def _rope_tables(nb):
    t = np.arange(T)
    quarter = HD // 4
    inv = ROPE_THETA ** (-jnp.arange(quarter, dtype=F32) / quarter)
    rows = jnp.asarray(t // GRID_W, F32)
    cols = jnp.asarray(t % GRID_W, F32)
    ang = jnp.concatenate([rows[:, None] * inv, cols[:, None] * inv], axis=-1)
    cos, sin = jnp.cos(ang), jnp.sin(ang)
    cosl = jnp.tile(jnp.concatenate([cos, cos], axis=-1), (nb, 2))
    sinl = jnp.tile(jnp.concatenate([-sin, sin], axis=-1), (nb, 2))
    cosl = jnp.concatenate([cosl, jnp.ones((nb * C, 128), F32)], axis=0)
    sinl = jnp.concatenate([sinl, jnp.zeros((nb * C, 128), F32)], axis=0)
    return cosl, sinl


def kernel(x, c, ctx, c_ctx, ada_w, ada_b, mlp_w1, mlp_w2, final_norm, ab_w_in, ab_w_out, ab_mu, ab_w0,
           ab_w_up, ab_a0, ab_a_up, ab_g_up, ab_k_k, ab_k_a, ab_r_k, ab_ln_w, ab_ln_b, ab_sink,
           cd_w_in, cd_w_out, cd_lam, cd_subln, cd_gnorm, hgrn_lb_logits):
    nb = x.shape[0]
    assert x.shape == (nb, T, D) and ctx.shape == (nb, C, D) and nb < 8
    assert ada_w.shape[0] == 2, "one AB layer followed by one CD layer"
    rx = nb * T
    rows = rx + nb * C
    x2, c2 = x.reshape(rx, D), ctx.reshape(nb * C, D)
    cosl, sinl = _rope_tables(nb)

    s_in = jnp.zeros((8, D), F32).at[:nb].set(c).at[nb].set(c_ctx)
    mod = _ada(s_in, ada_w, ada_b)

    w = ab_w_in[0]
    w_in0 = jnp.concatenate([w[:, :3 * RW], w[:, 3 * RW + 3 * LORA:], w[:, 3 * RW:3 * RW + 3 * LORA],
                             jnp.zeros((D, LORA), F32)], axis=1).astype(BF16)
    u = _norm_mod_matmul(x2, c2, mod, 0, w_in0, nb, 512, AB_N // 2)
    r, k, v, a, b, lw, g, bonus = _rwkv_prepare(u, nb, ab_mu[0], ab_w0[0], ab_w_up[0], ab_a0[0], ab_a_up[0],
                                                ab_g_up[0], ab_k_k[0], ab_k_a[0], ab_r_k[0].reshape(RW))
    y0, y1 = _rwkv_scan(r, k, v, a, b, lw, nb)
    kpad, vpad = _swa_prep(u, cosl, sinl, nb)
    sink = jnp.broadcast_to(ab_sink[0][:, None], (16, 128))
    yb_lat = _swa(u, cosl, sinl, kpad, vpad, sink, nb, False)
    yb_ctx = _swa(u, cosl, sinl, kpad, vpad, sink, nb, True)
    w_out0 = ab_w_out[0].astype(BF16)
    mix_inputs = functools.partial(_ab_mix_inputs, y0=y0, y1=y1, bonus=bonus, g=g, ln_w=ab_ln_w[0],
                                   ln_b=ab_ln_b[0], yb_lat=yb_lat, yb_ctx=yb_ctx)
    xc = _out_proj(mix_inputs, x2, c2, rows, mod, 0, w_out0[:RW], w_out0[RW:], nb)
    fn = final_norm.reshape(1, D)
    xc = _mlp(xc, rows, mod, 0, mlp_w1, mlp_w2, fn, nb, False)

    lam_init = 0.8 - 0.6 * math.exp(-0.3 * 1)
    lb_table = jnp.cumsum(jax.nn.softmax(hgrn_lb_logits.astype(F32), axis=0), axis=0)
    lb = (lb_table - lb_table[0])[1].reshape(2, 1, 1024)
    lf = cd_lam[0].astype(F32)
    lmb = jnp.exp(jnp.sum(lf[0] * lf[1])) - jnp.exp(jnp.sum(lf[2] * lf[3])) + lam_init
    u = _norm_mod_matmul(xc, None, mod, 1, cd_w_in[0], nb, 1024, 1024)
    kb, vb = _diff_prep(u, cosl, sinl, nb)
    yc = _diff_attn(u, cosl, sinl, kb, vb, jnp.full((1, 128), lmb, F32),
                    (cd_subln[0] * (1.0 - lam_init)).reshape(1, 128), nb)
    o0, o1 = _hgrn_scan(u, lb, nb)
    w_out1 = cd_w_out[0].astype(BF16)
    mix_inputs = functools.partial(_cd_mix_inputs, yc=yc, o0=o0, o1=o1, u=u, gnorm=cd_gnorm[0])
    xl = _out_proj(mix_inputs, xc, None, rx, mod, 1, w_out1[:1024], w_out1[1024:], nb)
    out = _mlp(xl, rx, mod, 1, mlp_w1, mlp_w2, fn, nb, True)
    return out.reshape(nb, T, D)
```

```python
import functools
import math

import jax
import jax.numpy as jnp
import numpy as np
from jax import lax
from jax.experimental import pallas as pl
from jax.experimental.pallas import tpu as pltpu

F32 = jnp.float32
BF16 = jnp.bfloat16

D = 2048
T = 2048
C = 256
GRID_W = 64
D_FF = 4 * D
HD = 64
ROPE_THETA = 10000.0
LOG2E = 1.4426950408889634
NORM_EPS = 1e-6
N_MOD = 6
RW = 1024
RWKV_GN_EPS = 64e-5
LORA = 64
SWA_WINDOW = 128
SWA_QB = 128
DIFF_QB = 256
OPROJ_TM = 512
CHUNK = 64
SCAN_CHUNKS = 4
CAST_BLOCKS = 32
SUB = 16
HG_DK = 128
HG_CLAMP = 80.0
GROUP = 256
V7X_VMEM_BYTES = 64 * 1024 * 1024
V7X_VMEM_LIMIT = V7X_VMEM_BYTES - 8 * 1024 * 1024

AB_Q, AB_KS, AB_VS, AB_LORA, AB_N = 3072, 4096, 4352, 4608, 4864


def _dot(a, b):
    return jnp.dot(a, b, preferred_element_type=F32)


def _dot_nt(a, b):
    return lax.dot_general(a, b, (((1,), (1,)), ((), ())), preferred_element_type=F32)


def _dot_tn(a, b):
    return lax.dot_general(a, b, (((0,), (0,)), ((), ())), preferred_element_type=F32)


def _split2(x):
    hi = x.astype(BF16)
    lo = (x - hi.astype(F32)).astype(BF16)
    return hi, lo


def _split3(x):
    hi = x.astype(BF16)
    r1 = x - hi.astype(F32)
    mid = r1.astype(BF16)
    lo = (r1 - mid.astype(F32)).astype(BF16)
    return hi, mid, lo


def _dot_exact_rhs(x, g):
    hi, lo = _split2(x)
    return _dot(hi, g) + _dot(lo, g)


def _tri_dot(tri, x):
    hi, mid, lo = _split3(x)
    return _dot(tri, hi) + _dot(tri, mid) + _dot(tri, lo)


def _seg_sum(x, g_down, g_up):
    return _dot_exact_rhs(_dot_exact_rhs(x, g_down), g_up)


def _rope(x, cosl, sinl):
    w = x.shape[-1]
    lane = lax.broadcasted_iota(jnp.int32, x.shape, x.ndim - 1)
    first = (lane & 63) < 32
    swapped = jnp.where(first, pltpu.roll(x, w - 32, axis=1), pltpu.roll(x, 32, axis=1))
    return x * cosl + swapped * sinl


def _sigmoid(x):
    return 1.0 / (1.0 + jnp.exp(-x))


def _softplus(x):
    return jnp.maximum(x, 0.0) + jnp.log(1.0 + jnp.exp(-jnp.abs(x)))


def _params(sem, vmem=None):
    return pltpu.CompilerParams(dimension_semantics=sem, vmem_limit_bytes=vmem)


def _ada_kernel(s_ref, w_ref, b_ref, o_ref):
    s = s_ref[...]
    s = s * _sigmoid(s)
    o_ref[...] = _dot(s.astype(BF16), w_ref[...].astype(BF16)) + b_ref[...]


def _ada(s_in, ada_w, ada_b):
    depth = ada_w.shape[0]
    n = ada_w.shape[2]
    tn = 1536
    out = pl.pallas_call(
        _ada_kernel,
        grid=(depth, n // tn),
        in_specs=[pl.BlockSpec((8, D), lambda l, j: (0, 0)),
                  pl.BlockSpec((None, D, tn), lambda l, j: (l, 0, j)),
                  pl.BlockSpec((None, 1, tn), lambda l, j: (l, 0, j))],
        out_specs=pl.BlockSpec((None, 8, tn), lambda l, j: (l, 0, j)),
        out_shape=jax.ShapeDtypeStruct((depth, 8, n), F32),
        compiler_params=_params(("arbitrary", "arbitrary"), V7X_VMEM_LIMIT),
        name="ada_mod",
    )(s_in, ada_w, ada_b.reshape(depth, 1, n))
    return out.reshape(depth, 8, 1, n)


def _mod_spec(layer, k, tm, nb, tile=lambda i: i):
    return pl.BlockSpec((None, None, 1, D),
                        lambda i, *_: (layer, jnp.minimum((tile(i) * tm) // T, nb), 0, k))


def _split_rows(lat, ctx, tm, width, col=0, tile=lambda i: i):
    if ctx is None:
        return (lat,), [pl.BlockSpec((tm, width), lambda i, *_: (tile(i), col))], lambda t, ref: ref[...]
    n_lat = lat.shape[0] // tm
    specs = [pl.BlockSpec((tm, width), lambda i, *_: (jnp.minimum(tile(i), n_lat - 1), col)),
             pl.BlockSpec((tm, width), lambda i, *_: (jnp.maximum(tile(i) - n_lat, 0), col),
                          pipeline_mode=pl.Buffered(1))]
    return (lat, ctx), specs, lambda t, lat_ref, ctx_ref: jnp.where(t < n_lat, lat_ref[...], ctx_ref[...])


def _nmm_kernel(*refs, n_x, pick):
    sh_ref, sc_ref, w_ref, o_ref, lhs_ref = refs[n_x:]

    @pl.when(pl.program_id(1) == 0)
    def _():
        x = pick(pl.program_id(0), *refs[:n_x])
        xn = x * lax.rsqrt(jnp.mean(x * x, axis=-1, keepdims=True) + NORM_EPS)
        lhs_ref[...] = (xn * (1.0 + sc_ref[...]) + sh_ref[...]).astype(BF16)

    o_ref[...] = _dot(lhs_ref[...], w_ref[...].astype(BF16)).astype(o_ref.dtype)


def _norm_mod_matmul(x_lat, x_ctx, mod, layer, w, nb, tm, tn):
    rows = x_lat.shape[0] + (0 if x_ctx is None else x_ctx.shape[0])
    n = w.shape[1]
    x_args, x_specs, pick = _split_rows(x_lat, x_ctx, tm, D)
    if x_ctx is None and tm > 512:
        x_specs = [pl.BlockSpec((tm, D), lambda i, j: (i, 0), pipeline_mode=pl.Buffered(1))]
    return pl.pallas_call(
        functools.partial(_nmm_kernel, n_x=len(x_args), pick=pick),
        grid=(rows // tm, n // tn),
        in_specs=x_specs + [_mod_spec(layer, 0, tm, nb), _mod_spec(layer, 1, tm, nb),
                            pl.BlockSpec((D, tn), lambda i, j: (0, j))],
        out_specs=pl.BlockSpec((tm, tn), lambda i, j: (i, j)),
        out_shape=jax.ShapeDtypeStruct((rows, n), F32),
        scratch_shapes=[pltpu.VMEM((tm, D), BF16)],
        compiler_params=_params(("parallel", "arbitrary"), V7X_VMEM_LIMIT),
        name=f"in_proj_{layer}",
    )(*x_args, mod, mod, w)


def _mlp_kernel(x_ref, sh_ref, sc_ref, gt_ref, w1_ref, w2_ref, fn_ref, o_ref, lhs_ref, *, final):
    f = pl.program_id(1)

    @pl.when(f == 0)
    def _():
        x = x_ref[...]
        xn = x * lax.rsqrt(jnp.mean(x * x, axis=-1, keepdims=True) + NORM_EPS)
        lhs_ref[...] = (xn * (1.0 + sc_ref[...]) + sh_ref[...]).astype(BF16)
        o_ref[...] = jnp.zeros_like(o_ref)

    h = jnp.maximum(_dot(lhs_ref[...], w1_ref[...]), 0.0)
    o_ref[...] += _dot((h * h).astype(BF16), w2_ref[...])

    @pl.when(f == pl.num_programs(1) - 1)
    def _():
        y = x_ref[...] + gt_ref[...] * o_ref[...]
        if final:
            y = y * lax.rsqrt(jnp.mean(y * y, axis=-1, keepdims=True) + NORM_EPS) * fn_ref[...]
        o_ref[...] = y


def _mlp(xc, rows, mod, layer, w1, w2, final_norm, nb, final):
    tm, tf = 512, 1024
    return pl.pallas_call(
        functools.partial(_mlp_kernel, final=final),
        grid=(rows // tm, D_FF // tf),
        in_specs=[pl.BlockSpec((tm, D), lambda i, f: (i, 0)),
                  _mod_spec(layer, 3, tm, nb), _mod_spec(layer, 4, tm, nb), _mod_spec(layer, 5, tm, nb),
                  pl.BlockSpec((D, tf), lambda i, f: (0, f)),
                  pl.BlockSpec((tf, D), lambda i, f: (f, 0)),
                  pl.BlockSpec((1, D), lambda i, f: (0, 0))],
        out_specs=pl.BlockSpec((tm, D), lambda i, f: (i, 0)),
        out_shape=jax.ShapeDtypeStruct((rows, D), F32),
        scratch_shapes=[pltpu.VMEM((tm, D), BF16)],
        compiler_params=_params(("parallel", "arbitrary"), V7X_VMEM_LIMIT),
        name=f"mlp_{layer}",
    )(xc, mod, mod, mod, w1, w2, final_norm)


def _oproj_kernel(*refs, n_mix, mix, n_x, pick):
    x_refs = refs[n_mix:n_mix + n_x]
    gt_ref, wa_ref, wb_ref, o_ref = refs[n_mix + n_x:]
    i = pl.program_id(0)
    ya, yb = mix(*refs[:n_mix], tile=i)
    y = _dot(ya.astype(BF16), wa_ref[...]) + _dot(yb.astype(BF16), wb_ref[...])
    o_ref[...] = pick(i, *x_refs) + gt_ref[...] * y


def _out_proj(mix_inputs, x_lat, x_ctx, rows, mod, layer, wa, wb, nb):
    tm = OPROJ_TM
    half = wa.shape[0]
    mix, mix_args, mix_specs = mix_inputs(lambda i: i)
    x_args, x_specs, pick = _split_rows(x_lat, x_ctx, tm, D)
    return pl.pallas_call(
        functools.partial(_oproj_kernel, n_mix=len(mix_args), mix=mix, n_x=len(x_args), pick=pick),
        grid=(rows // tm,),
        in_specs=list(mix_specs) + x_specs + [
            _mod_spec(layer, 2, tm, nb),
            pl.BlockSpec((half, D), lambda i: (0, 0), pipeline_mode=pl.Buffered(1)),
            pl.BlockSpec((half, D), lambda i: (0, 0), pipeline_mode=pl.Buffered(1))],
        out_specs=pl.BlockSpec((tm, D), lambda i: (i, 0)),
        out_shape=jax.ShapeDtypeStruct((rows, D), F32),
        compiler_params=_params(("parallel",), V7X_VMEM_LIMIT),
        name=f"out_proj_{layer}",
    )(*mix_args, *x_args, mod, wa, wb)


def _shifted(u, prev_row, next_row):
    tm = u.shape[0]
    row = lax.broadcasted_iota(jnp.int32, u.shape, 0)
    up = jnp.where(row == 0, prev_row, pltpu.roll(u, 1, axis=0))
    un = jnp.where(row == tm - 1, next_row, pltpu.roll(u, tm - 1, axis=0))
    return 0.5 * (up + un)


def _rwkv_prep_kernel(u_ref, ul_ref, up_ref, un_ref, ulp_ref, uln_ref, mu_ref, mul_ref, w0_ref, wup_ref,
                      a0_ref, aup_ref, gup_ref, kk_ref, ka_ref, rk_ref, gd_ref, gu_ref, w_ref,
                      r_out, k_out, v_out, a_out, b_out, lw_out, g_out, bonus_out, wo_ref):
    wo_ref[...] = w_ref[...].astype(BF16)
    u = u_ref[...]
    u = u + mu_ref[...] * (_shifted(u, up_ref[...], un_ref[...]) - u)
    ul = ul_ref[...]
    ul = ul + mul_ref[...] * (_shifted(ul, ulp_ref[...], uln_ref[...]) - ul)
    r, k, v = u[:, 0:RW], u[:, RW:2 * RW], u[:, 2 * RW:3 * RW]

    th = jnp.tanh(ul).astype(BF16)
    for d in range(2):
        w_log = -_softplus(-(w0_ref[d:d + 1, :] + _dot(th, wup_ref[d]))) - 0.5
        lw_out[d] = -jnp.exp(w_log)
    a = _sigmoid(a0_ref[...] + _dot(ul.astype(BF16), aup_ref[...]))
    g_out[...] = _dot(_sigmoid(ul).astype(BF16), gup_ref[...])

    kk = k * kk_ref[...]
    nrm = jnp.sqrt(_seg_sum(kk * kk, gd_ref[...], gu_ref[...]))
    kk = kk / jnp.maximum(nrm, 1e-12)
    k = k * (1.0 + (a - 1.0) * ka_ref[...])
    r_out[...] = r
    k_out[...] = k
    v_out[...] = v
    a_out[...] = -kk
    b_out[...] = kk * a
    bonus_out[...] = _seg_sum(r * k * rk_ref[...], gd_ref[...], gu_ref[...]) * v


def _seq_halo(u, tm, nb):
    rows = u.shape[0]
    nblk = rows // tm
    starts = np.arange(nblk) * tm
    seq_len = np.where(starts < nb * T, T, C)
    seq_off = np.where(starts < nb * T, starts % T, (starts - nb * T) % C)
    has_prev = seq_off > 0
    has_next = seq_off + tm < seq_len
    prev_idx = np.where(has_prev, starts - 1, 0)
    next_idx = np.where(has_next, starts + tm, 0)
    up = jnp.where(has_prev[:, None], u[prev_idx], 0.0)
    un = jnp.where(has_next[:, None], u[next_idx], 0.0)
    return up[:, None, :], un[:, None, :]


def _rwkv_prepare(u, nb, mu, w0, w_up, a0, a_up, g_up, k_k, k_a, r_k, w, layer):
    rows = u.shape[0]
    tm = 256
    nblk = rows // tm
    assert nblk >= CAST_BLOCKS
    cast_in, cast_out, cast_shape = _cast_specs(w, layer, lambda i: i)
    up, un = _seq_halo(u, tm, nb)
    pad = jnp.zeros((LORA,), F32)
    mu_rkv = mu[:3 * RW].reshape(1, 3 * RW)
    mu_l = jnp.concatenate([mu[3 * RW:], pad]).reshape(1, 4 * LORA)

    def lora_w(w, slot):
        z = jnp.zeros((4 * LORA, RW), F32)
        return z.at[slot * LORA:(slot + 1) * LORA].set(w).astype(BF16)

    wup = jnp.stack([lora_w(w_up[0], 0), lora_w(w_up[1], 0)])
    aup = lora_w(a_up, 1)
    gup = lora_w(g_up, 2)
    head = np.arange(RW) // HD
    g_down = jnp.asarray(head[:, None] == np.arange(128)[None, :], BF16)
    g_upm = jnp.asarray(np.arange(128)[:, None] == head[None, :], BF16)

    row = lambda w: pl.BlockSpec((tm, w), lambda i: (i, 0))
    vec = lambda w: pl.BlockSpec((1, w), lambda i: (0, 0))
    full = lambda *s: pl.BlockSpec(s, lambda i: (0,) * len(s))
    out_sd = jax.ShapeDtypeStruct((rows, RW), F32)
    outs = pl.pallas_call(
        _rwkv_prep_kernel,
        grid=(nblk,),
        in_specs=[pl.BlockSpec((tm, 3 * RW), lambda i: (i, 0)),
                  pl.BlockSpec((tm, 4 * LORA), lambda i: (i, AB_LORA // (4 * LORA))),
                  pl.BlockSpec((None, 1, 3 * RW), lambda i: (i, 0, 0)),
                  pl.BlockSpec((None, 1, 3 * RW), lambda i: (i, 0, 0)),
                  pl.BlockSpec((None, 1, 4 * LORA), lambda i: (i, 0, AB_LORA // (4 * LORA))),
                  pl.BlockSpec((None, 1, 4 * LORA), lambda i: (i, 0, AB_LORA // (4 * LORA))),
                  vec(3 * RW), vec(4 * LORA), full(2, RW), full(2, 4 * LORA, RW),
                  vec(RW), full(4 * LORA, RW), full(4 * LORA, RW), vec(RW), vec(RW), vec(RW),
                  full(RW, 128), full(128, RW), cast_in],
        out_specs=[row(RW), row(RW), row(RW), row(RW), row(RW),
                   pl.BlockSpec((2, tm, RW), lambda i: (0, i, 0)), row(RW), row(RW), cast_out],
        out_shape=[out_sd, out_sd, out_sd, out_sd, out_sd,
                   jax.ShapeDtypeStruct((2, rows, RW), F32), out_sd, out_sd, cast_shape],
        compiler_params=_params(("arbitrary",), V7X_VMEM_LIMIT),
        name="rwkv_prepare",
    )(u, u, up, un, up, un, mu_rkv, mu_l, w0, wup, a0.reshape(1, RW), aup, gup,
      k_k.reshape(1, RW), k_a.reshape(1, RW), r_k.reshape(1, RW), g_down, g_upm, w)
    return outs


def _chunk_block(d, b, s, nb, rows=CHUNK):
    nctx, nlat = C // rows, T // rows
    pos_c = jnp.where(d == 0, s, nctx - 1 - s)
    pos_l = jnp.where(d == 0, s - nctx, nlat - 1 - (s - nctx))
    return jnp.where(s < nctx, nb * nlat + b * nctx + pos_c, b * nlat + pos_l)


def _cast_specs(w, layer, step_of):
    rows, cols = w.shape[1] // CAST_BLOCKS, w.shape[2]
    blk = lambda *idx: jnp.minimum(step_of(*idx), CAST_BLOCKS - 1)
    return (pl.BlockSpec((None, rows, cols), lambda *idx: (layer, blk(*idx), 0)),
            pl.BlockSpec((rows, cols), lambda *idx: (blk(*idx), 0)),
            jax.ShapeDtypeStruct(w.shape[1:], BF16))


def _lane_stack(x, hw):
    head = lax.broadcasted_iota(jnp.int32, x.shape, 1) // hw
    return jnp.concatenate([jnp.where(head == h, x, 0.0) for h in range(GROUP // hw)], axis=0)


def _fold_rows(x, n):
    out = x[0:n]
    for h in range(1, x.shape[0] // n):
        out = out + x[h * n:(h + 1) * n]
    return out


def _block_masks(size, blk, d):
    row = lax.broadcasted_iota(jnp.int32, (size, size), 0)
    col = lax.broadcasted_iota(jnp.int32, (size, size), 1)
    rr, cc = row % blk, col % blk
    same = (row // blk) == (col // blk)
    if d == 0:
        return cc <= rr, cc < rr, same
    return cc >= rr, cc > rr, same


def _rwkv_scan_kernel(*refs):
    in_refs = (refs[0:6], refs[6:12])
    y_refs = refs[12:14]
    s_ref = refs[14]

    @pl.when(pl.program_id(1) == 0)
    def _():
        s_ref[...] = jnp.zeros_like(s_ref)

    n = CHUNK
    ngrp = RW // GROUP
    nch = 2 * ngrp
    rrow = lax.broadcasted_iota(jnp.int32, (n, GROUP), 0)
    rcol = lax.broadcasted_iota(jnp.int32, (n, GROUP), 1) % n
    eye_row = jnp.where(rrow == rcol, 1.0, 0.0)
    incl_row = (rcol <= rrow, rcol >= rrow)
    strict_row = (rcol < rrow, rcol > rrow)
    same = _block_masks(GROUP, n, 0)[2]

    def block_diag(x_row):
        xb = x_row.astype(BF16)
        return jnp.where(same, jnp.concatenate([xb] * (GROUP // n), axis=0), jnp.zeros((), BF16))

    def stack(x):
        return _lane_stack(x.astype(BF16), HD)

    pre = []
    for d in range(2):
        tri = jnp.where(_block_masks(n, n, d)[0], 1.0, 0.0).astype(BF16)
        order = range(SCAN_CHUNKS) if d == 0 else range(SCAN_CHUNKS - 1, -1, -1)
        pre_d = []
        for ci in order:
            rows = slice(ci * n, (ci + 1) * n)
            r, k, v, a, b, lw = [ref[rows, :] for ref in in_refs[d]]
            c = _tri_dot(tri, lw)
            c_last = c[n - 1:n] if d == 0 else c[0:1]
            p_inv = jnp.exp(-c)
            p_end = jnp.exp(c_last - c)
            pre_d.append(dict(rows=rows, at=a * jnp.exp(c - lw), rt=r * jnp.exp(c), bt=b * p_inv, kt=k * p_inv,
                              bp=b * p_end, kp=k * p_end, v=v, dec=jnp.exp(c_last)))
        pre.append(pre_d)
    chains = [(d, p, g) for p in range(SCAN_CHUNKS) for d in range(2) for g in range(ngrp)]
    nch = len(chains)
    sl = lambda g: slice(g * GROUP, (g + 1) * GROUP)
    part = lambda d, p, g, name: pre[d][p][name][:, sl(g)]

    a_st = [stack(part(d, p, g, "at")) for d, p, g in chains]
    v_st = [stack(part(d, p, g, "v")) for d, p, g in chains]
    gram = [_dot_nt(jnp.concatenate([part(d, p, g, "at"), part(d, p, g, "rt")], axis=0).astype(BF16),
                    jnp.concatenate([stack(part(d, p, g, "bt")), stack(part(d, p, g, "kt"))], axis=0))
            for d, p, g in chains]
    a_ab = [jnp.where(strict_row[d], gram[i][0:n, 0:GROUP], 0.0) for i, (d, p, g) in enumerate(chains)]
    a_ak = [jnp.where(strict_row[d], gram[i][0:n, GROUP:], 0.0).astype(BF16) for i, (d, p, g) in enumerate(chains)]
    a_r = [jnp.where(jnp.concatenate([incl_row[d]] * 2, axis=1), gram[i][n:2 * n, :], 0.0).astype(BF16)
           for i, (d, p, g) in enumerate(chains)]
    av = [_dot(a_ak[i], v_st[i]) for i in range(nch)]
    tm = [eye_row + m for m in a_ab]
    pw = [_dot(m.astype(BF16), block_diag(m)) for m in a_ab]
    for _ in range(4):
        both = [_dot(jnp.concatenate([t, p], axis=0).astype(BF16), block_diag(p)) for t, p in zip(tm, pw)]
        tm = [t + x[0:n] for t, x in zip(tm, both)]
        pw = [x[n:2 * n] for x in both]
    tm = [t + _dot(t.astype(BF16), block_diag(p)) for t, p in zip(tm, pw)]
    tx = [_dot(tm[i].astype(BF16), jnp.concatenate([a_st[i], stack(av[i])], axis=1))
          for i in range(nch)]

    state = {(d, g): s_ref[d, g] for d in range(2) for g in range(ngrp)}
    for p in range(SCAN_CHUNKS):
        ids = [i for i, ch in enumerate(chains) if ch[1] == p]
        uy0 = {i: _dot_nt(jnp.concatenate([tx[i][:, 0:GROUP], part(*chains[i], "rt")], axis=0).astype(BF16),
                          state[chains[i][0], chains[i][2]].astype(BF16)) for i in ids}
        u = {i: uy0[i][0:n] + tx[i][:, GROUP:] for i in ids}
        for i in ids:
            d, _, g = chains[i]
            y_refs[d][pre[d][p]["rows"], sl(g)] = (
                uy0[i][n:2 * n] + _dot(a_r[i], jnp.concatenate([stack(u[i]), v_st[i]], axis=0)))
        for i in ids:
            d, _, g = chains[i]
            upd = _dot_tn(jnp.concatenate([u[i], part(d, p, g, "v")], axis=0).astype(BF16),
                          jnp.concatenate([part(d, p, g, "bp"), part(d, p, g, "kp")], axis=0).astype(BF16))
            state[d, g] = state[d, g] * part(d, p, g, "dec") + jnp.where(same, upd, 0.0)
    for (d, g), s_new in state.items():
        s_ref[d, g] = s_new


def _rwkv_scan(r, k, v, a, b, lw, nb):
    rows = r.shape[0]
    blk_rows = SCAN_CHUNKS * CHUNK
    steps = (T + C) // blk_rows
    in_specs, args = [], []
    for d in range(2):
        blk = lambda bb, s, d=d: (_chunk_block(d, bb, s, nb, blk_rows), 0)
        in_specs += [pl.BlockSpec((blk_rows, RW), blk)] * 5
        in_specs.append(pl.BlockSpec((None, blk_rows, RW),
                                     lambda bb, s, d=d: (d, _chunk_block(d, bb, s, nb, blk_rows), 0)))
        args += [r, k, v, a, b, lw]
    out_sd = jax.ShapeDtypeStruct((rows, RW), F32)
    return pl.pallas_call(
        _rwkv_scan_kernel,
        grid=(nb, steps),
        in_specs=in_specs,
        out_specs=[pl.BlockSpec((blk_rows, RW), lambda bb, s, d=d: (_chunk_block(d, bb, s, nb, blk_rows), 0))
                   for d in range(2)],
        out_shape=[out_sd, out_sd],
        scratch_shapes=[pltpu.VMEM((2, RW // GROUP, GROUP, GROUP), F32)],
        compiler_params=_params(("parallel", "arbitrary"), V7X_VMEM_LIMIT),
        name="rwkv_scan",
    )(*args)


def _ab_mix(y0_ref, y1_ref, bonus_ref, g_ref, lnw_ref, lnb_ref, gd_ref, gu_ref, *yb_refs, pick, tile):
    y = y0_ref[...] + y1_ref[...]
    mu = _seg_sum(y, gd_ref[...], gu_ref[...]) * (1.0 / HD)
    yc = y - mu
    var = _seg_sum(yc * yc, gd_ref[...], gu_ref[...]) * (1.0 / HD)
    yn = yc * lax.rsqrt(var + RWKV_GN_EPS)
    return (yn * lnw_ref[...] + lnb_ref[...] + bonus_ref[...]) * g_ref[...], pick(tile, *yb_refs)


def _ab_mix_inputs(tile, y0, y1, bonus, g, ln_w, ln_b, yb_lat, yb_ctx):
    tm = OPROJ_TM
    head = np.arange(RW) // HD
    g_down = jnp.asarray(head[:, None] == np.arange(128)[None, :], BF16)
    g_upm = jnp.asarray(np.arange(128)[:, None] == head[None, :], BF16)
    row = pl.BlockSpec((tm, RW), lambda i: (tile(i), 0))
    vec = pl.BlockSpec((1, RW), lambda i: (0, 0))
    yb_args, yb_specs, pick = _split_rows(yb_lat, yb_ctx, tm, 1024, tile=tile)
    specs = [row, row, row, row, vec, vec,
             pl.BlockSpec((RW, 128), lambda i: (0, 0)), pl.BlockSpec((128, RW), lambda i: (0, 0))] + yb_specs
    args = (y0, y1, bonus, g, ln_w.reshape(1, RW), ln_b.reshape(1, RW), g_down, g_upm) + yb_args
    return functools.partial(_ab_mix, pick=pick), args, specs


def _kv_block(i, nb):
    per = T // 256
    return jnp.where(i < nb * per, i // per, i - nb * per), jnp.where(i < nb * per, i % per, per)


def _swa_prep_kernel(k_ref, v_ref, cos_ref, sin_ref, ko_ref, vo_ref):
    cosl, sinl = cos_ref[...], sin_ref[...]
    lane = lax.broadcasted_iota(jnp.int32, cosl.shape, 1)
    low = lane < HD
    for src, dst, rope in ((k_ref, ko_ref, True), (v_ref, vo_ref, False)):
        for t in range(2):
            x = src[:, t * 128:(t + 1) * 128]
            if rope:
                x = _rope(x, cosl, sinl)
            swapped = pltpu.roll(x, HD, axis=1)
            dst[:, (2 * t) * 128:(2 * t + 1) * 128] = jnp.where(low, x, swapped).astype(BF16)
            dst[:, (2 * t + 1) * 128:(2 * t + 2) * 128] = jnp.where(low, swapped, x).astype(BF16)


def _swa_prep(u, cosl, sinl, nb):
    rows = u.shape[0]
    tm = 256
    kv_out = pl.BlockSpec((None, tm, 512), lambda i: (*_kv_block(i, nb), 0))
    sd = jax.ShapeDtypeStruct((nb, T + C, 512), BF16)
    return pl.pallas_call(
        _swa_prep_kernel,
        grid=(rows // tm,),
        in_specs=[pl.BlockSpec((tm, 256), lambda i: (i, AB_KS // 256)),
                  pl.BlockSpec((tm, 256), lambda i: (i, AB_VS // 256)),
                  pl.BlockSpec((tm, 128), lambda i: (i, 0)),
                  pl.BlockSpec((tm, 128), lambda i: (i, 0))],
        out_specs=[kv_out, kv_out],
        out_shape=[sd, sd],
        compiler_params=_params(("parallel",), V7X_VMEM_LIMIT),
        name="swa_prep",
    )(u, u, cosl, sinl)


def _diff_prep_kernel(k_ref, v_ref, cos_ref, sin_ref, ko_ref, vo_ref):
    cosl, sinl = cos_ref[...], sin_ref[...]
    for t in range(8):
        sl = slice(t * 128, (t + 1) * 128)
        ko_ref[:, sl] = _rope(k_ref[:, sl], cosl, sinl).astype(BF16)
    vo_ref[...] = v_ref[...].astype(BF16)


def _diff_prep(u, cosl, sinl, nb):
    rows = u.shape[0]
    tm = 256
    kv_out = pl.BlockSpec((None, tm, 1024), lambda i: (*_kv_block(i, nb), 0))
    sd = jax.ShapeDtypeStruct((nb, T + C, 1024), BF16)
    return pl.pallas_call(
        _diff_prep_kernel,
        grid=(rows // tm,),
        in_specs=[pl.BlockSpec((tm, 1024), lambda i: (i, 1)),
                  pl.BlockSpec((tm, 1024), lambda i: (i, 2)),
                  pl.BlockSpec((tm, 128), lambda i: (i, 0)),
                  pl.BlockSpec((tm, 128), lambda i: (i, 0))],
        out_specs=[kv_out, kv_out],
        out_shape=[sd, sd],
        compiler_params=_params(("parallel",), V7X_VMEM_LIMIT),
        name="diff_prep",
    )(u, u, cosl, sinl)


def _swa_kernel(q_ref, cos_ref, sin_ref, k_ref, v_ref, sink_ref, *rest, ctx_mode):
    if ctx_mode:
        (o_ref,) = rest
    else:
        w_ref, o_ref, wo_ref = rest
        wo_ref[...] = w_ref[...].astype(BF16)
    n = pl.program_id(1)
    cosl, sinl = cos_ref[...], sin_ref[...]
    QB = q_ref.shape[0]
    span = QB + 2 * SWA_WINDOW
    if not ctx_mode:
        start = pl.multiple_of(jnp.clip(n * QB - SWA_WINDOW, 0, T - span), SWA_WINDOW)
        kpos = start + lax.broadcasted_iota(jnp.int32, (2 * QB, span), 1)
        qpos = n * QB + lax.broadcasted_iota(jnp.int32, (2 * QB, span), 0) % QB
        valid = jnp.abs(kpos - qpos) <= SWA_WINDOW
    low = lax.broadcasted_iota(jnp.int32, (QB, 128), 1) < HD
    top = lax.broadcasted_iota(jnp.int32, (2 * QB, 1), 0) < QB
    tile = lambda j: slice((j // 2) * 128, (j // 2 + 1) * 128)
    qs, sink = [], []
    for j in range(8):
        q = _rope(q_ref[:, j * 128:(j + 1) * 128], cosl, sinl) * (HD ** -0.5 * LOG2E)
        qs.append(jnp.concatenate([jnp.where(low, q, 0.0), jnp.where(low, 0.0, q)], axis=0).astype(BF16))
        sink.append(jnp.where(top, sink_ref[2 * j:2 * j + 1, 0:1], sink_ref[2 * j + 1:2 * j + 2, 0:1]) * LOG2E)
    s_c = [_dot_nt(qs[j], k_ref[T:T + C, tile(j)]) for j in range(8)]
    m = [jnp.maximum(jnp.max(s, axis=-1, keepdims=True), sk) for s, sk in zip(s_c, sink)]
    if not ctx_mode:
        s_w = [jnp.where(valid, _dot_nt(qs[j], k_ref[pl.ds(start, span), tile(j)]), -jnp.inf) for j in range(8)]
        m = [jnp.maximum(mm, jnp.max(s, axis=-1, keepdims=True)) for mm, s in zip(m, s_w)]
    p_c = [jnp.exp2(s - mm) for s, mm in zip(s_c, m)]
    den = [jnp.sum(p, axis=-1, keepdims=True) + jnp.exp2(sk - mm) for p, sk, mm in zip(p_c, sink, m)]
    pv = [_dot(p.astype(BF16), v_ref[T:T + C, tile(j)]) for j, p in enumerate(p_c)]
    if not ctx_mode:
        p_w = [jnp.exp2(s - mm) for s, mm in zip(s_w, m)]
        den = [dd + jnp.sum(p, axis=-1, keepdims=True) for dd, p in zip(den, p_w)]
        pv = [x + _dot(p.astype(BF16), v_ref[pl.ds(start, span), tile(j)]) for j, (x, p) in enumerate(zip(pv, p_w))]
    for j in range(8):
        o = pv[j] / den[j]
        o_ref[:, j * 128:(j + 1) * 128] = jnp.where(low, o[0:QB], o[QB:2 * QB])


def _swa(u, cosl, sinl, kpad, vpad, sink, nb, ctx_mode, w=None, layer=None):
    QB = SWA_QB
    nq = (C if ctx_mode else T) // QB
    base = nb * (T // QB) if ctx_mode else 0
    rows = nb * nq * QB
    rowblk = lambda b, n: base + b * nq + n
    kv = pl.BlockSpec((None, T + C, 512), lambda b, n: (b, 0, 0))
    in_specs = [pl.BlockSpec((QB, 1024), lambda b, n: (rowblk(b, n), AB_Q // 1024)),
                pl.BlockSpec((QB, 128), lambda b, n: (rowblk(b, n), 0)),
                pl.BlockSpec((QB, 128), lambda b, n: (rowblk(b, n), 0)),
                kv, kv, pl.BlockSpec((16, 128), lambda b, n: (0, 0))]
    out_specs = [pl.BlockSpec((QB, 1024), lambda b, n: (b * nq + n, 0))]
    out_shape = [jax.ShapeDtypeStruct((rows, 1024), F32)]
    args = [u, cosl, sinl, kpad, vpad, sink]
    if not ctx_mode:
        assert nb * nq >= CAST_BLOCKS
        cast_in, cast_out, cast_shape = _cast_specs(w, layer, lambda b, n: b * nq + n)
        in_specs.append(cast_in)
        out_specs.append(cast_out)
        out_shape.append(cast_shape)
        args.append(w)
    return pl.pallas_call(
        functools.partial(_swa_kernel, ctx_mode=ctx_mode),
        grid=(nb, nq),
        in_specs=in_specs,
        out_specs=out_specs,
        out_shape=out_shape,
        compiler_params=_params(("arbitrary", "arbitrary"), V7X_VMEM_LIMIT),
        name="swa_ctx" if ctx_mode else "swa_latent",
    )(*args)


def _diff_kernel(q_ref, cos_ref, sin_ref, k_ref, v_ref, lam_ref, sub_ref, w_ref, o_ref, wo_ref):
    wo_ref[...] = w_ref[...].astype(BF16)
    cosl, sinl = cos_ref[...], sin_ref[...]
    lam = lam_ref[...]
    nq = q_ref.shape[0]
    low = lax.broadcasted_iota(jnp.int32, (nq, 128), 1) < HD
    tile = lambda h: slice(h * 128, (h + 1) * 128)

    def logits(h):
        q = _rope(q_ref[:, tile(h)], cosl, sinl) * (HD ** -0.5 * LOG2E)
        qs = jnp.concatenate([jnp.where(low, q, 0.0), jnp.where(low, 0.0, q)], axis=0).astype(BF16)
        return _dot_nt(qs, k_ref[:, tile(h)])

    s_next = logits(0)
    for h in range(8):
        s = s_next
        if h + 1 < 8:
            s_next = logits(h + 1)
        e = jnp.exp2(s - jnp.max(s, axis=-1, keepdims=True))
        inv = 1.0 / jnp.sum(e, axis=-1, keepdims=True)
        o2 = _dot(e.astype(BF16), v_ref[:, tile(h)])
        o = o2[0:nq] * inv[0:nq] - o2[nq:2 * nq] * (lam[:, 0:1] * inv[nq:2 * nq])
        o = o * lax.rsqrt(jnp.mean(o * o, axis=-1, keepdims=True) + 1e-5)
        o_ref[:, tile(h)] = o * sub_ref[...]


def _diff_attn(u, cosl, sinl, kb, vb, lam, sub, nb, w, layer):
    QB = DIFF_QB
    nq = T // QB
    assert nb * nq >= CAST_BLOCKS
    cast_in, cast_out, cast_shape = _cast_specs(w, layer, lambda b, n: b * nq + n)
    kv = pl.BlockSpec((None, T + C, 1024), lambda b, n: (b, 0, 0), pipeline_mode=pl.Buffered(1))
    vec = pl.BlockSpec((1, 128), lambda b, n: (0, 0))
    return pl.pallas_call(
        _diff_kernel,
        grid=(nb, nq),
        in_specs=[pl.BlockSpec((QB, 1024), lambda b, n: (b * nq + n, 0)),
                  pl.BlockSpec((QB, 128), lambda b, n: (b * nq + n, 0)),
                  pl.BlockSpec((QB, 128), lambda b, n: (b * nq + n, 0)),
                  kv, kv, vec, vec, cast_in],
        out_specs=[pl.BlockSpec((QB, 1024), lambda b, n: (b * nq + n, 0)), cast_out],
        out_shape=[jax.ShapeDtypeStruct((nb * T, 1024), F32), cast_shape],
        compiler_params=_params(("arbitrary", "arbitrary"), V7X_VMEM_LIMIT),
        name="diff_attn",
    )(u, cosl, sinl, kb, vb, lam, sub, w)


def _hgrn_exact_att(q, kk, bcum, tmp_ref):
    n = CHUNK
    tmp_ref[0] = bcum
    tmp_ref[1] = kk
    coli = lax.broadcasted_iota(jnp.int32, (n, 2 * n), 1)

    def body(s, acc):
        bs = tmp_ref[0, pl.ds(s, 1), :]
        ks = tmp_ref[1, pl.ds(s, 1), :]
        w = q * jnp.exp(jnp.minimum(bcum - bs, 0.0)) * ks
        c0 = jnp.sum(w[:, 0:HG_DK], axis=-1, keepdims=True)
        c1 = jnp.sum(w[:, HG_DK:], axis=-1, keepdims=True)
        return acc + jnp.concatenate([jnp.where(coli == s, c0, 0.0), jnp.where(coli == s + n, c1, 0.0)], axis=0)

    return lax.fori_loop(0, n, body, jnp.zeros((2 * n, 2 * n), F32))


def _hgrn_scan_kernel(q0_ref, z0_ref, i0_ref, lb0_ref, q1_ref, z1_ref, i1_ref, lb1_ref, w_ref,
                      o0_ref, o1_ref, wo_ref, s_ref, g_ref, tmp_ref):
    in_refs = ((q0_ref, z0_ref, i0_ref, lb0_ref), (q1_ref, z1_ref, i1_ref, lb1_ref))
    o_refs = (o0_ref, o1_ref)
    wo_ref[...] = w_ref[...].astype(BF16)

    @pl.when(pl.program_id(1) == 0)
    def _():
        s_ref[...] = jnp.zeros_like(s_ref)

    n = CHUNK
    ngrp = 1024 // GROUP
    nsub = n // SUB
    pre = []
    for d in range(2):
        tri = jnp.where(_block_masks(n, n, d)[0], 1.0, 0.0).astype(BF16)
        lb = in_refs[d][3][...]
        order = range(SCAN_CHUNKS) if d == 0 else range(SCAN_CHUNKS - 1, -1, -1)
        pre_d = []
        for ci in order:
            rows = slice(ci * n, (ci + 1) * n)
            q, z, v = [ref[rows, :] for ref in in_refs[d][0:3]]
            logf = jnp.log(lb + (1.0 - lb) * _sigmoid(z))
            kk = (1.0 - lb) * _sigmoid(-z)
            bcum = _tri_dot(tri, logf)
            b_last = bcum[n - 1:n] if d == 0 else bcum[0:1]
            b_excl = bcum - logf
            qh, kh = [], []
            for sb in range(nsub):
                lo, hi = sb * SUB, (sb + 1) * SUB
                beta = b_excl[lo:lo + 1] if d == 0 else b_excl[hi - 1:hi]
                qh.append(q[lo:hi] * jnp.exp(bcum[lo:hi] - beta))
                kh.append(kk * jnp.exp(beta - bcum))
            pre_d.append(dict(rows=rows, q=q, v=v, kk=kk, bcum=bcum, qh=qh, kh=kh, qe=q * jnp.exp(bcum),
                              ke=kk * jnp.exp(b_last - bcum), dec=jnp.exp(b_last), min_logf=jnp.min(logf)))
        pre.append(pre_d)
    chains = [(d, p, g) for p in range(SCAN_CHUNKS) for d in range(2) for g in range(ngrp)]
    sl = lambda g: slice(g * GROUP, (g + 1) * GROUP)

    for i, (d, p, g) in enumerate(chains):
        c = pre[d][p]
        rows = [_dot_nt(_lane_stack(c["qh"][sb][:, sl(g)].astype(BF16), HG_DK),
                        _lane_stack(c["kh"][sb][:, sl(g)].astype(BF16), HG_DK)) for sb in range(nsub)]
        g_ref[i] = jnp.concatenate([rows[sb][h * SUB:(h + 1) * SUB] for h in range(2) for sb in range(nsub)], axis=0)

    min_logf = functools.reduce(jnp.minimum, [c["min_logf"] for pre_d in pre for c in pre_d])

    @pl.when(min_logf < -(HG_CLAMP / SUB))
    def _():
        for i, (d, p, g) in enumerate(chains):
            c = pre[d][p]
            g_ref[i] = _hgrn_exact_att(c["q"][:, sl(g)], c["kk"][:, sl(g)], c["bcum"][:, sl(g)], tmp_ref)

    state = {(d, g): s_ref[d, g] for d in range(2) for g in range(ngrp)}
    head_same = _block_masks(GROUP, HG_DK, 0)[2]
    for p in range(SCAN_CHUNKS):
        ids = [i for i, ch in enumerate(chains) if ch[1] == p]
        inter = {i: _dot_nt(pre[chains[i][0]][p]["qe"][:, sl(chains[i][2])].astype(BF16),
                            state[chains[i][0], chains[i][2]].astype(BF16)) for i in ids}
        for i in ids:
            d, _, g = chains[i]
            incl, _, same = _block_masks(2 * n, n, d)
            att = jnp.where(incl & same, g_ref[i], 0.0).astype(BF16)
            v_st = _lane_stack(pre[d][p]["v"][:, sl(g)].astype(BF16), HG_DK)
            o_refs[d][pre[d][p]["rows"], sl(g)] = inter[i] + _fold_rows(_dot(att, v_st), n)
        for i in ids:
            d, _, g = chains[i]
            upd = _dot_tn(pre[d][p]["v"][:, sl(g)].astype(BF16), pre[d][p]["ke"][:, sl(g)].astype(BF16))
            state[d, g] = state[d, g] * pre[d][p]["dec"][:, sl(g)] + jnp.where(head_same, upd, 0.0)
    for (d, g), s_new in state.items():
        s_ref[d, g] = s_new


def _hgrn_scan(u, lb, nb, w, layer):
    rows = u.shape[0]
    blk_rows = SCAN_CHUNKS * CHUNK
    steps = (T + C) // blk_rows
    assert nb * steps >= CAST_BLOCKS
    cast_in, cast_out, cast_shape = _cast_specs(w, layer, lambda bb, s: bb * steps + s)
    in_specs, args = [], []
    for d in range(2):
        col = lambda j, d=d: pl.BlockSpec((blk_rows, 1024),
                                          lambda bb, s: (_chunk_block(d, bb, s, nb, blk_rows), j))
        in_specs += [col(3), col(4 + d), col(6), pl.BlockSpec((None, 1, 1024), lambda bb, s, d=d: (d, 0, 0))]
        args += [u, u, u, lb]
    out_sd = jax.ShapeDtypeStruct((rows, 1024), F32)
    ngrp = 1024 // GROUP
    return pl.pallas_call(
        _hgrn_scan_kernel,
        grid=(nb, steps),
        in_specs=in_specs + [cast_in],
        out_specs=[pl.BlockSpec((blk_rows, 1024), lambda bb, s, d=d: (_chunk_block(d, bb, s, nb, blk_rows), 0))
                   for d in range(2)] + [cast_out],
        out_shape=[out_sd, out_sd, cast_shape],
        scratch_shapes=[pltpu.VMEM((2, ngrp, GROUP, GROUP), F32),
                        pltpu.VMEM((2 * ngrp * SCAN_CHUNKS, 2 * CHUNK, 2 * CHUNK), F32),
                        pltpu.VMEM((2, CHUNK, GROUP), F32)],
        compiler_params=_params(("arbitrary", "arbitrary"), V7X_VMEM_LIMIT),
        name="hgrn_scan",
    )(*args, w)


def _cd_mix(yc_ref, o0_ref, o1_ref, g_ref, gn_ref, *, tile):
    g = g_ref[...]
    heads = []
    for h in range(8):
        sl = slice(h * 128, (h + 1) * 128)
        o = o0_ref[:, sl] + o1_ref[:, sl]
        o = o * lax.rsqrt(jnp.mean(o * o, axis=-1, keepdims=True) + NORM_EPS) * gn_ref[...]
        gh = g[:, sl]
        heads.append(o * (gh * _sigmoid(gh)))
    return yc_ref[...], jnp.concatenate(heads, axis=1)


def _cd_mix_inputs(tile, yc, o0, o1, u, gnorm):
    tm = OPROJ_TM
    row = pl.BlockSpec((tm, 1024), lambda i: (tile(i), 0))
    specs = [row, row, row, pl.BlockSpec((tm, 1024), lambda i: (tile(i), 7)),
             pl.BlockSpec((1, 128), lambda i: (0, 0))]
    return _cd_mix, (yc, o0, o1, u, gnorm.reshape(1, 128)), specs


def _rope_tables(nb):
    t = np.arange(T)
    quarter = HD // 4
    inv = ROPE_THETA ** (-jnp.arange(quarter, dtype=F32) / quarter)
    rows = jnp.asarray(t // GRID_W, F32)
    cols = jnp.asarray(t % GRID_W, F32)
    ang = jnp.concatenate([rows[:, None] * inv, cols[:, None] * inv], axis=-1)
    cos, sin = jnp.cos(ang), jnp.sin(ang)
    cosl = jnp.tile(jnp.concatenate([cos, cos], axis=-1), (nb, 2))
    sinl = jnp.tile(jnp.concatenate([-sin, sin], axis=-1), (nb, 2))
    cosl = jnp.concatenate([cosl, jnp.ones((nb * C, 128), F32)], axis=0)
    sinl = jnp.concatenate([sinl, jnp.zeros((nb * C, 128), F32)], axis=0)
    return cosl, sinl


def kernel(x, c, ctx, c_ctx, ada_w, ada_b, mlp_w1, mlp_w2, final_norm, ab_w_in, ab_w_out, ab_mu, ab_w0,
           ab_w_up, ab_a0, ab_a_up, ab_g_up, ab_k_k, ab_k_a, ab_r_k, ab_ln_w, ab_ln_b, ab_sink,
           cd_w_in, cd_w_out, cd_lam, cd_subln, cd_gnorm, hgrn_lb_logits):
    nb = x.shape[0]
    assert x.shape == (nb, T, D) and ctx.shape == (nb, C, D) and nb < 8
    assert ada_w.shape[0] == 2, "one AB layer followed by one CD layer"
    rx = nb * T
    rows = rx + nb * C
    x2, c2 = x.reshape(rx, D), ctx.reshape(nb * C, D)
    cosl, sinl = _rope_tables(nb)

    s_in = jnp.zeros((8, D), F32).at[:nb].set(c).at[nb].set(c_ctx)
    mod = _ada(s_in, ada_w, ada_b)

    w = ab_w_in[0]
    w_in0 = jnp.concatenate([w[:, :3 * RW], w[:, 3 * RW + 3 * LORA:], w[:, 3 * RW:3 * RW + 3 * LORA],
                             jnp.zeros((D, LORA), F32)], axis=1).astype(BF16)
    u = _norm_mod_matmul(x2, c2, mod, 0, w_in0, nb, 512, AB_N // 2)
    r, k, v, a, b, lw, g, bonus, w1b = _rwkv_prepare(u, nb, ab_mu[0], ab_w0[0], ab_w_up[0], ab_a0[0],
                                                     ab_a_up[0], ab_g_up[0], ab_k_k[0], ab_k_a[0],
                                                     ab_r_k[0].reshape(RW), mlp_w1, 0)
    y0, y1 = _rwkv_scan(r, k, v, a, b, lw, nb)
    kpad, vpad = _swa_prep(u, cosl, sinl, nb)
    sink = jnp.broadcast_to(ab_sink[0][:, None], (16, 128))
    yb_lat, w2b = _swa(u, cosl, sinl, kpad, vpad, sink, nb, False, mlp_w2, 0)
    (yb_ctx,) = _swa(u, cosl, sinl, kpad, vpad, sink, nb, True)
    w_out0 = ab_w_out[0].astype(BF16)
    mix_inputs = functools.partial(_ab_mix_inputs, y0=y0, y1=y1, bonus=bonus, g=g, ln_w=ab_ln_w[0],
                                   ln_b=ab_ln_b[0], yb_lat=yb_lat, yb_ctx=yb_ctx)
    xc = _out_proj(mix_inputs, x2, c2, rows, mod, 0, w_out0[:RW], w_out0[RW:], nb)
    fn = final_norm.reshape(1, D)
    xc = _mlp(xc, rows, mod, 0, w1b, w2b, fn, nb, False)

    lam_init = 0.8 - 0.6 * math.exp(-0.3 * 1)
    lb_table = jnp.cumsum(jax.nn.softmax(hgrn_lb_logits.astype(F32), axis=0), axis=0)
    lb = (lb_table - lb_table[0])[1].reshape(2, 1, 1024)
    lf = cd_lam[0].astype(F32)
    lmb = jnp.exp(jnp.sum(lf[0] * lf[1])) - jnp.exp(jnp.sum(lf[2] * lf[3])) + lam_init
    u = _norm_mod_matmul(xc, None, mod, 1, cd_w_in[0], nb, 1024, 1024)
    kb, vb = _diff_prep(u, cosl, sinl, nb)
    yc, w2b = _diff_attn(u, cosl, sinl, kb, vb, jnp.full((1, 128), lmb, F32),
                         (cd_subln[0] * (1.0 - lam_init)).reshape(1, 128), nb, mlp_w2, 1)
    o0, o1, w1b = _hgrn_scan(u, lb, nb, mlp_w1, 1)
    w_out1 = cd_w_out[0].astype(BF16)
    mix_inputs = functools.partial(_cd_mix_inputs, yc=yc, o0=o0, o1=o1, u=u, gnorm=cd_gnorm[0])
    xl = _out_proj(mix_inputs, xc, None, rx, mod, 1, w_out1[:1024], w_out1[1024:], nb)
    out = _mlp(xl, rx, mod, 1, w1b, w2b, fn, nb, True)
    return out.reshape(nb, T, D)
```

```python
import functools
import math

import jax
import jax.numpy as jnp
import numpy as np
from jax import lax
from jax.experimental import pallas as pl
from jax.experimental.pallas import tpu as pltpu

F32 = jnp.float32
BF16 = jnp.bfloat16

D = 2048
T = 2048
C = 256
GRID_W = 64
D_FF = 4 * D
HD = 64
ROPE_THETA = 10000.0
LOG2E = 1.4426950408889634
NORM_EPS = 1e-6
N_MOD = 6
RW = 1024
RWKV_GN_EPS = 64e-5
LORA = 64
SWA_WINDOW = 128
SWA_QB = 128
DIFF_QB = 256
OPROJ_TM = 512
CHUNK = 64
SCAN_CHUNKS = 4
CAST_BLOCKS = 32
SUB = 16
HG_DK = 128
HG_CLAMP = 80.0
GROUP = 256
V7X_VMEM_BYTES = 64 * 1024 * 1024
V7X_VMEM_LIMIT = V7X_VMEM_BYTES - 8 * 1024 * 1024

AB_Q, AB_KS, AB_VS, AB_LORA, AB_N = 3072, 4096, 4352, 4608, 4864


def _dot(a, b):
    return jnp.dot(a, b, preferred_element_type=F32)


def _dot_nt(a, b):
    return lax.dot_general(a, b, (((1,), (1,)), ((), ())), preferred_element_type=F32)


def _dot_tn(a, b):
    return lax.dot_general(a, b, (((0,), (0,)), ((), ())), preferred_element_type=F32)


def _split2(x):
    hi = x.astype(BF16)
    lo = (x - hi.astype(F32)).astype(BF16)
    return hi, lo


def _split3(x):
    hi = x.astype(BF16)
    r1 = x - hi.astype(F32)
    mid = r1.astype(BF16)
    lo = (r1 - mid.astype(F32)).astype(BF16)
    return hi, mid, lo


def _dot_exact_rhs(x, g):
    hi, lo = _split2(x)
    return _dot(hi, g) + _dot(lo, g)


def _tri_dot(tri, x):
    hi, mid, lo = _split3(x)
    return _dot(tri, hi) + _dot(tri, mid) + _dot(tri, lo)


def _seg_sum(x, g_down, g_up):
    return _dot_exact_rhs(_dot_exact_rhs(x, g_down), g_up)


def _rope(x, cosl, sinl):
    w = x.shape[-1]
    lane = lax.broadcasted_iota(jnp.int32, x.shape, x.ndim - 1)
    first = (lane & 63) < 32
    swapped = jnp.where(first, pltpu.roll(x, w - 32, axis=1), pltpu.roll(x, 32, axis=1))
    return x * cosl + swapped * sinl


def _sigmoid(x):
    return 1.0 / (1.0 + jnp.exp(-x))


def _softplus(x):
    return jnp.maximum(x, 0.0) + jnp.log(1.0 + jnp.exp(-jnp.abs(x)))


def _params(sem, vmem=None):
    return pltpu.CompilerParams(dimension_semantics=sem, vmem_limit_bytes=vmem)


def _ada_kernel(s_ref, w_ref, b_ref, o_ref):
    s = s_ref[...]
    s = s * _sigmoid(s)
    o_ref[...] = _dot(s.astype(BF16), w_ref[...].astype(BF16)) + b_ref[...]


def _ada(s_in, ada_w, ada_b):
    depth = ada_w.shape[0]
    n = ada_w.shape[2]
    tn = 1536
    out = pl.pallas_call(
        _ada_kernel,
        grid=(depth, n // tn),
        in_specs=[pl.BlockSpec((8, D), lambda l, j: (0, 0)),
                  pl.BlockSpec((None, D, tn), lambda l, j: (l, 0, j)),
                  pl.BlockSpec((None, 1, tn), lambda l, j: (l, 0, j))],
        out_specs=pl.BlockSpec((None, 8, tn), lambda l, j: (l, 0, j)),
        out_shape=jax.ShapeDtypeStruct((depth, 8, n), F32),
        compiler_params=_params(("arbitrary", "arbitrary"), V7X_VMEM_LIMIT),
        name="ada_mod",
    )(s_in, ada_w, ada_b.reshape(depth, 1, n))
    return out.reshape(depth, 8, 1, n)


def _mod_spec(layer, k, tm, nb, tile=lambda i: i):
    return pl.BlockSpec((None, None, 1, D),
                        lambda i, *_: (layer, jnp.minimum((tile(i) * tm) // T, nb), 0, k))


def _split_rows(lat, ctx, tm, width, col=0, tile=lambda i: i):
    if ctx is None:
        return (lat,), [pl.BlockSpec((tm, width), lambda i, *_: (tile(i), col))], lambda t, ref: ref[...]
    n_lat = lat.shape[0] // tm
    specs = [pl.BlockSpec((tm, width), lambda i, *_: (jnp.minimum(tile(i), n_lat - 1), col)),
             pl.BlockSpec((tm, width), lambda i, *_: (jnp.maximum(tile(i) - n_lat, 0), col),
                          pipeline_mode=pl.Buffered(1))]
    return (lat, ctx), specs, lambda t, lat_ref, ctx_ref: jnp.where(t < n_lat, lat_ref[...], ctx_ref[...])


def _nmm_kernel(*refs, n_x, pick):
    sh_ref, sc_ref, w_ref, o_ref, lhs_ref = refs[n_x:]

    @pl.when(pl.program_id(1) == 0)
    def _():
        x = pick(pl.program_id(0), *refs[:n_x])
        xn = x * lax.rsqrt(jnp.mean(x * x, axis=-1, keepdims=True) + NORM_EPS)
        lhs_ref[...] = (xn * (1.0 + sc_ref[...]) + sh_ref[...]).astype(BF16)

    o_ref[...] = _dot(lhs_ref[...], w_ref[...].astype(BF16)).astype(o_ref.dtype)


def _norm_mod_matmul(x_lat, x_ctx, mod, layer, w, nb, tm, tn):
    rows = x_lat.shape[0] + (0 if x_ctx is None else x_ctx.shape[0])
    n = w.shape[1]
    x_args, x_specs, pick = _split_rows(x_lat, x_ctx, tm, D)
    if x_ctx is None and tm > 512:
        x_specs = [pl.BlockSpec((tm, D), lambda i, j: (i, 0), pipeline_mode=pl.Buffered(1))]
    return pl.pallas_call(
        functools.partial(_nmm_kernel, n_x=len(x_args), pick=pick),
        grid=(rows // tm, n // tn),
        in_specs=x_specs + [_mod_spec(layer, 0, tm, nb), _mod_spec(layer, 1, tm, nb),
                            pl.BlockSpec((D, tn), lambda i, j: (0, j))],
        out_specs=pl.BlockSpec((tm, tn), lambda i, j: (i, j)),
        out_shape=jax.ShapeDtypeStruct((rows, n), F32),
        scratch_shapes=[pltpu.VMEM((tm, D), BF16)],
        compiler_params=_params(("parallel", "arbitrary"), V7X_VMEM_LIMIT),
        name=f"in_proj_{layer}",
    )(*x_args, mod, mod, w)


def _mlp_kernel(x_ref, sh_ref, sc_ref, gt_ref, w1_ref, w2_ref, fn_ref, o_ref, lhs_ref, *, final):
    f = pl.program_id(1)

    @pl.when(f == 0)
    def _():
        x = x_ref[...]
        xn = x * lax.rsqrt(jnp.mean(x * x, axis=-1, keepdims=True) + NORM_EPS)
        lhs_ref[...] = (xn * (1.0 + sc_ref[...]) + sh_ref[...]).astype(BF16)
        o_ref[...] = jnp.zeros_like(o_ref)

    h = jnp.maximum(_dot(lhs_ref[...], w1_ref[...]), 0.0)
    o_ref[...] += _dot((h * h).astype(BF16), w2_ref[...])

    @pl.when(f == pl.num_programs(1) - 1)
    def _():
        y = x_ref[...] + gt_ref[...] * o_ref[...]
        if final:
            y = y * lax.rsqrt(jnp.mean(y * y, axis=-1, keepdims=True) + NORM_EPS) * fn_ref[...]
        o_ref[...] = y


def _mlp(xc, rows, mod, layer, w1, w2, final_norm, nb, final):
    tm, tf = 512, 1024
    return pl.pallas_call(
        functools.partial(_mlp_kernel, final=final),
        grid=(rows // tm, D_FF // tf),
        in_specs=[pl.BlockSpec((tm, D), lambda i, f: (i, 0)),
                  _mod_spec(layer, 3, tm, nb), _mod_spec(layer, 4, tm, nb), _mod_spec(layer, 5, tm, nb),
                  pl.BlockSpec((D, tf), lambda i, f: (0, f)),
                  pl.BlockSpec((tf, D), lambda i, f: (f, 0)),
                  pl.BlockSpec((1, D), lambda i, f: (0, 0))],
        out_specs=pl.BlockSpec((tm, D), lambda i, f: (i, 0)),
        out_shape=jax.ShapeDtypeStruct((rows, D), F32),
        scratch_shapes=[pltpu.VMEM((tm, D), BF16)],
        compiler_params=_params(("parallel", "arbitrary"), V7X_VMEM_LIMIT),
        name=f"mlp_{layer}",
    )(xc, mod, mod, mod, w1, w2, final_norm)


def _oproj_kernel(*refs, n_mix, mix, n_x, pick):
    x_refs = refs[n_mix:n_mix + n_x]
    gt_ref, wa_ref, wb_ref, o_ref = refs[n_mix + n_x:]
    i = pl.program_id(0)
    ya, yb = mix(*refs[:n_mix], tile=i)
    y = _dot(ya.astype(BF16), wa_ref[...]) + _dot(yb.astype(BF16), wb_ref[...])
    o_ref[...] = pick(i, *x_refs) + gt_ref[...] * y


def _out_proj(mix_inputs, x_lat, x_ctx, rows, mod, layer, wa, wb, nb):
    tm = OPROJ_TM
    half = wa.shape[0]
    mix, mix_args, mix_specs = mix_inputs(lambda i: i)
    x_args, x_specs, pick = _split_rows(x_lat, x_ctx, tm, D)
    return pl.pallas_call(
        functools.partial(_oproj_kernel, n_mix=len(mix_args), mix=mix, n_x=len(x_args), pick=pick),
        grid=(rows // tm,),
        in_specs=list(mix_specs) + x_specs + [
            _mod_spec(layer, 2, tm, nb),
            pl.BlockSpec((half, D), lambda i: (0, 0), pipeline_mode=pl.Buffered(1)),
            pl.BlockSpec((half, D), lambda i: (0, 0), pipeline_mode=pl.Buffered(1))],
        out_specs=pl.BlockSpec((tm, D), lambda i: (i, 0)),
        out_shape=jax.ShapeDtypeStruct((rows, D), F32),
        compiler_params=_params(("parallel",), V7X_VMEM_LIMIT),
        name=f"out_proj_{layer}",
    )(*mix_args, *x_args, mod, wa, wb)


def _shifted(u, prev_row, next_row):
    tm = u.shape[0]
    row = lax.broadcasted_iota(jnp.int32, u.shape, 0)
    up = jnp.where(row == 0, prev_row, pltpu.roll(u, 1, axis=0))
    un = jnp.where(row == tm - 1, next_row, pltpu.roll(u, tm - 1, axis=0))
    return 0.5 * (up + un)


def _rwkv_prep_kernel(u_ref, ul_ref, up_ref, un_ref, ulp_ref, uln_ref, mu_ref, mul_ref, w0_ref, wup_ref,
                      a0_ref, aup_ref, gup_ref, kk_ref, ka_ref, rk_ref, gd_ref, gu_ref,
                      r_out, k_out, v_out, a_out, b_out, lw_out, g_out, bonus_out):
    u = u_ref[...]
    u = u + mu_ref[...] * (_shifted(u, up_ref[...], un_ref[...]) - u)
    ul = ul_ref[...]
    ul = ul + mul_ref[...] * (_shifted(ul, ulp_ref[...], uln_ref[...]) - ul)
    r, k, v = u[:, 0:RW], u[:, RW:2 * RW], u[:, 2 * RW:3 * RW]

    th = jnp.tanh(ul).astype(BF16)
    for d in range(2):
        w_log = -_softplus(-(w0_ref[d:d + 1, :] + _dot(th, wup_ref[d]))) - 0.5
        lw_out[d] = -jnp.exp(w_log)
    a = _sigmoid(a0_ref[...] + _dot(ul.astype(BF16), aup_ref[...]))
    g_out[...] = _dot(_sigmoid(ul).astype(BF16), gup_ref[...])

    kk = k * kk_ref[...]
    nrm = jnp.sqrt(_seg_sum(kk * kk, gd_ref[...], gu_ref[...]))
    kk = kk / jnp.maximum(nrm, 1e-12)
    k = k * (1.0 + (a - 1.0) * ka_ref[...])
    r_out[...] = r
    k_out[...] = k
    v_out[...] = v
    a_out[...] = -kk
    b_out[...] = kk * a
    bonus_out[...] = _seg_sum(r * k * rk_ref[...], gd_ref[...], gu_ref[...]) * v


def _seq_halo(u, tm, nb):
    rows = u.shape[0]
    nblk = rows // tm
    starts = np.arange(nblk) * tm
    seq_len = np.where(starts < nb * T, T, C)
    seq_off = np.where(starts < nb * T, starts % T, (starts - nb * T) % C)
    has_prev = seq_off > 0
    has_next = seq_off + tm < seq_len
    prev_idx = np.where(has_prev, starts - 1, 0)
    next_idx = np.where(has_next, starts + tm, 0)
    up = jnp.where(has_prev[:, None], u[prev_idx], 0.0)
    un = jnp.where(has_next[:, None], u[next_idx], 0.0)
    return up[:, None, :], un[:, None, :]


def _rwkv_prepare(u, nb, mu, w0, w_up, a0, a_up, g_up, k_k, k_a, r_k):
    rows = u.shape[0]
    tm = 256
    nblk = rows // tm
    up, un = _seq_halo(u, tm, nb)
    pad = jnp.zeros((LORA,), F32)
    mu_rkv = mu[:3 * RW].reshape(1, 3 * RW)
    mu_l = jnp.concatenate([mu[3 * RW:], pad]).reshape(1, 4 * LORA)

    def lora_w(w, slot):
        z = jnp.zeros((4 * LORA, RW), F32)
        return z.at[slot * LORA:(slot + 1) * LORA].set(w).astype(BF16)

    wup = jnp.stack([lora_w(w_up[0], 0), lora_w(w_up[1], 0)])
    aup = lora_w(a_up, 1)
    gup = lora_w(g_up, 2)
    head = np.arange(RW) // HD
    g_down = jnp.asarray(head[:, None] == np.arange(128)[None, :], BF16)
    g_upm = jnp.asarray(np.arange(128)[:, None] == head[None, :], BF16)

    row = lambda w: pl.BlockSpec((tm, w), lambda i: (i, 0))
    vec = lambda w: pl.BlockSpec((1, w), lambda i: (0, 0))
    full = lambda *s: pl.BlockSpec(s, lambda i: (0,) * len(s))
    out_sd = jax.ShapeDtypeStruct((rows, RW), F32)
    outs = pl.pallas_call(
        _rwkv_prep_kernel,
        grid=(nblk,),
        in_specs=[pl.BlockSpec((tm, 3 * RW), lambda i: (i, 0)),
                  pl.BlockSpec((tm, 4 * LORA), lambda i: (i, AB_LORA // (4 * LORA))),
                  pl.BlockSpec((None, 1, 3 * RW), lambda i: (i, 0, 0)),
                  pl.BlockSpec((None, 1, 3 * RW), lambda i: (i, 0, 0)),
                  pl.BlockSpec((None, 1, 4 * LORA), lambda i: (i, 0, AB_LORA // (4 * LORA))),
                  pl.BlockSpec((None, 1, 4 * LORA), lambda i: (i, 0, AB_LORA // (4 * LORA))),
                  vec(3 * RW), vec(4 * LORA), full(2, RW), full(2, 4 * LORA, RW),
                  vec(RW), full(4 * LORA, RW), full(4 * LORA, RW), vec(RW), vec(RW), vec(RW),
                  full(RW, 128), full(128, RW)],
        out_specs=[row(RW), row(RW), row(RW), row(RW), row(RW),
                   pl.BlockSpec((2, tm, RW), lambda i: (0, i, 0)), row(RW), row(RW)],
        out_shape=[out_sd, out_sd, out_sd, out_sd, out_sd,
                   jax.ShapeDtypeStruct((2, rows, RW), F32), out_sd, out_sd],
        compiler_params=_params(("parallel",), V7X_VMEM_LIMIT),
        name="rwkv_prepare",
    )(u, u, up, un, up, un, mu_rkv, mu_l, w0, wup, a0.reshape(1, RW), aup, gup,
      k_k.reshape(1, RW), k_a.reshape(1, RW), r_k.reshape(1, RW), g_down, g_upm)
    return outs


def _chunk_block(d, b, s, nb, rows=CHUNK):
    nctx, nlat = C // rows, T // rows
    pos_c = jnp.where(d == 0, s, nctx - 1 - s)
    pos_l = jnp.where(d == 0, s - nctx, nlat - 1 - (s - nctx))
    return jnp.where(s < nctx, nb * nlat + b * nctx + pos_c, b * nlat + pos_l)


def _cast_specs(weights, step_of):
    blk = lambda *idx: jnp.minimum(step_of(*idx), CAST_BLOCKS - 1)
    in_specs, out_specs, out_shapes = [], [], []
    for w, layer in weights:
        rows, cols = w.shape[1] // CAST_BLOCKS, w.shape[2]
        in_specs.append(pl.BlockSpec((None, rows, cols), lambda *idx, layer=layer: (layer, blk(*idx), 0)))
        out_specs.append(pl.BlockSpec((rows, cols), lambda *idx: (blk(*idx), 0)))
        out_shapes.append(jax.ShapeDtypeStruct(w.shape[1:], BF16))
    return in_specs, out_specs, out_shapes, [w for w, _ in weights]


def _cast_blocks(in_refs, out_refs):
    for w_ref, wo_ref in zip(in_refs, out_refs, strict=True):
        wo_ref[...] = w_ref[...].astype(BF16)


def _lane_stack(x, hw):
    head = lax.broadcasted_iota(jnp.int32, x.shape, 1) // hw
    return jnp.concatenate([jnp.where(head == h, x, 0.0) for h in range(GROUP // hw)], axis=0)


def _fold_rows(x, n):
    out = x[0:n]
    for h in range(1, x.shape[0] // n):
        out = out + x[h * n:(h + 1) * n]
    return out


def _block_masks(size, blk, d):
    row = lax.broadcasted_iota(jnp.int32, (size, size), 0)
    col = lax.broadcasted_iota(jnp.int32, (size, size), 1)
    rr, cc = row % blk, col % blk
    same = (row // blk) == (col // blk)
    if d == 0:
        return cc <= rr, cc < rr, same
    return cc >= rr, cc > rr, same


def _rwkv_scan_kernel(*refs):
    in_refs = (refs[0:6], refs[6:12])
    y_refs = refs[12:14]
    s_ref = refs[14]

    @pl.when(pl.program_id(1) == 0)
    def _():
        s_ref[...] = jnp.zeros_like(s_ref)

    n = CHUNK
    ngrp = RW // GROUP
    nch = 2 * ngrp
    rrow = lax.broadcasted_iota(jnp.int32, (n, GROUP), 0)
    rcol = lax.broadcasted_iota(jnp.int32, (n, GROUP), 1) % n
    eye_row = jnp.where(rrow == rcol, 1.0, 0.0)
    incl_row = (rcol <= rrow, rcol >= rrow)
    strict_row = (rcol < rrow, rcol > rrow)
    same = _block_masks(GROUP, n, 0)[2]

    def block_diag(x_row):
        xb = x_row.astype(BF16)
        return jnp.where(same, jnp.concatenate([xb] * (GROUP // n), axis=0), jnp.zeros((), BF16))

    def stack(x):
        return _lane_stack(x.astype(BF16), HD)

    pre = []
    for d in range(2):
        tri = jnp.where(_block_masks(n, n, d)[0], 1.0, 0.0).astype(BF16)
        order = range(SCAN_CHUNKS) if d == 0 else range(SCAN_CHUNKS - 1, -1, -1)
        pre_d = []
        for ci in order:
            rows = slice(ci * n, (ci + 1) * n)
            r, k, v, a, b, lw = [ref[rows, :] for ref in in_refs[d]]
            c = _tri_dot(tri, lw)
            c_last = c[n - 1:n] if d == 0 else c[0:1]
            p_inv = jnp.exp(-c)
            p_end = jnp.exp(c_last - c)
            pre_d.append(dict(rows=rows, at=a * jnp.exp(c - lw), rt=r * jnp.exp(c), bt=b * p_inv, kt=k * p_inv,
                              bp=b * p_end, kp=k * p_end, v=v, dec=jnp.exp(c_last)))
        pre.append(pre_d)
    chains = [(d, p, g) for p in range(SCAN_CHUNKS) for d in range(2) for g in range(ngrp)]
    nch = len(chains)
    sl = lambda g: slice(g * GROUP, (g + 1) * GROUP)
    part = lambda d, p, g, name: pre[d][p][name][:, sl(g)]

    a_st = [stack(part(d, p, g, "at")) for d, p, g in chains]
    v_st = [stack(part(d, p, g, "v")) for d, p, g in chains]
    gram = [_dot_nt(jnp.concatenate([part(d, p, g, "at"), part(d, p, g, "rt")], axis=0).astype(BF16),
                    jnp.concatenate([stack(part(d, p, g, "bt")), stack(part(d, p, g, "kt"))], axis=0))
            for d, p, g in chains]
    a_ab = [jnp.where(strict_row[d], gram[i][0:n, 0:GROUP], 0.0) for i, (d, p, g) in enumerate(chains)]
    a_ak = [jnp.where(strict_row[d], gram[i][0:n, GROUP:], 0.0).astype(BF16) for i, (d, p, g) in enumerate(chains)]
    a_r = [jnp.where(jnp.concatenate([incl_row[d]] * 2, axis=1), gram[i][n:2 * n, :], 0.0).astype(BF16)
           for i, (d, p, g) in enumerate(chains)]
    av = [_dot(a_ak[i], v_st[i]) for i in range(nch)]
    tm = [eye_row + m for m in a_ab]
    pw = [_dot(m.astype(BF16), block_diag(m)) for m in a_ab]
    for _ in range(4):
        both = [_dot(jnp.concatenate([t, p], axis=0).astype(BF16), block_diag(p)) for t, p in zip(tm, pw)]
        tm = [t + x[0:n] for t, x in zip(tm, both)]
        pw = [x[n:2 * n] for x in both]
    tm = [t + _dot(t.astype(BF16), block_diag(p)) for t, p in zip(tm, pw)]
    tx = [_dot(tm[i].astype(BF16), jnp.concatenate([a_st[i], stack(av[i])], axis=1))
          for i in range(nch)]

    state = {(d, g): s_ref[d, g] for d in range(2) for g in range(ngrp)}
    for p in range(SCAN_CHUNKS):
        ids = [i for i, ch in enumerate(chains) if ch[1] == p]
        uy0 = {i: _dot_nt(jnp.concatenate([tx[i][:, 0:GROUP], part(*chains[i], "rt")], axis=0).astype(BF16),
                          state[chains[i][0], chains[i][2]].astype(BF16)) for i in ids}
        u = {i: uy0[i][0:n] + tx[i][:, GROUP:] for i in ids}
        for i in ids:
            d, _, g = chains[i]
            y_refs[d][pre[d][p]["rows"], sl(g)] = (
                uy0[i][n:2 * n] + _dot(a_r[i], jnp.concatenate([stack(u[i]), v_st[i]], axis=0)))
        for i in ids:
            d, _, g = chains[i]
            upd = _dot_tn(jnp.concatenate([u[i], part(d, p, g, "v")], axis=0).astype(BF16),
                          jnp.concatenate([part(d, p, g, "bp"), part(d, p, g, "kp")], axis=0).astype(BF16))
            state[d, g] = state[d, g] * part(d, p, g, "dec") + jnp.where(same, upd, 0.0)
    for (d, g), s_new in state.items():
        s_ref[d, g] = s_new


def _rwkv_scan(r, k, v, a, b, lw, nb):
    rows = r.shape[0]
    blk_rows = SCAN_CHUNKS * CHUNK
    steps = (T + C) // blk_rows
    in_specs, args = [], []
    for d in range(2):
        blk = lambda bb, s, d=d: (_chunk_block(d, bb, s, nb, blk_rows), 0)
        in_specs += [pl.BlockSpec((blk_rows, RW), blk)] * 5
        in_specs.append(pl.BlockSpec((None, blk_rows, RW),
                                     lambda bb, s, d=d: (d, _chunk_block(d, bb, s, nb, blk_rows), 0)))
        args += [r, k, v, a, b, lw]
    out_sd = jax.ShapeDtypeStruct((rows, RW), F32)
    return pl.pallas_call(
        _rwkv_scan_kernel,
        grid=(nb, steps),
        in_specs=in_specs,
        out_specs=[pl.BlockSpec((blk_rows, RW), lambda bb, s, d=d: (_chunk_block(d, bb, s, nb, blk_rows), 0))
                   for d in range(2)],
        out_shape=[out_sd, out_sd],
        scratch_shapes=[pltpu.VMEM((2, RW // GROUP, GROUP, GROUP), F32)],
        compiler_params=_params(("parallel", "arbitrary"), V7X_VMEM_LIMIT),
        name="rwkv_scan",
    )(*args)


def _ab_mix(y0_ref, y1_ref, bonus_ref, g_ref, lnw_ref, lnb_ref, gd_ref, gu_ref, *yb_refs, pick, tile):
    y = y0_ref[...] + y1_ref[...]
    mu = _seg_sum(y, gd_ref[...], gu_ref[...]) * (1.0 / HD)
    yc = y - mu
    var = _seg_sum(yc * yc, gd_ref[...], gu_ref[...]) * (1.0 / HD)
    yn = yc * lax.rsqrt(var + RWKV_GN_EPS)
    return (yn * lnw_ref[...] + lnb_ref[...] + bonus_ref[...]) * g_ref[...], pick(tile, *yb_refs)


def _ab_mix_inputs(tile, y0, y1, bonus, g, ln_w, ln_b, yb_lat, yb_ctx):
    tm = OPROJ_TM
    head = np.arange(RW) // HD
    g_down = jnp.asarray(head[:, None] == np.arange(128)[None, :], BF16)
    g_upm = jnp.asarray(np.arange(128)[:, None] == head[None, :], BF16)
    row = pl.BlockSpec((tm, RW), lambda i: (tile(i), 0))
    vec = pl.BlockSpec((1, RW), lambda i: (0, 0))
    yb_args, yb_specs, pick = _split_rows(yb_lat, yb_ctx, tm, 1024, tile=tile)
    specs = [row, row, row, row, vec, vec,
             pl.BlockSpec((RW, 128), lambda i: (0, 0)), pl.BlockSpec((128, RW), lambda i: (0, 0))] + yb_specs
    args = (y0, y1, bonus, g, ln_w.reshape(1, RW), ln_b.reshape(1, RW), g_down, g_upm) + yb_args
    return functools.partial(_ab_mix, pick=pick), args, specs


def _kv_block(i, nb):
    per = T // 256
    return jnp.where(i < nb * per, i // per, i - nb * per), jnp.where(i < nb * per, i % per, per)


def _swa_prep_kernel(k_ref, v_ref, cos_ref, sin_ref, ko_ref, vo_ref):
    cosl, sinl = cos_ref[...], sin_ref[...]
    lane = lax.broadcasted_iota(jnp.int32, cosl.shape, 1)
    low = lane < HD
    for src, dst, rope in ((k_ref, ko_ref, True), (v_ref, vo_ref, False)):
        for t in range(2):
            x = src[:, t * 128:(t + 1) * 128]
            if rope:
                x = _rope(x, cosl, sinl)
            swapped = pltpu.roll(x, HD, axis=1)
            dst[:, (2 * t) * 128:(2 * t + 1) * 128] = jnp.where(low, x, swapped).astype(BF16)
            dst[:, (2 * t + 1) * 128:(2 * t + 2) * 128] = jnp.where(low, swapped, x).astype(BF16)


def _swa_prep(u, cosl, sinl, nb):
    rows = u.shape[0]
    tm = 256
    kv_out = pl.BlockSpec((None, tm, 512), lambda i: (*_kv_block(i, nb), 0))
    sd = jax.ShapeDtypeStruct((nb, T + C, 512), BF16)
    return pl.pallas_call(
        _swa_prep_kernel,
        grid=(rows // tm,),
        in_specs=[pl.BlockSpec((tm, 256), lambda i: (i, AB_KS // 256)),
                  pl.BlockSpec((tm, 256), lambda i: (i, AB_VS // 256)),
                  pl.BlockSpec((tm, 128), lambda i: (i, 0)),
                  pl.BlockSpec((tm, 128), lambda i: (i, 0))],
        out_specs=[kv_out, kv_out],
        out_shape=[sd, sd],
        compiler_params=_params(("parallel",), V7X_VMEM_LIMIT),
        name="swa_prep",
    )(u, u, cosl, sinl)


def _diff_prep_kernel(k_ref, v_ref, cos_ref, sin_ref, ko_ref, vo_ref):
    cosl, sinl = cos_ref[...], sin_ref[...]
    for t in range(8):
        sl = slice(t * 128, (t + 1) * 128)
        ko_ref[:, sl] = _rope(k_ref[:, sl], cosl, sinl).astype(BF16)
    vo_ref[...] = v_ref[...].astype(BF16)


def _diff_prep(u, cosl, sinl, nb):
    rows = u.shape[0]
    tm = 256
    kv_out = pl.BlockSpec((None, tm, 1024), lambda i: (*_kv_block(i, nb), 0))
    sd = jax.ShapeDtypeStruct((nb, T + C, 1024), BF16)
    return pl.pallas_call(
        _diff_prep_kernel,
        grid=(rows // tm,),
        in_specs=[pl.BlockSpec((tm, 1024), lambda i: (i, 1)),
                  pl.BlockSpec((tm, 1024), lambda i: (i, 2)),
                  pl.BlockSpec((tm, 128), lambda i: (i, 0)),
                  pl.BlockSpec((tm, 128), lambda i: (i, 0))],
        out_specs=[kv_out, kv_out],
        out_shape=[sd, sd],
        compiler_params=_params(("parallel",), V7X_VMEM_LIMIT),
        name="diff_prep",
    )(u, u, cosl, sinl)


def _swa_kernel(q_ref, cos_ref, sin_ref, k_ref, v_ref, sink_ref, *rest):
    n_cast = len(rest) // 2
    o_ref = rest[n_cast]
    ctx_mode = n_cast == 0
    _cast_blocks(rest[:n_cast], rest[n_cast + 1:])
    n = pl.program_id(1)
    cosl, sinl = cos_ref[...], sin_ref[...]
    QB = q_ref.shape[0]
    span = QB + 2 * SWA_WINDOW
    if not ctx_mode:
        start = pl.multiple_of(jnp.clip(n * QB - SWA_WINDOW, 0, T - span), SWA_WINDOW)
        kpos = start + lax.broadcasted_iota(jnp.int32, (2 * QB, span), 1)
        qpos = n * QB + lax.broadcasted_iota(jnp.int32, (2 * QB, span), 0) % QB
        valid = jnp.abs(kpos - qpos) <= SWA_WINDOW
    low = lax.broadcasted_iota(jnp.int32, (QB, 128), 1) < HD
    top = lax.broadcasted_iota(jnp.int32, (2 * QB, 1), 0) < QB
    tile = lambda j: slice((j // 2) * 128, (j // 2 + 1) * 128)
    qs, sink = [], []
    for j in range(8):
        q = _rope(q_ref[:, j * 128:(j + 1) * 128], cosl, sinl) * (HD ** -0.5 * LOG2E)
        qs.append(jnp.concatenate([jnp.where(low, q, 0.0), jnp.where(low, 0.0, q)], axis=0).astype(BF16))
        sink.append(jnp.where(top, sink_ref[2 * j:2 * j + 1, 0:1], sink_ref[2 * j + 1:2 * j + 2, 0:1]) * LOG2E)
    s_c = [_dot_nt(qs[j], k_ref[T:T + C, tile(j)]) for j in range(8)]
    m = [jnp.maximum(jnp.max(s, axis=-1, keepdims=True), sk) for s, sk in zip(s_c, sink)]
    if not ctx_mode:
        s_w = [jnp.where(valid, _dot_nt(qs[j], k_ref[pl.ds(start, span), tile(j)]), -jnp.inf) for j in range(8)]
        m = [jnp.maximum(mm, jnp.max(s, axis=-1, keepdims=True)) for mm, s in zip(m, s_w)]
    p_c = [jnp.exp2(s - mm) for s, mm in zip(s_c, m)]
    den = [jnp.sum(p, axis=-1, keepdims=True) + jnp.exp2(sk - mm) for p, sk, mm in zip(p_c, sink, m)]
    pv = [_dot(p.astype(BF16), v_ref[T:T + C, tile(j)]) for j, p in enumerate(p_c)]
    if not ctx_mode:
        p_w = [jnp.exp2(s - mm) for s, mm in zip(s_w, m)]
        den = [dd + jnp.sum(p, axis=-1, keepdims=True) for dd, p in zip(den, p_w)]
        pv = [x + _dot(p.astype(BF16), v_ref[pl.ds(start, span), tile(j)]) for j, (x, p) in enumerate(zip(pv, p_w))]
    for j in range(8):
        o = pv[j] / den[j]
        o_ref[:, j * 128:(j + 1) * 128] = jnp.where(low, o[0:QB], o[QB:2 * QB])


def _swa(u, cosl, sinl, kpad, vpad, sink, nb, ctx_mode, weights=()):
    assert ctx_mode == (not weights)
    QB = SWA_QB
    nq = (C if ctx_mode else T) // QB
    base = nb * (T // QB) if ctx_mode else 0
    rows = nb * nq * QB
    rowblk = lambda b, n: base + b * nq + n
    kv = pl.BlockSpec((None, T + C, 512), lambda b, n: (b, 0, 0))
    in_specs = [pl.BlockSpec((QB, 1024), lambda b, n: (rowblk(b, n), AB_Q // 1024)),
                pl.BlockSpec((QB, 128), lambda b, n: (rowblk(b, n), 0)),
                pl.BlockSpec((QB, 128), lambda b, n: (rowblk(b, n), 0)),
                kv, kv, pl.BlockSpec((16, 128), lambda b, n: (0, 0))]
    out_specs = [pl.BlockSpec((QB, 1024), lambda b, n: (b * nq + n, 0))]
    out_shape = [jax.ShapeDtypeStruct((rows, 1024), F32)]
    args = [u, cosl, sinl, kpad, vpad, sink]
    if weights:
        assert nb * nq >= CAST_BLOCKS
        cast_in, cast_out, cast_shape, cast_args = _cast_specs(weights, lambda b, n: b * nq + n)
        in_specs += cast_in
        out_specs += cast_out
        out_shape += cast_shape
        args += cast_args
    return pl.pallas_call(
        _swa_kernel,
        grid=(nb, nq),
        in_specs=in_specs,
        out_specs=out_specs,
        out_shape=out_shape,
        compiler_params=_params(("arbitrary", "arbitrary"), V7X_VMEM_LIMIT),
        name="swa_ctx" if ctx_mode else "swa_latent",
    )(*args)


def _diff_kernel(q_ref, cos_ref, sin_ref, k_ref, v_ref, lam_ref, sub_ref, o_ref):
    cosl, sinl = cos_ref[...], sin_ref[...]
    lam = lam_ref[...]
    nq = q_ref.shape[0]
    low = lax.broadcasted_iota(jnp.int32, (nq, 128), 1) < HD
    tile = lambda h: slice(h * 128, (h + 1) * 128)

    def logits(h):
        q = _rope(q_ref[:, tile(h)], cosl, sinl) * (HD ** -0.5 * LOG2E)
        qs = jnp.concatenate([jnp.where(low, q, 0.0), jnp.where(low, 0.0, q)], axis=0).astype(BF16)
        return _dot_nt(qs, k_ref[:, tile(h)])

    s_next = logits(0)
    for h in range(8):
        s = s_next
        if h + 1 < 8:
            s_next = logits(h + 1)
        e = jnp.exp2(s - jnp.max(s, axis=-1, keepdims=True))
        inv = 1.0 / jnp.sum(e, axis=-1, keepdims=True)
        o2 = _dot(e.astype(BF16), v_ref[:, tile(h)])
        o = o2[0:nq] * inv[0:nq] - o2[nq:2 * nq] * (lam[:, 0:1] * inv[nq:2 * nq])
        o = o * lax.rsqrt(jnp.mean(o * o, axis=-1, keepdims=True) + 1e-5)
        o_ref[:, tile(h)] = o * sub_ref[...]


def _diff_attn(u, cosl, sinl, kb, vb, lam, sub, nb):
    QB = DIFF_QB
    nq = T // QB
    kv = pl.BlockSpec((None, T + C, 1024), lambda b, n: (b, 0, 0))
    vec = pl.BlockSpec((1, 128), lambda b, n: (0, 0))
    return pl.pallas_call(
        _diff_kernel,
        grid=(nb, nq),
        in_specs=[pl.BlockSpec((QB, 1024), lambda b, n: (b * nq + n, 0)),
                  pl.BlockSpec((QB, 128), lambda b, n: (b * nq + n, 0)),
                  pl.BlockSpec((QB, 128), lambda b, n: (b * nq + n, 0)),
                  kv, kv, vec, vec],
        out_specs=pl.BlockSpec((QB, 1024), lambda b, n: (b * nq + n, 0)),
        out_shape=jax.ShapeDtypeStruct((nb * T, 1024), F32),
        compiler_params=_params(("parallel", "arbitrary"), V7X_VMEM_LIMIT),
        name="diff_attn",
    )(u, cosl, sinl, kb, vb, lam, sub)


def _hgrn_exact_att(q, kk, bcum, tmp_ref):
    n = CHUNK
    tmp_ref[0] = bcum
    tmp_ref[1] = kk
    coli = lax.broadcasted_iota(jnp.int32, (n, 2 * n), 1)

    def body(s, acc):
        bs = tmp_ref[0, pl.ds(s, 1), :]
        ks = tmp_ref[1, pl.ds(s, 1), :]
        w = q * jnp.exp(jnp.minimum(bcum - bs, 0.0)) * ks
        c0 = jnp.sum(w[:, 0:HG_DK], axis=-1, keepdims=True)
        c1 = jnp.sum(w[:, HG_DK:], axis=-1, keepdims=True)
        return acc + jnp.concatenate([jnp.where(coli == s, c0, 0.0), jnp.where(coli == s + n, c1, 0.0)], axis=0)

    return lax.fori_loop(0, n, body, jnp.zeros((2 * n, 2 * n), F32))


def _hgrn_scan_kernel(*refs):
    n_cast = (len(refs) - 13) // 2
    in_refs = (refs[0:4], refs[4:8])
    o_refs = refs[8 + n_cast:10 + n_cast]
    s_ref, g_ref, tmp_ref = refs[10 + 2 * n_cast:]
    _cast_blocks(refs[8:8 + n_cast], refs[10 + n_cast:10 + 2 * n_cast])

    @pl.when(pl.program_id(1) == 0)
    def _():
        s_ref[...] = jnp.zeros_like(s_ref)

    n = CHUNK
    ngrp = 1024 // GROUP
    nsub = n // SUB
    pre = []
    for d in range(2):
        tri = jnp.where(_block_masks(n, n, d)[0], 1.0, 0.0).astype(BF16)
        lb = in_refs[d][3][...]
        order = range(SCAN_CHUNKS) if d == 0 else range(SCAN_CHUNKS - 1, -1, -1)
        pre_d = []
        for ci in order:
            rows = slice(ci * n, (ci + 1) * n)
            q, z, v = [ref[rows, :] for ref in in_refs[d][0:3]]
            logf = jnp.log(lb + (1.0 - lb) * _sigmoid(z))
            kk = (1.0 - lb) * _sigmoid(-z)
            bcum = _tri_dot(tri, logf)
            b_last = bcum[n - 1:n] if d == 0 else bcum[0:1]
            b_excl = bcum - logf
            qh, kh = [], []
            for sb in range(nsub):
                lo, hi = sb * SUB, (sb + 1) * SUB
                beta = b_excl[lo:lo + 1] if d == 0 else b_excl[hi - 1:hi]
                qh.append(q[lo:hi] * jnp.exp(bcum[lo:hi] - beta))
                kh.append(kk * jnp.exp(beta - bcum))
            pre_d.append(dict(rows=rows, q=q, v=v, kk=kk, bcum=bcum, qh=qh, kh=kh, qe=q * jnp.exp(bcum),
                              ke=kk * jnp.exp(b_last - bcum), dec=jnp.exp(b_last), min_logf=jnp.min(logf)))
        pre.append(pre_d)
    chains = [(d, p, g) for p in range(SCAN_CHUNKS) for d in range(2) for g in range(ngrp)]
    sl = lambda g: slice(g * GROUP, (g + 1) * GROUP)

    for i, (d, p, g) in enumerate(chains):
        c = pre[d][p]
        rows = [_dot_nt(_lane_stack(c["qh"][sb][:, sl(g)].astype(BF16), HG_DK),
                        _lane_stack(c["kh"][sb][:, sl(g)].astype(BF16), HG_DK)) for sb in range(nsub)]
        g_ref[i] = jnp.concatenate([rows[sb][h * SUB:(h + 1) * SUB] for h in range(2) for sb in range(nsub)], axis=0)

    min_logf = functools.reduce(jnp.minimum, [c["min_logf"] for pre_d in pre for c in pre_d])

    @pl.when(min_logf < -(HG_CLAMP / SUB))
    def _():
        for i, (d, p, g) in enumerate(chains):
            c = pre[d][p]
            g_ref[i] = _hgrn_exact_att(c["q"][:, sl(g)], c["kk"][:, sl(g)], c["bcum"][:, sl(g)], tmp_ref)

    state = {(d, g): s_ref[d, g] for d in range(2) for g in range(ngrp)}
    head_same = _block_masks(GROUP, HG_DK, 0)[2]
    for p in range(SCAN_CHUNKS):
        ids = [i for i, ch in enumerate(chains) if ch[1] == p]
        inter = {i: _dot_nt(pre[chains[i][0]][p]["qe"][:, sl(chains[i][2])].astype(BF16),
                            state[chains[i][0], chains[i][2]].astype(BF16)) for i in ids}
        for i in ids:
            d, _, g = chains[i]
            incl, _, same = _block_masks(2 * n, n, d)
            att = jnp.where(incl & same, g_ref[i], 0.0).astype(BF16)
            v_st = _lane_stack(pre[d][p]["v"][:, sl(g)].astype(BF16), HG_DK)
            o_refs[d][pre[d][p]["rows"], sl(g)] = inter[i] + _fold_rows(_dot(att, v_st), n)
        for i in ids:
            d, _, g = chains[i]
            upd = _dot_tn(pre[d][p]["v"][:, sl(g)].astype(BF16), pre[d][p]["ke"][:, sl(g)].astype(BF16))
            state[d, g] = state[d, g] * pre[d][p]["dec"][:, sl(g)] + jnp.where(head_same, upd, 0.0)
    for (d, g), s_new in state.items():
        s_ref[d, g] = s_new


def _hgrn_scan(u, lb, nb, weights):
    rows = u.shape[0]
    blk_rows = SCAN_CHUNKS * CHUNK
    steps = (T + C) // blk_rows
    assert nb * steps >= CAST_BLOCKS
    cast_in, cast_out, cast_shape, cast_args = _cast_specs(weights, lambda bb, s: bb * steps + s)
    in_specs, args = [], []
    for d in range(2):
        col = lambda j, d=d: pl.BlockSpec((blk_rows, 1024),
                                          lambda bb, s: (_chunk_block(d, bb, s, nb, blk_rows), j))
        in_specs += [col(3), col(4 + d), col(6), pl.BlockSpec((None, 1, 1024), lambda bb, s, d=d: (d, 0, 0))]
        args += [u, u, u, lb]
    out_sd = jax.ShapeDtypeStruct((rows, 1024), F32)
    ngrp = 1024 // GROUP
    return pl.pallas_call(
        _hgrn_scan_kernel,
        grid=(nb, steps),
        in_specs=in_specs + cast_in,
        out_specs=[pl.BlockSpec((blk_rows, 1024), lambda bb, s, d=d: (_chunk_block(d, bb, s, nb, blk_rows), 0))
                   for d in range(2)] + cast_out,
        out_shape=[out_sd, out_sd] + cast_shape,
        scratch_shapes=[pltpu.VMEM((2, ngrp, GROUP, GROUP), F32),
                        pltpu.VMEM((2 * ngrp * SCAN_CHUNKS, 2 * CHUNK, 2 * CHUNK), F32),
                        pltpu.VMEM((2, CHUNK, GROUP), F32)],
        compiler_params=_params(("arbitrary", "arbitrary"), V7X_VMEM_LIMIT),
        name="hgrn_scan",
    )(*args, *cast_args)


def _cd_mix(yc_ref, o0_ref, o1_ref, g_ref, gn_ref, *, tile):
    g = g_ref[...]
    heads = []
    for h in range(8):
        sl = slice(h * 128, (h + 1) * 128)
        o = o0_ref[:, sl] + o1_ref[:, sl]
        o = o * lax.rsqrt(jnp.mean(o * o, axis=-1, keepdims=True) + NORM_EPS) * gn_ref[...]
        gh = g[:, sl]
        heads.append(o * (gh * _sigmoid(gh)))
    return yc_ref[...], jnp.concatenate(heads, axis=1)


def _cd_mix_inputs(tile, yc, o0, o1, u, gnorm):
    tm = OPROJ_TM
    row = pl.BlockSpec((tm, 1024), lambda i: (tile(i), 0))
    specs = [row, row, row, pl.BlockSpec((tm, 1024), lambda i: (tile(i), 7)),
             pl.BlockSpec((1, 128), lambda i: (0, 0))]
    return _cd_mix, (yc, o0, o1, u, gnorm.reshape(1, 128)), specs


def _rope_tables(nb):
    t = np.arange(T)
    quarter = HD // 4
    inv = ROPE_THETA ** (-jnp.arange(quarter, dtype=F32) / quarter)
    rows = jnp.asarray(t // GRID_W, F32)
    cols = jnp.asarray(t % GRID_W, F32)
    ang = jnp.concatenate([rows[:, None] * inv, cols[:, None] * inv], axis=-1)
    cos, sin = jnp.cos(ang), jnp.sin(ang)
    cosl = jnp.tile(jnp.concatenate([cos, cos], axis=-1), (nb, 2))
    sinl = jnp.tile(jnp.concatenate([-sin, sin], axis=-1), (nb, 2))
    cosl = jnp.concatenate([cosl, jnp.ones((nb * C, 128), F32)], axis=0)
    sinl = jnp.concatenate([sinl, jnp.zeros((nb * C, 128), F32)], axis=0)
    return cosl, sinl


def kernel(x, c, ctx, c_ctx, ada_w, ada_b, mlp_w1, mlp_w2, final_norm, ab_w_in, ab_w_out, ab_mu, ab_w0,
           ab_w_up, ab_a0, ab_a_up, ab_g_up, ab_k_k, ab_k_a, ab_r_k, ab_ln_w, ab_ln_b, ab_sink,
           cd_w_in, cd_w_out, cd_lam, cd_subln, cd_gnorm, hgrn_lb_logits):
    nb = x.shape[0]
    assert x.shape == (nb, T, D) and ctx.shape == (nb, C, D) and nb < 8
    assert ada_w.shape[0] == 2, "one AB layer followed by one CD layer"
    rx = nb * T
    rows = rx + nb * C
    x2, c2 = x.reshape(rx, D), ctx.reshape(nb * C, D)
    cosl, sinl = _rope_tables(nb)

    s_in = jnp.zeros((8, D), F32).at[:nb].set(c).at[nb].set(c_ctx)
    mod = _ada(s_in, ada_w, ada_b)

    w = ab_w_in[0]
    w_in0 = jnp.concatenate([w[:, :3 * RW], w[:, 3 * RW + 3 * LORA:], w[:, 3 * RW:3 * RW + 3 * LORA],
                             jnp.zeros((D, LORA), F32)], axis=1).astype(BF16)
    u = _norm_mod_matmul(x2, c2, mod, 0, w_in0, nb, 512, AB_N // 2)
    r, k, v, a, b, lw, g, bonus = _rwkv_prepare(u, nb, ab_mu[0], ab_w0[0], ab_w_up[0], ab_a0[0], ab_a_up[0],
                                                ab_g_up[0], ab_k_k[0], ab_k_a[0], ab_r_k[0].reshape(RW))
    y0, y1 = _rwkv_scan(r, k, v, a, b, lw, nb)
    kpad, vpad = _swa_prep(u, cosl, sinl, nb)
    sink = jnp.broadcast_to(ab_sink[0][:, None], (16, 128))
    yb_lat, w1b, w2b, w_in1 = _swa(u, cosl, sinl, kpad, vpad, sink, nb, False,
                                   [(mlp_w1, 0), (mlp_w2, 0), (cd_w_in, 0)])
    (yb_ctx,) = _swa(u, cosl, sinl, kpad, vpad, sink, nb, True)
    w_out0 = ab_w_out[0].astype(BF16)
    mix_inputs = functools.partial(_ab_mix_inputs, y0=y0, y1=y1, bonus=bonus, g=g, ln_w=ab_ln_w[0],
                                   ln_b=ab_ln_b[0], yb_lat=yb_lat, yb_ctx=yb_ctx)
    xc = _out_proj(mix_inputs, x2, c2, rows, mod, 0, w_out0[:RW], w_out0[RW:], nb)
    fn = final_norm.reshape(1, D)
    xc = _mlp(xc, rows, mod, 0, w1b, w2b, fn, nb, False)

    lam_init = 0.8 - 0.6 * math.exp(-0.3 * 1)
    lb_table = jnp.cumsum(jax.nn.softmax(hgrn_lb_logits.astype(F32), axis=0), axis=0)
    lb = (lb_table - lb_table[0])[1].reshape(2, 1, 1024)
    lf = cd_lam[0].astype(F32)
    lmb = jnp.exp(jnp.sum(lf[0] * lf[1])) - jnp.exp(jnp.sum(lf[2] * lf[3])) + lam_init
    u = _norm_mod_matmul(xc, None, mod, 1, w_in1, nb, 1024, 1024)
    kb, vb = _diff_prep(u, cosl, sinl, nb)
    yc = _diff_attn(u, cosl, sinl, kb, vb, jnp.full((1, 128), lmb, F32),
                    (cd_subln[0] * (1.0 - lam_init)).reshape(1, 128), nb)
    o0, o1, w1b, w2b = _hgrn_scan(u, lb, nb, [(mlp_w1, 1), (mlp_w2, 1)])
    w_out1 = cd_w_out[0].astype(BF16)
    mix_inputs = functools.partial(_cd_mix_inputs, yc=yc, o0=o0, o1=o1, u=u, gnorm=cd_gnorm[0])
    xl = _out_proj(mix_inputs, xc, None, rx, mod, 1, w_out1[:1024], w_out1[1024:], nb)
    out = _mlp(xl, rx, mod, 1, w1b, w2b, fn, nb, True)
    return out.reshape(nb, T, D)
```

```python
import functools
import math

import jax
import jax.numpy as jnp
import numpy as np
from jax import lax
from jax.experimental import pallas as pl
from jax.experimental.pallas import tpu as pltpu

F32 = jnp.float32
BF16 = jnp.bfloat16

D = 2048
T = 2048
C = 256
GRID_W = 64
D_FF = 4 * D
HD = 64
ROPE_THETA = 10000.0
LOG2E = 1.4426950408889634
NORM_EPS = 1e-6
N_MOD = 6
RW = 1024
RWKV_GN_EPS = 64e-5
LORA = 64
SWA_WINDOW = 128
SWA_QB = 128
DIFF_QB = 256
OPROJ_TM = 512
CHUNK = 64
SCAN_CHUNKS = 4
CAST_BLOCK_COUNTS = (32, 64)
SUB = 16
HG_DK = 128
HG_CLAMP = 80.0
GROUP = 256
V7X_VMEM_BYTES = 64 * 1024 * 1024
V7X_VMEM_LIMIT = V7X_VMEM_BYTES - 8 * 1024 * 1024

AB_Q, AB_KS, AB_VS, AB_LORA, AB_N = 3072, 4096, 4352, 4608, 4864


def _dot(a, b):
    return jnp.dot(a, b, preferred_element_type=F32)


def _dot_nt(a, b):
    return lax.dot_general(a, b, (((1,), (1,)), ((), ())), preferred_element_type=F32)


def _dot_tn(a, b):
    return lax.dot_general(a, b, (((0,), (0,)), ((), ())), preferred_element_type=F32)


def _split2(x):
    hi = x.astype(BF16)
    lo = (x - hi.astype(F32)).astype(BF16)
    return hi, lo


def _split3(x):
    hi = x.astype(BF16)
    r1 = x - hi.astype(F32)
    mid = r1.astype(BF16)
    lo = (r1 - mid.astype(F32)).astype(BF16)
    return hi, mid, lo


def _dot_exact_rhs(x, g):
    hi, lo = _split2(x)
    return _dot(hi, g) + _dot(lo, g)


def _tri_dot(tri, x):
    hi, mid, lo = _split3(x)
    return _dot(tri, hi) + _dot(tri, mid) + _dot(tri, lo)


def _seg_sum(x, g_down, g_up):
    return _dot_exact_rhs(_dot_exact_rhs(x, g_down), g_up)


def _rope(x, cosl, sinl):
    w = x.shape[-1]
    lane = lax.broadcasted_iota(jnp.int32, x.shape, x.ndim - 1)
    first = (lane & 63) < 32
    swapped = jnp.where(first, pltpu.roll(x, w - 32, axis=1), pltpu.roll(x, 32, axis=1))
    return x * cosl + swapped * sinl


def _sigmoid(x):
    return 1.0 / (1.0 + jnp.exp(-x))


def _softplus(x):
    return jnp.maximum(x, 0.0) + jnp.log(1.0 + jnp.exp(-jnp.abs(x)))


def _params(sem, vmem=None):
    return pltpu.CompilerParams(dimension_semantics=sem, vmem_limit_bytes=vmem)


def _ada_kernel(s_ref, w_ref, b_ref, o_ref):
    s = s_ref[...]
    s = s * _sigmoid(s)
    o_ref[...] = _dot(s.astype(BF16), w_ref[...].astype(BF16)) + b_ref[...]


def _ada(s_in, ada_w, ada_b):
    depth = ada_w.shape[0]
    n = ada_w.shape[2]
    tn = 1536
    out = pl.pallas_call(
        _ada_kernel,
        grid=(depth, n // tn),
        in_specs=[pl.BlockSpec((8, D), lambda l, j: (0, 0)),
                  pl.BlockSpec((None, D, tn), lambda l, j: (l, 0, j)),
                  pl.BlockSpec((None, 1, tn), lambda l, j: (l, 0, j))],
        out_specs=pl.BlockSpec((None, 8, tn), lambda l, j: (l, 0, j)),
        out_shape=jax.ShapeDtypeStruct((depth, 8, n), F32),
        compiler_params=_params(("arbitrary", "arbitrary"), V7X_VMEM_LIMIT),
        name="ada_mod",
    )(s_in, ada_w, ada_b.reshape(depth, 1, n))
    return out.reshape(depth, 8, 1, n)


def _mod_spec(layer, k, tm, nb, tile=lambda i: i):
    return pl.BlockSpec((None, None, 1, D),
                        lambda i, *_: (layer, jnp.minimum((tile(i) * tm) // T, nb), 0, k))


def _split_rows(lat, ctx, tm, width, col=0, tile=lambda i: i):
    if ctx is None:
        return (lat,), [pl.BlockSpec((tm, width), lambda i, *_: (tile(i), col))], lambda t, ref: ref[...]
    n_lat = lat.shape[0] // tm
    specs = [pl.BlockSpec((tm, width), lambda i, *_: (jnp.minimum(tile(i), n_lat - 1), col)),
             pl.BlockSpec((tm, width), lambda i, *_: (jnp.maximum(tile(i) - n_lat, 0), col),
                          pipeline_mode=pl.Buffered(1))]
    return (lat, ctx), specs, lambda t, lat_ref, ctx_ref: jnp.where(t < n_lat, lat_ref[...], ctx_ref[...])


def _nmm_kernel(*refs, n_x, pick):
    sh_ref, sc_ref, w_ref, o_ref, lhs_ref = refs[n_x:]

    @pl.when(pl.program_id(1) == 0)
    def _():
        x = pick(pl.program_id(0), *refs[:n_x])
        xn = x * lax.rsqrt(jnp.mean(x * x, axis=-1, keepdims=True) + NORM_EPS)
        lhs_ref[...] = (xn * (1.0 + sc_ref[...]) + sh_ref[...]).astype(BF16)

    o_ref[...] = _dot(lhs_ref[...], w_ref[...].astype(BF16)).astype(o_ref.dtype)


def _norm_mod_matmul(x_lat, x_ctx, mod, layer, w, nb, tm, tn):
    rows = x_lat.shape[0] + (0 if x_ctx is None else x_ctx.shape[0])
    n = w.shape[1]
    x_args, x_specs, pick = _split_rows(x_lat, x_ctx, tm, D)
    if x_ctx is None and tm > 512:
        x_specs = [pl.BlockSpec((tm, D), lambda i, j: (i, 0), pipeline_mode=pl.Buffered(1))]
    return pl.pallas_call(
        functools.partial(_nmm_kernel, n_x=len(x_args), pick=pick),
        grid=(rows // tm, n // tn),
        in_specs=x_specs + [_mod_spec(layer, 0, tm, nb), _mod_spec(layer, 1, tm, nb),
                            pl.BlockSpec((D, tn), lambda i, j: (0, j))],
        out_specs=pl.BlockSpec((tm, tn), lambda i, j: (i, j)),
        out_shape=jax.ShapeDtypeStruct((rows, n), F32),
        scratch_shapes=[pltpu.VMEM((tm, D), BF16)],
        compiler_params=_params(("parallel", "arbitrary"), V7X_VMEM_LIMIT),
        name=f"in_proj_{layer}",
    )(*x_args, mod, mod, w)


def _mlp_kernel(x_ref, sh_ref, sc_ref, gt_ref, w1_ref, w2_ref, fn_ref, o_ref, lhs_ref, *, final):
    f = pl.program_id(1)

    @pl.when(f == 0)
    def _():
        x = x_ref[...]
        xn = x * lax.rsqrt(jnp.mean(x * x, axis=-1, keepdims=True) + NORM_EPS)
        lhs_ref[...] = (xn * (1.0 + sc_ref[...]) + sh_ref[...]).astype(BF16)
        o_ref[...] = jnp.zeros_like(o_ref)

    h = jnp.maximum(_dot(lhs_ref[...], w1_ref[...]), 0.0)
    o_ref[...] += _dot((h * h).astype(BF16), w2_ref[...])

    @pl.when(f == pl.num_programs(1) - 1)
    def _():
        y = x_ref[...] + gt_ref[...] * o_ref[...]
        if final:
            y = y * lax.rsqrt(jnp.mean(y * y, axis=-1, keepdims=True) + NORM_EPS) * fn_ref[...]
        o_ref[...] = y


def _mlp(xc, rows, mod, layer, w1, w2, final_norm, nb, final):
    tm, tf = 512, 1024
    return pl.pallas_call(
        functools.partial(_mlp_kernel, final=final),
        grid=(rows // tm, D_FF // tf),
        in_specs=[pl.BlockSpec((tm, D), lambda i, f: (i, 0)),
                  _mod_spec(layer, 3, tm, nb), _mod_spec(layer, 4, tm, nb), _mod_spec(layer, 5, tm, nb),
                  pl.BlockSpec((D, tf), lambda i, f: (0, f)),
                  pl.BlockSpec((tf, D), lambda i, f: (f, 0)),
                  pl.BlockSpec((1, D), lambda i, f: (0, 0))],
        out_specs=pl.BlockSpec((tm, D), lambda i, f: (i, 0)),
        out_shape=jax.ShapeDtypeStruct((rows, D), F32),
        scratch_shapes=[pltpu.VMEM((tm, D), BF16)],
        compiler_params=_params(("parallel", "arbitrary"), V7X_VMEM_LIMIT),
        name=f"mlp_{layer}",
    )(xc, mod, mod, mod, w1, w2, final_norm)


def _oproj_kernel(*refs, n_mix, mix, n_x, pick):
    x_refs = refs[n_mix:n_mix + n_x]
    gt_ref, wa_ref, wb_ref, o_ref = refs[n_mix + n_x:]
    i = pl.program_id(0)
    ya, yb = mix(*refs[:n_mix], tile=i)
    y = _dot(ya.astype(BF16), wa_ref[...]) + _dot(yb.astype(BF16), wb_ref[...])
    o_ref[...] = pick(i, *x_refs) + gt_ref[...] * y


def _out_proj(mix_inputs, x_lat, x_ctx, rows, mod, layer, w, nb):
    tm = OPROJ_TM
    half = w.shape[0] // 2
    mix, mix_args, mix_specs = mix_inputs(lambda i: i)
    x_args, x_specs, pick = _split_rows(x_lat, x_ctx, tm, D)
    return pl.pallas_call(
        functools.partial(_oproj_kernel, n_mix=len(mix_args), mix=mix, n_x=len(x_args), pick=pick),
        grid=(rows // tm,),
        in_specs=list(mix_specs) + x_specs + [
            _mod_spec(layer, 2, tm, nb),
            pl.BlockSpec((half, D), lambda i: (0, 0), pipeline_mode=pl.Buffered(1)),
            pl.BlockSpec((half, D), lambda i: (1, 0), pipeline_mode=pl.Buffered(1))],
        out_specs=pl.BlockSpec((tm, D), lambda i: (i, 0)),
        out_shape=jax.ShapeDtypeStruct((rows, D), F32),
        compiler_params=_params(("parallel",), V7X_VMEM_LIMIT),
        name=f"out_proj_{layer}",
    )(*mix_args, *x_args, mod, w, w)


def _shifted(u, prev_row, next_row):
    tm = u.shape[0]
    row = lax.broadcasted_iota(jnp.int32, u.shape, 0)
    up = jnp.where(row == 0, prev_row, pltpu.roll(u, 1, axis=0))
    un = jnp.where(row == tm - 1, next_row, pltpu.roll(u, tm - 1, axis=0))
    return 0.5 * (up + un)


def _rwkv_prep_kernel(u_ref, ul_ref, up_ref, un_ref, ulp_ref, uln_ref, mu_ref, mul_ref, w0_ref, wup_ref,
                      a0_ref, aup_ref, gup_ref, kk_ref, ka_ref, rk_ref, gd_ref, gu_ref,
                      r_out, k_out, v_out, a_out, b_out, lw_out, g_out, bonus_out):
    u = u_ref[...]
    u = u + mu_ref[...] * (_shifted(u, up_ref[...], un_ref[...]) - u)
    ul = ul_ref[...]
    ul = ul + mul_ref[...] * (_shifted(ul, ulp_ref[...], uln_ref[...]) - ul)
    r, k, v = u[:, 0:RW], u[:, RW:2 * RW], u[:, 2 * RW:3 * RW]

    th = jnp.tanh(ul).astype(BF16)
    for d in range(2):
        w_log = -_softplus(-(w0_ref[d:d + 1, :] + _dot(th, wup_ref[d]))) - 0.5
        lw_out[d] = -jnp.exp(w_log)
    a = _sigmoid(a0_ref[...] + _dot(ul.astype(BF16), aup_ref[...]))
    g_out[...] = _dot(_sigmoid(ul).astype(BF16), gup_ref[...])

    kk = k * kk_ref[...]
    nrm = jnp.sqrt(_seg_sum(kk * kk, gd_ref[...], gu_ref[...]))
    kk = kk / jnp.maximum(nrm, 1e-12)
    k = k * (1.0 + (a - 1.0) * ka_ref[...])
    r_out[...] = r
    k_out[...] = k
    v_out[...] = v
    a_out[...] = -kk
    b_out[...] = kk * a
    bonus_out[...] = _seg_sum(r * k * rk_ref[...], gd_ref[...], gu_ref[...]) * v


def _seq_halo(u, tm, nb):
    rows = u.shape[0]
    nblk = rows // tm
    starts = np.arange(nblk) * tm
    seq_len = np.where(starts < nb * T, T, C)
    seq_off = np.where(starts < nb * T, starts % T, (starts - nb * T) % C)
    has_prev = seq_off > 0
    has_next = seq_off + tm < seq_len
    prev_idx = np.where(has_prev, starts - 1, 0)
    next_idx = np.where(has_next, starts + tm, 0)
    up = jnp.where(has_prev[:, None], u[prev_idx], 0.0)
    un = jnp.where(has_next[:, None], u[next_idx], 0.0)
    return up[:, None, :], un[:, None, :]


def _rwkv_prepare(u, nb, mu, w0, w_up, a0, a_up, g_up, k_k, k_a, r_k):
    rows = u.shape[0]
    tm = 256
    nblk = rows // tm
    up, un = _seq_halo(u, tm, nb)
    pad = jnp.zeros((LORA,), F32)
    mu_rkv = mu[:3 * RW].reshape(1, 3 * RW)
    mu_l = jnp.concatenate([mu[3 * RW:], pad]).reshape(1, 4 * LORA)

    def lora_w(w, slot):
        z = jnp.zeros((4 * LORA, RW), F32)
        return z.at[slot * LORA:(slot + 1) * LORA].set(w).astype(BF16)

    wup = jnp.stack([lora_w(w_up[0], 0), lora_w(w_up[1], 0)])
    aup = lora_w(a_up, 1)
    gup = lora_w(g_up, 2)
    head = np.arange(RW) // HD
    g_down = jnp.asarray(head[:, None] == np.arange(128)[None, :], BF16)
    g_upm = jnp.asarray(np.arange(128)[:, None] == head[None, :], BF16)

    row = lambda w: pl.BlockSpec((tm, w), lambda i: (i, 0))
    vec = lambda w: pl.BlockSpec((1, w), lambda i: (0, 0))
    full = lambda *s: pl.BlockSpec(s, lambda i: (0,) * len(s))
    out_sd = jax.ShapeDtypeStruct((rows, RW), F32)
    outs = pl.pallas_call(
        _rwkv_prep_kernel,
        grid=(nblk,),
        in_specs=[pl.BlockSpec((tm, 3 * RW), lambda i: (i, 0)),
                  pl.BlockSpec((tm, 4 * LORA), lambda i: (i, AB_LORA // (4 * LORA))),
                  pl.BlockSpec((None, 1, 3 * RW), lambda i: (i, 0, 0)),
                  pl.BlockSpec((None, 1, 3 * RW), lambda i: (i, 0, 0)),
                  pl.BlockSpec((None, 1, 4 * LORA), lambda i: (i, 0, AB_LORA // (4 * LORA))),
                  pl.BlockSpec((None, 1, 4 * LORA), lambda i: (i, 0, AB_LORA // (4 * LORA))),
                  vec(3 * RW), vec(4 * LORA), full(2, RW), full(2, 4 * LORA, RW),
                  vec(RW), full(4 * LORA, RW), full(4 * LORA, RW), vec(RW), vec(RW), vec(RW),
                  full(RW, 128), full(128, RW)],
        out_specs=[row(RW), row(RW), row(RW), row(RW), row(RW),
                   pl.BlockSpec((2, tm, RW), lambda i: (0, i, 0)), row(RW), row(RW)],
        out_shape=[out_sd, out_sd, out_sd, out_sd, out_sd,
                   jax.ShapeDtypeStruct((2, rows, RW), F32), out_sd, out_sd],
        compiler_params=_params(("parallel",), V7X_VMEM_LIMIT),
        name="rwkv_prepare",
    )(u, u, up, un, up, un, mu_rkv, mu_l, w0, wup, a0.reshape(1, RW), aup, gup,
      k_k.reshape(1, RW), k_a.reshape(1, RW), r_k.reshape(1, RW), g_down, g_upm)
    return outs


def _chunk_block(d, b, s, nb, rows=CHUNK):
    nctx, nlat = C // rows, T // rows
    pos_c = jnp.where(d == 0, s, nctx - 1 - s)
    pos_l = jnp.where(d == 0, s - nctx, nlat - 1 - (s - nctx))
    return jnp.where(s < nctx, nb * nlat + b * nctx + pos_c, b * nlat + pos_l)


def _cast_specs(weights, step_of, n_steps):
    n_blocks = max(n for n in CAST_BLOCK_COUNTS if n <= n_steps)
    blk = lambda *idx: jnp.minimum(step_of(*idx), n_blocks - 1)
    in_specs, out_specs, out_shapes = [], [], []
    for w, layer in weights:
        rows, cols = w.shape[1] // n_blocks, w.shape[2]
        in_specs.append(pl.BlockSpec((None, rows, cols), lambda *idx, layer=layer: (layer, blk(*idx), 0)))
        out_specs.append(pl.BlockSpec((rows, cols), lambda *idx: (blk(*idx), 0)))
        out_shapes.append(jax.ShapeDtypeStruct(w.shape[1:], BF16))
    return in_specs, out_specs, out_shapes, [w for w, _ in weights]


def _cast_blocks(in_refs, out_refs):
    for w_ref, wo_ref in zip(in_refs, out_refs, strict=True):
        wo_ref[...] = w_ref[...].astype(BF16)


def _lane_stack(x, hw):
    head = lax.broadcasted_iota(jnp.int32, x.shape, 1) // hw
    return jnp.concatenate([jnp.where(head == h, x, 0.0) for h in range(GROUP // hw)], axis=0)


def _fold_rows(x, n):
    out = x[0:n]
    for h in range(1, x.shape[0] // n):
        out = out + x[h * n:(h + 1) * n]
    return out


def _block_masks(size, blk, d):
    row = lax.broadcasted_iota(jnp.int32, (size, size), 0)
    col = lax.broadcasted_iota(jnp.int32, (size, size), 1)
    rr, cc = row % blk, col % blk
    same = (row // blk) == (col // blk)
    if d == 0:
        return cc <= rr, cc < rr, same
    return cc >= rr, cc > rr, same


def _rwkv_scan_kernel(*refs):
    in_refs = (refs[0:6], refs[6:12])
    y_refs = refs[12:14]
    s_ref = refs[14]

    @pl.when(pl.program_id(1) == 0)
    def _():
        s_ref[...] = jnp.zeros_like(s_ref)

    n = CHUNK
    ngrp = RW // GROUP
    nch = 2 * ngrp
    rrow = lax.broadcasted_iota(jnp.int32, (n, GROUP), 0)
    rcol = lax.broadcasted_iota(jnp.int32, (n, GROUP), 1) % n
    eye_row = jnp.where(rrow == rcol, 1.0, 0.0)
    incl_row = (rcol <= rrow, rcol >= rrow)
    strict_row = (rcol < rrow, rcol > rrow)
    same = _block_masks(GROUP, n, 0)[2]

    def block_diag(x_row):
        xb = x_row.astype(BF16)
        return jnp.where(same, jnp.concatenate([xb] * (GROUP // n), axis=0), jnp.zeros((), BF16))

    def stack(x):
        return _lane_stack(x.astype(BF16), HD)

    pre = []
    for d in range(2):
        tri = jnp.where(_block_masks(n, n, d)[0], 1.0, 0.0).astype(BF16)
        order = range(SCAN_CHUNKS) if d == 0 else range(SCAN_CHUNKS - 1, -1, -1)
        pre_d = []
        for ci in order:
            rows = slice(ci * n, (ci + 1) * n)
            r, k, v, a, b, lw = [ref[rows, :] for ref in in_refs[d]]
            c = _tri_dot(tri, lw)
            c_last = c[n - 1:n] if d == 0 else c[0:1]
            p_inv = jnp.exp(-c)
            p_end = jnp.exp(c_last - c)
            pre_d.append(dict(rows=rows, at=a * jnp.exp(c - lw), rt=r * jnp.exp(c), bt=b * p_inv, kt=k * p_inv,
                              bp=b * p_end, kp=k * p_end, v=v, dec=jnp.exp(c_last)))
        pre.append(pre_d)
    chains = [(d, p, g) for p in range(SCAN_CHUNKS) for d in range(2) for g in range(ngrp)]
    nch = len(chains)
    sl = lambda g: slice(g * GROUP, (g + 1) * GROUP)
    part = lambda d, p, g, name: pre[d][p][name][:, sl(g)]

    a_st = [stack(part(d, p, g, "at")) for d, p, g in chains]
    v_st = [stack(part(d, p, g, "v")) for d, p, g in chains]
    gram = [_dot_nt(jnp.concatenate([part(d, p, g, "at"), part(d, p, g, "rt")], axis=0).astype(BF16),
                    jnp.concatenate([stack(part(d, p, g, "bt")), stack(part(d, p, g, "kt"))], axis=0))
            for d, p, g in chains]
    a_ab = [jnp.where(strict_row[d], gram[i][0:n, 0:GROUP], 0.0) for i, (d, p, g) in enumerate(chains)]
    a_ak = [jnp.where(strict_row[d], gram[i][0:n, GROUP:], 0.0).astype(BF16) for i, (d, p, g) in enumerate(chains)]
    a_r = [jnp.where(jnp.concatenate([incl_row[d]] * 2, axis=1), gram[i][n:2 * n, :], 0.0).astype(BF16)
           for i, (d, p, g) in enumerate(chains)]
    av = [_dot(a_ak[i], v_st[i]) for i in range(nch)]
    tm = [eye_row + m for m in a_ab]
    pw = [_dot(m.astype(BF16), block_diag(m)) for m in a_ab]
    for _ in range(4):
        both = [_dot(jnp.concatenate([t, p], axis=0).astype(BF16), block_diag(p)) for t, p in zip(tm, pw)]
        tm = [t + x[0:n] for t, x in zip(tm, both)]
        pw = [x[n:2 * n] for x in both]
    tm = [t + _dot(t.astype(BF16), block_diag(p)) for t, p in zip(tm, pw)]
    tx = [_dot(tm[i].astype(BF16), jnp.concatenate([a_st[i], stack(av[i])], axis=1))
          for i in range(nch)]

    state = {(d, g): s_ref[d, g] for d in range(2) for g in range(ngrp)}
    for p in range(SCAN_CHUNKS):
        ids = [i for i, ch in enumerate(chains) if ch[1] == p]
        uy0 = {i: _dot_nt(jnp.concatenate([tx[i][:, 0:GROUP], part(*chains[i], "rt")], axis=0).astype(BF16),
                          state[chains[i][0], chains[i][2]].astype(BF16)) for i in ids}
        u = {i: uy0[i][0:n] + tx[i][:, GROUP:] for i in ids}
        for i in ids:
            d, _, g = chains[i]
            y_refs[d][pre[d][p]["rows"], sl(g)] = (
                uy0[i][n:2 * n] + _dot(a_r[i], jnp.concatenate([stack(u[i]), v_st[i]], axis=0)))
        for i in ids:
            d, _, g = chains[i]
            upd = _dot_tn(jnp.concatenate([u[i], part(d, p, g, "v")], axis=0).astype(BF16),
                          jnp.concatenate([part(d, p, g, "bp"), part(d, p, g, "kp")], axis=0).astype(BF16))
            state[d, g] = state[d, g] * part(d, p, g, "dec") + jnp.where(same, upd, 0.0)
    for (d, g), s_new in state.items():
        s_ref[d, g] = s_new


def _rwkv_scan(r, k, v, a, b, lw, nb):
    rows = r.shape[0]
    blk_rows = SCAN_CHUNKS * CHUNK
    steps = (T + C) // blk_rows
    in_specs, args = [], []
    for d in range(2):
        blk = lambda bb, s, d=d: (_chunk_block(d, bb, s, nb, blk_rows), 0)
        in_specs += [pl.BlockSpec((blk_rows, RW), blk)] * 5
        in_specs.append(pl.BlockSpec((None, blk_rows, RW),
                                     lambda bb, s, d=d: (d, _chunk_block(d, bb, s, nb, blk_rows), 0)))
        args += [r, k, v, a, b, lw]
    out_sd = jax.ShapeDtypeStruct((rows, RW), F32)
    return pl.pallas_call(
        _rwkv_scan_kernel,
        grid=(nb, steps),
        in_specs=in_specs,
        out_specs=[pl.BlockSpec((blk_rows, RW), lambda bb, s, d=d: (_chunk_block(d, bb, s, nb, blk_rows), 0))
                   for d in range(2)],
        out_shape=[out_sd, out_sd],
        scratch_shapes=[pltpu.VMEM((2, RW // GROUP, GROUP, GROUP), F32)],
        compiler_params=_params(("parallel", "arbitrary"), V7X_VMEM_LIMIT),
        name="rwkv_scan",
    )(*args)


def _ab_mix(y0_ref, y1_ref, bonus_ref, g_ref, lnw_ref, lnb_ref, gd_ref, gu_ref, *yb_refs, pick, tile):
    y = y0_ref[...] + y1_ref[...]
    mu = _seg_sum(y, gd_ref[...], gu_ref[...]) * (1.0 / HD)
    yc = y - mu
    var = _seg_sum(yc * yc, gd_ref[...], gu_ref[...]) * (1.0 / HD)
    yn = yc * lax.rsqrt(var + RWKV_GN_EPS)
    return (yn * lnw_ref[...] + lnb_ref[...] + bonus_ref[...]) * g_ref[...], pick(tile, *yb_refs)


def _ab_mix_inputs(tile, y0, y1, bonus, g, ln_w, ln_b, yb_lat, yb_ctx):
    tm = OPROJ_TM
    head = np.arange(RW) // HD
    g_down = jnp.asarray(head[:, None] == np.arange(128)[None, :], BF16)
    g_upm = jnp.asarray(np.arange(128)[:, None] == head[None, :], BF16)
    row = pl.BlockSpec((tm, RW), lambda i: (tile(i), 0))
    vec = pl.BlockSpec((1, RW), lambda i: (0, 0))
    yb_args, yb_specs, pick = _split_rows(yb_lat, yb_ctx, tm, 1024, tile=tile)
    specs = [row, row, row, row, vec, vec,
             pl.BlockSpec((RW, 128), lambda i: (0, 0)), pl.BlockSpec((128, RW), lambda i: (0, 0))] + yb_specs
    args = (y0, y1, bonus, g, ln_w.reshape(1, RW), ln_b.reshape(1, RW), g_down, g_upm) + yb_args
    return functools.partial(_ab_mix, pick=pick), args, specs


def _kv_block(i, nb):
    per = T // 256
    return jnp.where(i < nb * per, i // per, i - nb * per), jnp.where(i < nb * per, i % per, per)


def _swa_prep_kernel(k_ref, v_ref, cos_ref, sin_ref, ko_ref, vo_ref):
    cosl, sinl = cos_ref[...], sin_ref[...]
    lane = lax.broadcasted_iota(jnp.int32, cosl.shape, 1)
    low = lane < HD
    for src, dst, rope in ((k_ref, ko_ref, True), (v_ref, vo_ref, False)):
        for t in range(2):
            x = src[:, t * 128:(t + 1) * 128]
            if rope:
                x = _rope(x, cosl, sinl)
            swapped = pltpu.roll(x, HD, axis=1)
            dst[:, (2 * t) * 128:(2 * t + 1) * 128] = jnp.where(low, x, swapped).astype(BF16)
            dst[:, (2 * t + 1) * 128:(2 * t + 2) * 128] = jnp.where(low, swapped, x).astype(BF16)


def _swa_prep(u, cosl, sinl, nb):
    rows = u.shape[0]
    tm = 256
    kv_out = pl.BlockSpec((None, tm, 512), lambda i: (*_kv_block(i, nb), 0))
    sd = jax.ShapeDtypeStruct((nb, T + C, 512), BF16)
    return pl.pallas_call(
        _swa_prep_kernel,
        grid=(rows // tm,),
        in_specs=[pl.BlockSpec((tm, 256), lambda i: (i, AB_KS // 256)),
                  pl.BlockSpec((tm, 256), lambda i: (i, AB_VS // 256)),
                  pl.BlockSpec((tm, 128), lambda i: (i, 0)),
                  pl.BlockSpec((tm, 128), lambda i: (i, 0))],
        out_specs=[kv_out, kv_out],
        out_shape=[sd, sd],
        compiler_params=_params(("parallel",), V7X_VMEM_LIMIT),
        name="swa_prep",
    )(u, u, cosl, sinl)


def _diff_prep_kernel(k_ref, v_ref, cos_ref, sin_ref, ko_ref, vo_ref):
    cosl, sinl = cos_ref[...], sin_ref[...]
    for t in range(8):
        sl = slice(t * 128, (t + 1) * 128)
        ko_ref[:, sl] = _rope(k_ref[:, sl], cosl, sinl).astype(BF16)
    vo_ref[...] = v_ref[...].astype(BF16)


def _diff_prep(u, cosl, sinl, nb):
    rows = u.shape[0]
    tm = 256
    kv_out = pl.BlockSpec((None, tm, 1024), lambda i: (*_kv_block(i, nb), 0))
    sd = jax.ShapeDtypeStruct((nb, T + C, 1024), BF16)
    return pl.pallas_call(
        _diff_prep_kernel,
        grid=(rows // tm,),
        in_specs=[pl.BlockSpec((tm, 1024), lambda i: (i, 1)),
                  pl.BlockSpec((tm, 1024), lambda i: (i, 2)),
                  pl.BlockSpec((tm, 128), lambda i: (i, 0)),
                  pl.BlockSpec((tm, 128), lambda i: (i, 0))],
        out_specs=[kv_out, kv_out],
        out_shape=[sd, sd],
        compiler_params=_params(("parallel",), V7X_VMEM_LIMIT),
        name="diff_prep",
    )(u, u, cosl, sinl)


def _swa_kernel(q_ref, cos_ref, sin_ref, k_ref, v_ref, sink_ref, *rest):
    n_cast = len(rest) // 2
    o_ref = rest[n_cast]
    ctx_mode = n_cast == 0
    _cast_blocks(rest[:n_cast], rest[n_cast + 1:])
    n = pl.program_id(1)
    cosl, sinl = cos_ref[...], sin_ref[...]
    QB = q_ref.shape[0]
    span = QB + 2 * SWA_WINDOW
    if not ctx_mode:
        start = pl.multiple_of(jnp.clip(n * QB - SWA_WINDOW, 0, T - span), SWA_WINDOW)
        kpos = start + lax.broadcasted_iota(jnp.int32, (2 * QB, span), 1)
        qpos = n * QB + lax.broadcasted_iota(jnp.int32, (2 * QB, span), 0) % QB
        valid = jnp.abs(kpos - qpos) <= SWA_WINDOW
    low = lax.broadcasted_iota(jnp.int32, (QB, 128), 1) < HD
    top = lax.broadcasted_iota(jnp.int32, (2 * QB, 1), 0) < QB
    tile = lambda j: slice((j // 2) * 128, (j // 2 + 1) * 128)
    qs, sink = [], []
    for j in range(8):
        q = _rope(q_ref[:, j * 128:(j + 1) * 128], cosl, sinl) * (HD ** -0.5 * LOG2E)
        qs.append(jnp.concatenate([jnp.where(low, q, 0.0), jnp.where(low, 0.0, q)], axis=0).astype(BF16))
        sink.append(jnp.where(top, sink_ref[2 * j:2 * j + 1, 0:1], sink_ref[2 * j + 1:2 * j + 2, 0:1]) * LOG2E)
    s_c = [_dot_nt(qs[j], k_ref[T:T + C, tile(j)]) for j in range(8)]
    m = [jnp.maximum(jnp.max(s, axis=-1, keepdims=True), sk) for s, sk in zip(s_c, sink)]
    if not ctx_mode:
        s_w = [jnp.where(valid, _dot_nt(qs[j], k_ref[pl.ds(start, span), tile(j)]), -jnp.inf) for j in range(8)]
        m = [jnp.maximum(mm, jnp.max(s, axis=-1, keepdims=True)) for mm, s in zip(m, s_w)]
    p_c = [jnp.exp2(s - mm) for s, mm in zip(s_c, m)]
    den = [jnp.sum(p, axis=-1, keepdims=True) + jnp.exp2(sk - mm) for p, sk, mm in zip(p_c, sink, m)]
    pv = [_dot(p.astype(BF16), v_ref[T:T + C, tile(j)]) for j, p in enumerate(p_c)]
    if not ctx_mode:
        p_w = [jnp.exp2(s - mm) for s, mm in zip(s_w, m)]
        den = [dd + jnp.sum(p, axis=-1, keepdims=True) for dd, p in zip(den, p_w)]
        pv = [x + _dot(p.astype(BF16), v_ref[pl.ds(start, span), tile(j)]) for j, (x, p) in enumerate(zip(pv, p_w))]
    for j in range(8):
        o = pv[j] / den[j]
        o_ref[:, j * 128:(j + 1) * 128] = jnp.where(low, o[0:QB], o[QB:2 * QB])


def _swa(u, cosl, sinl, kpad, vpad, sink, nb, ctx_mode, weights=()):
    assert ctx_mode == (not weights)
    QB = SWA_QB
    nq = (C if ctx_mode else T) // QB
    base = nb * (T // QB) if ctx_mode else 0
    rows = nb * nq * QB
    rowblk = lambda b, n: base + b * nq + n
    kv = pl.BlockSpec((None, T + C, 512), lambda b, n: (b, 0, 0))
    in_specs = [pl.BlockSpec((QB, 1024), lambda b, n: (rowblk(b, n), AB_Q // 1024)),
                pl.BlockSpec((QB, 128), lambda b, n: (rowblk(b, n), 0)),
                pl.BlockSpec((QB, 128), lambda b, n: (rowblk(b, n), 0)),
                kv, kv, pl.BlockSpec((16, 128), lambda b, n: (0, 0))]
    out_specs = [pl.BlockSpec((QB, 1024), lambda b, n: (b * nq + n, 0))]
    out_shape = [jax.ShapeDtypeStruct((rows, 1024), F32)]
    args = [u, cosl, sinl, kpad, vpad, sink]
    if weights:
        cast_in, cast_out, cast_shape, cast_args = _cast_specs(weights, lambda b, n: b * nq + n, nb * nq)
        in_specs += cast_in
        out_specs += cast_out
        out_shape += cast_shape
        args += cast_args
    return pl.pallas_call(
        _swa_kernel,
        grid=(nb, nq),
        in_specs=in_specs,
        out_specs=out_specs,
        out_shape=out_shape,
        compiler_params=_params(("arbitrary", "arbitrary"), V7X_VMEM_LIMIT),
        name="swa_ctx" if ctx_mode else "swa_latent",
    )(*args)


def _diff_kernel(q_ref, cos_ref, sin_ref, k_ref, v_ref, lam_ref, sub_ref, o_ref):
    cosl, sinl = cos_ref[...], sin_ref[...]
    lam = lam_ref[...]
    nq = q_ref.shape[0]
    low = lax.broadcasted_iota(jnp.int32, (nq, 128), 1) < HD
    tile = lambda h: slice(h * 128, (h + 1) * 128)

    def logits(h):
        q = _rope(q_ref[:, tile(h)], cosl, sinl) * (HD ** -0.5 * LOG2E)
        qs = jnp.concatenate([jnp.where(low, q, 0.0), jnp.where(low, 0.0, q)], axis=0).astype(BF16)
        return _dot_nt(qs, k_ref[:, tile(h)])

    s_next = logits(0)
    for h in range(8):
        s = s_next
        if h + 1 < 8:
            s_next = logits(h + 1)
        e = jnp.exp2(s - jnp.max(s, axis=-1, keepdims=True))
        inv = 1.0 / jnp.sum(e, axis=-1, keepdims=True)
        o2 = _dot(e.astype(BF16), v_ref[:, tile(h)])
        o = o2[0:nq] * inv[0:nq] - o2[nq:2 * nq] * (lam[:, 0:1] * inv[nq:2 * nq])
        o = o * lax.rsqrt(jnp.mean(o * o, axis=-1, keepdims=True) + 1e-5)
        o_ref[:, tile(h)] = o * sub_ref[...]


def _diff_attn(u, cosl, sinl, kb, vb, lam, sub, nb):
    QB = DIFF_QB
    nq = T // QB
    kv = pl.BlockSpec((None, T + C, 1024), lambda b, n: (b, 0, 0))
    vec = pl.BlockSpec((1, 128), lambda b, n: (0, 0))
    return pl.pallas_call(
        _diff_kernel,
        grid=(nb, nq),
        in_specs=[pl.BlockSpec((QB, 1024), lambda b, n: (b * nq + n, 0)),
                  pl.BlockSpec((QB, 128), lambda b, n: (b * nq + n, 0)),
                  pl.BlockSpec((QB, 128), lambda b, n: (b * nq + n, 0)),
                  kv, kv, vec, vec],
        out_specs=pl.BlockSpec((QB, 1024), lambda b, n: (b * nq + n, 0)),
        out_shape=jax.ShapeDtypeStruct((nb * T, 1024), F32),
        compiler_params=_params(("parallel", "arbitrary"), V7X_VMEM_LIMIT),
        name="diff_attn",
    )(u, cosl, sinl, kb, vb, lam, sub)


def _hgrn_exact_att(q, kk, bcum, tmp_ref):
    n = CHUNK
    tmp_ref[0] = bcum
    tmp_ref[1] = kk
    coli = lax.broadcasted_iota(jnp.int32, (n, 2 * n), 1)

    def body(s, acc):
        bs = tmp_ref[0, pl.ds(s, 1), :]
        ks = tmp_ref[1, pl.ds(s, 1), :]
        w = q * jnp.exp(jnp.minimum(bcum - bs, 0.0)) * ks
        c0 = jnp.sum(w[:, 0:HG_DK], axis=-1, keepdims=True)
        c1 = jnp.sum(w[:, HG_DK:], axis=-1, keepdims=True)
        return acc + jnp.concatenate([jnp.where(coli == s, c0, 0.0), jnp.where(coli == s + n, c1, 0.0)], axis=0)

    return lax.fori_loop(0, n, body, jnp.zeros((2 * n, 2 * n), F32))


def _hgrn_scan_kernel(*refs):
    n_cast = (len(refs) - 13) // 2
    in_refs = (refs[0:4], refs[4:8])
    o_refs = refs[8 + n_cast:10 + n_cast]
    s_ref, g_ref, tmp_ref = refs[10 + 2 * n_cast:]
    _cast_blocks(refs[8:8 + n_cast], refs[10 + n_cast:10 + 2 * n_cast])

    @pl.when(pl.program_id(1) == 0)
    def _():
        s_ref[...] = jnp.zeros_like(s_ref)

    n = CHUNK
    ngrp = 1024 // GROUP
    nsub = n // SUB
    pre = []
    for d in range(2):
        tri = jnp.where(_block_masks(n, n, d)[0], 1.0, 0.0).astype(BF16)
        lb = in_refs[d][3][...]
        order = range(SCAN_CHUNKS) if d == 0 else range(SCAN_CHUNKS - 1, -1, -1)
        pre_d = []
        for ci in order:
            rows = slice(ci * n, (ci + 1) * n)
            q, z, v = [ref[rows, :] for ref in in_refs[d][0:3]]
            logf = jnp.log(lb + (1.0 - lb) * _sigmoid(z))
            kk = (1.0 - lb) * _sigmoid(-z)
            bcum = _tri_dot(tri, logf)
            b_last = bcum[n - 1:n] if d == 0 else bcum[0:1]
            b_excl = bcum - logf
            qh, kh = [], []
            for sb in range(nsub):
                lo, hi = sb * SUB, (sb + 1) * SUB
                beta = b_excl[lo:lo + 1] if d == 0 else b_excl[hi - 1:hi]
                qh.append(q[lo:hi] * jnp.exp(bcum[lo:hi] - beta))
                kh.append(kk * jnp.exp(beta - bcum))
            pre_d.append(dict(rows=rows, q=q, v=v, kk=kk, bcum=bcum, qh=qh, kh=kh, qe=q * jnp.exp(bcum),
                              ke=kk * jnp.exp(b_last - bcum), dec=jnp.exp(b_last), min_logf=jnp.min(logf)))
        pre.append(pre_d)
    chains = [(d, p, g) for p in range(SCAN_CHUNKS) for d in range(2) for g in range(ngrp)]
    sl = lambda g: slice(g * GROUP, (g + 1) * GROUP)

    for i, (d, p, g) in enumerate(chains):
        c = pre[d][p]
        rows = [_dot_nt(_lane_stack(c["qh"][sb][:, sl(g)].astype(BF16), HG_DK),
                        _lane_stack(c["kh"][sb][:, sl(g)].astype(BF16), HG_DK)) for sb in range(nsub)]
        g_ref[i] = jnp.concatenate([rows[sb][h * SUB:(h + 1) * SUB] for h in range(2) for sb in range(nsub)], axis=0)

    min_logf = functools.reduce(jnp.minimum, [c["min_logf"] for pre_d in pre for c in pre_d])

    @pl.when(min_logf < -(HG_CLAMP / SUB))
    def _():
        for i, (d, p, g) in enumerate(chains):
            c = pre[d][p]
            g_ref[i] = _hgrn_exact_att(c["q"][:, sl(g)], c["kk"][:, sl(g)], c["bcum"][:, sl(g)], tmp_ref)

    state = {(d, g): s_ref[d, g] for d in range(2) for g in range(ngrp)}
    head_same = _block_masks(GROUP, HG_DK, 0)[2]
    for p in range(SCAN_CHUNKS):
        ids = [i for i, ch in enumerate(chains) if ch[1] == p]
        inter = {i: _dot_nt(pre[chains[i][0]][p]["qe"][:, sl(chains[i][2])].astype(BF16),
                            state[chains[i][0], chains[i][2]].astype(BF16)) for i in ids}
        for i in ids:
            d, _, g = chains[i]
            incl, _, same = _block_masks(2 * n, n, d)
            att = jnp.where(incl & same, g_ref[i], 0.0).astype(BF16)
            v_st = _lane_stack(pre[d][p]["v"][:, sl(g)].astype(BF16), HG_DK)
            o_refs[d][pre[d][p]["rows"], sl(g)] = inter[i] + _fold_rows(_dot(att, v_st), n)
        for i in ids:
            d, _, g = chains[i]
            upd = _dot_tn(pre[d][p]["v"][:, sl(g)].astype(BF16), pre[d][p]["ke"][:, sl(g)].astype(BF16))
            state[d, g] = state[d, g] * pre[d][p]["dec"][:, sl(g)] + jnp.where(head_same, upd, 0.0)
    for (d, g), s_new in state.items():
        s_ref[d, g] = s_new


def _hgrn_scan(u, lb, nb, weights):
    rows = u.shape[0]
    blk_rows = SCAN_CHUNKS * CHUNK
    steps = (T + C) // blk_rows
    cast_in, cast_out, cast_shape, cast_args = _cast_specs(weights, lambda bb, s: bb * steps + s, nb * steps)
    in_specs, args = [], []
    for d in range(2):
        col = lambda j, d=d: pl.BlockSpec((blk_rows, 1024),
                                          lambda bb, s: (_chunk_block(d, bb, s, nb, blk_rows), j))
        in_specs += [col(3), col(4 + d), col(6), pl.BlockSpec((None, 1, 1024), lambda bb, s, d=d: (d, 0, 0))]
        args += [u, u, u, lb]
    out_sd = jax.ShapeDtypeStruct((rows, 1024), F32)
    ngrp = 1024 // GROUP
    return pl.pallas_call(
        _hgrn_scan_kernel,
        grid=(nb, steps),
        in_specs=in_specs + cast_in,
        out_specs=[pl.BlockSpec((blk_rows, 1024), lambda bb, s, d=d: (_chunk_block(d, bb, s, nb, blk_rows), 0))
                   for d in range(2)] + cast_out,
        out_shape=[out_sd, out_sd] + cast_shape,
        scratch_shapes=[pltpu.VMEM((2, ngrp, GROUP, GROUP), F32),
                        pltpu.VMEM((2 * ngrp * SCAN_CHUNKS, 2 * CHUNK, 2 * CHUNK), F32),
                        pltpu.VMEM((2, CHUNK, GROUP), F32)],
        compiler_params=_params(("arbitrary", "arbitrary"), V7X_VMEM_LIMIT),
        name="hgrn_scan",
    )(*args, *cast_args)


def _cd_mix(yc_ref, o0_ref, o1_ref, g_ref, gn_ref, *, tile):
    g = g_ref[...]
    heads = []
    for h in range(8):
        sl = slice(h * 128, (h + 1) * 128)
        o = o0_ref[:, sl] + o1_ref[:, sl]
        o = o * lax.rsqrt(jnp.mean(o * o, axis=-1, keepdims=True) + NORM_EPS) * gn_ref[...]
        gh = g[:, sl]
        heads.append(o * (gh * _sigmoid(gh)))
    return yc_ref[...], jnp.concatenate(heads, axis=1)


def _cd_mix_inputs(tile, yc, o0, o1, u, gnorm):
    tm = OPROJ_TM
    row = pl.BlockSpec((tm, 1024), lambda i: (tile(i), 0))
    specs = [row, row, row, pl.BlockSpec((tm, 1024), lambda i: (tile(i), 7)),
             pl.BlockSpec((1, 128), lambda i: (0, 0))]
    return _cd_mix, (yc, o0, o1, u, gnorm.reshape(1, 128)), specs


def _rope_tables(nb):
    t = np.arange(T)
    quarter = HD // 4
    inv = ROPE_THETA ** (-jnp.arange(quarter, dtype=F32) / quarter)
    rows = jnp.asarray(t // GRID_W, F32)
    cols = jnp.asarray(t % GRID_W, F32)
    ang = jnp.concatenate([rows[:, None] * inv, cols[:, None] * inv], axis=-1)
    cos, sin = jnp.cos(ang), jnp.sin(ang)
    cosl = jnp.tile(jnp.concatenate([cos, cos], axis=-1), (nb, 2))
    sinl = jnp.tile(jnp.concatenate([-sin, sin], axis=-1), (nb, 2))
    cosl = jnp.concatenate([cosl, jnp.ones((nb * C, 128), F32)], axis=0)
    sinl = jnp.concatenate([sinl, jnp.zeros((nb * C, 128), F32)], axis=0)
    return cosl, sinl


def kernel(x, c, ctx, c_ctx, ada_w, ada_b, mlp_w1, mlp_w2, final_norm, ab_w_in, ab_w_out, ab_mu, ab_w0,
           ab_w_up, ab_a0, ab_a_up, ab_g_up, ab_k_k, ab_k_a, ab_r_k, ab_ln_w, ab_ln_b, ab_sink,
           cd_w_in, cd_w_out, cd_lam, cd_subln, cd_gnorm, hgrn_lb_logits):
    nb = x.shape[0]
    assert x.shape == (nb, T, D) and ctx.shape == (nb, C, D) and nb < 8
    assert ada_w.shape[0] == 2, "one AB layer followed by one CD layer"
    rx = nb * T
    rows = rx + nb * C
    x2, c2 = x.reshape(rx, D), ctx.reshape(nb * C, D)
    cosl, sinl = _rope_tables(nb)

    s_in = jnp.zeros((8, D), F32).at[:nb].set(c).at[nb].set(c_ctx)
    mod = _ada(s_in, ada_w, ada_b)

    w = ab_w_in[0]
    w_in0 = jnp.concatenate([w[:, :3 * RW], w[:, 3 * RW + 3 * LORA:], w[:, 3 * RW:3 * RW + 3 * LORA],
                             jnp.zeros((D, LORA), F32)], axis=1).astype(BF16)
    u = _norm_mod_matmul(x2, c2, mod, 0, w_in0, nb, 512, AB_N // 2)
    r, k, v, a, b, lw, g, bonus = _rwkv_prepare(u, nb, ab_mu[0], ab_w0[0], ab_w_up[0], ab_a0[0], ab_a_up[0],
                                                ab_g_up[0], ab_k_k[0], ab_k_a[0], ab_r_k[0].reshape(RW))
    y0, y1 = _rwkv_scan(r, k, v, a, b, lw, nb)
    kpad, vpad = _swa_prep(u, cosl, sinl, nb)
    sink = jnp.broadcast_to(ab_sink[0][:, None], (16, 128))
    yb_lat, w1b, w2b, w_in1, w_out0 = _swa(u, cosl, sinl, kpad, vpad, sink, nb, False,
                                           [(mlp_w1, 0), (mlp_w2, 0), (cd_w_in, 0), (ab_w_out, 0)])
    (yb_ctx,) = _swa(u, cosl, sinl, kpad, vpad, sink, nb, True)
    mix_inputs = functools.partial(_ab_mix_inputs, y0=y0, y1=y1, bonus=bonus, g=g, ln_w=ab_ln_w[0],
                                   ln_b=ab_ln_b[0], yb_lat=yb_lat, yb_ctx=yb_ctx)
    xc = _out_proj(mix_inputs, x2, c2, rows, mod, 0, w_out0, nb)
    fn = final_norm.reshape(1, D)
    xc = _mlp(xc, rows, mod, 0, w1b, w2b, fn, nb, False)

    lam_init = 0.8 - 0.6 * math.exp(-0.3 * 1)
    lb_table = jnp.cumsum(jax.nn.softmax(hgrn_lb_logits.astype(F32), axis=0), axis=0)
    lb = (lb_table - lb_table[0])[1].reshape(2, 1, 1024)
    lf = cd_lam[0].astype(F32)
    lmb = jnp.exp(jnp.sum(lf[0] * lf[1])) - jnp.exp(jnp.sum(lf[2] * lf[3])) + lam_init
    u = _norm_mod_matmul(xc, None, mod, 1, w_in1, nb, 512, 2048)
    kb, vb = _diff_prep(u, cosl, sinl, nb)
    yc = _diff_attn(u, cosl, sinl, kb, vb, jnp.full((1, 128), lmb, F32),
                    (cd_subln[0] * (1.0 - lam_init)).reshape(1, 128), nb)
    o0, o1, w1b, w2b, w_out1 = _hgrn_scan(u, lb, nb, [(mlp_w1, 1), (mlp_w2, 1), (cd_w_out, 0)])
    mix_inputs = functools.partial(_cd_mix_inputs, yc=yc, o0=o0, o1=o1, u=u, gnorm=cd_gnorm[0])
    xl = _out_proj(mix_inputs, xc, None, rx, mod, 1, w_out1, nb)
    out = _mlp(xl, rx, mod, 1, w1b, w2b, fn, nb, True)
    return out.reshape(nb, T, D)
```

```python
import functools
import math

import jax
import jax.numpy as jnp
import numpy as np
from jax import lax
from jax.experimental import pallas as pl
from jax.experimental.pallas import tpu as pltpu

F32 = jnp.float32
BF16 = jnp.bfloat16

D = 2048
T = 2048
C = 256
GRID_W = 64
D_FF = 4 * D
HD = 64
ROPE_THETA = 10000.0
LOG2E = 1.4426950408889634
NORM_EPS = 1e-6
N_MOD = 6
RW = 1024
RWKV_GN_EPS = 64e-5
LORA = 64
SWA_WINDOW = 128
SWA_QB = 128
DIFF_QB = 256
OPROJ_TM = 512
CHUNK = 64
SCAN_CHUNKS = 4
CAST_BLOCK_COUNTS = (32, 64)
SUB = 16
HG_DK = 128
HG_CLAMP = 80.0
GROUP = 256
V7X_VMEM_BYTES = 64 * 1024 * 1024
V7X_VMEM_LIMIT = V7X_VMEM_BYTES - 8 * 1024 * 1024

AB_Q, AB_KS, AB_VS, AB_LORA, AB_N = 3072, 4096, 4352, 4608, 4864


def _dot(a, b):
    return jnp.dot(a, b, preferred_element_type=F32)


def _dot_nt(a, b):
    return lax.dot_general(a, b, (((1,), (1,)), ((), ())), preferred_element_type=F32)


def _dot_tn(a, b):
    return lax.dot_general(a, b, (((0,), (0,)), ((), ())), preferred_element_type=F32)


def _split2(x):
    hi = x.astype(BF16)
    lo = (x - hi.astype(F32)).astype(BF16)
    return hi, lo


def _split3(x):
    hi = x.astype(BF16)
    r1 = x - hi.astype(F32)
    mid = r1.astype(BF16)
    lo = (r1 - mid.astype(F32)).astype(BF16)
    return hi, mid, lo


def _dot_exact_rhs(x, g):
    hi, lo = _split2(x)
    return _dot(hi, g) + _dot(lo, g)


def _tri_dot(tri, x):
    hi, mid, lo = _split3(x)
    return _dot(tri, hi) + _dot(tri, mid) + _dot(tri, lo)


def _seg_sum(x, g_down, g_up):
    return _dot_exact_rhs(_dot_exact_rhs(x, g_down), g_up)


def _rope(x, cosl, sinl):
    w = x.shape[-1]
    lane = lax.broadcasted_iota(jnp.int32, x.shape, x.ndim - 1)
    first = (lane & 63) < 32
    swapped = jnp.where(first, pltpu.roll(x, w - 32, axis=1), pltpu.roll(x, 32, axis=1))
    return x * cosl + swapped * sinl


def _sigmoid(x):
    return 1.0 / (1.0 + jnp.exp(-x))


def _softplus(x):
    return jnp.maximum(x, 0.0) + jnp.log(1.0 + jnp.exp(-jnp.abs(x)))


def _params(sem, vmem=None):
    return pltpu.CompilerParams(dimension_semantics=sem, vmem_limit_bytes=vmem)


def _ada_kernel(s_ref, w_ref, b_ref, o_ref):
    s = s_ref[...]
    s = s * _sigmoid(s)
    o_ref[...] = _dot(s.astype(BF16), w_ref[...].astype(BF16)) + b_ref[...]


def _ada(s_in, ada_w, ada_b):
    depth = ada_w.shape[0]
    n = ada_w.shape[2]
    tn = 1536
    out = pl.pallas_call(
        _ada_kernel,
        grid=(depth, n // tn),
        in_specs=[pl.BlockSpec((8, D), lambda l, j: (0, 0)),
                  pl.BlockSpec((None, D, tn), lambda l, j: (l, 0, j)),
                  pl.BlockSpec((None, 1, tn), lambda l, j: (l, 0, j))],
        out_specs=pl.BlockSpec((None, 8, tn), lambda l, j: (l, 0, j)),
        out_shape=jax.ShapeDtypeStruct((depth, 8, n), F32),
        compiler_params=_params(("arbitrary", "arbitrary"), V7X_VMEM_LIMIT),
        name="ada_mod",
    )(s_in, ada_w, ada_b.reshape(depth, 1, n))
    return out.reshape(depth, 8, 1, n)


def _mod_spec(layer, k, tm, nb, tile=lambda i: i):
    return pl.BlockSpec((None, None, 1, D),
                        lambda i, *_: (layer, jnp.minimum((tile(i) * tm) // T, nb), 0, k))


def _split_rows(lat, ctx, tm, width, col=0, tile=lambda i: i):
    if ctx is None:
        return (lat,), [pl.BlockSpec((tm, width), lambda i, *_: (tile(i), col))], lambda t, ref: ref[...]
    n_lat = lat.shape[0] // tm
    specs = [pl.BlockSpec((tm, width), lambda i, *_: (jnp.minimum(tile(i), n_lat - 1), col)),
             pl.BlockSpec((tm, width), lambda i, *_: (jnp.maximum(tile(i) - n_lat, 0), col),
                          pipeline_mode=pl.Buffered(1))]
    return (lat, ctx), specs, lambda t, lat_ref, ctx_ref: jnp.where(t < n_lat, lat_ref[...], ctx_ref[...])


def _nmm_kernel(*refs, n_x, pick):
    sh_ref, sc_ref, w_ref, o_ref, lhs_ref = refs[n_x:]

    @pl.when(pl.program_id(1) == 0)
    def _():
        x = pick(pl.program_id(0), *refs[:n_x])
        xn = x * lax.rsqrt(jnp.mean(x * x, axis=-1, keepdims=True) + NORM_EPS)
        lhs_ref[...] = (xn * (1.0 + sc_ref[...]) + sh_ref[...]).astype(BF16)

    o_ref[...] = _dot(lhs_ref[...], w_ref[...].astype(BF16)).astype(o_ref.dtype)


def _norm_mod_matmul(x_lat, x_ctx, mod, layer, w, nb, tm, tn):
    rows = x_lat.shape[0] + (0 if x_ctx is None else x_ctx.shape[0])
    n = w.shape[1]
    x_args, x_specs, pick = _split_rows(x_lat, x_ctx, tm, D)
    if x_ctx is None and tm > 512:
        x_specs = [pl.BlockSpec((tm, D), lambda i, j: (i, 0), pipeline_mode=pl.Buffered(1))]
    return pl.pallas_call(
        functools.partial(_nmm_kernel, n_x=len(x_args), pick=pick),
        grid=(rows // tm, n // tn),
        in_specs=x_specs + [_mod_spec(layer, 0, tm, nb), _mod_spec(layer, 1, tm, nb),
                            pl.BlockSpec((D, tn), lambda i, j: (0, j))],
        out_specs=pl.BlockSpec((tm, tn), lambda i, j: (i, j)),
        out_shape=jax.ShapeDtypeStruct((rows, n), F32),
        scratch_shapes=[pltpu.VMEM((tm, D), BF16)],
        compiler_params=_params(("parallel", "arbitrary"), V7X_VMEM_LIMIT),
        name=f"in_proj_{layer}",
    )(*x_args, mod, mod, w)


def _mlp_kernel(x_ref, sh_ref, sc_ref, gt_ref, w1_ref, w2_ref, fn_ref, o_ref, lhs_ref, *, final):
    f = pl.program_id(1)

    @pl.when(f == 0)
    def _():
        x = x_ref[...]
        xn = x * lax.rsqrt(jnp.mean(x * x, axis=-1, keepdims=True) + NORM_EPS)
        lhs_ref[...] = (xn * (1.0 + sc_ref[...]) + sh_ref[...]).astype(BF16)
        o_ref[...] = jnp.zeros_like(o_ref)

    h = jnp.maximum(_dot(lhs_ref[...], w1_ref[...]), 0.0)
    o_ref[...] += _dot((h * h).astype(BF16), w2_ref[...])

    @pl.when(f == pl.num_programs(1) - 1)
    def _():
        y = x_ref[...] + gt_ref[...] * o_ref[...]
        if final:
            y = y * lax.rsqrt(jnp.mean(y * y, axis=-1, keepdims=True) + NORM_EPS) * fn_ref[...]
        o_ref[...] = y


def _mlp(xc, rows, mod, layer, w1, w2, final_norm, nb, final):
    tm, tf = 512, 2048
    return pl.pallas_call(
        functools.partial(_mlp_kernel, final=final),
        grid=(rows // tm, D_FF // tf),
        in_specs=[pl.BlockSpec((tm, D), lambda i, f: (i, 0), pipeline_mode=pl.Buffered(1)),
                  _mod_spec(layer, 3, tm, nb), _mod_spec(layer, 4, tm, nb), _mod_spec(layer, 5, tm, nb),
                  pl.BlockSpec((D, tf), lambda i, f: (0, f)),
                  pl.BlockSpec((tf, D), lambda i, f: (f, 0)),
                  pl.BlockSpec((1, D), lambda i, f: (0, 0))],
        out_specs=pl.BlockSpec((tm, D), lambda i, f: (i, 0)),
        out_shape=jax.ShapeDtypeStruct((rows, D), F32),
        scratch_shapes=[pltpu.VMEM((tm, D), BF16)],
        compiler_params=_params(("parallel", "arbitrary"), V7X_VMEM_LIMIT),
        name=f"mlp_{layer}",
    )(xc, mod, mod, mod, w1, w2, final_norm)


def _oproj_kernel(*refs, n_mix, mix, n_x, pick):
    x_refs = refs[n_mix:n_mix + n_x]
    gt_ref, wa_ref, wb_ref, o_ref = refs[n_mix + n_x:]
    i = pl.program_id(0)
    ya, yb = mix(*refs[:n_mix], tile=i)
    y = _dot(ya.astype(BF16), wa_ref[...]) + _dot(yb.astype(BF16), wb_ref[...])
    o_ref[...] = pick(i, *x_refs) + gt_ref[...] * y


def _out_proj(mix_inputs, x_lat, x_ctx, rows, mod, layer, w, nb):
    tm = OPROJ_TM
    half = w.shape[0] // 2
    mix, mix_args, mix_specs = mix_inputs(lambda i: i)
    x_args, x_specs, pick = _split_rows(x_lat, x_ctx, tm, D)
    return pl.pallas_call(
        functools.partial(_oproj_kernel, n_mix=len(mix_args), mix=mix, n_x=len(x_args), pick=pick),
        grid=(rows // tm,),
        in_specs=list(mix_specs) + x_specs + [
            _mod_spec(layer, 2, tm, nb),
            pl.BlockSpec((half, D), lambda i: (0, 0), pipeline_mode=pl.Buffered(1)),
            pl.BlockSpec((half, D), lambda i: (1, 0), pipeline_mode=pl.Buffered(1))],
        out_specs=pl.BlockSpec((tm, D), lambda i: (i, 0)),
        out_shape=jax.ShapeDtypeStruct((rows, D), F32),
        compiler_params=_params(("parallel",), V7X_VMEM_LIMIT),
        name=f"out_proj_{layer}",
    )(*mix_args, *x_args, mod, w, w)


def _shifted(u, prev_row, next_row):
    tm = u.shape[0]
    row = lax.broadcasted_iota(jnp.int32, u.shape, 0)
    up = jnp.where(row == 0, prev_row, pltpu.roll(u, 1, axis=0))
    un = jnp.where(row == tm - 1, next_row, pltpu.roll(u, tm - 1, axis=0))
    return 0.5 * (up + un)


def _rwkv_prep_kernel(u_ref, ul_ref, up_ref, un_ref, ulp_ref, uln_ref, mu_ref, mul_ref, w0_ref, wup_ref,
                      a0_ref, aup_ref, gup_ref, kk_ref, ka_ref, rk_ref, gd_ref, gu_ref,
                      r_out, k_out, v_out, a_out, b_out, lw_out, g_out, bonus_out):
    u = u_ref[...]
    u = u + mu_ref[...] * (_shifted(u, up_ref[...], un_ref[...]) - u)
    ul = ul_ref[...]
    ul = ul + mul_ref[...] * (_shifted(ul, ulp_ref[...], uln_ref[...]) - ul)
    r, k, v = u[:, 0:RW], u[:, RW:2 * RW], u[:, 2 * RW:3 * RW]

    th = jnp.tanh(ul).astype(BF16)
    for d in range(2):
        w_log = -_softplus(-(w0_ref[d:d + 1, :] + _dot(th, wup_ref[d]))) - 0.5
        lw_out[d] = -jnp.exp(w_log)
    a = _sigmoid(a0_ref[...] + _dot(ul.astype(BF16), aup_ref[...]))
    g_out[...] = _dot(_sigmoid(ul).astype(BF16), gup_ref[...])

    kk = k * kk_ref[...]
    nrm = jnp.sqrt(_seg_sum(kk * kk, gd_ref[...], gu_ref[...]))
    kk = kk / jnp.maximum(nrm, 1e-12)
    k = k * (1.0 + (a - 1.0) * ka_ref[...])
    r_out[...] = r
    k_out[...] = k
    v_out[...] = v
    a_out[...] = -kk
    b_out[...] = kk * a
    bonus_out[...] = _seg_sum(r * k * rk_ref[...], gd_ref[...], gu_ref[...]) * v


def _seq_halo(u, tm, nb):
    rows = u.shape[0]
    nblk = rows // tm
    starts = np.arange(nblk) * tm
    seq_len = np.where(starts < nb * T, T, C)
    seq_off = np.where(starts < nb * T, starts % T, (starts - nb * T) % C)
    has_prev = seq_off > 0
    has_next = seq_off + tm < seq_len
    prev_idx = np.where(has_prev, starts - 1, 0)
    next_idx = np.where(has_next, starts + tm, 0)
    up = jnp.where(has_prev[:, None], u[prev_idx], 0.0)
    un = jnp.where(has_next[:, None], u[next_idx], 0.0)
    return up[:, None, :], un[:, None, :]


def _rwkv_prepare(u, nb, mu, w0, w_up, a0, a_up, g_up, k_k, k_a, r_k):
    rows = u.shape[0]
    tm = 256
    nblk = rows // tm
    up, un = _seq_halo(u, tm, nb)
    pad = jnp.zeros((LORA,), F32)
    mu_rkv = mu[:3 * RW].reshape(1, 3 * RW)
    mu_l = jnp.concatenate([mu[3 * RW:], pad]).reshape(1, 4 * LORA)

    def lora_w(w, slot):
        z = jnp.zeros((4 * LORA, RW), F32)
        return z.at[slot * LORA:(slot + 1) * LORA].set(w).astype(BF16)

    wup = jnp.stack([lora_w(w_up[0], 0), lora_w(w_up[1], 0)])
    aup = lora_w(a_up, 1)
    gup = lora_w(g_up, 2)
    head = np.arange(RW) // HD
    g_down = jnp.asarray(head[:, None] == np.arange(128)[None, :], BF16)
    g_upm = jnp.asarray(np.arange(128)[:, None] == head[None, :], BF16)

    row = lambda w: pl.BlockSpec((tm, w), lambda i: (i, 0))
    vec = lambda w: pl.BlockSpec((1, w), lambda i: (0, 0))
    full = lambda *s: pl.BlockSpec(s, lambda i: (0,) * len(s))
    out_sd = jax.ShapeDtypeStruct((rows, RW), F32)
    outs = pl.pallas_call(
        _rwkv_prep_kernel,
        grid=(nblk,),
        in_specs=[pl.BlockSpec((tm, 3 * RW), lambda i: (i, 0)),
                  pl.BlockSpec((tm, 4 * LORA), lambda i: (i, AB_LORA // (4 * LORA))),
                  pl.BlockSpec((None, 1, 3 * RW), lambda i: (i, 0, 0)),
                  pl.BlockSpec((None, 1, 3 * RW), lambda i: (i, 0, 0)),
                  pl.BlockSpec((None, 1, 4 * LORA), lambda i: (i, 0, AB_LORA // (4 * LORA))),
                  pl.BlockSpec((None, 1, 4 * LORA), lambda i: (i, 0, AB_LORA // (4 * LORA))),
                  vec(3 * RW), vec(4 * LORA), full(2, RW), full(2, 4 * LORA, RW),
                  vec(RW), full(4 * LORA, RW), full(4 * LORA, RW), vec(RW), vec(RW), vec(RW),
                  full(RW, 128), full(128, RW)],
        out_specs=[row(RW), row(RW), row(RW), row(RW), row(RW),
                   pl.BlockSpec((2, tm, RW), lambda i: (0, i, 0)), row(RW), row(RW)],
        out_shape=[out_sd, out_sd, out_sd, out_sd, out_sd,
                   jax.ShapeDtypeStruct((2, rows, RW), F32), out_sd, out_sd],
        compiler_params=_params(("parallel",), V7X_VMEM_LIMIT),
        name="rwkv_prepare",
    )(u, u, up, un, up, un, mu_rkv, mu_l, w0, wup, a0.reshape(1, RW), aup, gup,
      k_k.reshape(1, RW), k_a.reshape(1, RW), r_k.reshape(1, RW), g_down, g_upm)
    return outs


def _chunk_block(d, b, s, nb, rows=CHUNK):
    nctx, nlat = C // rows, T // rows
    pos_c = jnp.where(d == 0, s, nctx - 1 - s)
    pos_l = jnp.where(d == 0, s - nctx, nlat - 1 - (s - nctx))
    return jnp.where(s < nctx, nb * nlat + b * nctx + pos_c, b * nlat + pos_l)


def _cast_specs(weights, step_of, n_steps):
    n_blocks = max(n for n in CAST_BLOCK_COUNTS if n <= n_steps)
    blk = lambda *idx: jnp.minimum(step_of(*idx), n_blocks - 1)
    in_specs, out_specs, out_shapes = [], [], []
    for w, layer in weights:
        rows, cols = w.shape[1] // n_blocks, w.shape[2]
        in_specs.append(pl.BlockSpec((None, rows, cols), lambda *idx, layer=layer: (layer, blk(*idx), 0)))
        out_specs.append(pl.BlockSpec((rows, cols), lambda *idx: (blk(*idx), 0)))
        out_shapes.append(jax.ShapeDtypeStruct(w.shape[1:], BF16))
    return in_specs, out_specs, out_shapes, [w for w, _ in weights]


def _cast_blocks(in_refs, out_refs):
    for w_ref, wo_ref in zip(in_refs, out_refs, strict=True):
        wo_ref[...] = w_ref[...].astype(BF16)


def _lane_stack(x, hw):
    head = lax.broadcasted_iota(jnp.int32, x.shape, 1) // hw
    return jnp.concatenate([jnp.where(head == h, x, 0.0) for h in range(GROUP // hw)], axis=0)


def _fold_rows(x, n):
    out = x[0:n]
    for h in range(1, x.shape[0] // n):
        out = out + x[h * n:(h + 1) * n]
    return out


def _block_masks(size, blk, d):
    row = lax.broadcasted_iota(jnp.int32, (size, size), 0)
    col = lax.broadcasted_iota(jnp.int32, (size, size), 1)
    rr, cc = row % blk, col % blk
    same = (row // blk) == (col // blk)
    if d == 0:
        return cc <= rr, cc < rr, same
    return cc >= rr, cc > rr, same


def _rwkv_scan_kernel(*refs):
    in_refs = (refs[0:6], refs[6:12])
    y_refs = refs[12:14]
    s_ref = refs[14]

    @pl.when(pl.program_id(1) == 0)
    def _():
        s_ref[...] = jnp.zeros_like(s_ref)

    n = CHUNK
    ngrp = RW // GROUP
    nch = 2 * ngrp
    rrow = lax.broadcasted_iota(jnp.int32, (n, GROUP), 0)
    rcol = lax.broadcasted_iota(jnp.int32, (n, GROUP), 1) % n
    eye_row = jnp.where(rrow == rcol, 1.0, 0.0)
    incl_row = (rcol <= rrow, rcol >= rrow)
    strict_row = (rcol < rrow, rcol > rrow)
    same = _block_masks(GROUP, n, 0)[2]

    def block_diag(x_row):
        xb = x_row.astype(BF16)
        return jnp.where(same, jnp.concatenate([xb] * (GROUP // n), axis=0), jnp.zeros((), BF16))

    def stack(x):
        return _lane_stack(x.astype(BF16), HD)

    pre = []
    for d in range(2):
        tri = jnp.where(_block_masks(n, n, d)[0], 1.0, 0.0).astype(BF16)
        order = range(SCAN_CHUNKS) if d == 0 else range(SCAN_CHUNKS - 1, -1, -1)
        pre_d = []
        for ci in order:
            rows = slice(ci * n, (ci + 1) * n)
            r, k, v, a, b, lw = [ref[rows, :] for ref in in_refs[d]]
            c = _tri_dot(tri, lw)
            c_last = c[n - 1:n] if d == 0 else c[0:1]
            p_inv = jnp.exp(-c)
            p_end = jnp.exp(c_last - c)
            pre_d.append(dict(rows=rows, at=a * jnp.exp(c - lw), rt=r * jnp.exp(c), bt=b * p_inv, kt=k * p_inv,
                              bp=b * p_end, kp=k * p_end, v=v, dec=jnp.exp(c_last)))
        pre.append(pre_d)
    chains = [(d, p, g) for p in range(SCAN_CHUNKS) for d in range(2) for g in range(ngrp)]
    nch = len(chains)
    sl = lambda g: slice(g * GROUP, (g + 1) * GROUP)
    part = lambda d, p, g, name: pre[d][p][name][:, sl(g)]

    a_st = [stack(part(d, p, g, "at")) for d, p, g in chains]
    v_st = [stack(part(d, p, g, "v")) for d, p, g in chains]
    gram = [_dot_nt(jnp.concatenate([part(d, p, g, "at"), part(d, p, g, "rt")], axis=0).astype(BF16),
                    jnp.concatenate([stack(part(d, p, g, "bt")), stack(part(d, p, g, "kt"))], axis=0))
            for d, p, g in chains]
    a_ab = [jnp.where(strict_row[d], gram[i][0:n, 0:GROUP], 0.0) for i, (d, p, g) in enumerate(chains)]
    a_ak = [jnp.where(strict_row[d], gram[i][0:n, GROUP:], 0.0).astype(BF16) for i, (d, p, g) in enumerate(chains)]
    a_r = [jnp.where(jnp.concatenate([incl_row[d]] * 2, axis=1), gram[i][n:2 * n, :], 0.0).astype(BF16)
           for i, (d, p, g) in enumerate(chains)]
    av = [_dot(a_ak[i], v_st[i]) for i in range(nch)]
    tm = [eye_row + m for m in a_ab]
    pw = [_dot(m.astype(BF16), block_diag(m)) for m in a_ab]
    for _ in range(4):
        both = [_dot(jnp.concatenate([t, p], axis=0).astype(BF16), block_diag(p)) for t, p in zip(tm, pw)]
        tm = [t + x[0:n] for t, x in zip(tm, both)]
        pw = [x[n:2 * n] for x in both]
    tm = [t + _dot(t.astype(BF16), block_diag(p)) for t, p in zip(tm, pw)]
    tx = [_dot(tm[i].astype(BF16), jnp.concatenate([a_st[i], stack(av[i])], axis=1))
          for i in range(nch)]

    state = {(d, g): s_ref[d, g] for d in range(2) for g in range(ngrp)}
    for p in range(SCAN_CHUNKS):
        ids = [i for i, ch in enumerate(chains) if ch[1] == p]
        uy0 = {i: _dot_nt(jnp.concatenate([tx[i][:, 0:GROUP], part(*chains[i], "rt")], axis=0).astype(BF16),
                          state[chains[i][0], chains[i][2]].astype(BF16)) for i in ids}
        u = {i: uy0[i][0:n] + tx[i][:, GROUP:] for i in ids}
        for i in ids:
            d, _, g = chains[i]
            y_refs[d][pre[d][p]["rows"], sl(g)] = (
                uy0[i][n:2 * n] + _dot(a_r[i], jnp.concatenate([stack(u[i]), v_st[i]], axis=0)))
        for i in ids:
            d, _, g = chains[i]
            upd = _dot_tn(jnp.concatenate([u[i], part(d, p, g, "v")], axis=0).astype(BF16),
                          jnp.concatenate([part(d, p, g, "bp"), part(d, p, g, "kp")], axis=0).astype(BF16))
            state[d, g] = state[d, g] * part(d, p, g, "dec") + jnp.where(same, upd, 0.0)
    for (d, g), s_new in state.items():
        s_ref[d, g] = s_new


def _rwkv_scan(r, k, v, a, b, lw, nb):
    rows = r.shape[0]
    blk_rows = SCAN_CHUNKS * CHUNK
    steps = (T + C) // blk_rows
    in_specs, args = [], []
    for d in range(2):
        blk = lambda bb, s, d=d: (_chunk_block(d, bb, s, nb, blk_rows), 0)
        in_specs += [pl.BlockSpec((blk_rows, RW), blk)] * 5
        in_specs.append(pl.BlockSpec((None, blk_rows, RW),
                                     lambda bb, s, d=d: (d, _chunk_block(d, bb, s, nb, blk_rows), 0)))
        args += [r, k, v, a, b, lw]
    out_sd = jax.ShapeDtypeStruct((rows, RW), F32)
    return pl.pallas_call(
        _rwkv_scan_kernel,
        grid=(nb, steps),
        in_specs=in_specs,
        out_specs=[pl.BlockSpec((blk_rows, RW), lambda bb, s, d=d: (_chunk_block(d, bb, s, nb, blk_rows), 0))
                   for d in range(2)],
        out_shape=[out_sd, out_sd],
        scratch_shapes=[pltpu.VMEM((2, RW // GROUP, GROUP, GROUP), F32)],
        compiler_params=_params(("parallel", "arbitrary"), V7X_VMEM_LIMIT),
        name="rwkv_scan",
    )(*args)


def _ab_mix(y0_ref, y1_ref, bonus_ref, g_ref, lnw_ref, lnb_ref, gd_ref, gu_ref, *yb_refs, pick, tile):
    y = y0_ref[...] + y1_ref[...]
    mu = _seg_sum(y, gd_ref[...], gu_ref[...]) * (1.0 / HD)
    yc = y - mu
    var = _seg_sum(yc * yc, gd_ref[...], gu_ref[...]) * (1.0 / HD)
    yn = yc * lax.rsqrt(var + RWKV_GN_EPS)
    return (yn * lnw_ref[...] + lnb_ref[...] + bonus_ref[...]) * g_ref[...], pick(tile, *yb_refs)


def _ab_mix_inputs(tile, y0, y1, bonus, g, ln_w, ln_b, yb_lat, yb_ctx):
    tm = OPROJ_TM
    head = np.arange(RW) // HD
    g_down = jnp.asarray(head[:, None] == np.arange(128)[None, :], BF16)
    g_upm = jnp.asarray(np.arange(128)[:, None] == head[None, :], BF16)
    row = pl.BlockSpec((tm, RW), lambda i: (tile(i), 0))
    vec = pl.BlockSpec((1, RW), lambda i: (0, 0))
    yb_args, yb_specs, pick = _split_rows(yb_lat, yb_ctx, tm, 1024, tile=tile)
    specs = [row, row, row, row, vec, vec,
             pl.BlockSpec((RW, 128), lambda i: (0, 0)), pl.BlockSpec((128, RW), lambda i: (0, 0))] + yb_specs
    args = (y0, y1, bonus, g, ln_w.reshape(1, RW), ln_b.reshape(1, RW), g_down, g_upm) + yb_args
    return functools.partial(_ab_mix, pick=pick), args, specs


def _kv_block(i, nb):
    per = T // 256
    return jnp.where(i < nb * per, i // per, i - nb * per), jnp.where(i < nb * per, i % per, per)


def _swa_prep_kernel(k_ref, v_ref, cos_ref, sin_ref, ko_ref, vo_ref):
    cosl, sinl = cos_ref[...], sin_ref[...]
    lane = lax.broadcasted_iota(jnp.int32, cosl.shape, 1)
    low = lane < HD
    for src, dst, rope in ((k_ref, ko_ref, True), (v_ref, vo_ref, False)):
        for t in range(2):
            x = src[:, t * 128:(t + 1) * 128]
            if rope:
                x = _rope(x, cosl, sinl)
            swapped = pltpu.roll(x, HD, axis=1)
            dst[:, (2 * t) * 128:(2 * t + 1) * 128] = jnp.where(low, x, swapped).astype(BF16)
            dst[:, (2 * t + 1) * 128:(2 * t + 2) * 128] = jnp.where(low, swapped, x).astype(BF16)


def _swa_prep(u, cosl, sinl, nb):
    rows = u.shape[0]
    tm = 256
    kv_out = pl.BlockSpec((None, tm, 512), lambda i: (*_kv_block(i, nb), 0))
    sd = jax.ShapeDtypeStruct((nb, T + C, 512), BF16)
    return pl.pallas_call(
        _swa_prep_kernel,
        grid=(rows // tm,),
        in_specs=[pl.BlockSpec((tm, 256), lambda i: (i, AB_KS // 256)),
                  pl.BlockSpec((tm, 256), lambda i: (i, AB_VS // 256)),
                  pl.BlockSpec((tm, 128), lambda i: (i, 0)),
                  pl.BlockSpec((tm, 128), lambda i: (i, 0))],
        out_specs=[kv_out, kv_out],
        out_shape=[sd, sd],
        compiler_params=_params(("parallel",), V7X_VMEM_LIMIT),
        name="swa_prep",
    )(u, u, cosl, sinl)


def _diff_prep_kernel(k_ref, v_ref, cos_ref, sin_ref, ko_ref, vo_ref):
    cosl, sinl = cos_ref[...], sin_ref[...]
    for t in range(8):
        sl = slice(t * 128, (t + 1) * 128)
        ko_ref[:, sl] = _rope(k_ref[:, sl], cosl, sinl).astype(BF16)
    vo_ref[...] = v_ref[...].astype(BF16)


def _diff_prep(u, cosl, sinl, nb):
    rows = u.shape[0]
    tm = 256
    kv_out = pl.BlockSpec((None, tm, 1024), lambda i: (*_kv_block(i, nb), 0))
    sd = jax.ShapeDtypeStruct((nb, T + C, 1024), BF16)
    return pl.pallas_call(
        _diff_prep_kernel,
        grid=(rows // tm,),
        in_specs=[pl.BlockSpec((tm, 1024), lambda i: (i, 1)),
                  pl.BlockSpec((tm, 1024), lambda i: (i, 2)),
                  pl.BlockSpec((tm, 128), lambda i: (i, 0)),
                  pl.BlockSpec((tm, 128), lambda i: (i, 0))],
        out_specs=[kv_out, kv_out],
        out_shape=[sd, sd],
        compiler_params=_params(("parallel",), V7X_VMEM_LIMIT),
        name="diff_prep",
    )(u, u, cosl, sinl)


def _swa_kernel(q_ref, cos_ref, sin_ref, k_ref, v_ref, sink_ref, *rest):
    n_cast = len(rest) // 2
    o_ref = rest[n_cast]
    ctx_mode = n_cast == 0
    _cast_blocks(rest[:n_cast], rest[n_cast + 1:])
    n = pl.program_id(1)
    cosl, sinl = cos_ref[...], sin_ref[...]
    QB = q_ref.shape[0]
    span = QB + 2 * SWA_WINDOW
    if not ctx_mode:
        start = pl.multiple_of(jnp.clip(n * QB - SWA_WINDOW, 0, T - span), SWA_WINDOW)
        kpos = start + lax.broadcasted_iota(jnp.int32, (2 * QB, span), 1)
        qpos = n * QB + lax.broadcasted_iota(jnp.int32, (2 * QB, span), 0) % QB
        valid = jnp.abs(kpos - qpos) <= SWA_WINDOW
    low = lax.broadcasted_iota(jnp.int32, (QB, 128), 1) < HD
    top = lax.broadcasted_iota(jnp.int32, (2 * QB, 1), 0) < QB
    tile = lambda j: slice((j // 2) * 128, (j // 2 + 1) * 128)
    qs, sink = [], []
    for j in range(8):
        q = _rope(q_ref[:, j * 128:(j + 1) * 128], cosl, sinl) * (HD ** -0.5 * LOG2E)
        qs.append(jnp.concatenate([jnp.where(low, q, 0.0), jnp.where(low, 0.0, q)], axis=0).astype(BF16))
        sink.append(jnp.where(top, sink_ref[2 * j:2 * j + 1, 0:1], sink_ref[2 * j + 1:2 * j + 2, 0:1]) * LOG2E)
    s_c = [_dot_nt(qs[j], k_ref[T:T + C, tile(j)]) for j in range(8)]
    m = [jnp.maximum(jnp.max(s, axis=-1, keepdims=True), sk) for s, sk in zip(s_c, sink)]
    if not ctx_mode:
        s_w = [jnp.where(valid, _dot_nt(qs[j], k_ref[pl.ds(start, span), tile(j)]), -jnp.inf) for j in range(8)]
        m = [jnp.maximum(mm, jnp.max(s, axis=-1, keepdims=True)) for mm, s in zip(m, s_w)]
    p_c = [jnp.exp2(s - mm) for s, mm in zip(s_c, m)]
    den = [jnp.sum(p, axis=-1, keepdims=True) + jnp.exp2(sk - mm) for p, sk, mm in zip(p_c, sink, m)]
    pv = [_dot(p.astype(BF16), v_ref[T:T + C, tile(j)]) for j, p in enumerate(p_c)]
    if not ctx_mode:
        p_w = [jnp.exp2(s - mm) for s, mm in zip(s_w, m)]
        den = [dd + jnp.sum(p, axis=-1, keepdims=True) for dd, p in zip(den, p_w)]
        pv = [x + _dot(p.astype(BF16), v_ref[pl.ds(start, span), tile(j)]) for j, (x, p) in enumerate(zip(pv, p_w))]
    for j in range(8):
        o = pv[j] / den[j]
        o_ref[:, j * 128:(j + 1) * 128] = jnp.where(low, o[0:QB], o[QB:2 * QB])


def _swa(u, cosl, sinl, kpad, vpad, sink, nb, ctx_mode, weights=()):
    assert ctx_mode == (not weights)
    QB = SWA_QB
    nq = (C if ctx_mode else T) // QB
    base = nb * (T // QB) if ctx_mode else 0
    rows = nb * nq * QB
    rowblk = lambda b, n: base + b * nq + n
    kv = pl.BlockSpec((None, T + C, 512), lambda b, n: (b, 0, 0))
    in_specs = [pl.BlockSpec((QB, 1024), lambda b, n: (rowblk(b, n), AB_Q // 1024)),
                pl.BlockSpec((QB, 128), lambda b, n: (rowblk(b, n), 0)),
                pl.BlockSpec((QB, 128), lambda b, n: (rowblk(b, n), 0)),
                kv, kv, pl.BlockSpec((16, 128), lambda b, n: (0, 0))]
    out_specs = [pl.BlockSpec((QB, 1024), lambda b, n: (b * nq + n, 0))]
    out_shape = [jax.ShapeDtypeStruct((rows, 1024), F32)]
    args = [u, cosl, sinl, kpad, vpad, sink]
    if weights:
        cast_in, cast_out, cast_shape, cast_args = _cast_specs(weights, lambda b, n: b * nq + n, nb * nq)
        in_specs += cast_in
        out_specs += cast_out
        out_shape += cast_shape
        args += cast_args
    return pl.pallas_call(
        _swa_kernel,
        grid=(nb, nq),
        in_specs=in_specs,
        out_specs=out_specs,
        out_shape=out_shape,
        compiler_params=_params(("arbitrary", "arbitrary"), V7X_VMEM_LIMIT),
        name="swa_ctx" if ctx_mode else "swa_latent",
    )(*args)


def _diff_kernel(q_ref, cos_ref, sin_ref, k_ref, v_ref, lam_ref, sub_ref, o_ref):
    cosl, sinl = cos_ref[...], sin_ref[...]
    lam = lam_ref[...]
    nq = q_ref.shape[0]
    low = lax.broadcasted_iota(jnp.int32, (nq, 128), 1) < HD
    tile = lambda h: slice(h * 128, (h + 1) * 128)

    def logits(h):
        q = _rope(q_ref[:, tile(h)], cosl, sinl) * (HD ** -0.5 * LOG2E)
        qs = jnp.concatenate([jnp.where(low, q, 0.0), jnp.where(low, 0.0, q)], axis=0).astype(BF16)
        return _dot_nt(qs, k_ref[:, tile(h)])

    s_next = logits(0)
    for h in range(8):
        s = s_next
        if h + 1 < 8:
            s_next = logits(h + 1)
        e = jnp.exp2(s - jnp.max(s, axis=-1, keepdims=True))
        inv = 1.0 / jnp.sum(e, axis=-1, keepdims=True)
        o2 = _dot(e.astype(BF16), v_ref[:, tile(h)])
        o = o2[0:nq] * inv[0:nq] - o2[nq:2 * nq] * (lam[:, 0:1] * inv[nq:2 * nq])
        o = o * lax.rsqrt(jnp.mean(o * o, axis=-1, keepdims=True) + 1e-5)
        o_ref[:, tile(h)] = o * sub_ref[...]


def _diff_attn(u, cosl, sinl, kb, vb, lam, sub, nb):
    QB = DIFF_QB
    nq = T // QB
    kv = pl.BlockSpec((None, T + C, 1024), lambda b, n: (b, 0, 0))
    vec = pl.BlockSpec((1, 128), lambda b, n: (0, 0))
    return pl.pallas_call(
        _diff_kernel,
        grid=(nb, nq),
        in_specs=[pl.BlockSpec((QB, 1024), lambda b, n: (b * nq + n, 0)),
                  pl.BlockSpec((QB, 128), lambda b, n: (b * nq + n, 0)),
                  pl.BlockSpec((QB, 128), lambda b, n: (b * nq + n, 0)),
                  kv, kv, vec, vec],
        out_specs=pl.BlockSpec((QB, 1024), lambda b, n: (b * nq + n, 0)),
        out_shape=jax.ShapeDtypeStruct((nb * T, 1024), F32),
        compiler_params=_params(("parallel", "arbitrary"), V7X_VMEM_LIMIT),
        name="diff_attn",
    )(u, cosl, sinl, kb, vb, lam, sub)


def _hgrn_exact_att(q, kk, bcum, tmp_ref):
    n = CHUNK
    tmp_ref[0] = bcum
    tmp_ref[1] = kk
    coli = lax.broadcasted_iota(jnp.int32, (n, 2 * n), 1)

    def body(s, acc):
        bs = tmp_ref[0, pl.ds(s, 1), :]
        ks = tmp_ref[1, pl.ds(s, 1), :]
        w = q * jnp.exp(jnp.minimum(bcum - bs, 0.0)) * ks
        c0 = jnp.sum(w[:, 0:HG_DK], axis=-1, keepdims=True)
        c1 = jnp.sum(w[:, HG_DK:], axis=-1, keepdims=True)
        return acc + jnp.concatenate([jnp.where(coli == s, c0, 0.0), jnp.where(coli == s + n, c1, 0.0)], axis=0)

    return lax.fori_loop(0, n, body, jnp.zeros((2 * n, 2 * n), F32))


def _hgrn_scan_kernel(*refs):
    n_cast = (len(refs) - 13) // 2
    in_refs = (refs[0:4], refs[4:8])
    o_refs = refs[8 + n_cast:10 + n_cast]
    s_ref, g_ref, tmp_ref = refs[10 + 2 * n_cast:]
    _cast_blocks(refs[8:8 + n_cast], refs[10 + n_cast:10 + 2 * n_cast])

    @pl.when(pl.program_id(1) == 0)
    def _():
        s_ref[...] = jnp.zeros_like(s_ref)

    n = CHUNK
    ngrp = 1024 // GROUP
    nsub = n // SUB
    pre = []
    for d in range(2):
        tri = jnp.where(_block_masks(n, n, d)[0], 1.0, 0.0).astype(BF16)
        lb = in_refs[d][3][...]
        order = range(SCAN_CHUNKS) if d == 0 else range(SCAN_CHUNKS - 1, -1, -1)
        pre_d = []
        for ci in order:
            rows = slice(ci * n, (ci + 1) * n)
            q, z, v = [ref[rows, :] for ref in in_refs[d][0:3]]
            logf = jnp.log(lb + (1.0 - lb) * _sigmoid(z))
            kk = (1.0 - lb) * _sigmoid(-z)
            bcum = _tri_dot(tri, logf)
            b_last = bcum[n - 1:n] if d == 0 else bcum[0:1]
            b_excl = bcum - logf
            qh, kh = [], []
            for sb in range(nsub):
                lo, hi = sb * SUB, (sb + 1) * SUB
                beta = b_excl[lo:lo + 1] if d == 0 else b_excl[hi - 1:hi]
                qh.append(q[lo:hi] * jnp.exp(bcum[lo:hi] - beta))
                kh.append(kk * jnp.exp(beta - bcum))
            pre_d.append(dict(rows=rows, q=q, v=v, kk=kk, bcum=bcum, qh=qh, kh=kh, qe=q * jnp.exp(bcum),
                              ke=kk * jnp.exp(b_last - bcum), dec=jnp.exp(b_last), min_logf=jnp.min(logf)))
        pre.append(pre_d)
    chains = [(d, p, g) for p in range(SCAN_CHUNKS) for d in range(2) for g in range(ngrp)]
    sl = lambda g: slice(g * GROUP, (g + 1) * GROUP)

    for i, (d, p, g) in enumerate(chains):
        c = pre[d][p]
        rows = [_dot_nt(_lane_stack(c["qh"][sb][:, sl(g)].astype(BF16), HG_DK),
                        _lane_stack(c["kh"][sb][:, sl(g)].astype(BF16), HG_DK)) for sb in range(nsub)]
        g_ref[i] = jnp.concatenate([rows[sb][h * SUB:(h + 1) * SUB] for h in range(2) for sb in range(nsub)], axis=0)

    min_logf = functools.reduce(jnp.minimum, [c["min_logf"] for pre_d in pre for c in pre_d])

    @pl.when(min_logf < -(HG_CLAMP / SUB))
    def _():
        for i, (d, p, g) in enumerate(chains):
            c = pre[d][p]
            g_ref[i] = _hgrn_exact_att(c["q"][:, sl(g)], c["kk"][:, sl(g)], c["bcum"][:, sl(g)], tmp_ref)

    state = {(d, g): s_ref[d, g] for d in range(2) for g in range(ngrp)}
    head_same = _block_masks(GROUP, HG_DK, 0)[2]
    for p in range(SCAN_CHUNKS):
        ids = [i for i, ch in enumerate(chains) if ch[1] == p]
        inter = {i: _dot_nt(pre[chains[i][0]][p]["qe"][:, sl(chains[i][2])].astype(BF16),
                            state[chains[i][0], chains[i][2]].astype(BF16)) for i in ids}
        for i in ids:
            d, _, g = chains[i]
            incl, _, same = _block_masks(2 * n, n, d)
            att = jnp.where(incl & same, g_ref[i], 0.0).astype(BF16)
            v_st = _lane_stack(pre[d][p]["v"][:, sl(g)].astype(BF16), HG_DK)
            o_refs[d][pre[d][p]["rows"], sl(g)] = inter[i] + _fold_rows(_dot(att, v_st), n)
        for i in ids:
            d, _, g = chains[i]
            upd = _dot_tn(pre[d][p]["v"][:, sl(g)].astype(BF16), pre[d][p]["ke"][:, sl(g)].astype(BF16))
            state[d, g] = state[d, g] * pre[d][p]["dec"][:, sl(g)] + jnp.where(head_same, upd, 0.0)
    for (d, g), s_new in state.items():
        s_ref[d, g] = s_new


def _hgrn_scan(u, lb, nb, weights):
    rows = u.shape[0]
    blk_rows = SCAN_CHUNKS * CHUNK
    steps = (T + C) // blk_rows
    cast_in, cast_out, cast_shape, cast_args = _cast_specs(weights, lambda bb, s: bb * steps + s, nb * steps)
    in_specs, args = [], []
    for d in range(2):
        col = lambda j, d=d: pl.BlockSpec((blk_rows, 1024),
                                          lambda bb, s: (_chunk_block(d, bb, s, nb, blk_rows), j))
        in_specs += [col(3), col(4 + d), col(6), pl.BlockSpec((None, 1, 1024), lambda bb, s, d=d: (d, 0, 0))]
        args += [u, u, u, lb]
    out_sd = jax.ShapeDtypeStruct((rows, 1024), F32)
    ngrp = 1024 // GROUP
    return pl.pallas_call(
        _hgrn_scan_kernel,
        grid=(nb, steps),
        in_specs=in_specs + cast_in,
        out_specs=[pl.BlockSpec((blk_rows, 1024), lambda bb, s, d=d: (_chunk_block(d, bb, s, nb, blk_rows), 0))
                   for d in range(2)] + cast_out,
        out_shape=[out_sd, out_sd] + cast_shape,
        scratch_shapes=[pltpu.VMEM((2, ngrp, GROUP, GROUP), F32),
                        pltpu.VMEM((2 * ngrp * SCAN_CHUNKS, 2 * CHUNK, 2 * CHUNK), F32),
                        pltpu.VMEM((2, CHUNK, GROUP), F32)],
        compiler_params=_params(("arbitrary", "arbitrary"), V7X_VMEM_LIMIT),
        name="hgrn_scan",
    )(*args, *cast_args)


def _cd_mix(yc_ref, o0_ref, o1_ref, g_ref, gn_ref, *, tile):
    g = g_ref[...]
    heads = []
    for h in range(8):
        sl = slice(h * 128, (h + 1) * 128)
        o = o0_ref[:, sl] + o1_ref[:, sl]
        o = o * lax.rsqrt(jnp.mean(o * o, axis=-1, keepdims=True) + NORM_EPS) * gn_ref[...]
        gh = g[:, sl]
        heads.append(o * (gh * _sigmoid(gh)))
    return yc_ref[...], jnp.concatenate(heads, axis=1)


def _cd_mix_inputs(tile, yc, o0, o1, u, gnorm):
    tm = OPROJ_TM
    row = pl.BlockSpec((tm, 1024), lambda i: (tile(i), 0))
    specs = [row, row, row, pl.BlockSpec((tm, 1024), lambda i: (tile(i), 7)),
             pl.BlockSpec((1, 128), lambda i: (0, 0))]
    return _cd_mix, (yc, o0, o1, u, gnorm.reshape(1, 128)), specs


def _rope_tables(nb):
    t = np.arange(T)
    quarter = HD // 4
    inv = ROPE_THETA ** (-jnp.arange(quarter, dtype=F32) / quarter)
    rows = jnp.asarray(t // GRID_W, F32)
    cols = jnp.asarray(t % GRID_W, F32)
    ang = jnp.concatenate([rows[:, None] * inv, cols[:, None] * inv], axis=-1)
    cos, sin = jnp.cos(ang), jnp.sin(ang)
    cosl = jnp.tile(jnp.concatenate([cos, cos], axis=-1), (nb, 2))
    sinl = jnp.tile(jnp.concatenate([-sin, sin], axis=-1), (nb, 2))
    cosl = jnp.concatenate([cosl, jnp.ones((nb * C, 128), F32)], axis=0)
    sinl = jnp.concatenate([sinl, jnp.zeros((nb * C, 128), F32)], axis=0)
    return cosl, sinl


def kernel(x, c, ctx, c_ctx, ada_w, ada_b, mlp_w1, mlp_w2, final_norm, ab_w_in, ab_w_out, ab_mu, ab_w0,
           ab_w_up, ab_a0, ab_a_up, ab_g_up, ab_k_k, ab_k_a, ab_r_k, ab_ln_w, ab_ln_b, ab_sink,
           cd_w_in, cd_w_out, cd_lam, cd_subln, cd_gnorm, hgrn_lb_logits):
    nb = x.shape[0]
    assert x.shape == (nb, T, D) and ctx.shape == (nb, C, D) and nb < 8
    assert ada_w.shape[0] == 2, "one AB layer followed by one CD layer"
    rx = nb * T
    rows = rx + nb * C
    x2, c2 = x.reshape(rx, D), ctx.reshape(nb * C, D)
    cosl, sinl = _rope_tables(nb)

    s_in = jnp.zeros((8, D), F32).at[:nb].set(c).at[nb].set(c_ctx)
    mod = _ada(s_in, ada_w, ada_b)

    w = ab_w_in[0]
    w_in0 = jnp.concatenate([w[:, :3 * RW], w[:, 3 * RW + 3 * LORA:], w[:, 3 * RW:3 * RW + 3 * LORA],
                             jnp.zeros((D, LORA), F32)], axis=1).astype(BF16)
    u = _norm_mod_matmul(x2, c2, mod, 0, w_in0, nb, 512, AB_N // 2)
    r, k, v, a, b, lw, g, bonus = _rwkv_prepare(u, nb, ab_mu[0], ab_w0[0], ab_w_up[0], ab_a0[0], ab_a_up[0],
                                                ab_g_up[0], ab_k_k[0], ab_k_a[0], ab_r_k[0].reshape(RW))
    y0, y1 = _rwkv_scan(r, k, v, a, b, lw, nb)
    kpad, vpad = _swa_prep(u, cosl, sinl, nb)
    sink = jnp.broadcast_to(ab_sink[0][:, None], (16, 128))
    yb_lat, w1b, w2b, w_in1, w_out0 = _swa(u, cosl, sinl, kpad, vpad, sink, nb, False,
                                           [(mlp_w1, 0), (mlp_w2, 0), (cd_w_in, 0), (ab_w_out, 0)])
    (yb_ctx,) = _swa(u, cosl, sinl, kpad, vpad, sink, nb, True)
    mix_inputs = functools.partial(_ab_mix_inputs, y0=y0, y1=y1, bonus=bonus, g=g, ln_w=ab_ln_w[0],
                                   ln_b=ab_ln_b[0], yb_lat=yb_lat, yb_ctx=yb_ctx)
    xc = _out_proj(mix_inputs, x2, c2, rows, mod, 0, w_out0, nb)
    fn = final_norm.reshape(1, D)
    xc = _mlp(xc, rows, mod, 0, w1b, w2b, fn, nb, False)

    lam_init = 0.8 - 0.6 * math.exp(-0.3 * 1)
    lb_table = jnp.cumsum(jax.nn.softmax(hgrn_lb_logits.astype(F32), axis=0), axis=0)
    lb = (lb_table - lb_table[0])[1].reshape(2, 1, 1024)
    lf = cd_lam[0].astype(F32)
    lmb = jnp.exp(jnp.sum(lf[0] * lf[1])) - jnp.exp(jnp.sum(lf[2] * lf[3])) + lam_init
    u = _norm_mod_matmul(xc, None, mod, 1, w_in1, nb, 512, 2048)
    kb, vb = _diff_prep(u, cosl, sinl, nb)
    yc = _diff_attn(u, cosl, sinl, kb, vb, jnp.full((1, 128), lmb, F32),
                    (cd_subln[0] * (1.0 - lam_init)).reshape(1, 128), nb)
    o0, o1, w1b, w2b, w_out1 = _hgrn_scan(u, lb, nb, [(mlp_w1, 1), (mlp_w2, 1), (cd_w_out, 0)])
    mix_inputs = functools.partial(_cd_mix_inputs, yc=yc, o0=o0, o1=o1, u=u, gnorm=cd_gnorm[0])
    xl = _out_proj(mix_inputs, xc, None, rx, mod, 1, w_out1, nb)
    out = _mlp(xl, rx, mod, 1, w1b, w2b, fn, nb, True)
    return out.reshape(nb, T, D)
```

```python
import functools
import math

import jax
import jax.numpy as jnp
import numpy as np
from jax import lax
from jax.experimental import pallas as pl
from jax.experimental.pallas import tpu as pltpu

F32 = jnp.float32
BF16 = jnp.bfloat16

D = 2048
T = 2048
C = 256
GRID_W = 64
D_FF = 4 * D
HD = 64
ROPE_THETA = 10000.0
LOG2E = 1.4426950408889634
NORM_EPS = 1e-6
N_MOD = 6
RW = 1024
RWKV_GN_EPS = 64e-5
LORA = 64
SWA_WINDOW = 128
SWA_QB = 128
DIFF_QB = 256
OPROJ_TM = 512
CHUNK = 64
SCAN_CHUNKS = 4
CAST_BLOCK_COUNTS = (32, 64)
SUB = 16
HG_DK = 128
HG_CLAMP = 80.0
GROUP = 256
V7X_VMEM_BYTES = 64 * 1024 * 1024
V7X_VMEM_LIMIT = V7X_VMEM_BYTES - 8 * 1024 * 1024

AB_Q, AB_KS, AB_VS, AB_LORA, AB_N = 3072, 4096, 4352, 4608, 4864


def _dot(a, b):
    return jnp.dot(a, b, preferred_element_type=F32)


def _dot_nt(a, b):
    return lax.dot_general(a, b, (((1,), (1,)), ((), ())), preferred_element_type=F32)


def _dot_tn(a, b):
    return lax.dot_general(a, b, (((0,), (0,)), ((), ())), preferred_element_type=F32)


def _split2(x):
    hi = x.astype(BF16)
    lo = (x - hi.astype(F32)).astype(BF16)
    return hi, lo


def _split3(x):
    hi = x.astype(BF16)
    r1 = x - hi.astype(F32)
    mid = r1.astype(BF16)
    lo = (r1 - mid.astype(F32)).astype(BF16)
    return hi, mid, lo


def _dot_exact_rhs(x, g):
    hi, lo = _split2(x)
    return _dot(hi, g) + _dot(lo, g)


def _tri_dot(tri, x):
    hi, mid, lo = _split3(x)
    return _dot(tri, hi) + _dot(tri, mid) + _dot(tri, lo)


def _seg_sum(x, g_down, g_up):
    return _dot_exact_rhs(_dot_exact_rhs(x, g_down), g_up)


def _rope(x, cosl, sinl):
    w = x.shape[-1]
    lane = lax.broadcasted_iota(jnp.int32, x.shape, x.ndim - 1)
    first = (lane & 63) < 32
    swapped = jnp.where(first, pltpu.roll(x, w - 32, axis=1), pltpu.roll(x, 32, axis=1))
    return x * cosl + swapped * sinl


def _sigmoid(x):
    return 1.0 / (1.0 + jnp.exp(-x))


def _softplus(x):
    return jnp.maximum(x, 0.0) + jnp.log(1.0 + jnp.exp(-jnp.abs(x)))


def _params(sem, vmem=None):
    return pltpu.CompilerParams(dimension_semantics=sem, vmem_limit_bytes=vmem)


def _ada_kernel(s_ref, w_ref, b_ref, o_ref):
    s = s_ref[...]
    s = s * _sigmoid(s)
    o_ref[...] = _dot(s.astype(BF16), w_ref[...].astype(BF16)) + b_ref[...]


def _ada(s_in, ada_w, ada_b):
    depth = ada_w.shape[0]
    n = ada_w.shape[2]
    tn = 1536
    out = pl.pallas_call(
        _ada_kernel,
        grid=(depth, n // tn),
        in_specs=[pl.BlockSpec((8, D), lambda l, j: (0, 0)),
                  pl.BlockSpec((None, D, tn), lambda l, j: (l, 0, j)),
                  pl.BlockSpec((None, 1, tn), lambda l, j: (l, 0, j))],
        out_specs=pl.BlockSpec((None, 8, tn), lambda l, j: (l, 0, j)),
        out_shape=jax.ShapeDtypeStruct((depth, 8, n), F32),
        compiler_params=_params(("arbitrary", "arbitrary"), V7X_VMEM_LIMIT),
        name="ada_mod",
    )(s_in, ada_w, ada_b.reshape(depth, 1, n))
    return out.reshape(depth, 8, 1, n)


def _mod_spec(layer, k, tm, nb, tile=lambda i: i):
    return pl.BlockSpec((None, None, 1, D),
                        lambda i, *_: (layer, jnp.minimum((tile(i) * tm) // T, nb), 0, k))


def _split_rows(lat, ctx, tm, width, col=0, tile=lambda i: i):
    if ctx is None:
        return ((lat,), [pl.BlockSpec((tm, width), lambda i, *_: (tile(i), col))],
                lambda t, ref, rows=slice(None): ref[rows, :])
    n_lat = lat.shape[0] // tm
    specs = [pl.BlockSpec((tm, width), lambda i, *_: (jnp.minimum(tile(i), n_lat - 1), col)),
             pl.BlockSpec((tm, width), lambda i, *_: (jnp.maximum(tile(i) - n_lat, 0), col),
                          pipeline_mode=pl.Buffered(1))]
    return ((lat, ctx), specs,
            lambda t, lat_ref, ctx_ref, rows=slice(None): jnp.where(t < n_lat, lat_ref[rows, :], ctx_ref[rows, :]))


def _nmm_kernel(*refs, n_x, pick):
    sh_ref, sc_ref, w_ref, o_ref, lhs_ref = refs[n_x:]

    @pl.when(pl.program_id(1) == 0)
    def _():
        x = pick(pl.program_id(0), *refs[:n_x])
        xn = x * lax.rsqrt(jnp.mean(x * x, axis=-1, keepdims=True) + NORM_EPS)
        lhs_ref[...] = (xn * (1.0 + sc_ref[...]) + sh_ref[...]).astype(BF16)

    o_ref[...] = _dot(lhs_ref[...], w_ref[...].astype(BF16)).astype(o_ref.dtype)


def _norm_mod_matmul(x_lat, x_ctx, mod, layer, w, nb, tm, tn):
    rows = x_lat.shape[0] + (0 if x_ctx is None else x_ctx.shape[0])
    n = w.shape[1]
    x_args, x_specs, pick = _split_rows(x_lat, x_ctx, tm, D)
    if x_ctx is None and tm > 512:
        x_specs = [pl.BlockSpec((tm, D), lambda i, j: (i, 0), pipeline_mode=pl.Buffered(1))]
    return pl.pallas_call(
        functools.partial(_nmm_kernel, n_x=len(x_args), pick=pick),
        grid=(rows // tm, n // tn),
        in_specs=x_specs + [_mod_spec(layer, 0, tm, nb), _mod_spec(layer, 1, tm, nb),
                            pl.BlockSpec((D, tn), lambda i, j: (0, j))],
        out_specs=pl.BlockSpec((tm, tn), lambda i, j: (i, j)),
        out_shape=jax.ShapeDtypeStruct((rows, n), F32),
        scratch_shapes=[pltpu.VMEM((tm, D), BF16)],
        compiler_params=_params(("parallel", "arbitrary"), V7X_VMEM_LIMIT),
        name=f"in_proj_{layer}",
    )(*x_args, mod, mod, w)


def _mlp_kernel(x_ref, sh_ref, sc_ref, gt_ref, w1_ref, w2_ref, fn_ref, o_ref, lhs_ref, *, final):
    f = pl.program_id(1)

    @pl.when(f == 0)
    def _():
        x = x_ref[...]
        xn = x * lax.rsqrt(jnp.mean(x * x, axis=-1, keepdims=True) + NORM_EPS)
        lhs_ref[...] = (xn * (1.0 + sc_ref[...]) + sh_ref[...]).astype(BF16)
        o_ref[...] = jnp.zeros_like(o_ref)

    h = jnp.maximum(_dot(lhs_ref[...], w1_ref[...]), 0.0)
    o_ref[...] += _dot((h * h).astype(BF16), w2_ref[...])

    @pl.when(f == pl.num_programs(1) - 1)
    def _():
        y = x_ref[...] + gt_ref[...] * o_ref[...]
        if final:
            y = y * lax.rsqrt(jnp.mean(y * y, axis=-1, keepdims=True) + NORM_EPS) * fn_ref[...]
        o_ref[...] = y


def _mlp(xc, rows, mod, layer, w1, w2, final_norm, nb, final):
    tm, tf = 512, 1024
    return pl.pallas_call(
        functools.partial(_mlp_kernel, final=final),
        grid=(rows // tm, D_FF // tf),
        in_specs=[pl.BlockSpec((tm, D), lambda i, f: (i, 0)),
                  _mod_spec(layer, 3, tm, nb), _mod_spec(layer, 4, tm, nb), _mod_spec(layer, 5, tm, nb),
                  pl.BlockSpec((D, tf), lambda i, f: (0, f)),
                  pl.BlockSpec((tf, D), lambda i, f: (f, 0)),
                  pl.BlockSpec((1, D), lambda i, f: (0, 0))],
        out_specs=pl.BlockSpec((tm, D), lambda i, f: (i, 0)),
        out_shape=jax.ShapeDtypeStruct((rows, D), F32),
        scratch_shapes=[pltpu.VMEM((tm, D), BF16)],
        compiler_params=_params(("parallel", "arbitrary"), V7X_VMEM_LIMIT),
        name=f"mlp_{layer}",
    )(xc, mod, mod, mod, w1, w2, final_norm)


def _oproj_kernel(*refs, n_mix, mix, n_x, pick):
    x_refs = refs[n_mix:n_mix + n_x]
    gt_ref, wa_ref, wb_ref, o_ref = refs[n_mix + n_x:]
    i = pl.program_id(0)
    tm = o_ref.shape[0]
    for h in range(2):
        rows = slice(h * tm // 2, (h + 1) * tm // 2)
        ya, yb = mix(*refs[:n_mix], tile=i, rows=rows)
        y = _dot(ya.astype(BF16), wa_ref[...]) + _dot(yb.astype(BF16), wb_ref[...])
        o_ref[rows, :] = pick(i, *x_refs, rows=rows) + gt_ref[...] * y


def _out_proj(mix_inputs, x_lat, x_ctx, rows, mod, layer, w, nb):
    tm = OPROJ_TM
    half = w.shape[0] // 2
    mix, mix_args, mix_specs = mix_inputs(lambda i: i)
    x_args, x_specs, pick = _split_rows(x_lat, x_ctx, tm, D)
    return pl.pallas_call(
        functools.partial(_oproj_kernel, n_mix=len(mix_args), mix=mix, n_x=len(x_args), pick=pick),
        grid=(rows // tm,),
        in_specs=list(mix_specs) + x_specs + [
            _mod_spec(layer, 2, tm, nb),
            pl.BlockSpec((half, D), lambda i: (0, 0), pipeline_mode=pl.Buffered(1)),
            pl.BlockSpec((half, D), lambda i: (1, 0), pipeline_mode=pl.Buffered(1))],
        out_specs=pl.BlockSpec((tm, D), lambda i: (i, 0)),
        out_shape=jax.ShapeDtypeStruct((rows, D), F32),
        compiler_params=_params(("parallel",), V7X_VMEM_LIMIT),
        name=f"out_proj_{layer}",
    )(*mix_args, *x_args, mod, w, w)


def _shifted(u, prev_row, next_row):
    tm = u.shape[0]
    row = lax.broadcasted_iota(jnp.int32, u.shape, 0)
    up = jnp.where(row == 0, prev_row, pltpu.roll(u, 1, axis=0))
    un = jnp.where(row == tm - 1, next_row, pltpu.roll(u, tm - 1, axis=0))
    return 0.5 * (up + un)


def _rwkv_prep_kernel(u_ref, ul_ref, up_ref, un_ref, ulp_ref, uln_ref, mu_ref, mul_ref, w0_ref, wup_ref,
                      a0_ref, aup_ref, gup_ref, kk_ref, ka_ref, rk_ref, gd_ref, gu_ref,
                      r_out, k_out, v_out, a_out, b_out, lw_out, g_out, bonus_out):
    u = u_ref[...]
    u = u + mu_ref[...] * (_shifted(u, up_ref[...], un_ref[...]) - u)
    ul = ul_ref[...]
    ul = ul + mul_ref[...] * (_shifted(ul, ulp_ref[...], uln_ref[...]) - ul)
    r, k, v = u[:, 0:RW], u[:, RW:2 * RW], u[:, 2 * RW:3 * RW]

    th = jnp.tanh(ul).astype(BF16)
    for d in range(2):
        w_log = -_softplus(-(w0_ref[d:d + 1, :] + _dot(th, wup_ref[d]))) - 0.5
        lw_out[d] = -jnp.exp(w_log)
    a = _sigmoid(a0_ref[...] + _dot(ul.astype(BF16), aup_ref[...]))
    g_out[...] = _dot(_sigmoid(ul).astype(BF16), gup_ref[...])

    kk = k * kk_ref[...]
    nrm = jnp.sqrt(_seg_sum(kk * kk, gd_ref[...], gu_ref[...]))
    kk = kk / jnp.maximum(nrm, 1e-12)
    k = k * (1.0 + (a - 1.0) * ka_ref[...])
    r_out[...] = r
    k_out[...] = k
    v_out[...] = v
    a_out[...] = -kk
    b_out[...] = kk * a
    bonus_out[...] = _seg_sum(r * k * rk_ref[...], gd_ref[...], gu_ref[...]) * v


def _seq_halo(u, tm, nb):
    rows = u.shape[0]
    nblk = rows // tm
    starts = np.arange(nblk) * tm
    seq_len = np.where(starts < nb * T, T, C)
    seq_off = np.where(starts < nb * T, starts % T, (starts - nb * T) % C)
    has_prev = seq_off > 0
    has_next = seq_off + tm < seq_len
    prev_idx = np.where(has_prev, starts - 1, 0)
    next_idx = np.where(has_next, starts + tm, 0)
    up = jnp.where(has_prev[:, None], u[prev_idx], 0.0)
    un = jnp.where(has_next[:, None], u[next_idx], 0.0)
    return up[:, None, :], un[:, None, :]


def _rwkv_prepare(u, nb, mu, w0, w_up, a0, a_up, g_up, k_k, k_a, r_k):
    rows = u.shape[0]
    tm = 256
    nblk = rows // tm
    up, un = _seq_halo(u, tm, nb)
    pad = jnp.zeros((LORA,), F32)
    mu_rkv = mu[:3 * RW].reshape(1, 3 * RW)
    mu_l = jnp.concatenate([mu[3 * RW:], pad]).reshape(1, 4 * LORA)

    def lora_w(w, slot):
        z = jnp.zeros((4 * LORA, RW), F32)
        return z.at[slot * LORA:(slot + 1) * LORA].set(w).astype(BF16)

    wup = jnp.stack([lora_w(w_up[0], 0), lora_w(w_up[1], 0)])
    aup = lora_w(a_up, 1)
    gup = lora_w(g_up, 2)
    head = np.arange(RW) // HD
    g_down = jnp.asarray(head[:, None] == np.arange(128)[None, :], BF16)
    g_upm = jnp.asarray(np.arange(128)[:, None] == head[None, :], BF16)

    row = lambda w: pl.BlockSpec((tm, w), lambda i: (i, 0))
    vec = lambda w: pl.BlockSpec((1, w), lambda i: (0, 0))
    full = lambda *s: pl.BlockSpec(s, lambda i: (0,) * len(s))
    out_sd = jax.ShapeDtypeStruct((rows, RW), F32)
    outs = pl.pallas_call(
        _rwkv_prep_kernel,
        grid=(nblk,),
        in_specs=[pl.BlockSpec((tm, 3 * RW), lambda i: (i, 0)),
                  pl.BlockSpec((tm, 4 * LORA), lambda i: (i, AB_LORA // (4 * LORA))),
                  pl.BlockSpec((None, 1, 3 * RW), lambda i: (i, 0, 0)),
                  pl.BlockSpec((None, 1, 3 * RW), lambda i: (i, 0, 0)),
                  pl.BlockSpec((None, 1, 4 * LORA), lambda i: (i, 0, AB_LORA // (4 * LORA))),
                  pl.BlockSpec((None, 1, 4 * LORA), lambda i: (i, 0, AB_LORA // (4 * LORA))),
                  vec(3 * RW), vec(4 * LORA), full(2, RW), full(2, 4 * LORA, RW),
                  vec(RW), full(4 * LORA, RW), full(4 * LORA, RW), vec(RW), vec(RW), vec(RW),
                  full(RW, 128), full(128, RW)],
        out_specs=[row(RW), row(RW), row(RW), row(RW), row(RW),
                   pl.BlockSpec((2, tm, RW), lambda i: (0, i, 0)), row(RW), row(RW)],
        out_shape=[out_sd, out_sd, out_sd, out_sd, out_sd,
                   jax.ShapeDtypeStruct((2, rows, RW), F32), out_sd, out_sd],
        compiler_params=_params(("parallel",), V7X_VMEM_LIMIT),
        name="rwkv_prepare",
    )(u, u, up, un, up, un, mu_rkv, mu_l, w0, wup, a0.reshape(1, RW), aup, gup,
      k_k.reshape(1, RW), k_a.reshape(1, RW), r_k.reshape(1, RW), g_down, g_upm)
    return outs


def _chunk_block(d, b, s, nb, rows=CHUNK):
    nctx, nlat = C // rows, T // rows
    pos_c = jnp.where(d == 0, s, nctx - 1 - s)
    pos_l = jnp.where(d == 0, s - nctx, nlat - 1 - (s - nctx))
    return jnp.where(s < nctx, nb * nlat + b * nctx + pos_c, b * nlat + pos_l)


def _cast_specs(weights, step_of, n_steps):
    n_blocks = max(n for n in CAST_BLOCK_COUNTS if n <= n_steps)
    blk = lambda *idx: jnp.minimum(step_of(*idx), n_blocks - 1)
    in_specs, out_specs, out_shapes = [], [], []
    for w, layer in weights:
        rows, cols = w.shape[1] // n_blocks, w.shape[2]
        in_specs.append(pl.BlockSpec((None, rows, cols), lambda *idx, layer=layer: (layer, blk(*idx), 0)))
        out_specs.append(pl.BlockSpec((rows, cols), lambda *idx: (blk(*idx), 0)))
        out_shapes.append(jax.ShapeDtypeStruct(w.shape[1:], BF16))
    return in_specs, out_specs, out_shapes, [w for w, _ in weights]


def _cast_blocks(in_refs, out_refs):
    for w_ref, wo_ref in zip(in_refs, out_refs, strict=True):
        wo_ref[...] = w_ref[...].astype(BF16)


def _lane_stack(x, hw):
    head = lax.broadcasted_iota(jnp.int32, x.shape, 1) // hw
    return jnp.concatenate([jnp.where(head == h, x, 0.0) for h in range(GROUP // hw)], axis=0)


def _fold_rows(x, n):
    out = x[0:n]
    for h in range(1, x.shape[0] // n):
        out = out + x[h * n:(h + 1) * n]
    return out


def _block_masks(size, blk, d):
    row = lax.broadcasted_iota(jnp.int32, (size, size), 0)
    col = lax.broadcasted_iota(jnp.int32, (size, size), 1)
    rr, cc = row % blk, col % blk
    same = (row // blk) == (col // blk)
    if d == 0:
        return cc <= rr, cc < rr, same
    return cc >= rr, cc > rr, same


def _rwkv_scan_kernel(*refs):
    in_refs = (refs[0:6], refs[6:12])
    y_refs = refs[12:14]
    s_ref = refs[14]

    @pl.when(pl.program_id(1) == 0)
    def _():
        s_ref[...] = jnp.zeros_like(s_ref)

    n = CHUNK
    ngrp = RW // GROUP
    nch = 2 * ngrp
    rrow = lax.broadcasted_iota(jnp.int32, (n, GROUP), 0)
    rcol = lax.broadcasted_iota(jnp.int32, (n, GROUP), 1) % n
    eye_row = jnp.where(rrow == rcol, 1.0, 0.0)
    incl_row = (rcol <= rrow, rcol >= rrow)
    strict_row = (rcol < rrow, rcol > rrow)
    same = _block_masks(GROUP, n, 0)[2]

    def block_diag(x_row):
        xb = x_row.astype(BF16)
        return jnp.where(same, jnp.concatenate([xb] * (GROUP // n), axis=0), jnp.zeros((), BF16))

    def stack(x):
        return _lane_stack(x.astype(BF16), HD)

    pre = []
    for d in range(2):
        tri = jnp.where(_block_masks(n, n, d)[0], 1.0, 0.0).astype(BF16)
        order = range(SCAN_CHUNKS) if d == 0 else range(SCAN_CHUNKS - 1, -1, -1)
        pre_d = []
        for ci in order:
            rows = slice(ci * n, (ci + 1) * n)
            r, k, v, a, b, lw = [ref[rows, :] for ref in in_refs[d]]
            c = _tri_dot(tri, lw)
            c_last = c[n - 1:n] if d == 0 else c[0:1]
            p_inv = jnp.exp(-c)
            p_end = jnp.exp(c_last - c)
            pre_d.append(dict(rows=rows, at=a * jnp.exp(c - lw), rt=r * jnp.exp(c), bt=b * p_inv, kt=k * p_inv,
                              bp=b * p_end, kp=k * p_end, v=v, dec=jnp.exp(c_last)))
        pre.append(pre_d)
    chains = [(d, p, g) for p in range(SCAN_CHUNKS) for d in range(2) for g in range(ngrp)]
    nch = len(chains)
    sl = lambda g: slice(g * GROUP, (g + 1) * GROUP)
    part = lambda d, p, g, name: pre[d][p][name][:, sl(g)]

    a_st = [stack(part(d, p, g, "at")) for d, p, g in chains]
    v_st = [stack(part(d, p, g, "v")) for d, p, g in chains]
    gram = [_dot_nt(jnp.concatenate([part(d, p, g, "at"), part(d, p, g, "rt")], axis=0).astype(BF16),
                    jnp.concatenate([stack(part(d, p, g, "bt")), stack(part(d, p, g, "kt"))], axis=0))
            for d, p, g in chains]
    a_ab = [jnp.where(strict_row[d], gram[i][0:n, 0:GROUP], 0.0) for i, (d, p, g) in enumerate(chains)]
    a_ak = [jnp.where(strict_row[d], gram[i][0:n, GROUP:], 0.0).astype(BF16) for i, (d, p, g) in enumerate(chains)]
    a_r = [jnp.where(jnp.concatenate([incl_row[d]] * 2, axis=1), gram[i][n:2 * n, :], 0.0).astype(BF16)
           for i, (d, p, g) in enumerate(chains)]
    av = [_dot(a_ak[i], v_st[i]) for i in range(nch)]
    tm = [eye_row + m for m in a_ab]
    pw = [_dot(m.astype(BF16), block_diag(m)) for m in a_ab]
    for _ in range(4):
        both = [_dot(jnp.concatenate([t, p], axis=0).astype(BF16), block_diag(p)) for t, p in zip(tm, pw)]
        tm = [t + x[0:n] for t, x in zip(tm, both)]
        pw = [x[n:2 * n] for x in both]
    tm = [t + _dot(t.astype(BF16), block_diag(p)) for t, p in zip(tm, pw)]
    tx = [_dot(tm[i].astype(BF16), jnp.concatenate([a_st[i], stack(av[i])], axis=1))
          for i in range(nch)]

    state = {(d, g): s_ref[d, g] for d in range(2) for g in range(ngrp)}
    for p in range(SCAN_CHUNKS):
        ids = [i for i, ch in enumerate(chains) if ch[1] == p]
        uy0 = {i: _dot_nt(jnp.concatenate([tx[i][:, 0:GROUP], part(*chains[i], "rt")], axis=0).astype(BF16),
                          state[chains[i][0], chains[i][2]].astype(BF16)) for i in ids}
        u = {i: uy0[i][0:n] + tx[i][:, GROUP:] for i in ids}
        for i in ids:
            d, _, g = chains[i]
            y_refs[d][pre[d][p]["rows"], sl(g)] = (
                uy0[i][n:2 * n] + _dot(a_r[i], jnp.concatenate([stack(u[i]), v_st[i]], axis=0)))
        for i in ids:
            d, _, g = chains[i]
            upd = _dot_tn(jnp.concatenate([u[i], part(d, p, g, "v")], axis=0).astype(BF16),
                          jnp.concatenate([part(d, p, g, "bp"), part(d, p, g, "kp")], axis=0).astype(BF16))
            state[d, g] = state[d, g] * part(d, p, g, "dec") + jnp.where(same, upd, 0.0)
    for (d, g), s_new in state.items():
        s_ref[d, g] = s_new


def _rwkv_scan(r, k, v, a, b, lw, nb):
    rows = r.shape[0]
    blk_rows = SCAN_CHUNKS * CHUNK
    steps = (T + C) // blk_rows
    in_specs, args = [], []
    for d in range(2):
        blk = lambda bb, s, d=d: (_chunk_block(d, bb, s, nb, blk_rows), 0)
        in_specs += [pl.BlockSpec((blk_rows, RW), blk)] * 5
        in_specs.append(pl.BlockSpec((None, blk_rows, RW),
                                     lambda bb, s, d=d: (d, _chunk_block(d, bb, s, nb, blk_rows), 0)))
        args += [r, k, v, a, b, lw]
    out_sd = jax.ShapeDtypeStruct((rows, RW), F32)
    return pl.pallas_call(
        _rwkv_scan_kernel,
        grid=(nb, steps),
        in_specs=in_specs,
        out_specs=[pl.BlockSpec((blk_rows, RW), lambda bb, s, d=d: (_chunk_block(d, bb, s, nb, blk_rows), 0))
                   for d in range(2)],
        out_shape=[out_sd, out_sd],
        scratch_shapes=[pltpu.VMEM((2, RW // GROUP, GROUP, GROUP), F32)],
        compiler_params=_params(("parallel", "arbitrary"), V7X_VMEM_LIMIT),
        name="rwkv_scan",
    )(*args)


def _ab_mix(y0_ref, y1_ref, bonus_ref, g_ref, lnw_ref, lnb_ref, gd_ref, gu_ref, *yb_refs, pick, tile, rows):
    y = y0_ref[rows, :] + y1_ref[rows, :]
    mu = _seg_sum(y, gd_ref[...], gu_ref[...]) * (1.0 / HD)
    yc = y - mu
    var = _seg_sum(yc * yc, gd_ref[...], gu_ref[...]) * (1.0 / HD)
    yn = yc * lax.rsqrt(var + RWKV_GN_EPS)
    return ((yn * lnw_ref[...] + lnb_ref[...] + bonus_ref[rows, :]) * g_ref[rows, :],
            pick(tile, *yb_refs, rows=rows))


def _ab_mix_inputs(tile, y0, y1, bonus, g, ln_w, ln_b, yb_lat, yb_ctx):
    tm = OPROJ_TM
    head = np.arange(RW) // HD
    g_down = jnp.asarray(head[:, None] == np.arange(128)[None, :], BF16)
    g_upm = jnp.asarray(np.arange(128)[:, None] == head[None, :], BF16)
    row = pl.BlockSpec((tm, RW), lambda i: (tile(i), 0))
    vec = pl.BlockSpec((1, RW), lambda i: (0, 0))
    yb_args, yb_specs, pick = _split_rows(yb_lat, yb_ctx, tm, 1024, tile=tile)
    specs = [row, row, row, row, vec, vec,
             pl.BlockSpec((RW, 128), lambda i: (0, 0)), pl.BlockSpec((128, RW), lambda i: (0, 0))] + yb_specs
    args = (y0, y1, bonus, g, ln_w.reshape(1, RW), ln_b.reshape(1, RW), g_down, g_upm) + yb_args
    return functools.partial(_ab_mix, pick=pick), args, specs


def _kv_block(i, nb):
    per = T // 256
    return jnp.where(i < nb * per, i // per, i - nb * per), jnp.where(i < nb * per, i % per, per)


def _swa_prep_kernel(k_ref, v_ref, cos_ref, sin_ref, ko_ref, vo_ref):
    cosl, sinl = cos_ref[...], sin_ref[...]
    lane = lax.broadcasted_iota(jnp.int32, cosl.shape, 1)
    low = lane < HD
    for src, dst, rope in ((k_ref, ko_ref, True), (v_ref, vo_ref, False)):
        for t in range(2):
            x = src[:, t * 128:(t + 1) * 128]
            if rope:
                x = _rope(x, cosl, sinl)
            swapped = pltpu.roll(x, HD, axis=1)
            dst[:, (2 * t) * 128:(2 * t + 1) * 128] = jnp.where(low, x, swapped).astype(BF16)
            dst[:, (2 * t + 1) * 128:(2 * t + 2) * 128] = jnp.where(low, swapped, x).astype(BF16)


def _swa_prep(u, cosl, sinl, nb):
    rows = u.shape[0]
    tm = 256
    kv_out = pl.BlockSpec((None, tm, 512), lambda i: (*_kv_block(i, nb), 0))
    sd = jax.ShapeDtypeStruct((nb, T + C, 512), BF16)
    return pl.pallas_call(
        _swa_prep_kernel,
        grid=(rows // tm,),
        in_specs=[pl.BlockSpec((tm, 256), lambda i: (i, AB_KS // 256)),
                  pl.BlockSpec((tm, 256), lambda i: (i, AB_VS // 256)),
                  pl.BlockSpec((tm, 128), lambda i: (i, 0)),
                  pl.BlockSpec((tm, 128), lambda i: (i, 0))],
        out_specs=[kv_out, kv_out],
        out_shape=[sd, sd],
        compiler_params=_params(("parallel",), V7X_VMEM_LIMIT),
        name="swa_prep",
    )(u, u, cosl, sinl)


def _diff_prep_kernel(k_ref, v_ref, cos_ref, sin_ref, ko_ref, vo_ref):
    cosl, sinl = cos_ref[...], sin_ref[...]
    for t in range(8):
        sl = slice(t * 128, (t + 1) * 128)
        ko_ref[:, sl] = _rope(k_ref[:, sl], cosl, sinl).astype(BF16)
    vo_ref[...] = v_ref[...].astype(BF16)


def _diff_prep(u, cosl, sinl, nb):
    rows = u.shape[0]
    tm = 256
    kv_out = pl.BlockSpec((None, tm, 1024), lambda i: (*_kv_block(i, nb), 0))
    sd = jax.ShapeDtypeStruct((nb, T + C, 1024), BF16)
    return pl.pallas_call(
        _diff_prep_kernel,
        grid=(rows // tm,),
        in_specs=[pl.BlockSpec((tm, 1024), lambda i: (i, 1)),
                  pl.BlockSpec((tm, 1024), lambda i: (i, 2)),
                  pl.BlockSpec((tm, 128), lambda i: (i, 0)),
                  pl.BlockSpec((tm, 128), lambda i: (i, 0))],
        out_specs=[kv_out, kv_out],
        out_shape=[sd, sd],
        compiler_params=_params(("parallel",), V7X_VMEM_LIMIT),
        name="diff_prep",
    )(u, u, cosl, sinl)


def _swa_kernel(q_ref, cos_ref, sin_ref, k_ref, v_ref, sink_ref, *rest):
    n_cast = len(rest) // 2
    o_ref = rest[n_cast]
    ctx_mode = n_cast == 0
    _cast_blocks(rest[:n_cast], rest[n_cast + 1:])
    n = pl.program_id(1)
    cosl, sinl = cos_ref[...], sin_ref[...]
    QB = q_ref.shape[0]
    span = QB + 2 * SWA_WINDOW
    if not ctx_mode:
        start = pl.multiple_of(jnp.clip(n * QB - SWA_WINDOW, 0, T - span), SWA_WINDOW)
        kpos = start + lax.broadcasted_iota(jnp.int32, (2 * QB, span), 1)
        qpos = n * QB + lax.broadcasted_iota(jnp.int32, (2 * QB, span), 0) % QB
        valid = jnp.abs(kpos - qpos) <= SWA_WINDOW
    low = lax.broadcasted_iota(jnp.int32, (QB, 128), 1) < HD
    top = lax.broadcasted_iota(jnp.int32, (2 * QB, 1), 0) < QB
    tile = lambda j: slice((j // 2) * 128, (j // 2 + 1) * 128)
    qs, sink = [], []
    for j in range(8):
        q = _rope(q_ref[:, j * 128:(j + 1) * 128], cosl, sinl) * (HD ** -0.5 * LOG2E)
        qs.append(jnp.concatenate([jnp.where(low, q, 0.0), jnp.where(low, 0.0, q)], axis=0).astype(BF16))
        sink.append(jnp.where(top, sink_ref[2 * j:2 * j + 1, 0:1], sink_ref[2 * j + 1:2 * j + 2, 0:1]) * LOG2E)
    s_c = [_dot_nt(qs[j], k_ref[T:T + C, tile(j)]) for j in range(8)]
    m = [jnp.maximum(jnp.max(s, axis=-1, keepdims=True), sk) for s, sk in zip(s_c, sink)]
    if not ctx_mode:
        s_w = [jnp.where(valid, _dot_nt(qs[j], k_ref[pl.ds(start, span), tile(j)]), -jnp.inf) for j in range(8)]
        m = [jnp.maximum(mm, jnp.max(s, axis=-1, keepdims=True)) for mm, s in zip(m, s_w)]
    p_c = [jnp.exp2(s - mm) for s, mm in zip(s_c, m)]
    den = [jnp.sum(p, axis=-1, keepdims=True) + jnp.exp2(sk - mm) for p, sk, mm in zip(p_c, sink, m)]
    pv = [_dot(p.astype(BF16), v_ref[T:T + C, tile(j)]) for j, p in enumerate(p_c)]
    if not ctx_mode:
        p_w = [jnp.exp2(s - mm) for s, mm in zip(s_w, m)]
        den = [dd + jnp.sum(p, axis=-1, keepdims=True) for dd, p in zip(den, p_w)]
        pv = [x + _dot(p.astype(BF16), v_ref[pl.ds(start, span), tile(j)]) for j, (x, p) in enumerate(zip(pv, p_w))]
    for j in range(8):
        o = pv[j] / den[j]
        o_ref[:, j * 128:(j + 1) * 128] = jnp.where(low, o[0:QB], o[QB:2 * QB])


def _swa(u, cosl, sinl, kpad, vpad, sink, nb, ctx_mode, weights=()):
    assert ctx_mode == (not weights)
    QB = SWA_QB
    nq = (C if ctx_mode else T) // QB
    base = nb * (T // QB) if ctx_mode else 0
    rows = nb * nq * QB
    rowblk = lambda b, n: base + b * nq + n
    kv = pl.BlockSpec((None, T + C, 512), lambda b, n: (b, 0, 0))
    in_specs = [pl.BlockSpec((QB, 1024), lambda b, n: (rowblk(b, n), AB_Q // 1024)),
                pl.BlockSpec((QB, 128), lambda b, n: (rowblk(b, n), 0)),
                pl.BlockSpec((QB, 128), lambda b, n: (rowblk(b, n), 0)),
                kv, kv, pl.BlockSpec((16, 128), lambda b, n: (0, 0))]
    out_specs = [pl.BlockSpec((QB, 1024), lambda b, n: (b * nq + n, 0))]
    out_shape = [jax.ShapeDtypeStruct((rows, 1024), F32)]
    args = [u, cosl, sinl, kpad, vpad, sink]
    if weights:
        cast_in, cast_out, cast_shape, cast_args = _cast_specs(weights, lambda b, n: b * nq + n, nb * nq)
        in_specs += cast_in
        out_specs += cast_out
        out_shape += cast_shape
        args += cast_args
    return pl.pallas_call(
        _swa_kernel,
        grid=(nb, nq),
        in_specs=in_specs,
        out_specs=out_specs,
        out_shape=out_shape,
        compiler_params=_params(("arbitrary", "arbitrary"), V7X_VMEM_LIMIT),
        name="swa_ctx" if ctx_mode else "swa_latent",
    )(*args)


def _diff_kernel(q_ref, cos_ref, sin_ref, k_ref, v_ref, lam_ref, sub_ref, o_ref):
    cosl, sinl = cos_ref[...], sin_ref[...]
    lam = lam_ref[...]
    nq = q_ref.shape[0]
    low = lax.broadcasted_iota(jnp.int32, (nq, 128), 1) < HD
    tile = lambda h: slice(h * 128, (h + 1) * 128)

    def logits(h):
        q = _rope(q_ref[:, tile(h)], cosl, sinl) * (HD ** -0.5 * LOG2E)
        qs = jnp.concatenate([jnp.where(low, q, 0.0), jnp.where(low, 0.0, q)], axis=0).astype(BF16)
        return _dot_nt(qs, k_ref[:, tile(h)])

    s_next = logits(0)
    for h in range(8):
        s = s_next
        if h + 1 < 8:
            s_next = logits(h + 1)
        e = jnp.exp2(s - jnp.max(s, axis=-1, keepdims=True))
        inv = 1.0 / jnp.sum(e, axis=-1, keepdims=True)
        o2 = _dot(e.astype(BF16), v_ref[:, tile(h)])
        o = o2[0:nq] * inv[0:nq] - o2[nq:2 * nq] * (lam[:, 0:1] * inv[nq:2 * nq])
        o = o * lax.rsqrt(jnp.mean(o * o, axis=-1, keepdims=True) + 1e-5)
        o_ref[:, tile(h)] = o * sub_ref[...]


def _diff_attn(u, cosl, sinl, kb, vb, lam, sub, nb):
    QB = DIFF_QB
    nq = T // QB
    kv = pl.BlockSpec((None, T + C, 1024), lambda b, n: (b, 0, 0))
    vec = pl.BlockSpec((1, 128), lambda b, n: (0, 0))
    return pl.pallas_call(
        _diff_kernel,
        grid=(nb, nq),
        in_specs=[pl.BlockSpec((QB, 1024), lambda b, n: (b * nq + n, 0)),
                  pl.BlockSpec((QB, 128), lambda b, n: (b * nq + n, 0)),
                  pl.BlockSpec((QB, 128), lambda b, n: (b * nq + n, 0)),
                  kv, kv, vec, vec],
        out_specs=pl.BlockSpec((QB, 1024), lambda b, n: (b * nq + n, 0)),
        out_shape=jax.ShapeDtypeStruct((nb * T, 1024), F32),
        compiler_params=_params(("parallel", "arbitrary"), V7X_VMEM_LIMIT),
        name="diff_attn",
    )(u, cosl, sinl, kb, vb, lam, sub)


def _hgrn_exact_att(q, kk, bcum, tmp_ref):
    n = CHUNK
    tmp_ref[0] = bcum
    tmp_ref[1] = kk
    coli = lax.broadcasted_iota(jnp.int32, (n, 2 * n), 1)

    def body(s, acc):
        bs = tmp_ref[0, pl.ds(s, 1), :]
        ks = tmp_ref[1, pl.ds(s, 1), :]
        w = q * jnp.exp(jnp.minimum(bcum - bs, 0.0)) * ks
        c0 = jnp.sum(w[:, 0:HG_DK], axis=-1, keepdims=True)
        c1 = jnp.sum(w[:, HG_DK:], axis=-1, keepdims=True)
        return acc + jnp.concatenate([jnp.where(coli == s, c0, 0.0), jnp.where(coli == s + n, c1, 0.0)], axis=0)

    return lax.fori_loop(0, n, body, jnp.zeros((2 * n, 2 * n), F32))


def _hgrn_scan_kernel(*refs):
    n_cast = (len(refs) - 13) // 2
    in_refs = (refs[0:4], refs[4:8])
    o_refs = refs[8 + n_cast:10 + n_cast]
    s_ref, g_ref, tmp_ref = refs[10 + 2 * n_cast:]
    _cast_blocks(refs[8:8 + n_cast], refs[10 + n_cast:10 + 2 * n_cast])

    @pl.when(pl.program_id(1) == 0)
    def _():
        s_ref[...] = jnp.zeros_like(s_ref)

    n = CHUNK
    ngrp = 1024 // GROUP
    nsub = n // SUB
    pre = []
    for d in range(2):
        tri = jnp.where(_block_masks(n, n, d)[0], 1.0, 0.0).astype(BF16)
        lb = in_refs[d][3][...]
        order = range(SCAN_CHUNKS) if d == 0 else range(SCAN_CHUNKS - 1, -1, -1)
        pre_d = []
        for ci in order:
            rows = slice(ci * n, (ci + 1) * n)
            q, z, v = [ref[rows, :] for ref in in_refs[d][0:3]]
            logf = jnp.log(lb + (1.0 - lb) * _sigmoid(z))
            kk = (1.0 - lb) * _sigmoid(-z)
            bcum = _tri_dot(tri, logf)
            b_last = bcum[n - 1:n] if d == 0 else bcum[0:1]
            b_excl = bcum - logf
            qh, kh = [], []
            for sb in range(nsub):
                lo, hi = sb * SUB, (sb + 1) * SUB
                beta = b_excl[lo:lo + 1] if d == 0 else b_excl[hi - 1:hi]
                qh.append(q[lo:hi] * jnp.exp(bcum[lo:hi] - beta))
                kh.append(kk * jnp.exp(beta - bcum))
            pre_d.append(dict(rows=rows, q=q, v=v, kk=kk, bcum=bcum, qh=qh, kh=kh, qe=q * jnp.exp(bcum),
                              ke=kk * jnp.exp(b_last - bcum), dec=jnp.exp(b_last), min_logf=jnp.min(logf)))
        pre.append(pre_d)
    chains = [(d, p, g) for p in range(SCAN_CHUNKS) for d in range(2) for g in range(ngrp)]
    sl = lambda g: slice(g * GROUP, (g + 1) * GROUP)

    for i, (d, p, g) in enumerate(chains):
        c = pre[d][p]
        rows = [_dot_nt(_lane_stack(c["qh"][sb][:, sl(g)].astype(BF16), HG_DK),
                        _lane_stack(c["kh"][sb][:, sl(g)].astype(BF16), HG_DK)) for sb in range(nsub)]
        g_ref[i] = jnp.concatenate([rows[sb][h * SUB:(h + 1) * SUB] for h in range(2) for sb in range(nsub)], axis=0)

    min_logf = functools.reduce(jnp.minimum, [c["min_logf"] for pre_d in pre for c in pre_d])

    @pl.when(min_logf < -(HG_CLAMP / SUB))
    def _():
        for i, (d, p, g) in enumerate(chains):
            c = pre[d][p]
            g_ref[i] = _hgrn_exact_att(c["q"][:, sl(g)], c["kk"][:, sl(g)], c["bcum"][:, sl(g)], tmp_ref)

    state = {(d, g): s_ref[d, g] for d in range(2) for g in range(ngrp)}
    head_same = _block_masks(GROUP, HG_DK, 0)[2]
    for p in range(SCAN_CHUNKS):
        ids = [i for i, ch in enumerate(chains) if ch[1] == p]
        inter = {i: _dot_nt(pre[chains[i][0]][p]["qe"][:, sl(chains[i][2])].astype(BF16),
                            state[chains[i][0], chains[i][2]].astype(BF16)) for i in ids}
        for i in ids:
            d, _, g = chains[i]
            incl, _, same = _block_masks(2 * n, n, d)
            att = jnp.where(incl & same, g_ref[i], 0.0).astype(BF16)
            v_st = _lane_stack(pre[d][p]["v"][:, sl(g)].astype(BF16), HG_DK)
            o_refs[d][pre[d][p]["rows"], sl(g)] = inter[i] + _fold_rows(_dot(att, v_st), n)
        for i in ids:
            d, _, g = chains[i]
            upd = _dot_tn(pre[d][p]["v"][:, sl(g)].astype(BF16), pre[d][p]["ke"][:, sl(g)].astype(BF16))
            state[d, g] = state[d, g] * pre[d][p]["dec"][:, sl(g)] + jnp.where(head_same, upd, 0.0)
    for (d, g), s_new in state.items():
        s_ref[d, g] = s_new


def _hgrn_scan(u, lb, nb, weights):
    rows = u.shape[0]
    blk_rows = SCAN_CHUNKS * CHUNK
    steps = (T + C) // blk_rows
    cast_in, cast_out, cast_shape, cast_args = _cast_specs(weights, lambda bb, s: bb * steps + s, nb * steps)
    in_specs, args = [], []
    for d in range(2):
        col = lambda j, d=d: pl.BlockSpec((blk_rows, 1024),
                                          lambda bb, s: (_chunk_block(d, bb, s, nb, blk_rows), j))
        in_specs += [col(3), col(4 + d), col(6), pl.BlockSpec((None, 1, 1024), lambda bb, s, d=d: (d, 0, 0))]
        args += [u, u, u, lb]
    out_sd = jax.ShapeDtypeStruct((rows, 1024), F32)
    ngrp = 1024 // GROUP
    return pl.pallas_call(
        _hgrn_scan_kernel,
        grid=(nb, steps),
        in_specs=in_specs + cast_in,
        out_specs=[pl.BlockSpec((blk_rows, 1024), lambda bb, s, d=d: (_chunk_block(d, bb, s, nb, blk_rows), 0))
                   for d in range(2)] + cast_out,
        out_shape=[out_sd, out_sd] + cast_shape,
        scratch_shapes=[pltpu.VMEM((2, ngrp, GROUP, GROUP), F32),
                        pltpu.VMEM((2 * ngrp * SCAN_CHUNKS, 2 * CHUNK, 2 * CHUNK), F32),
                        pltpu.VMEM((2, CHUNK, GROUP), F32)],
        compiler_params=_params(("arbitrary", "arbitrary"), V7X_VMEM_LIMIT),
        name="hgrn_scan",
    )(*args, *cast_args)


def _cd_mix(yc_ref, o0_ref, o1_ref, g_ref, gn_ref, *, tile, rows):
    g = g_ref[rows, :]
    heads = []
    for h in range(8):
        sl = slice(h * 128, (h + 1) * 128)
        o = o0_ref[rows, sl] + o1_ref[rows, sl]
        o = o * lax.rsqrt(jnp.mean(o * o, axis=-1, keepdims=True) + NORM_EPS) * gn_ref[...]
        gh = g[:, sl]
        heads.append(o * (gh * _sigmoid(gh)))
    return yc_ref[rows, :], jnp.concatenate(heads, axis=1)


def _cd_mix_inputs(tile, yc, o0, o1, u, gnorm):
    tm = OPROJ_TM
    row = pl.BlockSpec((tm, 1024), lambda i: (tile(i), 0))
    specs = [row, row, row, pl.BlockSpec((tm, 1024), lambda i: (tile(i), 7)),
             pl.BlockSpec((1, 128), lambda i: (0, 0))]
    return _cd_mix, (yc, o0, o1, u, gnorm.reshape(1, 128)), specs


def _rope_tables(nb):
    t = np.arange(T)
    quarter = HD // 4
    inv = ROPE_THETA ** (-jnp.arange(quarter, dtype=F32) / quarter)
    rows = jnp.asarray(t // GRID_W, F32)
    cols = jnp.asarray(t % GRID_W, F32)
    ang = jnp.concatenate([rows[:, None] * inv, cols[:, None] * inv], axis=-1)
    cos, sin = jnp.cos(ang), jnp.sin(ang)
    cosl = jnp.tile(jnp.concatenate([cos, cos], axis=-1), (nb, 2))
    sinl = jnp.tile(jnp.concatenate([-sin, sin], axis=-1), (nb, 2))
    cosl = jnp.concatenate([cosl, jnp.ones((nb * C, 128), F32)], axis=0)
    sinl = jnp.concatenate([sinl, jnp.zeros((nb * C, 128), F32)], axis=0)
    return cosl, sinl


def kernel(x, c, ctx, c_ctx, ada_w, ada_b, mlp_w1, mlp_w2, final_norm, ab_w_in, ab_w_out, ab_mu, ab_w0,
           ab_w_up, ab_a0, ab_a_up, ab_g_up, ab_k_k, ab_k_a, ab_r_k, ab_ln_w, ab_ln_b, ab_sink,
           cd_w_in, cd_w_out, cd_lam, cd_subln, cd_gnorm, hgrn_lb_logits):
    nb = x.shape[0]
    assert x.shape == (nb, T, D) and ctx.shape == (nb, C, D) and nb < 8
    assert ada_w.shape[0] == 2, "one AB layer followed by one CD layer"
    rx = nb * T
    rows = rx + nb * C
    x2, c2 = x.reshape(rx, D), ctx.reshape(nb * C, D)
    cosl, sinl = _rope_tables(nb)

    s_in = jnp.zeros((8, D), F32).at[:nb].set(c).at[nb].set(c_ctx)
    mod = _ada(s_in, ada_w, ada_b)

    w = ab_w_in[0]
    w_in0 = jnp.concatenate([w[:, :3 * RW], w[:, 3 * RW + 3 * LORA:], w[:, 3 * RW:3 * RW + 3 * LORA],
                             jnp.zeros((D, LORA), F32)], axis=1).astype(BF16)
    u = _norm_mod_matmul(x2, c2, mod, 0, w_in0, nb, 512, AB_N // 2)
    r, k, v, a, b, lw, g, bonus = _rwkv_prepare(u, nb, ab_mu[0], ab_w0[0], ab_w_up[0], ab_a0[0], ab_a_up[0],
                                                ab_g_up[0], ab_k_k[0], ab_k_a[0], ab_r_k[0].reshape(RW))
    y0, y1 = _rwkv_scan(r, k, v, a, b, lw, nb)
    kpad, vpad = _swa_prep(u, cosl, sinl, nb)
    sink = jnp.broadcast_to(ab_sink[0][:, None], (16, 128))
    yb_lat, w1b, w2b, w_in1, w_out0 = _swa(u, cosl, sinl, kpad, vpad, sink, nb, False,
                                           [(mlp_w1, 0), (mlp_w2, 0), (cd_w_in, 0), (ab_w_out, 0)])
    (yb_ctx,) = _swa(u, cosl, sinl, kpad, vpad, sink, nb, True)
    mix_inputs = functools.partial(_ab_mix_inputs, y0=y0, y1=y1, bonus=bonus, g=g, ln_w=ab_ln_w[0],
                                   ln_b=ab_ln_b[0], yb_lat=yb_lat, yb_ctx=yb_ctx)
    xc = _out_proj(mix_inputs, x2, c2, rows, mod, 0, w_out0, nb)
    fn = final_norm.reshape(1, D)
    xc = _mlp(xc, rows, mod, 0, w1b, w2b, fn, nb, False)

    lam_init = 0.8 - 0.6 * math.exp(-0.3 * 1)
    lb_table = jnp.cumsum(jax.nn.softmax(hgrn_lb_logits.astype(F32), axis=0), axis=0)
    lb = (lb_table - lb_table[0])[1].reshape(2, 1, 1024)
    lf = cd_lam[0].astype(F32)
    lmb = jnp.exp(jnp.sum(lf[0] * lf[1])) - jnp.exp(jnp.sum(lf[2] * lf[3])) + lam_init
    u = _norm_mod_matmul(xc, None, mod, 1, w_in1, nb, 512, 2048)
    kb, vb = _diff_prep(u, cosl, sinl, nb)
    yc = _diff_attn(u, cosl, sinl, kb, vb, jnp.full((1, 128), lmb, F32),
                    (cd_subln[0] * (1.0 - lam_init)).reshape(1, 128), nb)
    o0, o1, w1b, w2b, w_out1 = _hgrn_scan(u, lb, nb, [(mlp_w1, 1), (mlp_w2, 1), (cd_w_out, 0)])
    mix_inputs = functools.partial(_cd_mix_inputs, yc=yc, o0=o0, o1=o1, u=u, gnorm=cd_gnorm[0])
    xl = _out_proj(mix_inputs, xc, None, rx, mod, 1, w_out1, nb)
    out = _mlp(xl, rx, mod, 1, w1b, w2b, fn, nb, True)
    return out.reshape(nb, T, D)
```
